```python
import jax, jax.numpy as jnp
from jax import lax
import numpy as np

D_MODEL = 1024
BATCH = 1
SEQ = 16384
DEPTH = 4

GRID_W = 64
CTX_LEN = 256
NA_HEADS = 8
NA_HEAD_DIM = 64
NA_KH = 8
NA_KW = 16
NA_W = NA_HEADS * NA_HEAD_DIM
MLA_HEADS = 8
MLA_NOPE = 64
MLA_ROPE = 32
MLA_V = 64
MLA_Q_LORA = 256
MLA_KV_LORA = 128
MLA_OUT_W = MLA_HEADS * MLA_V
ROPE_THETA = 10000.0
Q_BLOCK = 128
MIX_W = NA_W + MLA_OUT_W
IN_COLS = 3 * NA_W + MLA_Q_LORA + MLA_KV_LORA + MLA_ROPE
SPLITS = [NA_W, 2 * NA_W, 3 * NA_W, 3 * NA_W + MLA_Q_LORA, 3 * NA_W + MLA_Q_LORA + MLA_KV_LORA]
POOL_WINDOWS = (2, 4, 8, 16)
POOL_GROUPS = len(POOL_WINDOWS)
POOL_GROUP = D_MODEL // POOL_GROUPS
FFN_DIM = 2816
N_EXPERTS = 8
TOP_K = 2
EXPERT_DIM = 3584
N_ATTN_LAYERS = (DEPTH + 1) // 2
N_POOL_LAYERS = DEPTH // 2
DEEPNORM_ALPHA = (2 * DEPTH) ** 0.25
DEEPNORM_BETA = (8 * DEPTH) ** -0.25
LN_EPS = 1e-5
RMS_EPS = 1e-6

kernel_name = 'hybrid_na_mla_pool_moe_trunk'


def layer_norm(x, g, b):
    xf = x.astype(jnp.float32)
    mu = jnp.mean(xf, -1, keepdims=True)
    var = jnp.mean(jnp.square(xf - mu), -1, keepdims=True)
    return ((xf - mu) * lax.rsqrt(var + LN_EPS)).astype(x.dtype) * g + b


def rms_norm(x, g):
    xf = x.astype(jnp.float32)
    return (xf * lax.rsqrt(jnp.mean(xf * xf, -1, keepdims=True) + RMS_EPS)).astype(x.dtype) * g


def modulate(h, shift, scale):
    return h * (1 + scale) + shift


def softmax_f32(s, like):
    return jax.nn.softmax(s.astype(jnp.float32), axis=-1).astype(like.dtype)


def _rotate(x, ang):
    half = x.shape[-1] // 2
    x1, x2 = x[..., :half], x[..., half:]
    cos = jnp.cos(ang).astype(x.dtype)
    sin = jnp.sin(ang).astype(x.dtype)
    return jnp.concatenate([x1 * cos - x2 * sin, x1 * sin + x2 * cos], -1)


def axial_rope(x, ang_row, ang_col):
    if x.ndim == 4:
        ang_row, ang_col = ang_row[:, None], ang_col[:, None]
    half = x.shape[-1] // 2
    return jnp.concatenate([_rotate(x[..., :half], ang_row), _rotate(x[..., half:], ang_col)], -1)


def grid_angles(n):
    t = jnp.arange(n, dtype=jnp.int32)
    row = (t // GRID_W).astype(jnp.float32)
    col = (t % GRID_W).astype(jnp.float32)
    n_freq = MLA_ROPE // 4
    inv = 1.0 / (ROPE_THETA ** (jnp.arange(n_freq, dtype=jnp.float32) / n_freq))
    return row[:, None] * inv, col[:, None] * inv


def neighbourhood_attention(q, k, v, kc, vc, rel_bias):
    b, s, h, dh = q.shape
    rows = s // GRID_W
    kh = min(NA_KH, rows)
    n_loc = kh * NA_KW
    scale = dh ** -0.5
    cols = jnp.arange(GRID_W)
    col_start = jnp.clip(cols - NA_KW // 2, 0, GRID_W - NA_KW)
    key_cols = col_start[:, None] + jnp.arange(NA_KW)
    dcol = key_cols - cols[:, None] + NA_KW - 1

    def row_fn(r):
        row_start = jnp.clip(r - kh // 2, 0, rows - kh)
        key_rows = row_start + jnp.arange(kh)
        idx = (key_rows[None, :, None] * GRID_W + key_cols[:, None, :]).reshape(GRID_W, n_loc)
        drow = key_rows - r + NA_KH - 1
        bias = rel_bias[:, drow[None, :, None], dcol[:, None, :]].reshape(h, GRID_W, n_loc)
        q_r = lax.dynamic_slice_in_dim(q, r * GRID_W, GRID_W, axis=1)
        k_g = jnp.take(k, idx, axis=1)
        v_g = jnp.take(v, idx, axis=1)
        s_loc = jnp.einsum('bqhd,bqkhd->bhqk', q_r, k_g) * scale + bias
        s_ctx = jnp.einsum('bqhd,bkhd->bhqk', q_r, kc) * scale
        p = softmax_f32(jnp.concatenate([s_loc, s_ctx], -1), v)
        return (jnp.einsum('bhqk,bqkhd->bqhd', p[..., :n_loc], v_g)
                + jnp.einsum('bhqk,bkhd->bqhd', p[..., n_loc:], vc))

    out = lax.map(row_fn, jnp.arange(rows))
    return jnp.moveaxis(out, 0, 1).reshape(b, s, h, dh)


def dense_attention(q, k, v, scale):
    s = jnp.einsum('bqhd,bkhd->bhqk', q, k) * scale
    return jnp.einsum('bhqk,bkhd->bqhd', softmax_f32(s, v), v)


def mla_attention(qn, qr, kn, kr, v, scale):
    b, n, h, _ = qn.shape
    nb = n // Q_BLOCK

    def blk(args):
        qn_b, qr_b = args
        s = (jnp.einsum('bqhd,bkhd->bhqk', qn_b, kn) + jnp.einsum('bqhr,bkr->bhqk', qr_b, kr)) * scale
        return jnp.einsum('bhqk,bkhd->bqhd', softmax_f32(s, v), v)

    qn_blocks = qn.reshape(b, nb, Q_BLOCK, h, qn.shape[-1]).swapaxes(0, 1)
    qr_blocks = qr.reshape(b, nb, Q_BLOCK, h, qr.shape[-1]).swapaxes(0, 1)
    out = lax.map(blk, (qn_blocks, qr_blocks))
    return out.swapaxes(0, 1).reshape(b, n, h, v.shape[-1])


def _project(hh, w_in, q_norm, w_q_up, kv_norm, w_kv_up):
    bsz, n, _ = hh.shape
    qa, ka, va, q_c, kv_c, k_rope = jnp.split(hh @ w_in, SPLITS, axis=-1)
    heads = lambda t: t.reshape(bsz, n, NA_HEADS, NA_HEAD_DIM)
    q_mla = (rms_norm(q_c, q_norm) @ w_q_up).reshape(bsz, n, MLA_HEADS, MLA_NOPE + MLA_ROPE)
    kv_mla = (rms_norm(kv_c, kv_norm) @ w_kv_up).reshape(bsz, n, MLA_HEADS, MLA_NOPE + MLA_V)
    return (heads(qa), heads(ka), heads(va), q_mla[..., :MLA_NOPE], q_mla[..., MLA_NOPE:],
            kv_mla[..., :MLA_NOPE], k_rope, kv_mla[..., MLA_NOPE:])


def attention_mixer(h_lat, h_ctx, w_in, rel_bias, q_norm, w_q_up, kv_norm, w_kv_up, w_out, with_ctx_out):
    b, s, _ = h_lat.shape
    qa, ka, va, qn, qr, kn, kr, vb = _project(h_lat, w_in, q_norm, w_q_up, kv_norm, w_kv_up)
    qa_c, ka_c, va_c, qn_c, qr_c, kn_c, kr_c, vb_c = _project(h_ctx, w_in, q_norm, w_q_up, kv_norm, w_kv_up)
    ang_r, ang_c = grid_angles(s)
    qr = axial_rope(qr, ang_r, ang_c)
    kr = axial_rope(kr, ang_r, ang_c)
    mla_scale = (MLA_NOPE + MLA_ROPE) ** -0.5
    o_a = neighbourhood_attention(qa, ka, va, ka_c, va_c, rel_bias)
    o_b = mla_attention(qn, qr, jnp.concatenate([kn_c, kn], 1), jnp.concatenate([kr_c, kr], 1),
                        jnp.concatenate([vb_c, vb], 1), mla_scale)
    y_lat = jnp.concatenate([o_a.reshape(b, s, NA_W), o_b.reshape(b, s, MLA_OUT_W)], -1) @ w_out
    if not with_ctx_out:
        return y_lat, None
    n_ctx = h_ctx.shape[1]
    o_a_c = dense_attention(qa_c, ka_c, va_c, NA_HEAD_DIM ** -0.5)
    o_b_c = mla_attention(qn_c, qr_c, kn_c, kr_c, vb_c, mla_scale)
    y_ctx = jnp.concatenate([o_a_c.reshape(b, n_ctx, NA_W), o_b_c.reshape(b, n_ctx, MLA_OUT_W)], -1) @ w_out
    return y_lat, y_ctx


def pool_mixer(h, w, scale):
    b, n, d = h.shape
    t = jnp.arange(n)
    half = jnp.array(POOL_WINDOWS, dtype=jnp.int32) // 2
    lo = jnp.clip(t[:, None] - half, 0, n)
    hi = jnp.clip(t[:, None] + half, 0, n)
    cnt = (hi - lo).astype(jnp.float32)
    hf = h.astype(jnp.float32).reshape(b, n, POOL_GROUPS, POOL_GROUP)
    cs = jnp.concatenate([jnp.zeros((b, 1, POOL_GROUPS, POOL_GROUP), jnp.float32), jnp.cumsum(hf, axis=1)], 1)
    grp = jnp.arange(POOL_GROUPS)
    win_sum = cs[:, hi, grp] - cs[:, lo, grp]
    mixed = (win_sum / cnt[..., None] - hf).astype(h.dtype)
    return jnp.einsum('bngc,gcd->bngd', mixed, w).reshape(b, n, d) * scale


def swiglu(h, wg, wu, wd):
    return (jax.nn.silu(h @ wg) * (h @ wu)) @ wd


def moe_swiglu(h, router, wg, wu, wd):
    logits = (h @ router).astype(jnp.float32)
    top_val, top_idx = lax.top_k(logits, TOP_K)
    top_w = jax.nn.softmax(top_val, axis=-1)
    gates = jnp.sum(jax.nn.one_hot(top_idx, N_EXPERTS, dtype=jnp.float32) * top_w[..., None], axis=-2).astype(h.dtype)
    out = jnp.zeros_like(h)
    for e in range(N_EXPERTS):
        out = out + gates[..., e:e + 1] * swiglu(h, wg[e], wu[e], wd[e])
    return out


def setup_inputs(seed: int = 0) -> dict:
    key = jax.random.key(seed)
    ks = iter(jax.random.split(key, 40))

    def nrm(shape, scale):
        return jax.random.normal(next(ks), shape, jnp.float32) * scale

    D = D_MODEL
    beta = DEEPNORM_BETA
    return {
        'x': nrm((BATCH, SEQ, D), 1.0),
        'c': nrm((BATCH, D), 1.0),
        'ctx': nrm((BATCH, CTX_LEN, D), 1.0),
        'c_ctx': nrm((D,), 1.0),
        'mod_w': nrm((DEPTH, D, 6 * D), D ** -0.5),
        'mod_b': nrm((DEPTH, 6 * D), 0.02),
        'ln1_g': 1.0 + nrm((DEPTH, D), 0.02),
        'ln1_b': nrm((DEPTH, D), 0.02),
        'ln2_g': 1.0 + nrm((DEPTH, D), 0.02),
        'ln2_b': nrm((DEPTH, D), 0.02),
        'attn_w_in': nrm((N_ATTN_LAYERS, D, IN_COLS), D ** -0.5),
        'na_rel_bias': nrm((N_ATTN_LAYERS, NA_HEADS, 2 * NA_KH - 1, 2 * NA_KW - 1), 0.5),
        'mla_q_norm': 1.0 + nrm((N_ATTN_LAYERS, MLA_Q_LORA), 0.02),
        'mla_w_q_up': nrm((N_ATTN_LAYERS, MLA_Q_LORA, MLA_HEADS * (MLA_NOPE + MLA_ROPE)), MLA_Q_LORA ** -0.5),
        'mla_kv_norm': 1.0 + nrm((N_ATTN_LAYERS, MLA_KV_LORA), 0.02),
        'mla_w_kv_up': nrm((N_ATTN_LAYERS, MLA_KV_LORA, MLA_HEADS * (MLA_NOPE + MLA_V)), MLA_KV_LORA ** -0.5),
        'attn_w_out': nrm((N_ATTN_LAYERS, MIX_W, D), MIX_W ** -0.5 * beta),
        'ffn_w_gate': nrm((N_ATTN_LAYERS, D, FFN_DIM), D ** -0.5),
        'ffn_w_up': nrm((N_ATTN_LAYERS, D, FFN_DIM), D ** -0.5),
        'ffn_w_down': nrm((N_ATTN_LAYERS, FFN_DIM, D), FFN_DIM ** -0.5 * beta),
        'pool_w': nrm((N_POOL_LAYERS, POOL_GROUPS, POOL_GROUP, POOL_GROUP), POOL_GROUP ** -0.5 * beta),
        'pool_scale': 1.0 + nrm((N_POOL_LAYERS, D), 0.02),
        'moe_router': nrm((N_POOL_LAYERS, D, N_EXPERTS), D ** -0.5),
        'moe_w_gate': nrm((N_POOL_LAYERS, N_EXPERTS, D, EXPERT_DIM), D ** -0.5),
        'moe_w_up': nrm((N_POOL_LAYERS, N_EXPERTS, D, EXPERT_DIM), D ** -0.5),
        'moe_w_down': nrm((N_POOL_LAYERS, N_EXPERTS, EXPERT_DIM, D), EXPERT_DIM ** -0.5 * beta),
    }


def reference(x, c, ctx, c_ctx, mod_w, mod_b, ln1_g, ln1_b, ln2_g, ln2_b, attn_w_in, na_rel_bias,
              mla_q_norm, mla_w_q_up, mla_kv_norm, mla_w_kv_up, attn_w_out, ffn_w_gate, ffn_w_up,
              ffn_w_down, pool_w, pool_scale, moe_router, moe_w_gate, moe_w_up, moe_w_down):
    for i in range(DEPTH):
        j = i // 2
        even = i % 2 == 0
        ctx_live = any(l % 2 == 0 for l in range(i + 1, DEPTH))
        m_lat = (jax.nn.silu(c) @ mod_w[i] + mod_b[i])[:, None, :]
        sh1, sc1, g1, sh2, sc2, g2 = jnp.split(m_lat, 6, axis=-1)
        m_ctx = jax.nn.silu(c_ctx) @ mod_w[i] + mod_b[i]
        csh1, csc1, cg1, csh2, csc2, cg2 = jnp.split(m_ctx, 6, axis=-1)

        h_lat = modulate(x, sh1, sc1)
        if even:
            h_ctx = modulate(ctx, csh1, csc1)
            y_lat, y_ctx = attention_mixer(h_lat, h_ctx, attn_w_in[j], na_rel_bias[j], mla_q_norm[j],
                                           mla_w_q_up[j], mla_kv_norm[j], mla_w_kv_up[j], attn_w_out[j],
                                           ctx_live)
        else:
            y_lat = pool_mixer(h_lat, pool_w[j], pool_scale[j])
            y_ctx = pool_mixer(modulate(ctx, csh1, csc1), pool_w[j], pool_scale[j]) if ctx_live else None
        x = layer_norm(DEEPNORM_ALPHA * x + g1 * y_lat, ln1_g[i], ln1_b[i])
        if ctx_live:
            ctx = layer_norm(DEEPNORM_ALPHA * ctx + cg1 * y_ctx, ln1_g[i], ln1_b[i])

        h = modulate(x, sh2, sc2)
        n_ctx = ctx.shape[1] if ctx_live else 0
        if ctx_live:
            h = jnp.concatenate([modulate(ctx, csh2, csc2), h], axis=1)
        if even:
            y = swiglu(h, ffn_w_gate[j], ffn_w_up[j], ffn_w_down[j])
        else:
            y = moe_swiglu(h, moe_router[j], moe_w_gate[j], moe_w_up[j], moe_w_down[j])
        x = layer_norm(DEEPNORM_ALPHA * x + g2 * y[:, n_ctx:], ln2_g[i], ln2_b[i])
        if ctx_live:
            ctx = layer_norm(DEEPNORM_ALPHA * ctx + cg2 * y[:, :n_ctx], ln2_g[i], ln2_b[i])
    return x
```

```python
import functools

import numpy as np
import jax
import jax.numpy as jnp
from jax import lax
from jax.experimental import pallas as pl
from jax.experimental.pallas import tpu as pltpu

F32 = jnp.float32
BF16 = jnp.bfloat16

D_MODEL = 1024
GRID_W = 64
N_HEADS = 8
HEAD_PAD = 128
HP = N_HEADS * HEAD_PAD
NA_HEAD_DIM = 64
NA_KH = 8
NA_KW = 16
MLA_NOPE = 64
MLA_ROPE = 32
MLA_V = 64
MLA_Q_LORA = 256
MLA_KV_LORA = 128
ROPE_THETA = 10000.0
POOL_WINDOWS = (2, 4, 8, 16)
POOL_GROUP = D_MODEL // len(POOL_WINDOWS)
POOL_HALO = 8
FFN_DIM = 2816
N_EXPERTS = 8
EXPERT_DIM = 3584
DEPTH = 4
DEEPNORM_ALPHA = (2 * DEPTH) ** 0.25
LN_EPS = 1e-5
RMS_EPS = 1e-6
LOG2E = 1.4426950408889634
NEG_BIG = -1e30

TM = 256
NA_QROWS = 4
NA_KROWS = NA_QROWS + NA_KH - 1
MLA_TQ = 512
MLA_TK = 1280
MOE_TM = 640
MOE_TF = 512
VMEM_LIMIT = 56 * 1024 * 1024


def _cparams(sem):
    return pltpu.CompilerParams(dimension_semantics=sem, vmem_limit_bytes=VMEM_LIMIT)


def _layer_norm(v, g, b):
    mu = jnp.mean(v, axis=-1, keepdims=True)
    d = v - mu
    var = jnp.mean(d * d, axis=-1, keepdims=True)
    return d * lax.rsqrt(var + LN_EPS) * g + b


def _silu(v):
    return v * jax.nn.sigmoid(v)


def _mod_kernel(ct_ref, w_ref, b_ref, o_ref):
    s = _silu(ct_ref[...])
    w = w_ref[...]
    b = b_ref[...]
    r0 = jnp.sum(w * s[:, 0:1], axis=0, keepdims=True) + b
    r1 = jnp.sum(w * s[:, 1:2], axis=0, keepdims=True) + b
    o_ref[...] = jnp.concatenate([r0, r1], axis=0)


def _modulation(ct, mod_w, mod_b):
    depth, d, n6 = mod_w.shape
    tn = 1536
    return pl.pallas_call(
        _mod_kernel,
        grid=(depth, n6 // tn),
        in_specs=[
            pl.BlockSpec((d, 2), lambda l, j: (0, 0)),
            pl.BlockSpec((None, d, tn), lambda l, j: (l, 0, j)),
            pl.BlockSpec((None, 1, tn), lambda l, j: (l, 0, j)),
        ],
        out_specs=pl.BlockSpec((None, 2, tn), lambda l, j: (l, 0, j)),
        out_shape=jax.ShapeDtypeStruct((depth, 2, n6), F32),
        compiler_params=_cparams(("arbitrary", "arbitrary")),
        name="modulation",
    )(ct, mod_w, mod_b.reshape(depth, 1, n6))


def _attn_proj_kernel(x_ref, mod_ref, win_ref, qn_ref, wq_ref, kvn_ref, wkv_ref, cos_ref, sin_ref,
                      qa_ref, ka_ref, va_ref, qm_ref, kmt_ref, vm_ref):
    h = x_ref[...] * (1.0 + mod_ref[1:2, :]) + mod_ref[0:1, :]
    p = jnp.dot(h.astype(BF16), win_ref[...], preferred_element_type=F32)
    qa_ref[...] = (p[:, 0:HP] * (NA_HEAD_DIM ** -0.5)).astype(BF16)
    ka_ref[...] = p[:, HP:2 * HP].astype(BF16)
    va_ref[...] = p[:, 2 * HP:3 * HP].astype(BF16)
    o = 3 * HP
    q_c = p[:, o:o + MLA_Q_LORA]
    o += MLA_Q_LORA
    kv_c = p[:, o:o + MLA_KV_LORA]
    o += MLA_KV_LORA
    krp = p[:, o:o + HEAD_PAD]
    krs = p[:, o + HEAD_PAD:o + 2 * HEAD_PAD]
    qn = q_c * lax.rsqrt(jnp.mean(q_c * q_c, axis=-1, keepdims=True) + RMS_EPS) * qn_ref[...]
    kvn = kv_c * lax.rsqrt(jnp.mean(kv_c * kv_c, axis=-1, keepdims=True) + RMS_EPS) * kvn_ref[...]
    q2 = jnp.dot(qn.astype(BF16), wq_ref[...], preferred_element_type=F32)
    kv2 = jnp.dot(kvn.astype(BF16), wkv_ref[...], preferred_element_type=F32)
    cos = cos_ref[...]
    sin = sin_ref[...]
    kr = krp * cos + krs * sin
    q_scale = (MLA_NOPE + MLA_ROPE) ** -0.5 * LOG2E
    for hd in range(N_HEADS):
        lo = hd * HEAD_PAD
        qh = (q2[:, lo:lo + HEAD_PAD] * cos + q2[:, HP + lo:HP + lo + HEAD_PAD] * sin) * q_scale
        qm_ref[:, lo:lo + HEAD_PAD] = qh.astype(BF16)
        kh = kv2[:, lo:lo + HEAD_PAD] + kr
        kmt_ref[hd] = kh.T.astype(BF16)
    vm_ref[...] = kv2[:, HP:2 * HP].astype(BF16)


def _attn_proj(xs, mods, layer, n_lat_tiles, w_in, q_norm, w_q, kv_norm, w_kv, cos_t, sin_t):
    m = xs.shape[0]
    nt = m // TM
    tok = lambda i: (i, 0)
    full = lambda i: (0, 0)
    act = jax.ShapeDtypeStruct((m, HP), BF16)
    return pl.pallas_call(
        _attn_proj_kernel,
        grid=(nt,),
        in_specs=[
            pl.BlockSpec((TM, D_MODEL), tok),
            pl.BlockSpec((None, None, 6, D_MODEL), lambda i: (layer, i // n_lat_tiles, 0, 0)),
            pl.BlockSpec(w_in.shape, full),
            pl.BlockSpec(q_norm.shape, full),
            pl.BlockSpec(w_q.shape, full),
            pl.BlockSpec(kv_norm.shape, full),
            pl.BlockSpec(w_kv.shape, full),
            pl.BlockSpec((TM, HEAD_PAD), tok),
            pl.BlockSpec((TM, HEAD_PAD), tok),
        ],
        out_specs=[
            pl.BlockSpec((TM, HP), tok),
            pl.BlockSpec((TM, HP), tok),
            pl.BlockSpec((TM, HP), tok),
            pl.BlockSpec((TM, HP), tok),
            pl.BlockSpec((N_HEADS, HEAD_PAD, TM), lambda i: (0, 0, i)),
            pl.BlockSpec((TM, HP), tok),
        ],
        out_shape=[act, act, act, act, jax.ShapeDtypeStruct((N_HEADS, HEAD_PAD, m), BF16), act],
        compiler_params=_cparams(("arbitrary",)),
        name="attn_proj",
    )(xs, mods, w_in, q_norm, w_q, kv_norm, w_kv, cos_t, sin_t)


_NT_DIMS = (((1,), (1,)), ((), ()))


def _na_kernel(q_ref, k_ref, v_ref, kc_ref, vc_ref, bm_ref, o_ref, *, n_rows):
    b = pl.program_id(1)
    kr0 = jnp.clip(b * NA_QROWS - NA_KH // 2, 0, n_rows - NA_KROWS)
    start = pl.multiple_of(kr0 * GRID_W, GRID_W)
    nk = NA_KROWS * GRID_W
    q = q_ref[...]
    kw = k_ref[pl.ds(start, nk), :]
    vw = v_ref[pl.ds(start, nk), :]
    s_loc = lax.dot_general(q, kw, _NT_DIMS, preferred_element_type=F32) + bm_ref[...]
    s_ctx = lax.dot_general(q, kc_ref[...], _NT_DIMS, preferred_element_type=F32)
    m = jnp.maximum(jnp.max(s_loc, axis=-1, keepdims=True), jnp.max(s_ctx, axis=-1, keepdims=True))
    p_loc = jnp.exp(s_loc - m)
    p_ctx = jnp.exp(s_ctx - m)
    l = jnp.sum(p_loc, axis=-1, keepdims=True) + jnp.sum(p_ctx, axis=-1, keepdims=True)
    o = (jnp.dot(p_loc.astype(BF16), vw, preferred_element_type=F32)
         + jnp.dot(p_ctx.astype(BF16), vc_ref[...], preferred_element_type=F32))
    o_ref[...] = (o / l).astype(BF16)


def _na_attention(qa, ka, va, bias_mask, s):
    n_rows = s // GRID_W
    nb = n_rows // NA_QROWS
    tq = NA_QROWS * GRID_W
    ctx_blk = s // tq
    pat = lambda b: jnp.where(b == 0, 0, jnp.where(b == nb - 1, 2, 1))
    return pl.pallas_call(
        functools.partial(_na_kernel, n_rows=n_rows),
        grid=(N_HEADS, nb),
        in_specs=[
            pl.BlockSpec((tq, HEAD_PAD), lambda h, b: (b, h)),
            pl.BlockSpec((s, HEAD_PAD), lambda h, b: (0, h)),
            pl.BlockSpec((s, HEAD_PAD), lambda h, b: (0, h)),
            pl.BlockSpec((tq, HEAD_PAD), lambda h, b: (ctx_blk, h)),
            pl.BlockSpec((tq, HEAD_PAD), lambda h, b: (ctx_blk, h)),
            pl.BlockSpec((None, None, tq, NA_KROWS * GRID_W), lambda h, b: (pat(b), h, 0, 0)),
        ],
        out_specs=pl.BlockSpec((tq, HEAD_PAD), lambda h, b: (b, h)),
        out_shape=jax.ShapeDtypeStruct((s, HP), BF16),
        compiler_params=_cparams(("arbitrary", "arbitrary")),
        name="na_attention",
    )(qa, ka, va, ka, va, bias_mask)


def _ctx_attn_kernel(q_ref, k_ref, v_ref, o_ref, *, k_transposed, base2):
    q = q_ref[...]
    if k_transposed:
        s = jnp.dot(q, k_ref[...], preferred_element_type=F32)
    else:
        s = lax.dot_general(q, k_ref[...], _NT_DIMS, preferred_element_type=F32)
    m = jnp.max(s, axis=-1, keepdims=True)
    p = jnp.exp2(s - m) if base2 else jnp.exp(s - m)
    l = jnp.sum(p, axis=-1, keepdims=True)
    o = jnp.dot(p.astype(BF16), v_ref[...], preferred_element_type=F32)
    o_ref[...] = (o / l).astype(BF16)


def _ctx_attention(q, k, v, s, k_transposed, base2, name):
    blk = s // TM
    if k_transposed:
        k_spec = pl.BlockSpec((None, HEAD_PAD, TM), lambda h: (h, 0, blk))
    else:
        k_spec = pl.BlockSpec((TM, HEAD_PAD), lambda h: (blk, h))
    return pl.pallas_call(
        functools.partial(_ctx_attn_kernel, k_transposed=k_transposed, base2=base2),
        grid=(N_HEADS,),
        in_specs=[
            pl.BlockSpec((TM, HEAD_PAD), lambda h: (blk, h)),
            k_spec,
            pl.BlockSpec((TM, HEAD_PAD), lambda h: (blk, h)),
        ],
        out_specs=pl.BlockSpec((TM, HEAD_PAD), lambda h: (0, h)),
        out_shape=jax.ShapeDtypeStruct((TM, HP), BF16),
        compiler_params=_cparams(("arbitrary",)),
        name=name,
    )(q, k, v)


def _mla_kernel(q_ref, kt_ref, v_ref, o_ref, *, n_chunks):
    q = q_ref[...]
    tq = q.shape[0]

    def body(c, carry):
        m, l, acc = carry
        off = pl.multiple_of(c * MLA_TK, MLA_TK)
        s = jnp.dot(q, kt_ref[:, pl.ds(off, MLA_TK)], preferred_element_type=F32)
        m_new = jnp.maximum(m, jnp.max(s, axis=-1, keepdims=True))
        alpha = jnp.exp2(m - m_new)
        p = jnp.exp2(s - m_new)
        l = alpha * l + jnp.sum(p, axis=-1, keepdims=True)
        acc = alpha * acc + jnp.dot(p.astype(BF16), v_ref[pl.ds(off, MLA_TK), :],
                                    preferred_element_type=F32)
        return m_new, l, acc

    init = (jnp.full((tq, 1), NEG_BIG, F32), jnp.zeros((tq, 1), F32), jnp.zeros((tq, HEAD_PAD), F32))
    _, l, acc = lax.fori_loop(0, n_chunks, body, init)
    o_ref[...] = (acc / l).astype(BF16)


def _mla_attention(qm, kmt, vm, s):
    m = qm.shape[0]
    return pl.pallas_call(
        functools.partial(_mla_kernel, n_chunks=m // MLA_TK),
        grid=(N_HEADS, s // MLA_TQ),
        in_specs=[
            pl.BlockSpec((MLA_TQ, HEAD_PAD), lambda h, i: (i, h)),
            pl.BlockSpec((None, HEAD_PAD, m), lambda h, i: (h, 0, 0)),
            pl.BlockSpec((m, HEAD_PAD), lambda h, i: (0, h)),
        ],
        out_specs=pl.BlockSpec((MLA_TQ, HEAD_PAD), lambda h, i: (i, h)),
        out_shape=jax.ShapeDtypeStruct((s, HP), BF16),
        compiler_params=_cparams(("arbitrary", "arbitrary")),
        name="mla_attention",
    )(qm, kmt, vm)


def _out_proj_kernel(x_ref, mod_ref, oal_ref, oac_ref, obl_ref, obc_ref, wa_ref, wb_ref, g_ref, b_ref,
                     o_ref, *, n_lat_tiles):
    is_ctx = pl.program_id(0) >= n_lat_tiles
    oa = jnp.where(is_ctx, oac_ref[...], oal_ref[...])
    ob = jnp.where(is_ctx, obc_ref[...], obl_ref[...])
    y = (jnp.dot(oa, wa_ref[...], preferred_element_type=F32)
         + jnp.dot(ob, wb_ref[...], preferred_element_type=F32))
    v = DEEPNORM_ALPHA * x_ref[...] + mod_ref[2:3, :] * y
    o_ref[...] = _layer_norm(v, g_ref[...], b_ref[...])


def _out_proj(xs, mods, layer, n_lat_tiles, oa_lat, oa_ctx, ob_lat, ob_ctx, wa, wb, ln_g, ln_b):
    m = xs.shape[0]
    tok = lambda i: (i, 0)
    lat = lambda i: (jnp.minimum(i, n_lat_tiles - 1), 0)
    full = lambda i: (0, 0)
    return pl.pallas_call(
        functools.partial(_out_proj_kernel, n_lat_tiles=n_lat_tiles),
        grid=(m // TM,),
        in_specs=[
            pl.BlockSpec((TM, D_MODEL), tok),
            pl.BlockSpec((None, None, 6, D_MODEL), lambda i: (layer, i // n_lat_tiles, 0, 0)),
            pl.BlockSpec((TM, HP), lat),
            pl.BlockSpec((TM, HP), full),
            pl.BlockSpec((TM, HP), lat),
            pl.BlockSpec((TM, HP), full),
            pl.BlockSpec(wa.shape, full),
            pl.BlockSpec(wb.shape, full),
            pl.BlockSpec((1, D_MODEL), full),
            pl.BlockSpec((1, D_MODEL), full),
        ],
        out_specs=pl.BlockSpec((TM, D_MODEL), tok),
        out_shape=jax.ShapeDtypeStruct((m, D_MODEL), F32),
        compiler_params=_cparams(("arbitrary",)),
        name="out_proj_ln",
    )(xs, mods, oa_lat, oa_ctx, ob_lat, ob_ctx, wa, wb, ln_g, ln_b)


def _ffn_kernel(x_ref, mod_ref, wg_ref, wu_ref, wd_ref, g_ref, b_ref, o_ref):
    x = x_ref[...]
    h = (x * (1.0 + mod_ref[4:5, :]) + mod_ref[3:4, :]).astype(BF16)
    gate = jnp.dot(h, wg_ref[...], preferred_element_type=F32)
    up = jnp.dot(h, wu_ref[...], preferred_element_type=F32)
    a = (_silu(gate) * up).astype(BF16)
    y = jnp.dot(a, wd_ref[...], preferred_element_type=F32)
    v = DEEPNORM_ALPHA * x + mod_ref[5:6, :] * y
    o_ref[...] = _layer_norm(v, g_ref[...], b_ref[...])


def _ffn(xs, mods, layer, n_lat_tiles, wg, wu, wd, ln_g, ln_b):
    m = xs.shape[0]
    tok = lambda i: (i, 0)
    full = lambda i: (0, 0)
    return pl.pallas_call(
        _ffn_kernel,
        grid=(m // TM,),
        in_specs=[
            pl.BlockSpec((TM, D_MODEL), tok),
            pl.BlockSpec((None, None, 6, D_MODEL), lambda i: (layer, i // n_lat_tiles, 0, 0)),
            pl.BlockSpec(wg.shape, full),
            pl.BlockSpec(wu.shape, full),
            pl.BlockSpec(wd.shape, full),
            pl.BlockSpec((1, D_MODEL), full),
            pl.BlockSpec((1, D_MODEL), full),
        ],
        out_specs=pl.BlockSpec((TM, D_MODEL), tok),
        out_shape=jax.ShapeDtypeStruct((m, D_MODEL), F32),
        compiler_params=_cparams(("arbitrary",)),
        name="ffn_ln",
    )(xs, mods, wg, wu, wd, ln_g, ln_b)


def _pool_kernel(x_ref, xp_ref, xn_ref, mod_ref, w_ref, sc_ref, g_ref, b_ref, o_ref, ext_ref,
                 *, n_lat_tiles, s):
    i = pl.program_id(0)
    is_ctx = i >= n_lat_tiles
    j = jnp.where(is_ctx, i - n_lat_tiles, i)
    last = jnp.where(is_ctx, 0, n_lat_tiles - 1)
    n_seq = jnp.where(is_ctx, TM, s)
    scale1 = 1.0 + mod_ref[1:2, :]
    shift1 = mod_ref[0:1, :]
    x = x_ref[...]
    h = x * scale1 + shift1
    hp = jnp.where(j != 0, xp_ref[...] * scale1 + shift1, 0.0)
    hn = jnp.where(j != last, xn_ref[...] * scale1 + shift1, 0.0)
    ext_ref[0:POOL_HALO, :] = hp
    ext_ref[POOL_HALO:POOL_HALO + TM, :] = h
    ext_ref[POOL_HALO + TM:POOL_HALO + TM + POOL_HALO, :] = hn
    t = j * TM + lax.broadcasted_iota(jnp.int32, (TM, 1), 0)
    ys = []
    for g, win in enumerate(POOL_WINDOWS):
        half = win // 2
        lo = g * POOL_GROUP
        acc = ext_ref[POOL_HALO - half:POOL_HALO - half + TM, lo:lo + POOL_GROUP]
        for dlt in range(-half + 1, half):
            acc = acc + ext_ref[POOL_HALO + dlt:POOL_HALO + dlt + TM, lo:lo + POOL_GROUP]
        cnt = (jnp.minimum(t + half, n_seq) - jnp.maximum(t - half, 0)).astype(F32)
        mixed = acc / cnt - h[:, lo:lo + POOL_GROUP]
        yg = jnp.dot(mixed.astype(BF16), w_ref[g], preferred_element_type=F32)
        ys.append(yg)
    y = jnp.concatenate(ys, axis=-1) * sc_ref[...]
    v = DEEPNORM_ALPHA * x + mod_ref[2:3, :] * y
    o_ref[...] = _layer_norm(v, g_ref[...], b_ref[...])


def _pool(xs, mods, layer, n_lat_tiles, s, pool_w, pool_scale, ln_g, ln_b):
    m = xs.shape[0]
    per = TM // POOL_HALO
    n_halo_blocks = m // POOL_HALO
    tok = lambda i: (i, 0)
    full = lambda i: (0, 0)
    return pl.pallas_call(
        functools.partial(_pool_kernel, n_lat_tiles=n_lat_tiles, s=s),
        grid=(m // TM,),
        in_specs=[
            pl.BlockSpec((TM, D_MODEL), tok),
            pl.BlockSpec((POOL_HALO, D_MODEL), lambda i: (jnp.maximum(i * per - 1, 0), 0)),
            pl.BlockSpec((POOL_HALO, D_MODEL), lambda i: (jnp.minimum((i + 1) * per, n_halo_blocks - 1), 0)),
            pl.BlockSpec((None, None, 6, D_MODEL), lambda i: (layer, i // n_lat_tiles, 0, 0)),
            pl.BlockSpec(pool_w.shape, lambda i: (0, 0, 0)),
            pl.BlockSpec((1, D_MODEL), full),
            pl.BlockSpec((1, D_MODEL), full),
            pl.BlockSpec((1, D_MODEL), full),
        ],
        out_specs=pl.BlockSpec((TM, D_MODEL), tok),
        out_shape=jax.ShapeDtypeStruct((m, D_MODEL), F32),
        scratch_shapes=[pltpu.VMEM((TM + 2 * POOL_HALO, D_MODEL), F32)],
        compiler_params=_cparams(("arbitrary",)),
        name="pool_ln",
    )(xs, xs, xs, mods, pool_w, pool_scale, ln_g, ln_b)


def _router_kernel(x_ref, mod_ref, rw_ref, h_ref, gates_ref):
    h = x_ref[...] * (1.0 + mod_ref[4:5, :]) + mod_ref[3:4, :]
    h_ref[...] = h.astype(BF16)
    logits = jnp.dot(h, rw_ref[...], preferred_element_type=F32, precision=lax.Precision.HIGHEST)
    lane = lax.broadcasted_iota(jnp.int32, logits.shape, 1)
    logits = jnp.where(lane < N_EXPERTS, logits, -jnp.inf)
    m1 = jnp.max(logits, axis=-1, keepdims=True)
    i1 = jnp.min(jnp.where(logits == m1, lane, HEAD_PAD), axis=-1, keepdims=True)
    rest = jnp.where(lane == i1, -jnp.inf, logits)
    m2 = jnp.max(rest, axis=-1, keepdims=True)
    i2 = jnp.min(jnp.where(rest == m2, lane, HEAD_PAD), axis=-1, keepdims=True)
    e2 = jnp.exp(m2 - m1)
    w1 = 1.0 / (1.0 + e2)
    w2 = e2 / (1.0 + e2)
    gates_ref[...] = jnp.where(lane == i1, w1, jnp.where(lane == i2, w2, 0.0))


def _router(xs, mods, layer, n_lat_tiles, rw):
    m = xs.shape[0]
    tok = lambda i: (i, 0)
    return pl.pallas_call(
        _router_kernel,
        grid=(m // TM,),
        in_specs=[
            pl.BlockSpec((TM, D_MODEL), tok),
            pl.BlockSpec((None, None, 6, D_MODEL), lambda i: (layer, i // n_lat_tiles, 0, 0)),
            pl.BlockSpec(rw.shape, lambda i: (0, 0)),
        ],
        out_specs=[pl.BlockSpec((TM, D_MODEL), tok), pl.BlockSpec((TM, HEAD_PAD), tok)],
        out_shape=[jax.ShapeDtypeStruct((m, D_MODEL), BF16), jax.ShapeDtypeStruct((m, HEAD_PAD), F32)],
        compiler_params=_cparams(("arbitrary",)),
        name="moe_router",
    )(xs, mods, rw)


def _moe_kernel(h_ref, gates_ref, wg_ref, wu_ref, wd_ref, x_ref, mod_ref, g_ref, b_ref, o_ref, *, s):
    i = pl.program_id(0)
    e = pl.program_id(1)
    f = pl.program_id(2)

    @pl.when((e == 0) & (f == 0))
    def _():
        o_ref[...] = jnp.zeros_like(o_ref)

    h = h_ref[...]
    gate = jnp.dot(h, wg_ref[...], preferred_element_type=F32)
    up = jnp.dot(h, wu_ref[...], preferred_element_type=F32)
    a = (_silu(gate) * up).astype(BF16)
    y = jnp.dot(a, wd_ref[...], preferred_element_type=F32)
    gates = gates_ref[...]
    lane = lax.broadcasted_iota(jnp.int32, gates.shape, 1)
    ge = jnp.sum(jnp.where(lane == e, gates, 0.0), axis=-1, keepdims=True)
    o_ref[...] += ge * y

    @pl.when((e == pl.num_programs(1) - 1) & (f == pl.num_programs(2) - 1))
    def _():
        row = i * MOE_TM + lax.broadcasted_iota(jnp.int32, (MOE_TM, 1), 0)
        g2 = jnp.where(row >= s, mod_ref[1, 5:6, :], mod_ref[0, 5:6, :])
        v = DEEPNORM_ALPHA * x_ref[...] + g2 * o_ref[...]
        o_ref[...] = _layer_norm(v, g_ref[...], b_ref[...])


def _moe(h, gates, wg, wu, wd, xs, mods, layer, s, ln_g, ln_b):
    m = xs.shape[0]
    tok = lambda i, e, f: (i, 0)
    full = lambda i, e, f: (0, 0)
    return pl.pallas_call(
        functools.partial(_moe_kernel, s=s),
        grid=(m // MOE_TM, N_EXPERTS, EXPERT_DIM // MOE_TF),
        in_specs=[
            pl.BlockSpec((MOE_TM, D_MODEL), tok),
            pl.BlockSpec((MOE_TM, HEAD_PAD), tok),
            pl.BlockSpec((None, D_MODEL, MOE_TF), lambda i, e, f: (e, 0, f)),
            pl.BlockSpec((None, D_MODEL, MOE_TF), lambda i, e, f: (e, 0, f)),
            pl.BlockSpec((None, MOE_TF, D_MODEL), lambda i, e, f: (e, f, 0)),
            pl.BlockSpec((MOE_TM, D_MODEL), tok),
            pl.BlockSpec((None, 2, 6, D_MODEL), lambda i, e, f: (layer, 0, 0, 0)),
            pl.BlockSpec((1, D_MODEL), full),
            pl.BlockSpec((1, D_MODEL), full),
        ],
        out_specs=pl.BlockSpec((MOE_TM, D_MODEL), tok),
        out_shape=jax.ShapeDtypeStruct((m, D_MODEL), F32),
        compiler_params=_cparams(("arbitrary", "arbitrary", "arbitrary")),
        name="moe_experts_ln",
    )(h, gates, wg, wu, wd, xs, mods, ln_g, ln_b)


_ROPE_SWAP = np.array(list(range(8, 16)) + list(range(0, 8)) + list(range(24, 32)) + list(range(16, 24)))


def _pad_heads(w, width):
    lead = w.shape[:-1]
    w = w.reshape(lead + (N_HEADS, width))
    w = jnp.pad(w, [(0, 0)] * len(lead) + [(0, 0), (0, HEAD_PAD - width)])
    return w.reshape(lead + (HP,))


def _prep_attn_weights(w_in, w_q_up, w_kv_up, w_out):
    na_w = N_HEADS * NA_HEAD_DIM
    qa, ka, va = (w_in[:, k * na_w:(k + 1) * na_w] for k in range(3))
    o = 3 * na_w
    w_qc = w_in[:, o:o + MLA_Q_LORA]
    o += MLA_Q_LORA
    w_kvc = w_in[:, o:o + MLA_KV_LORA]
    o += MLA_KV_LORA
    w_kr = w_in[:, o:o + MLA_ROPE]
    rope_pad = [(0, 0), (MLA_NOPE, HEAD_PAD - MLA_NOPE - MLA_ROPE)]
    w_in_aug = jnp.concatenate(
        [_pad_heads(qa, NA_HEAD_DIM), _pad_heads(ka, NA_HEAD_DIM), _pad_heads(va, NA_HEAD_DIM),
         w_qc, w_kvc, jnp.pad(w_kr, rope_pad), jnp.pad(w_kr[:, _ROPE_SWAP], rope_pad)], axis=1).astype(BF16)

    wq = w_q_up.reshape(MLA_Q_LORA, N_HEADS, MLA_NOPE + MLA_ROPE)
    wq_full = jnp.pad(wq, [(0, 0), (0, 0), (0, HEAD_PAD - MLA_NOPE - MLA_ROPE)])
    wq_swap = jnp.pad(wq[:, :, MLA_NOPE:][:, :, _ROPE_SWAP], [(0, 0), (0, 0), rope_pad[1]])
    w_q = jnp.concatenate([wq_full.reshape(MLA_Q_LORA, HP), wq_swap.reshape(MLA_Q_LORA, HP)], axis=1).astype(BF16)

    wkv = w_kv_up.reshape(MLA_KV_LORA, N_HEADS, MLA_NOPE + MLA_V)
    wkn = jnp.pad(wkv[:, :, :MLA_NOPE], [(0, 0), (0, 0), (0, HEAD_PAD - MLA_NOPE)])
    wv = jnp.pad(wkv[:, :, MLA_NOPE:], [(0, 0), (0, 0), (0, HEAD_PAD - MLA_V)])
    w_kv = jnp.concatenate([wkn.reshape(MLA_KV_LORA, HP), wv.reshape(MLA_KV_LORA, HP)], axis=1).astype(BF16)

    def pad_rows(w, width):
        w = w.reshape(N_HEADS, width, D_MODEL)
        return jnp.pad(w, [(0, 0), (0, HEAD_PAD - width), (0, 0)]).reshape(HP, D_MODEL).astype(BF16)

    wa = pad_rows(w_out[:na_w], NA_HEAD_DIM)
    wb = pad_rows(w_out[na_w:], MLA_V)
    return w_in_aug, w_q, w_kv, wa, wb


def _rope_tables(s, n_ctx):
    t = jnp.arange(s, dtype=jnp.int32)
    row = (t // GRID_W).astype(F32)
    col = (t % GRID_W).astype(F32)
    n_freq = MLA_ROPE // 4
    inv = 1.0 / (ROPE_THETA ** (jnp.arange(n_freq, dtype=F32) / n_freq))
    ar = row[:, None] * inv
    ac = col[:, None] * inv
    cos = jnp.concatenate([jnp.cos(ar), jnp.cos(ar), jnp.cos(ac), jnp.cos(ac)], axis=1)
    sin = jnp.concatenate([-jnp.sin(ar), jnp.sin(ar), -jnp.sin(ac), jnp.sin(ac)], axis=1)
    right = HEAD_PAD - MLA_NOPE - MLA_ROPE
    cos = jnp.pad(cos, [(0, n_ctx), (MLA_NOPE, right)], constant_values=1.0)
    sin = jnp.pad(sin, [(0, n_ctx), (MLA_NOPE, right)])
    return cos, sin


def _na_bias_mask(rel_bias, n_rows):
    nb = n_rows // NA_QROWS
    c = np.arange(GRID_W)
    cs = np.clip(c - NA_KW // 2, 0, GRID_W - NA_KW)
    kc = np.arange(GRID_W)
    dcol = kc[None, :] - c[:, None] + NA_KW - 1
    ok_c = (kc[None, :] >= cs[:, None]) & (kc[None, :] < cs[:, None] + NA_KW)
    out = []
    for b in (0, 1, nb - 1):
        kr0 = int(np.clip(b * NA_QROWS - NA_KH // 2, 0, n_rows - NA_KROWS))
        r = b * NA_QROWS + np.arange(NA_QROWS)
        rs = np.clip(r - NA_KH // 2, 0, n_rows - NA_KH)
        kr = kr0 + np.arange(NA_KROWS)
        drow = kr[None, :] - r[:, None] + NA_KH - 1
        ok_r = (kr[None, :] >= rs[:, None]) & (kr[None, :] < rs[:, None] + NA_KH)
        di = np.clip(drow, 0, 2 * NA_KH - 2)[:, None, :, None]
        dj = np.clip(dcol, 0, 2 * NA_KW - 2)[None, :, None, :]
        ok = ok_r[:, None, :, None] & ok_c[None, :, None, :]
        vals = rel_bias[:, di, dj]
        vals = jnp.where(ok[None], vals, NEG_BIG)
        out.append(vals.reshape(N_HEADS, NA_QROWS * GRID_W, NA_KROWS * GRID_W))
    return jnp.stack(out, axis=0)


def kernel(x, c, ctx, c_ctx, mod_w, mod_b, ln1_g, ln1_b, ln2_g, ln2_b, attn_w_in, na_rel_bias, mla_q_norm,
           mla_w_q_up, mla_kv_norm, mla_w_kv_up, attn_w_out, ffn_w_gate, ffn_w_up, ffn_w_down, pool_w,
           pool_scale, moe_router, moe_w_gate, moe_w_up, moe_w_down):
    assert x.shape[0] == 1 and c.shape[0] == 1 and ctx.shape[0] == 1
    s = x.shape[1]
    n_ctx = ctx.shape[1]
    assert n_ctx == TM and s % (NA_QROWS * GRID_W) == 0 and s % MLA_TQ == 0
    assert (s + n_ctx) % MLA_TK == 0 and (s + n_ctx) % MOE_TM == 0
    n_lat_tiles = s // TM
    depth = mod_w.shape[0]

    xs = jnp.concatenate([x[0], ctx[0]], axis=0)
    ct = jnp.stack([c[0], c_ctx], axis=1)
    mods = _modulation(ct, mod_w, mod_b).reshape(depth, 2, 6, D_MODEL)
    cos_t, sin_t = _rope_tables(s, n_ctx)
    row = lambda v: v.reshape(1, -1)

    for i in range(depth):
        j = i // 2
        if i % 2 == 0:
            w_in, w_q, w_kv, wa, wb = _prep_attn_weights(attn_w_in[j], mla_w_q_up[j], mla_w_kv_up[j],
                                                         attn_w_out[j])
            qa, ka, va, qm, kmt, vm = _attn_proj(xs, mods, i, n_lat_tiles, w_in, row(mla_q_norm[j]), w_q,
                                                 row(mla_kv_norm[j]), w_kv, cos_t, sin_t)
            bias_mask = _na_bias_mask(na_rel_bias[j], s // GRID_W)
            oa_lat = _na_attention(qa, ka, va, bias_mask, s)
            oa_ctx = _ctx_attention(qa, ka, va, s, False, False, "na_ctx_attention")
            ob_lat = _mla_attention(qm, kmt, vm, s)
            ob_ctx = _ctx_attention(qm, kmt, vm, s, True, True, "mla_ctx_attention")
            xs = _out_proj(xs, mods, i, n_lat_tiles, oa_lat, oa_ctx, ob_lat, ob_ctx, wa, wb,
                           row(ln1_g[i]), row(ln1_b[i]))
            xs = _ffn(xs, mods, i, n_lat_tiles, ffn_w_gate[j].astype(BF16), ffn_w_up[j].astype(BF16),
                      ffn_w_down[j].astype(BF16), row(ln2_g[i]), row(ln2_b[i]))
        else:
            xs = _pool(xs, mods, i, n_lat_tiles, s, pool_w[j].astype(BF16), row(pool_scale[j]),
                       row(ln1_g[i]), row(ln1_b[i]))
            rw = jnp.pad(moe_router[j], [(0, 0), (0, HEAD_PAD - N_EXPERTS)])
            h, gates = _router(xs, mods, i, n_lat_tiles, rw)
            xs = _moe(h, gates, moe_w_gate[j].astype(BF16), moe_w_up[j].astype(BF16),
                      moe_w_down[j].astype(BF16), xs, mods, i, s, row(ln2_g[i]), row(ln2_b[i]))
    return xs[:s][None]
```

```python
import functools

import numpy as np
import jax
import jax.numpy as jnp
from jax import lax
from jax.experimental import pallas as pl
from jax.experimental.pallas import tpu as pltpu

F32 = jnp.float32
BF16 = jnp.bfloat16

D_MODEL = 1024
GRID_W = 64
N_HEADS = 8
HEAD_PAD = 128
HP = N_HEADS * HEAD_PAD
NA_HEAD_DIM = 64
NA_KH = 8
NA_KW = 16
MLA_NOPE = 64
MLA_ROPE = 32
MLA_V = 64
MLA_Q_LORA = 256
MLA_KV_LORA = 128
ROPE_THETA = 10000.0
POOL_WINDOWS = (2, 4, 8, 16)
POOL_GROUP = D_MODEL // len(POOL_WINDOWS)
POOL_HALO = 8
FFN_DIM = 2816
N_EXPERTS = 8
EXPERT_DIM = 3584
DEPTH = 4
DEEPNORM_ALPHA = (2 * DEPTH) ** 0.25
LN_EPS = 1e-5
RMS_EPS = 1e-6
LOG2E = 1.4426950408889634
NEG_BIG = -1e30

TM = 256
NA_QROWS = 4
NA_KROWS = NA_QROWS + NA_KH - 1
MLA_TQ = 512
MLA_TK = 640
MLA_UNROLL = 8
MOE_TM = 640
MOE_TF = 512
VMEM_LIMIT = 56 * 1024 * 1024


def _cparams(sem):
    return pltpu.CompilerParams(dimension_semantics=sem, vmem_limit_bytes=VMEM_LIMIT)


def _layer_norm(v, g, b):
    mu = jnp.mean(v, axis=-1, keepdims=True)
    d = v - mu
    var = jnp.mean(d * d, axis=-1, keepdims=True)
    return d * lax.rsqrt(var + LN_EPS) * g + b


def _silu(v):
    return v * jax.nn.sigmoid(v)


def _mod_kernel(ct_ref, w_ref, b_ref, o_ref):
    s = _silu(ct_ref[...])
    w = w_ref[...]
    b = b_ref[...]
    r0 = jnp.sum(w * s[:, 0:1], axis=0, keepdims=True) + b
    r1 = jnp.sum(w * s[:, 1:2], axis=0, keepdims=True) + b
    o_ref[...] = jnp.concatenate([r0, r1], axis=0)


def _modulation(ct, mod_w, mod_b):
    depth, d, n6 = mod_w.shape
    tn = 1536
    return pl.pallas_call(
        _mod_kernel,
        grid=(depth, n6 // tn),
        in_specs=[
            pl.BlockSpec((d, 2), lambda l, j: (0, 0)),
            pl.BlockSpec((None, d, tn), lambda l, j: (l, 0, j)),
            pl.BlockSpec((None, 1, tn), lambda l, j: (l, 0, j)),
        ],
        out_specs=pl.BlockSpec((None, 2, tn), lambda l, j: (l, 0, j)),
        out_shape=jax.ShapeDtypeStruct((depth, 2, n6), F32),
        compiler_params=_cparams(("arbitrary", "arbitrary")),
        name="modulation",
    )(ct, mod_w, mod_b.reshape(depth, 1, n6))


def _attn_proj_kernel(x_ref, mod_ref, win_ref, qn_ref, wq_ref, kvn_ref, wkv_ref, cos_ref, sin_ref,
                      qa_ref, ka_ref, va_ref, qm_ref, kmt_ref, vm_ref):
    h = x_ref[...] * (1.0 + mod_ref[1:2, :]) + mod_ref[0:1, :]
    p = jnp.dot(h.astype(BF16), win_ref[...], preferred_element_type=F32)
    qa_ref[...] = (p[:, 0:HP] * (NA_HEAD_DIM ** -0.5)).astype(BF16)
    ka_ref[...] = p[:, HP:2 * HP].astype(BF16)
    va_ref[...] = p[:, 2 * HP:3 * HP].astype(BF16)
    o = 3 * HP
    q_c = p[:, o:o + MLA_Q_LORA]
    o += MLA_Q_LORA
    kv_c = p[:, o:o + MLA_KV_LORA]
    o += MLA_KV_LORA
    krp = p[:, o:o + HEAD_PAD]
    krs = p[:, o + HEAD_PAD:o + 2 * HEAD_PAD]
    qn = q_c * lax.rsqrt(jnp.mean(q_c * q_c, axis=-1, keepdims=True) + RMS_EPS) * qn_ref[...]
    kvn = kv_c * lax.rsqrt(jnp.mean(kv_c * kv_c, axis=-1, keepdims=True) + RMS_EPS) * kvn_ref[...]
    q2 = jnp.dot(qn.astype(BF16), wq_ref[...], preferred_element_type=F32)
    kv2 = jnp.dot(kvn.astype(BF16), wkv_ref[...], preferred_element_type=F32)
    cos = cos_ref[...]
    sin = sin_ref[...]
    kr = krp * cos + krs * sin
    q_scale = (MLA_NOPE + MLA_ROPE) ** -0.5 * LOG2E
    for hd in range(N_HEADS):
        lo = hd * HEAD_PAD
        qh = (q2[:, lo:lo + HEAD_PAD] * cos + q2[:, HP + lo:HP + lo + HEAD_PAD] * sin) * q_scale
        qm_ref[:, lo:lo + HEAD_PAD] = qh.astype(BF16)
        kh = kv2[:, lo:lo + HEAD_PAD] + kr
        kmt_ref[hd] = kh.T.astype(BF16)
    vm_ref[...] = kv2[:, HP:2 * HP].astype(BF16)


def _attn_proj(xs, mods, layer, n_lat_tiles, w_in, q_norm, w_q, kv_norm, w_kv, cos_t, sin_t):
    m = xs.shape[0]
    nt = m // TM
    tok = lambda i: (i, 0)
    full = lambda i: (0, 0)
    act = jax.ShapeDtypeStruct((m, HP), BF16)
    return pl.pallas_call(
        _attn_proj_kernel,
        grid=(nt,),
        in_specs=[
            pl.BlockSpec((TM, D_MODEL), tok),
            pl.BlockSpec((None, None, 6, D_MODEL), lambda i: (layer, i // n_lat_tiles, 0, 0)),
            pl.BlockSpec(w_in.shape, full),
            pl.BlockSpec(q_norm.shape, full),
            pl.BlockSpec(w_q.shape, full),
            pl.BlockSpec(kv_norm.shape, full),
            pl.BlockSpec(w_kv.shape, full),
            pl.BlockSpec((TM, HEAD_PAD), tok),
            pl.BlockSpec((TM, HEAD_PAD), tok),
        ],
        out_specs=[
            pl.BlockSpec((TM, HP), tok),
            pl.BlockSpec((TM, HP), tok),
            pl.BlockSpec((TM, HP), tok),
            pl.BlockSpec((TM, HP), tok),
            pl.BlockSpec((N_HEADS, HEAD_PAD, TM), lambda i: (0, 0, i)),
            pl.BlockSpec((TM, HP), tok),
        ],
        out_shape=[act, act, act, act, jax.ShapeDtypeStruct((N_HEADS, HEAD_PAD, m), BF16), act],
        compiler_params=_cparams(("arbitrary",)),
        name="attn_proj",
    )(xs, mods, w_in, q_norm, w_q, kv_norm, w_kv, cos_t, sin_t)


_NT_DIMS = (((1,), (1,)), ((), ()))


def _na_kernel(q_ref, k_ref, v_ref, kc_ref, vc_ref, bm_ref, o_ref, *, n_rows):
    b = pl.program_id(1)
    kr0 = jnp.clip(b * NA_QROWS - NA_KH // 2, 0, n_rows - NA_KROWS)
    start = pl.multiple_of(kr0 * GRID_W, GRID_W)
    nk = NA_KROWS * GRID_W
    q = q_ref[...]
    kw = k_ref[pl.ds(start, nk), :]
    vw = v_ref[pl.ds(start, nk), :]
    s_loc = lax.dot_general(q, kw, _NT_DIMS, preferred_element_type=F32) + bm_ref[...]
    s_ctx = lax.dot_general(q, kc_ref[...], _NT_DIMS, preferred_element_type=F32)
    m = jnp.maximum(jnp.max(s_loc, axis=-1, keepdims=True), jnp.max(s_ctx, axis=-1, keepdims=True))
    p_loc = jnp.exp(s_loc - m)
    p_ctx = jnp.exp(s_ctx - m)
    l = jnp.sum(p_loc, axis=-1, keepdims=True) + jnp.sum(p_ctx, axis=-1, keepdims=True)
    o = (jnp.dot(p_loc.astype(BF16), vw, preferred_element_type=F32)
         + jnp.dot(p_ctx.astype(BF16), vc_ref[...], preferred_element_type=F32))
    o_ref[...] = (o / l).astype(BF16)


def _na_attention(qa, ka, va, bias_mask, s):
    n_rows = s // GRID_W
    nb = n_rows // NA_QROWS
    tq = NA_QROWS * GRID_W
    ctx_blk = s // tq
    pat = lambda b: jnp.where(b == 0, 0, jnp.where(b == nb - 1, 2, 1))
    return pl.pallas_call(
        functools.partial(_na_kernel, n_rows=n_rows),
        grid=(N_HEADS, nb),
        in_specs=[
            pl.BlockSpec((tq, HEAD_PAD), lambda h, b: (b, h)),
            pl.BlockSpec((s, HEAD_PAD), lambda h, b: (0, h)),
            pl.BlockSpec((s, HEAD_PAD), lambda h, b: (0, h)),
            pl.BlockSpec((tq, HEAD_PAD), lambda h, b: (ctx_blk, h)),
            pl.BlockSpec((tq, HEAD_PAD), lambda h, b: (ctx_blk, h)),
            pl.BlockSpec((None, None, tq, NA_KROWS * GRID_W), lambda h, b: (pat(b), h, 0, 0)),
        ],
        out_specs=pl.BlockSpec((tq, HEAD_PAD), lambda h, b: (b, h)),
        out_shape=jax.ShapeDtypeStruct((s, HP), BF16),
        compiler_params=_cparams(("arbitrary", "arbitrary")),
        name="na_attention",
    )(qa, ka, va, ka, va, bias_mask)


def _ctx_attn_kernel(q_ref, k_ref, v_ref, o_ref, *, k_transposed, base2):
    q = q_ref[...]
    if k_transposed:
        s = jnp.dot(q, k_ref[...], preferred_element_type=F32)
    else:
        s = lax.dot_general(q, k_ref[...], _NT_DIMS, preferred_element_type=F32)
    m = jnp.max(s, axis=-1, keepdims=True)
    p = jnp.exp2(s - m) if base2 else jnp.exp(s - m)
    l = jnp.sum(p, axis=-1, keepdims=True)
    o = jnp.dot(p.astype(BF16), v_ref[...], preferred_element_type=F32)
    o_ref[...] = (o / l).astype(BF16)


def _ctx_attention(q, k, v, s, k_transposed, base2, name):
    blk = s // TM
    if k_transposed:
        k_spec = pl.BlockSpec((None, HEAD_PAD, TM), lambda h: (h, 0, blk))
    else:
        k_spec = pl.BlockSpec((TM, HEAD_PAD), lambda h: (blk, h))
    return pl.pallas_call(
        functools.partial(_ctx_attn_kernel, k_transposed=k_transposed, base2=base2),
        grid=(N_HEADS,),
        in_specs=[
            pl.BlockSpec((TM, HEAD_PAD), lambda h: (blk, h)),
            k_spec,
            pl.BlockSpec((TM, HEAD_PAD), lambda h: (blk, h)),
        ],
        out_specs=pl.BlockSpec((TM, HEAD_PAD), lambda h: (0, h)),
        out_shape=jax.ShapeDtypeStruct((TM, HP), BF16),
        compiler_params=_cparams(("arbitrary",)),
        name=name,
    )(q, k, v)


def _mla_kernel(q_ref, kt_ref, v_ref, o_ref, s0_ref, s1_ref, *, n_chunks):
    q = q_ref[...]
    tq = q.shape[0]

    def scores(c, s_ref):
        off = pl.multiple_of(c * MLA_TK, MLA_TK)
        s_ref[...] = jnp.dot(q, kt_ref[:, pl.ds(off, MLA_TK)], preferred_element_type=F32)

    def update(c, s_ref, carry):
        m, l, acc = carry
        off = pl.multiple_of(c * MLA_TK, MLA_TK)
        s = s_ref[...]
        m_new = jnp.maximum(m, jnp.max(s, axis=-1, keepdims=True))
        alpha = jnp.exp2(m - m_new)
        p = jnp.exp2(s - m_new)
        l = alpha * l + jnp.sum(p, axis=-1, keepdims=True)
        acc = alpha * acc + jnp.dot(p.astype(BF16), v_ref[pl.ds(off, MLA_TK), :],
                                    preferred_element_type=F32)
        return m_new, l, acc

    bufs = (s0_ref, s1_ref)

    def group(g, carry):
        c0 = g * MLA_UNROLL
        for u in range(MLA_UNROLL):
            scores(c0 + u + 1, bufs[(u + 1) % 2])
            carry = update(c0 + u, bufs[u % 2], carry)
        return carry

    carry = (jnp.full((tq, 1), NEG_BIG, F32), jnp.zeros((tq, 1), F32), jnp.zeros((tq, HEAD_PAD), F32))
    scores(0, s0_ref)
    n_groups = (n_chunks - 2) // MLA_UNROLL
    carry = lax.fori_loop(0, n_groups, group, carry)
    for c in range(n_groups * MLA_UNROLL, n_chunks):
        if c + 1 < n_chunks:
            scores(c + 1, bufs[(c + 1) % 2])
        carry = update(c, bufs[c % 2], carry)
    _, l, acc = carry
    o_ref[...] = (acc / l).astype(BF16)


def _mla_attention(qm, kmt, vm, s):
    m = qm.shape[0]
    n_chunks = m // MLA_TK
    assert n_chunks % 2 == 0 and n_chunks >= 2
    return pl.pallas_call(
        functools.partial(_mla_kernel, n_chunks=n_chunks),
        grid=(N_HEADS, s // MLA_TQ),
        in_specs=[
            pl.BlockSpec((MLA_TQ, HEAD_PAD), lambda h, i: (i, h)),
            pl.BlockSpec((None, HEAD_PAD, m), lambda h, i: (h, 0, 0)),
            pl.BlockSpec((m, HEAD_PAD), lambda h, i: (0, h)),
        ],
        out_specs=pl.BlockSpec((MLA_TQ, HEAD_PAD), lambda h, i: (i, h)),
        out_shape=jax.ShapeDtypeStruct((s, HP), BF16),
        scratch_shapes=[pltpu.VMEM((MLA_TQ, MLA_TK), F32), pltpu.VMEM((MLA_TQ, MLA_TK), F32)],
        compiler_params=_cparams(("arbitrary", "arbitrary")),
        name="mla_attention",
    )(qm, kmt, vm)


def _out_proj_kernel(x_ref, mod_ref, oal_ref, oac_ref, obl_ref, obc_ref, wa_ref, wb_ref, g_ref, b_ref,
                     o_ref, *, n_lat_tiles):
    is_ctx = pl.program_id(0) >= n_lat_tiles
    oa = jnp.where(is_ctx, oac_ref[...], oal_ref[...])
    ob = jnp.where(is_ctx, obc_ref[...], obl_ref[...])
    y = (jnp.dot(oa, wa_ref[...], preferred_element_type=F32)
         + jnp.dot(ob, wb_ref[...], preferred_element_type=F32))
    v = DEEPNORM_ALPHA * x_ref[...] + mod_ref[2:3, :] * y
    o_ref[...] = _layer_norm(v, g_ref[...], b_ref[...])


def _out_proj(xs, mods, layer, n_lat_tiles, oa_lat, oa_ctx, ob_lat, ob_ctx, wa, wb, ln_g, ln_b):
    m = xs.shape[0]
    tok = lambda i: (i, 0)
    lat = lambda i: (jnp.minimum(i, n_lat_tiles - 1), 0)
    full = lambda i: (0, 0)
    return pl.pallas_call(
        functools.partial(_out_proj_kernel, n_lat_tiles=n_lat_tiles),
        grid=(m // TM,),
        in_specs=[
            pl.BlockSpec((TM, D_MODEL), tok),
            pl.BlockSpec((None, None, 6, D_MODEL), lambda i: (layer, i // n_lat_tiles, 0, 0)),
            pl.BlockSpec((TM, HP), lat),
            pl.BlockSpec((TM, HP), full),
            pl.BlockSpec((TM, HP), lat),
            pl.BlockSpec((TM, HP), full),
            pl.BlockSpec(wa.shape, full),
            pl.BlockSpec(wb.shape, full),
            pl.BlockSpec((1, D_MODEL), full),
            pl.BlockSpec((1, D_MODEL), full),
        ],
        out_specs=pl.BlockSpec((TM, D_MODEL), tok),
        out_shape=jax.ShapeDtypeStruct((m, D_MODEL), F32),
        compiler_params=_cparams(("arbitrary",)),
        name="out_proj_ln",
    )(xs, mods, oa_lat, oa_ctx, ob_lat, ob_ctx, wa, wb, ln_g, ln_b)


def _ffn_kernel(x_ref, mod_ref, wg_ref, wu_ref, wd_ref, g_ref, b_ref, o_ref):
    x = x_ref[...]
    h = (x * (1.0 + mod_ref[4:5, :]) + mod_ref[3:4, :]).astype(BF16)
    gate = jnp.dot(h, wg_ref[...], preferred_element_type=F32)
    up = jnp.dot(h, wu_ref[...], preferred_element_type=F32)
    a = (_silu(gate) * up).astype(BF16)
    y = jnp.dot(a, wd_ref[...], preferred_element_type=F32)
    v = DEEPNORM_ALPHA * x + mod_ref[5:6, :] * y
    o_ref[...] = _layer_norm(v, g_ref[...], b_ref[...])


def _ffn(xs, mods, layer, n_lat_tiles, wg, wu, wd, ln_g, ln_b):
    m = xs.shape[0]
    tok = lambda i: (i, 0)
    full = lambda i: (0, 0)
    return pl.pallas_call(
        _ffn_kernel,
        grid=(m // TM,),
        in_specs=[
            pl.BlockSpec((TM, D_MODEL), tok),
            pl.BlockSpec((None, None, 6, D_MODEL), lambda i: (layer, i // n_lat_tiles, 0, 0)),
            pl.BlockSpec(wg.shape, full),
            pl.BlockSpec(wu.shape, full),
            pl.BlockSpec(wd.shape, full),
            pl.BlockSpec((1, D_MODEL), full),
            pl.BlockSpec((1, D_MODEL), full),
        ],
        out_specs=pl.BlockSpec((TM, D_MODEL), tok),
        out_shape=jax.ShapeDtypeStruct((m, D_MODEL), F32),
        compiler_params=_cparams(("arbitrary",)),
        name="ffn_ln",
    )(xs, mods, wg, wu, wd, ln_g, ln_b)


def _pool_kernel(x_ref, xp_ref, xn_ref, mod_ref, w_ref, sc_ref, g_ref, b_ref, o_ref, ext_ref,
                 *, n_lat_tiles, s):
    i = pl.program_id(0)
    is_ctx = i >= n_lat_tiles
    j = jnp.where(is_ctx, i - n_lat_tiles, i)
    last = jnp.where(is_ctx, 0, n_lat_tiles - 1)
    n_seq = jnp.where(is_ctx, TM, s)
    scale1 = 1.0 + mod_ref[1:2, :]
    shift1 = mod_ref[0:1, :]
    x = x_ref[...]
    h = x * scale1 + shift1
    hp = jnp.where(j != 0, xp_ref[...] * scale1 + shift1, 0.0)
    hn = jnp.where(j != last, xn_ref[...] * scale1 + shift1, 0.0)
    ext_ref[0:POOL_HALO, :] = hp
    ext_ref[POOL_HALO:POOL_HALO + TM, :] = h
    ext_ref[POOL_HALO + TM:POOL_HALO + TM + POOL_HALO, :] = hn
    t = j * TM + lax.broadcasted_iota(jnp.int32, (TM, 1), 0)
    ys = []
    for g, win in enumerate(POOL_WINDOWS):
        half = win // 2
        lo = g * POOL_GROUP
        acc = ext_ref[POOL_HALO - half:POOL_HALO - half + TM, lo:lo + POOL_GROUP]
        for dlt in range(-half + 1, half):
            acc = acc + ext_ref[POOL_HALO + dlt:POOL_HALO + dlt + TM, lo:lo + POOL_GROUP]
        cnt = (jnp.minimum(t + half, n_seq) - jnp.maximum(t - half, 0)).astype(F32)
        mixed = acc / cnt - h[:, lo:lo + POOL_GROUP]
        yg = jnp.dot(mixed.astype(BF16), w_ref[g], preferred_element_type=F32)
        ys.append(yg)
    y = jnp.concatenate(ys, axis=-1) * sc_ref[...]
    v = DEEPNORM_ALPHA * x + mod_ref[2:3, :] * y
    o_ref[...] = _layer_norm(v, g_ref[...], b_ref[...])


def _pool(xs, mods, layer, n_lat_tiles, s, pool_w, pool_scale, ln_g, ln_b):
    m = xs.shape[0]
    per = TM // POOL_HALO
    n_halo_blocks = m // POOL_HALO
    tok = lambda i: (i, 0)
    full = lambda i: (0, 0)
    return pl.pallas_call(
        functools.partial(_pool_kernel, n_lat_tiles=n_lat_tiles, s=s),
        grid=(m // TM,),
        in_specs=[
            pl.BlockSpec((TM, D_MODEL), tok),
            pl.BlockSpec((POOL_HALO, D_MODEL), lambda i: (jnp.maximum(i * per - 1, 0), 0)),
            pl.BlockSpec((POOL_HALO, D_MODEL), lambda i: (jnp.minimum((i + 1) * per, n_halo_blocks - 1), 0)),
            pl.BlockSpec((None, None, 6, D_MODEL), lambda i: (layer, i // n_lat_tiles, 0, 0)),
            pl.BlockSpec(pool_w.shape, lambda i: (0, 0, 0)),
            pl.BlockSpec((1, D_MODEL), full),
            pl.BlockSpec((1, D_MODEL), full),
            pl.BlockSpec((1, D_MODEL), full),
        ],
        out_specs=pl.BlockSpec((TM, D_MODEL), tok),
        out_shape=jax.ShapeDtypeStruct((m, D_MODEL), F32),
        scratch_shapes=[pltpu.VMEM((TM + 2 * POOL_HALO, D_MODEL), F32)],
        compiler_params=_cparams(("arbitrary",)),
        name="pool_ln",
    )(xs, xs, xs, mods, pool_w, pool_scale, ln_g, ln_b)


def _router_kernel(x_ref, mod_ref, rw_ref, h_ref, gates_ref):
    h = x_ref[...] * (1.0 + mod_ref[4:5, :]) + mod_ref[3:4, :]
    h_ref[...] = h.astype(BF16)
    logits = jnp.dot(h, rw_ref[...], preferred_element_type=F32, precision=lax.Precision.HIGHEST)
    lane = lax.broadcasted_iota(jnp.int32, logits.shape, 1)
    logits = jnp.where(lane < N_EXPERTS, logits, -jnp.inf)
    m1 = jnp.max(logits, axis=-1, keepdims=True)
    i1 = jnp.min(jnp.where(logits == m1, lane, HEAD_PAD), axis=-1, keepdims=True)
    rest = jnp.where(lane == i1, -jnp.inf, logits)
    m2 = jnp.max(rest, axis=-1, keepdims=True)
    i2 = jnp.min(jnp.where(rest == m2, lane, HEAD_PAD), axis=-1, keepdims=True)
    e2 = jnp.exp(m2 - m1)
    w1 = 1.0 / (1.0 + e2)
    w2 = e2 / (1.0 + e2)
    gates_ref[...] = jnp.where(lane == i1, w1, jnp.where(lane == i2, w2, 0.0))


def _router(xs, mods, layer, n_lat_tiles, rw):
    m = xs.shape[0]
    tok = lambda i: (i, 0)
    return pl.pallas_call(
        _router_kernel,
        grid=(m // TM,),
        in_specs=[
            pl.BlockSpec((TM, D_MODEL), tok),
            pl.BlockSpec((None, None, 6, D_MODEL), lambda i: (layer, i // n_lat_tiles, 0, 0)),
            pl.BlockSpec(rw.shape, lambda i: (0, 0)),
        ],
        out_specs=[pl.BlockSpec((TM, D_MODEL), tok), pl.BlockSpec((TM, HEAD_PAD), tok)],
        out_shape=[jax.ShapeDtypeStruct((m, D_MODEL), BF16), jax.ShapeDtypeStruct((m, HEAD_PAD), F32)],
        compiler_params=_cparams(("arbitrary",)),
        name="moe_router",
    )(xs, mods, rw)


def _moe_kernel(h_ref, gates_ref, wg_ref, wu_ref, wd_ref, x_ref, mod_ref, g_ref, b_ref, o_ref, *, s):
    i = pl.program_id(0)
    e = pl.program_id(1)
    f = pl.program_id(2)

    @pl.when((e == 0) & (f == 0))
    def _():
        o_ref[...] = jnp.zeros_like(o_ref)

    h = h_ref[...]
    gate = jnp.dot(h, wg_ref[...], preferred_element_type=F32)
    up = jnp.dot(h, wu_ref[...], preferred_element_type=F32)
    a = (_silu(gate) * up).astype(BF16)
    y = jnp.dot(a, wd_ref[...], preferred_element_type=F32)
    gates = gates_ref[...]
    lane = lax.broadcasted_iota(jnp.int32, gates.shape, 1)
    ge = jnp.sum(jnp.where(lane == e, gates, 0.0), axis=-1, keepdims=True)
    o_ref[...] += ge * y

    @pl.when((e == pl.num_programs(1) - 1) & (f == pl.num_programs(2) - 1))
    def _():
        row = i * MOE_TM + lax.broadcasted_iota(jnp.int32, (MOE_TM, 1), 0)
        g2 = jnp.where(row >= s, mod_ref[1, 5:6, :], mod_ref[0, 5:6, :])
        v = DEEPNORM_ALPHA * x_ref[...] + g2 * o_ref[...]
        o_ref[...] = _layer_norm(v, g_ref[...], b_ref[...])


def _moe(h, gates, wg, wu, wd, xs, mods, layer, s, ln_g, ln_b):
    m = xs.shape[0]
    tok = lambda i, e, f: (i, 0)
    full = lambda i, e, f: (0, 0)
    return pl.pallas_call(
        functools.partial(_moe_kernel, s=s),
        grid=(m // MOE_TM, N_EXPERTS, EXPERT_DIM // MOE_TF),
        in_specs=[
            pl.BlockSpec((MOE_TM, D_MODEL), tok),
            pl.BlockSpec((MOE_TM, HEAD_PAD), tok),
            pl.BlockSpec((None, D_MODEL, MOE_TF), lambda i, e, f: (e, 0, f)),
            pl.BlockSpec((None, D_MODEL, MOE_TF), lambda i, e, f: (e, 0, f)),
            pl.BlockSpec((None, MOE_TF, D_MODEL), lambda i, e, f: (e, f, 0)),
            pl.BlockSpec((MOE_TM, D_MODEL), tok),
            pl.BlockSpec((None, 2, 6, D_MODEL), lambda i, e, f: (layer, 0, 0, 0)),
            pl.BlockSpec((1, D_MODEL), full),
            pl.BlockSpec((1, D_MODEL), full),
        ],
        out_specs=pl.BlockSpec((MOE_TM, D_MODEL), tok),
        out_shape=jax.ShapeDtypeStruct((m, D_MODEL), F32),
        compiler_params=_cparams(("arbitrary", "arbitrary", "arbitrary")),
        name="moe_experts_ln",
    )(h, gates, wg, wu, wd, xs, mods, ln_g, ln_b)


_ROPE_SWAP = np.array(list(range(8, 16)) + list(range(0, 8)) + list(range(24, 32)) + list(range(16, 24)))


def _pad_heads(w, width):
    lead = w.shape[:-1]
    w = w.reshape(lead + (N_HEADS, width))
    w = jnp.pad(w, [(0, 0)] * len(lead) + [(0, 0), (0, HEAD_PAD - width)])
    return w.reshape(lead + (HP,))


def _prep_attn_weights(w_in, w_q_up, w_kv_up, w_out):
    na_w = N_HEADS * NA_HEAD_DIM
    qa, ka, va = (w_in[:, k * na_w:(k + 1) * na_w] for k in range(3))
    o = 3 * na_w
    w_qc = w_in[:, o:o + MLA_Q_LORA]
    o += MLA_Q_LORA
    w_kvc = w_in[:, o:o + MLA_KV_LORA]
    o += MLA_KV_LORA
    w_kr = w_in[:, o:o + MLA_ROPE]
    rope_pad = [(0, 0), (MLA_NOPE, HEAD_PAD - MLA_NOPE - MLA_ROPE)]
    w_in_aug = jnp.concatenate(
        [_pad_heads(qa, NA_HEAD_DIM), _pad_heads(ka, NA_HEAD_DIM), _pad_heads(va, NA_HEAD_DIM),
         w_qc, w_kvc, jnp.pad(w_kr, rope_pad), jnp.pad(w_kr[:, _ROPE_SWAP], rope_pad)], axis=1).astype(BF16)

    wq = w_q_up.reshape(MLA_Q_LORA, N_HEADS, MLA_NOPE + MLA_ROPE)
    wq_full = jnp.pad(wq, [(0, 0), (0, 0), (0, HEAD_PAD - MLA_NOPE - MLA_ROPE)])
    wq_swap = jnp.pad(wq[:, :, MLA_NOPE:][:, :, _ROPE_SWAP], [(0, 0), (0, 0), rope_pad[1]])
    w_q = jnp.concatenate([wq_full.reshape(MLA_Q_LORA, HP), wq_swap.reshape(MLA_Q_LORA, HP)], axis=1).astype(BF16)

    wkv = w_kv_up.reshape(MLA_KV_LORA, N_HEADS, MLA_NOPE + MLA_V)
    wkn = jnp.pad(wkv[:, :, :MLA_NOPE], [(0, 0), (0, 0), (0, HEAD_PAD - MLA_NOPE)])
    wv = jnp.pad(wkv[:, :, MLA_NOPE:], [(0, 0), (0, 0), (0, HEAD_PAD - MLA_V)])
    w_kv = jnp.concatenate([wkn.reshape(MLA_KV_LORA, HP), wv.reshape(MLA_KV_LORA, HP)], axis=1).astype(BF16)

    def pad_rows(w, width):
        w = w.reshape(N_HEADS, width, D_MODEL)
        return jnp.pad(w, [(0, 0), (0, HEAD_PAD - width), (0, 0)]).reshape(HP, D_MODEL).astype(BF16)

    wa = pad_rows(w_out[:na_w], NA_HEAD_DIM)
    wb = pad_rows(w_out[na_w:], MLA_V)
    return w_in_aug, w_q, w_kv, wa, wb


def _rope_tables(s, n_ctx):
    t = jnp.arange(s, dtype=jnp.int32)
    row = (t // GRID_W).astype(F32)
    col = (t % GRID_W).astype(F32)
    n_freq = MLA_ROPE // 4
    inv = 1.0 / (ROPE_THETA ** (jnp.arange(n_freq, dtype=F32) / n_freq))
    ar = row[:, None] * inv
    ac = col[:, None] * inv
    cos = jnp.concatenate([jnp.cos(ar), jnp.cos(ar), jnp.cos(ac), jnp.cos(ac)], axis=1)
    sin = jnp.concatenate([-jnp.sin(ar), jnp.sin(ar), -jnp.sin(ac), jnp.sin(ac)], axis=1)
    right = HEAD_PAD - MLA_NOPE - MLA_ROPE
    cos = jnp.pad(cos, [(0, n_ctx), (MLA_NOPE, right)], constant_values=1.0)
    sin = jnp.pad(sin, [(0, n_ctx), (MLA_NOPE, right)])
    return cos, sin


def _na_bias_mask(rel_bias, n_rows):
    nb = n_rows // NA_QROWS
    c = np.arange(GRID_W)
    cs = np.clip(c - NA_KW // 2, 0, GRID_W - NA_KW)
    kc = np.arange(GRID_W)
    ok_c = (kc[None, :] >= cs[:, None]) & (kc[None, :] < cs[:, None] + NA_KW)
    n_dcol = 2 * NA_KW - 1
    left = GRID_W - NA_KW
    period = 2 * GRID_W
    u = jnp.pad(rel_bias, [(0, 0), (0, 0), (left, period - n_dcol - left)])
    flat = jnp.tile(u, (1, 1, GRID_W + 1))[:, :, :GRID_W * (period - 1)]
    toep = flat.reshape(N_HEADS, 2 * NA_KH - 1, GRID_W, period - 1)[:, :, :, GRID_W - 1:]
    toep = jnp.where(ok_c, toep, NEG_BIG)
    masked = jnp.full((N_HEADS, GRID_W, GRID_W), NEG_BIG, F32)
    out = []
    for b in (0, 1, nb - 1):
        kr0 = int(np.clip(b * NA_QROWS - NA_KH // 2, 0, n_rows - NA_KROWS))
        q_blocks = []
        for qr in range(NA_QROWS):
            r = b * NA_QROWS + qr
            rs = int(np.clip(r - NA_KH // 2, 0, n_rows - NA_KH))
            k_blocks = []
            for klr in range(NA_KROWS):
                kr = kr0 + klr
                k_blocks.append(toep[:, kr - r + NA_KH - 1] if rs <= kr < rs + NA_KH else masked)
            q_blocks.append(jnp.concatenate(k_blocks, axis=-1))
        out.append(jnp.concatenate(q_blocks, axis=1))
    return jnp.stack(out, axis=0)


def kernel(x, c, ctx, c_ctx, mod_w, mod_b, ln1_g, ln1_b, ln2_g, ln2_b, attn_w_in, na_rel_bias, mla_q_norm,
           mla_w_q_up, mla_kv_norm, mla_w_kv_up, attn_w_out, ffn_w_gate, ffn_w_up, ffn_w_down, pool_w,
           pool_scale, moe_router, moe_w_gate, moe_w_up, moe_w_down):
    assert x.shape[0] == 1 and c.shape[0] == 1 and ctx.shape[0] == 1
    s = x.shape[1]
    n_ctx = ctx.shape[1]
    assert n_ctx == TM and s % (NA_QROWS * GRID_W) == 0 and s % MLA_TQ == 0
    assert (s + n_ctx) % MLA_TK == 0 and (s + n_ctx) % MOE_TM == 0
    n_lat_tiles = s // TM
    depth = mod_w.shape[0]

    xs = jnp.concatenate([x[0], ctx[0]], axis=0)
    ct = jnp.stack([c[0], c_ctx], axis=1)
    mods = _modulation(ct, mod_w, mod_b).reshape(depth, 2, 6, D_MODEL)
    cos_t, sin_t = _rope_tables(s, n_ctx)
    row = lambda v: v.reshape(1, -1)

    for i in range(depth):
        j = i // 2
        if i % 2 == 0:
            w_in, w_q, w_kv, wa, wb = _prep_attn_weights(attn_w_in[j], mla_w_q_up[j], mla_w_kv_up[j],
                                                         attn_w_out[j])
            qa, ka, va, qm, kmt, vm = _attn_proj(xs, mods, i, n_lat_tiles, w_in, row(mla_q_norm[j]), w_q,
                                                 row(mla_kv_norm[j]), w_kv, cos_t, sin_t)
            bias_mask = _na_bias_mask(na_rel_bias[j], s // GRID_W)
            oa_lat = _na_attention(qa, ka, va, bias_mask, s)
            oa_ctx = _ctx_attention(qa, ka, va, s, False, False, "na_ctx_attention")
            ob_lat = _mla_attention(qm, kmt, vm, s)
            ob_ctx = _ctx_attention(qm, kmt, vm, s, True, True, "mla_ctx_attention")
            xs = _out_proj(xs, mods, i, n_lat_tiles, oa_lat, oa_ctx, ob_lat, ob_ctx, wa, wb,
                           row(ln1_g[i]), row(ln1_b[i]))
            xs = _ffn(xs, mods, i, n_lat_tiles, ffn_w_gate[j].astype(BF16), ffn_w_up[j].astype(BF16),
                      ffn_w_down[j].astype(BF16), row(ln2_g[i]), row(ln2_b[i]))
        else:
            xs = _pool(xs, mods, i, n_lat_tiles, s, pool_w[j].astype(BF16), row(pool_scale[j]),
                       row(ln1_g[i]), row(ln1_b[i]))
            rw = jnp.pad(moe_router[j], [(0, 0), (0, HEAD_PAD - N_EXPERTS)])
            h, gates = _router(xs, mods, i, n_lat_tiles, rw)
            xs = _moe(h, gates, moe_w_gate[j].astype(BF16), moe_w_up[j].astype(BF16),
                      moe_w_down[j].astype(BF16), xs, mods, i, s, row(ln2_g[i]), row(ln2_b[i]))
    return xs[:s][None]
```

```python
import functools

import numpy as np
import jax
import jax.numpy as jnp
from jax import lax
from jax.experimental import pallas as pl
from jax.experimental.pallas import tpu as pltpu

F32 = jnp.float32
BF16 = jnp.bfloat16

D_MODEL = 1024
GRID_W = 64
N_HEADS = 8
HEAD_PAD = 128
HP = N_HEADS * HEAD_PAD
NA_HEAD_DIM = 64
NA_KH = 8
NA_KW = 16
MLA_NOPE = 64
MLA_ROPE = 32
MLA_V = 64
MLA_Q_LORA = 256
MLA_KV_LORA = 128
ROPE_THETA = 10000.0
POOL_WINDOWS = (2, 4, 8, 16)
POOL_GROUP = D_MODEL // len(POOL_WINDOWS)
POOL_HALO = 8
FFN_DIM = 2816
N_EXPERTS = 8
EXPERT_DIM = 3584
DEPTH = 4
DEEPNORM_ALPHA = (2 * DEPTH) ** 0.25
LN_EPS = 1e-5
RMS_EPS = 1e-6
LOG2E = 1.4426950408889634
NEG_BIG = -1e30

TM = 256
NA_QROWS = 4
NA_KROWS = NA_QROWS + NA_KH - 1
NA_QB = 4
MLA_TQ = 512
MLA_TK = 640
MLA_UNROLL = 8
MOE_TM = 512
MOE_TF = 1792
VMEM_LIMIT = 56 * 1024 * 1024


def _cparams(sem):
    return pltpu.CompilerParams(dimension_semantics=sem, vmem_limit_bytes=VMEM_LIMIT)


def _layer_norm(v, g, b):
    mu = jnp.mean(v, axis=-1, keepdims=True)
    d = v - mu
    var = jnp.mean(d * d, axis=-1, keepdims=True)
    return d * lax.rsqrt(var + LN_EPS) * g + b


def _silu(v):
    return v * jax.nn.sigmoid(v)


def _mod_kernel(ct_ref, w_ref, b_ref, o_ref):
    s = _silu(ct_ref[...])
    w = w_ref[...]
    b = b_ref[...]
    r0 = jnp.sum(w * s[:, 0:1], axis=0, keepdims=True) + b
    r1 = jnp.sum(w * s[:, 1:2], axis=0, keepdims=True) + b
    o_ref[...] = jnp.concatenate([r0, r1], axis=0)


def _modulation(ct, mod_w, mod_b):
    depth, d, n6 = mod_w.shape
    tn = 1536
    return pl.pallas_call(
        _mod_kernel,
        grid=(depth, n6 // tn),
        in_specs=[
            pl.BlockSpec((d, 2), lambda l, j: (0, 0)),
            pl.BlockSpec((None, d, tn), lambda l, j: (l, 0, j)),
            pl.BlockSpec((None, 1, tn), lambda l, j: (l, 0, j)),
        ],
        out_specs=pl.BlockSpec((None, 2, tn), lambda l, j: (l, 0, j)),
        out_shape=jax.ShapeDtypeStruct((depth, 2, n6), F32),
        compiler_params=_cparams(("arbitrary", "arbitrary")),
        name="modulation",
    )(ct, mod_w, mod_b.reshape(depth, 1, n6))


def _attn_proj_kernel(x_ref, mod_ref, win_ref, qn_ref, wq_ref, kvn_ref, wkv_ref, cos_ref, sin_ref,
                      qa_ref, ka_ref, va_ref, qm_ref, kmt_ref, vm_ref):
    h = x_ref[...] * (1.0 + mod_ref[1:2, :]) + mod_ref[0:1, :]
    p = jnp.dot(h.astype(BF16), win_ref[...], preferred_element_type=F32)
    qa_ref[...] = (p[:, 0:HP] * (NA_HEAD_DIM ** -0.5)).astype(BF16)
    ka_ref[...] = p[:, HP:2 * HP].astype(BF16)
    va_ref[...] = p[:, 2 * HP:3 * HP].astype(BF16)
    o = 3 * HP
    q_c = p[:, o:o + MLA_Q_LORA]
    o += MLA_Q_LORA
    kv_c = p[:, o:o + MLA_KV_LORA]
    o += MLA_KV_LORA
    krp = p[:, o:o + HEAD_PAD]
    krs = p[:, o + HEAD_PAD:o + 2 * HEAD_PAD]
    qn = q_c * lax.rsqrt(jnp.mean(q_c * q_c, axis=-1, keepdims=True) + RMS_EPS) * qn_ref[...]
    kvn = kv_c * lax.rsqrt(jnp.mean(kv_c * kv_c, axis=-1, keepdims=True) + RMS_EPS) * kvn_ref[...]
    q2 = jnp.dot(qn.astype(BF16), wq_ref[...], preferred_element_type=F32)
    kv2 = jnp.dot(kvn.astype(BF16), wkv_ref[...], preferred_element_type=F32)
    cos = cos_ref[...]
    sin = sin_ref[...]
    kr = krp * cos + krs * sin
    q_scale = (MLA_NOPE + MLA_ROPE) ** -0.5 * LOG2E
    for hd in range(N_HEADS):
        lo = hd * HEAD_PAD
        qh = (q2[:, lo:lo + HEAD_PAD] * cos + q2[:, HP + lo:HP + lo + HEAD_PAD] * sin) * q_scale
        qm_ref[:, lo:lo + HEAD_PAD] = qh.astype(BF16)
        kh = kv2[:, lo:lo + HEAD_PAD] + kr
        kmt_ref[hd] = kh.T.astype(BF16)
    vm_ref[...] = kv2[:, HP:2 * HP].astype(BF16)


def _attn_proj(xs, mods, layer, n_lat_tiles, w_in, q_norm, w_q, kv_norm, w_kv, cos_t, sin_t):
    m = xs.shape[0]
    nt = m // TM
    tok = lambda i: (i, 0)
    full = lambda i: (0, 0)
    act = jax.ShapeDtypeStruct((m, HP), BF16)
    return pl.pallas_call(
        _attn_proj_kernel,
        grid=(nt,),
        in_specs=[
            pl.BlockSpec((TM, D_MODEL), tok),
            pl.BlockSpec((None, None, 6, D_MODEL), lambda i: (layer, i // n_lat_tiles, 0, 0)),
            pl.BlockSpec(w_in.shape, full),
            pl.BlockSpec(q_norm.shape, full),
            pl.BlockSpec(w_q.shape, full),
            pl.BlockSpec(kv_norm.shape, full),
            pl.BlockSpec(w_kv.shape, full),
            pl.BlockSpec((TM, HEAD_PAD), tok),
            pl.BlockSpec((TM, HEAD_PAD), tok),
        ],
        out_specs=[
            pl.BlockSpec((TM, HP), tok),
            pl.BlockSpec((TM, HP), tok),
            pl.BlockSpec((TM, HP), tok),
            pl.BlockSpec((TM, HP), tok),
            pl.BlockSpec((N_HEADS, HEAD_PAD, TM), lambda i: (0, 0, i)),
            pl.BlockSpec((TM, HP), tok),
        ],
        out_shape=[act, act, act, act, jax.ShapeDtypeStruct((N_HEADS, HEAD_PAD, m), BF16), act],
        compiler_params=_cparams(("arbitrary",)),
        name="attn_proj",
    )(xs, mods, w_in, q_norm, w_q, kv_norm, w_kv, cos_t, sin_t)


_NT_DIMS = (((1,), (1,)), ((), ()))


def _na_kernel(q_ref, k_ref, v_ref, kc_ref, vc_ref, *rest, n_rows):
    bm_refs, o_ref = rest[:NA_QB], rest[NA_QB]
    tq = NA_QROWS * GRID_W
    nk = NA_KROWS * GRID_W
    kc = kc_ref[...]
    vc = vc_ref[...]
    for u in range(NA_QB):
        b = pl.program_id(1) * NA_QB + u
        kr0 = jnp.clip(b * NA_QROWS - NA_KH // 2, 0, n_rows - NA_KROWS)
        start = pl.multiple_of(kr0 * GRID_W, GRID_W)
        q = q_ref[u * tq:(u + 1) * tq, :]
        kw = k_ref[pl.ds(start, nk), :]
        vw = v_ref[pl.ds(start, nk), :]
        s_loc = lax.dot_general(q, kw, _NT_DIMS, preferred_element_type=F32) + bm_refs[u][...]
        s_ctx = lax.dot_general(q, kc, _NT_DIMS, preferred_element_type=F32)
        m = jnp.maximum(jnp.max(s_loc, axis=-1, keepdims=True), jnp.max(s_ctx, axis=-1, keepdims=True))
        p_loc = jnp.exp(s_loc - m)
        p_ctx = jnp.exp(s_ctx - m)
        l = jnp.sum(p_loc, axis=-1, keepdims=True) + jnp.sum(p_ctx, axis=-1, keepdims=True)
        o = (jnp.dot(p_loc.astype(BF16), vw, preferred_element_type=F32)
             + jnp.dot(p_ctx.astype(BF16), vc, preferred_element_type=F32))
        o_ref[u * tq:(u + 1) * tq, :] = (o / l).astype(BF16)


def _na_attention(qa, ka, va, bias_mask, s):
    n_rows = s // GRID_W
    nb = n_rows // NA_QROWS
    tq = NA_QROWS * GRID_W
    assert nb % NA_QB == 0
    ctx_blk = s // tq
    pat = lambda b: jnp.where(b == 0, 0, jnp.where(b == nb - 1, 2, 1))
    bm_spec = lambda u: pl.BlockSpec((None, None, tq, NA_KROWS * GRID_W),
                                     lambda h, j: (pat(j * NA_QB + u), h, 0, 0))
    return pl.pallas_call(
        functools.partial(_na_kernel, n_rows=n_rows),
        grid=(N_HEADS, nb // NA_QB),
        in_specs=[
            pl.BlockSpec((NA_QB * tq, HEAD_PAD), lambda h, j: (j, h)),
            pl.BlockSpec((s, HEAD_PAD), lambda h, j: (0, h)),
            pl.BlockSpec((s, HEAD_PAD), lambda h, j: (0, h)),
            pl.BlockSpec((tq, HEAD_PAD), lambda h, j: (ctx_blk, h)),
            pl.BlockSpec((tq, HEAD_PAD), lambda h, j: (ctx_blk, h)),
        ] + [bm_spec(u) for u in range(NA_QB)],
        out_specs=pl.BlockSpec((NA_QB * tq, HEAD_PAD), lambda h, j: (j, h)),
        out_shape=jax.ShapeDtypeStruct((s, HP), BF16),
        compiler_params=_cparams(("arbitrary", "arbitrary")),
        name="na_attention",
    )(qa, ka, va, ka, va, *([bias_mask] * NA_QB))


def _ctx_attn_kernel(q_ref, k_ref, v_ref, o_ref, *, k_transposed, base2):
    q = q_ref[...]
    if k_transposed:
        s = jnp.dot(q, k_ref[...], preferred_element_type=F32)
    else:
        s = lax.dot_general(q, k_ref[...], _NT_DIMS, preferred_element_type=F32)
    m = jnp.max(s, axis=-1, keepdims=True)
    p = jnp.exp2(s - m) if base2 else jnp.exp(s - m)
    l = jnp.sum(p, axis=-1, keepdims=True)
    o = jnp.dot(p.astype(BF16), v_ref[...], preferred_element_type=F32)
    o_ref[...] = (o / l).astype(BF16)


def _ctx_attention(q, k, v, s, k_transposed, base2, name):
    blk = s // TM
    if k_transposed:
        k_spec = pl.BlockSpec((None, HEAD_PAD, TM), lambda h: (h, 0, blk))
    else:
        k_spec = pl.BlockSpec((TM, HEAD_PAD), lambda h: (blk, h))
    return pl.pallas_call(
        functools.partial(_ctx_attn_kernel, k_transposed=k_transposed, base2=base2),
        grid=(N_HEADS,),
        in_specs=[
            pl.BlockSpec((TM, HEAD_PAD), lambda h: (blk, h)),
            k_spec,
            pl.BlockSpec((TM, HEAD_PAD), lambda h: (blk, h)),
        ],
        out_specs=pl.BlockSpec((TM, HEAD_PAD), lambda h: (0, h)),
        out_shape=jax.ShapeDtypeStruct((TM, HP), BF16),
        compiler_params=_cparams(("arbitrary",)),
        name=name,
    )(q, k, v)


def _mla_kernel(q_ref, kt_ref, v_ref, o_ref, s0_ref, s1_ref, *, n_chunks):
    q = q_ref[...]
    tq = q.shape[0]

    def scores(c, s_ref):
        off = pl.multiple_of(c * MLA_TK, MLA_TK)
        s_ref[...] = jnp.dot(q, kt_ref[:, pl.ds(off, MLA_TK)], preferred_element_type=F32)

    def update(c, s_ref, carry):
        m, l, acc = carry
        off = pl.multiple_of(c * MLA_TK, MLA_TK)
        s = s_ref[...]
        m_new = jnp.maximum(m, jnp.max(s, axis=-1, keepdims=True))
        alpha = jnp.exp2(m - m_new)
        p = jnp.exp2(s - m_new)
        l = alpha * l + jnp.sum(p, axis=-1, keepdims=True)
        acc = alpha * acc + jnp.dot(p.astype(BF16), v_ref[pl.ds(off, MLA_TK), :],
                                    preferred_element_type=F32)
        return m_new, l, acc

    bufs = (s0_ref, s1_ref)

    def group(g, carry):
        c0 = g * MLA_UNROLL
        for u in range(MLA_UNROLL):
            scores(c0 + u + 1, bufs[(u + 1) % 2])
            carry = update(c0 + u, bufs[u % 2], carry)
        return carry

    carry = (jnp.full((tq, 1), NEG_BIG, F32), jnp.zeros((tq, 1), F32), jnp.zeros((tq, HEAD_PAD), F32))
    scores(0, s0_ref)
    n_groups = (n_chunks - 2) // MLA_UNROLL
    carry = lax.fori_loop(0, n_groups, group, carry)
    for c in range(n_groups * MLA_UNROLL, n_chunks):
        if c + 1 < n_chunks:
            scores(c + 1, bufs[(c + 1) % 2])
        carry = update(c, bufs[c % 2], carry)
    _, l, acc = carry
    o_ref[...] = (acc / l).astype(BF16)


def _mla_attention(qm, kmt, vm, s):
    m = qm.shape[0]
    n_chunks = m // MLA_TK
    assert n_chunks % 2 == 0 and n_chunks >= 2
    return pl.pallas_call(
        functools.partial(_mla_kernel, n_chunks=n_chunks),
        grid=(N_HEADS, s // MLA_TQ),
        in_specs=[
            pl.BlockSpec((MLA_TQ, HEAD_PAD), lambda h, i: (i, h)),
            pl.BlockSpec((None, HEAD_PAD, m), lambda h, i: (h, 0, 0)),
            pl.BlockSpec((m, HEAD_PAD), lambda h, i: (0, h)),
        ],
        out_specs=pl.BlockSpec((MLA_TQ, HEAD_PAD), lambda h, i: (i, h)),
        out_shape=jax.ShapeDtypeStruct((s, HP), BF16),
        scratch_shapes=[pltpu.VMEM((MLA_TQ, MLA_TK), F32), pltpu.VMEM((MLA_TQ, MLA_TK), F32)],
        compiler_params=_cparams(("arbitrary", "arbitrary")),
        name="mla_attention",
    )(qm, kmt, vm)


def _out_proj_kernel(x_ref, mod_ref, oal_ref, oac_ref, obl_ref, obc_ref, wa_ref, wb_ref, g_ref, b_ref,
                     o_ref, *, n_lat_tiles):
    is_ctx = pl.program_id(0) >= n_lat_tiles
    oa = jnp.where(is_ctx, oac_ref[...], oal_ref[...])
    ob = jnp.where(is_ctx, obc_ref[...], obl_ref[...])
    y = (jnp.dot(oa, wa_ref[...], preferred_element_type=F32)
         + jnp.dot(ob, wb_ref[...], preferred_element_type=F32))
    v = DEEPNORM_ALPHA * x_ref[...] + mod_ref[2:3, :] * y
    o_ref[...] = _layer_norm(v, g_ref[...], b_ref[...])


def _out_proj(xs, mods, layer, n_lat_tiles, oa_lat, oa_ctx, ob_lat, ob_ctx, wa, wb, ln_g, ln_b):
    m = xs.shape[0]
    tok = lambda i: (i, 0)
    lat = lambda i: (jnp.minimum(i, n_lat_tiles - 1), 0)
    full = lambda i: (0, 0)
    return pl.pallas_call(
        functools.partial(_out_proj_kernel, n_lat_tiles=n_lat_tiles),
        grid=(m // TM,),
        in_specs=[
            pl.BlockSpec((TM, D_MODEL), tok),
            pl.BlockSpec((None, None, 6, D_MODEL), lambda i: (layer, i // n_lat_tiles, 0, 0)),
            pl.BlockSpec((TM, HP), lat),
            pl.BlockSpec((TM, HP), full),
            pl.BlockSpec((TM, HP), lat),
            pl.BlockSpec((TM, HP), full),
            pl.BlockSpec(wa.shape, full),
            pl.BlockSpec(wb.shape, full),
            pl.BlockSpec((1, D_MODEL), full),
            pl.BlockSpec((1, D_MODEL), full),
        ],
        out_specs=pl.BlockSpec((TM, D_MODEL), tok),
        out_shape=jax.ShapeDtypeStruct((m, D_MODEL), F32),
        compiler_params=_cparams(("arbitrary",)),
        name="out_proj_ln",
    )(xs, mods, oa_lat, oa_ctx, ob_lat, ob_ctx, wa, wb, ln_g, ln_b)


def _ffn_kernel(x_ref, mod_ref, wg_ref, wu_ref, wd_ref, g_ref, b_ref, o_ref):
    x = x_ref[...]
    h = (x * (1.0 + mod_ref[4:5, :]) + mod_ref[3:4, :]).astype(BF16)
    gate = jnp.dot(h, wg_ref[...], preferred_element_type=F32)
    up = jnp.dot(h, wu_ref[...], preferred_element_type=F32)
    a = (_silu(gate) * up).astype(BF16)
    y = jnp.dot(a, wd_ref[...], preferred_element_type=F32)
    v = DEEPNORM_ALPHA * x + mod_ref[5:6, :] * y
    o_ref[...] = _layer_norm(v, g_ref[...], b_ref[...])


def _ffn(xs, mods, layer, n_lat_tiles, wg, wu, wd, ln_g, ln_b):
    m = xs.shape[0]
    tok = lambda i: (i, 0)
    full = lambda i: (0, 0)
    return pl.pallas_call(
        _ffn_kernel,
        grid=(m // TM,),
        in_specs=[
            pl.BlockSpec((TM, D_MODEL), tok),
            pl.BlockSpec((None, None, 6, D_MODEL), lambda i: (layer, i // n_lat_tiles, 0, 0)),
            pl.BlockSpec(wg.shape, full),
            pl.BlockSpec(wu.shape, full),
            pl.BlockSpec(wd.shape, full),
            pl.BlockSpec((1, D_MODEL), full),
            pl.BlockSpec((1, D_MODEL), full),
        ],
        out_specs=pl.BlockSpec((TM, D_MODEL), tok),
        out_shape=jax.ShapeDtypeStruct((m, D_MODEL), F32),
        compiler_params=_cparams(("arbitrary",)),
        name="ffn_ln",
    )(xs, mods, wg, wu, wd, ln_g, ln_b)


def _pool_kernel(x_ref, xp_ref, xn_ref, mod_ref, w_ref, sc_ref, g_ref, b_ref, o_ref, ext_ref,
                 *, n_lat_tiles, s):
    i = pl.program_id(0)
    is_ctx = i >= n_lat_tiles
    j = jnp.where(is_ctx, i - n_lat_tiles, i)
    last = jnp.where(is_ctx, 0, n_lat_tiles - 1)
    n_seq = jnp.where(is_ctx, TM, s)
    scale1 = 1.0 + mod_ref[1:2, :]
    shift1 = mod_ref[0:1, :]
    x = x_ref[...]
    h = x * scale1 + shift1
    hp = jnp.where(j != 0, xp_ref[...] * scale1 + shift1, 0.0)
    hn = jnp.where(j != last, xn_ref[...] * scale1 + shift1, 0.0)
    ext_ref[0:POOL_HALO, :] = hp
    ext_ref[POOL_HALO:POOL_HALO + TM, :] = h
    ext_ref[POOL_HALO + TM:POOL_HALO + TM + POOL_HALO, :] = hn
    t = j * TM + lax.broadcasted_iota(jnp.int32, (TM, 1), 0)
    ys = []
    for g, win in enumerate(POOL_WINDOWS):
        half = win // 2
        lo = g * POOL_GROUP
        acc = ext_ref[POOL_HALO - half:POOL_HALO - half + TM, lo:lo + POOL_GROUP]
        for dlt in range(-half + 1, half):
            acc = acc + ext_ref[POOL_HALO + dlt:POOL_HALO + dlt + TM, lo:lo + POOL_GROUP]
        cnt = (jnp.minimum(t + half, n_seq) - jnp.maximum(t - half, 0)).astype(F32)
        mixed = acc / cnt - h[:, lo:lo + POOL_GROUP]
        yg = jnp.dot(mixed.astype(BF16), w_ref[g], preferred_element_type=F32)
        ys.append(yg)
    y = jnp.concatenate(ys, axis=-1) * sc_ref[...]
    v = DEEPNORM_ALPHA * x + mod_ref[2:3, :] * y
    o_ref[...] = _layer_norm(v, g_ref[...], b_ref[...])


def _pool(xs, mods, layer, n_lat_tiles, s, pool_w, pool_scale, ln_g, ln_b):
    m = xs.shape[0]
    per = TM // POOL_HALO
    n_halo_blocks = m // POOL_HALO
    tok = lambda i: (i, 0)
    full = lambda i: (0, 0)
    return pl.pallas_call(
        functools.partial(_pool_kernel, n_lat_tiles=n_lat_tiles, s=s),
        grid=(m // TM,),
        in_specs=[
            pl.BlockSpec((TM, D_MODEL), tok),
            pl.BlockSpec((POOL_HALO, D_MODEL), lambda i: (jnp.maximum(i * per - 1, 0), 0)),
            pl.BlockSpec((POOL_HALO, D_MODEL), lambda i: (jnp.minimum((i + 1) * per, n_halo_blocks - 1), 0)),
            pl.BlockSpec((None, None, 6, D_MODEL), lambda i: (layer, i // n_lat_tiles, 0, 0)),
            pl.BlockSpec(pool_w.shape, lambda i: (0, 0, 0)),
            pl.BlockSpec((1, D_MODEL), full),
            pl.BlockSpec((1, D_MODEL), full),
            pl.BlockSpec((1, D_MODEL), full),
        ],
        out_specs=pl.BlockSpec((TM, D_MODEL), tok),
        out_shape=jax.ShapeDtypeStruct((m, D_MODEL), F32),
        scratch_shapes=[pltpu.VMEM((TM + 2 * POOL_HALO, D_MODEL), F32)],
        compiler_params=_cparams(("arbitrary",)),
        name="pool_ln",
    )(xs, xs, xs, mods, pool_w, pool_scale, ln_g, ln_b)


def _router_kernel(x_ref, mod_ref, rw_ref, tri_ref, idx_ref, wts_ref, cnt_ref, base_ref):
    @pl.when(pl.program_id(0) == 0)
    def _():
        base_ref[...] = jnp.zeros_like(base_ref)

    h = x_ref[...] * (1.0 + mod_ref[4:5, :]) + mod_ref[3:4, :]
    logits = jnp.dot(h, rw_ref[...], preferred_element_type=F32, precision=lax.Precision.HIGHEST)
    lane = lax.broadcasted_iota(jnp.int32, logits.shape, 1)
    logits = jnp.where(lane < N_EXPERTS, logits, -jnp.inf)
    m1 = jnp.max(logits, axis=-1, keepdims=True)
    i1 = jnp.min(jnp.where(logits == m1, lane, HEAD_PAD), axis=-1, keepdims=True)
    rest = jnp.where(lane == i1, -jnp.inf, logits)
    m2 = jnp.max(rest, axis=-1, keepdims=True)
    i2 = jnp.min(jnp.where(rest == m2, lane, HEAD_PAD), axis=-1, keepdims=True)
    e2 = jnp.exp(m2 - m1)
    w1 = 1.0 / (1.0 + e2)
    w2 = e2 / (1.0 + e2)
    wts_ref[...] = jnp.where(lane == 0, w1, jnp.where(lane == 1, w2, 0.0))

    chosen = jnp.where(lane == i1, 1.0, jnp.where(lane == i2, 1.0, 0.0))
    before = jnp.dot(tri_ref[...], chosen.astype(BF16), preferred_element_type=F32)
    rank = before + base_ref[0:1, :]
    r1 = jnp.sum(jnp.where(lane == i1, rank, 0.0), axis=-1, keepdims=True)
    r2 = jnp.sum(jnp.where(lane == i2, rank, 0.0), axis=-1, keepdims=True)
    base_ref[0:1, :] = base_ref[0:1, :] + jnp.sum(chosen, axis=0, keepdims=True)
    cnt_ref[...] = base_ref[...]
    packed = jnp.where(lane == 0, i1.astype(F32),
                       jnp.where(lane == 1, i2.astype(F32),
                                 jnp.where(lane == 2, r1, jnp.where(lane == 3, r2, 0.0))))
    idx_ref[...] = packed.T[0:8, :].astype(jnp.int32)


def _router(xs, mods, layer, n_lat_tiles, rw):
    m = xs.shape[0]
    nt = m // TM
    tok = lambda i: (i, 0)
    tri = jnp.asarray(np.tril(np.ones((TM, TM), np.float32), -1), dtype=BF16)
    return pl.pallas_call(
        _router_kernel,
        grid=(nt,),
        in_specs=[
            pl.BlockSpec((TM, D_MODEL), tok),
            pl.BlockSpec((None, None, 6, D_MODEL), lambda i: (layer, i // n_lat_tiles, 0, 0)),
            pl.BlockSpec(rw.shape, lambda i: (0, 0)),
            pl.BlockSpec((TM, TM), lambda i: (0, 0)),
        ],
        out_specs=[
            pl.BlockSpec((None, 8, TM), lambda i: (i, 0, 0)),
            pl.BlockSpec((TM, HEAD_PAD), tok),
            pl.BlockSpec((8, HEAD_PAD), lambda i: (0, 0)),
        ],
        out_shape=[
            jax.ShapeDtypeStruct((nt, 8, TM), jnp.int32),
            jax.ShapeDtypeStruct((m, HEAD_PAD), F32),
            jax.ShapeDtypeStruct((8, HEAD_PAD), F32),
        ],
        scratch_shapes=[pltpu.VMEM((8, HEAD_PAD), F32)],
        compiler_params=_cparams(("arbitrary",)),
        name="moe_router",
    )(xs, mods, rw, tri)


def _row_slot(offs_ref, idx_ref, k, r):
    return offs_ref[idx_ref[k, r]] + idx_ref[2 + k, r]


def _dispatch_kernel(offs_ref, idx_ref, x_ref, mod_ref, zero_ref, xs_ref, hbuf, sems):
    del zero_ref
    i = pl.program_id(0)
    n = pl.num_programs(0)
    slot = i % 2

    def wait_slot(sl):
        for _ in range(2):
            pltpu.make_async_copy(hbuf.at[sl], xs_ref.at[pl.ds(0, TM)], sems.at[sl]).wait()

    @pl.when(i >= 2)
    def _():
        wait_slot(slot)

    hbuf[slot] = x_ref[...] * (1.0 + mod_ref[4:5, :]) + mod_ref[3:4, :]

    def issue(r, carry):
        for k in range(2):
            dst = _row_slot(offs_ref, idx_ref, k, r)
            pltpu.make_async_copy(hbuf.at[slot, pl.ds(r, 1)], xs_ref.at[pl.ds(dst, 1)], sems.at[slot]).start()
        return carry

    lax.fori_loop(0, TM, issue, 0, unroll=8)

    @pl.when(i == n - 1)
    def _():
        wait_slot(slot)

        @pl.when(n >= 2)
        def _():
            wait_slot(1 - slot)


def _dispatch(offs, idx, xs, mods, layer, n_lat_tiles, n_sorted):
    m = xs.shape[0]
    zeros = jnp.zeros((n_sorted, D_MODEL), F32)
    grid_spec = pltpu.PrefetchScalarGridSpec(
        num_scalar_prefetch=1,
        grid=(m // TM,),
        in_specs=[
            pl.BlockSpec((None, 8, TM), lambda i, offs: (i, 0, 0), memory_space=pltpu.SMEM),
            pl.BlockSpec((TM, D_MODEL), lambda i, offs: (i, 0)),
            pl.BlockSpec((None, None, 6, D_MODEL), lambda i, offs: (layer, i // n_lat_tiles, 0, 0)),
            pl.BlockSpec(memory_space=pl.ANY),
        ],
        out_specs=pl.BlockSpec(memory_space=pl.ANY),
        scratch_shapes=[pltpu.VMEM((2, TM, D_MODEL), F32), pltpu.SemaphoreType.DMA((2,))],
    )
    return pl.pallas_call(
        _dispatch_kernel,
        grid_spec=grid_spec,
        out_shape=jax.ShapeDtypeStruct((n_sorted, D_MODEL), F32),
        input_output_aliases={4: 0},
        compiler_params=_cparams(("arbitrary",)),
        name="moe_dispatch",
    )(offs, idx, xs, mods, zeros)


def _experts_kernel(te_ref, nu_ref, x_ref, wg_ref, wu_ref, wd_ref, o_ref, xb_ref):
    del te_ref
    f = pl.program_id(1)

    @pl.when((pl.program_id(0) >= nu_ref[0]) & (f == 0))
    def _():
        o_ref[...] = jnp.zeros_like(o_ref)

    @pl.when(pl.program_id(0) < nu_ref[0])
    def _():
        @pl.when(f == 0)
        def _():
            xb_ref[...] = x_ref[...].astype(BF16)

        xb = xb_ref[...]
        gate = jnp.dot(xb, wg_ref[...], preferred_element_type=F32)
        up = jnp.dot(xb, wu_ref[...], preferred_element_type=F32)
        a = (_silu(gate) * up).astype(BF16)
        y = jnp.dot(a, wd_ref[...], preferred_element_type=F32)

        @pl.when(f == 0)
        def _():
            o_ref[...] = y

        @pl.when(f > 0)
        def _():
            o_ref[...] += y


def _experts(tile_expert, n_used, xs_sorted, wg, wu, wd):
    n_sorted = xs_sorted.shape[0]
    nf = EXPERT_DIM // MOE_TF
    row_blk = lambda i, f, te, nu: (jnp.minimum(i, nu[0] - 1), 0)
    f_blk = lambda i, f, nu: jnp.where(i < nu[0], f, nf - 1)
    grid_spec = pltpu.PrefetchScalarGridSpec(
        num_scalar_prefetch=2,
        grid=(n_sorted // MOE_TM, nf),
        in_specs=[
            pl.BlockSpec((MOE_TM, D_MODEL), row_blk),
            pl.BlockSpec((None, D_MODEL, MOE_TF), lambda i, f, te, nu: (te[i], 0, f_blk(i, f, nu))),
            pl.BlockSpec((None, D_MODEL, MOE_TF), lambda i, f, te, nu: (te[i], 0, f_blk(i, f, nu))),
            pl.BlockSpec((None, MOE_TF, D_MODEL), lambda i, f, te, nu: (te[i], f_blk(i, f, nu), 0)),
        ],
        out_specs=pl.BlockSpec((MOE_TM, D_MODEL), lambda i, f, te, nu: (i, 0)),
        scratch_shapes=[pltpu.VMEM((MOE_TM, D_MODEL), BF16)],
    )
    return pl.pallas_call(
        _experts_kernel,
        grid_spec=grid_spec,
        out_shape=jax.ShapeDtypeStruct((n_sorted, D_MODEL), F32),
        compiler_params=_cparams(("arbitrary", "arbitrary")),
        name="moe_experts",
    )(tile_expert, n_used, xs_sorted, wg, wu, wd)


def _combine_kernel(offs_ref, idx_ref, idxn_ref, wts_ref, x_ref, mod_ref, g_ref, b_ref, ys_ref, o_ref,
                    ybuf, sems):
    i = pl.program_id(0)
    n = pl.num_programs(0)
    slot = i % 2

    def fetch(ids_ref, sl):
        def issue(r, carry):
            for k in range(2):
                src = _row_slot(offs_ref, ids_ref, k, r)
                pltpu.make_async_copy(ys_ref.at[pl.ds(src, 1)], ybuf.at[sl, k, pl.ds(r, 1)], sems.at[sl]).start()
            return carry

        lax.fori_loop(0, TM, issue, 0, unroll=8)

    @pl.when(i == 0)
    def _():
        fetch(idx_ref, 0)

    @pl.when(i + 1 < n)
    def _():
        fetch(idxn_ref, 1 - slot)

    for k in range(2):
        pltpu.make_async_copy(ys_ref.at[pl.ds(0, TM)], ybuf.at[slot, k], sems.at[slot]).wait()

    wts = wts_ref[...]
    y = wts[:, 0:1] * ybuf[slot, 0] + wts[:, 1:2] * ybuf[slot, 1]
    v = DEEPNORM_ALPHA * x_ref[...] + mod_ref[5:6, :] * y
    o_ref[...] = _layer_norm(v, g_ref[...], b_ref[...])


def _combine(offs, idx, wts, xs, mods, layer, n_lat_tiles, ys_sorted, ln_g, ln_b):
    m = xs.shape[0]
    nt = m // TM
    tok = lambda i, offs: (i, 0)
    full = lambda i, offs: (0, 0)
    grid_spec = pltpu.PrefetchScalarGridSpec(
        num_scalar_prefetch=1,
        grid=(nt,),
        in_specs=[
            pl.BlockSpec((None, 8, TM), lambda i, offs: (i, 0, 0), memory_space=pltpu.SMEM),
            pl.BlockSpec((None, 8, TM), lambda i, offs: (jnp.minimum(i + 1, nt - 1), 0, 0),
                         memory_space=pltpu.SMEM),
            pl.BlockSpec((TM, HEAD_PAD), tok),
            pl.BlockSpec((TM, D_MODEL), tok),
            pl.BlockSpec((None, None, 6, D_MODEL), lambda i, offs: (layer, i // n_lat_tiles, 0, 0)),
            pl.BlockSpec((1, D_MODEL), full),
            pl.BlockSpec((1, D_MODEL), full),
            pl.BlockSpec(memory_space=pl.ANY),
        ],
        out_specs=pl.BlockSpec((TM, D_MODEL), tok),
        scratch_shapes=[pltpu.VMEM((2, 2, TM, D_MODEL), F32), pltpu.SemaphoreType.DMA((2,))],
    )
    return pl.pallas_call(
        _combine_kernel,
        grid_spec=grid_spec,
        out_shape=jax.ShapeDtypeStruct((m, D_MODEL), F32),
        compiler_params=_cparams(("arbitrary",)),
        name="moe_combine_ln",
    )(offs, idx, idx, wts, xs, mods, ln_g, ln_b, ys_sorted)


def _moe_layer(xs, mods, layer, n_lat_tiles, router_w, wg, wu, wd, ln_g, ln_b):
    m = xs.shape[0]
    n_tiles = (2 * m + N_EXPERTS * (MOE_TM - 1) + MOE_TM - 1) // MOE_TM
    n_sorted = n_tiles * MOE_TM
    rw = jnp.pad(router_w, [(0, 0), (0, HEAD_PAD - N_EXPERTS)])
    idx, wts, cnt = _router(xs, mods, layer, n_lat_tiles, rw)
    counts = cnt[0, :N_EXPERTS].astype(jnp.int32)
    padded = (counts + MOE_TM - 1) // MOE_TM * MOE_TM
    ends = jnp.cumsum(padded)
    offs = ends - padded
    n_used = (ends[-1:] // MOE_TM).astype(jnp.int32)
    tile_row = jnp.arange(n_tiles, dtype=jnp.int32) * MOE_TM
    tile_row = jnp.minimum(tile_row, ends[-1] - MOE_TM)
    tile_expert = jnp.sum((tile_row[:, None] >= ends[None, :]).astype(jnp.int32), axis=1)
    xs_sorted = _dispatch(offs, idx, xs, mods, layer, n_lat_tiles, n_sorted)
    ys_sorted = _experts(tile_expert, n_used, xs_sorted, wg, wu, wd)
    return _combine(offs, idx, wts, xs, mods, layer, n_lat_tiles, ys_sorted, ln_g, ln_b)


_ROPE_SWAP = np.array(list(range(8, 16)) + list(range(0, 8)) + list(range(24, 32)) + list(range(16, 24)))


def _pad_heads(w, width):
    lead = w.shape[:-1]
    w = w.reshape(lead + (N_HEADS, width))
    w = jnp.pad(w, [(0, 0)] * len(lead) + [(0, 0), (0, HEAD_PAD - width)])
    return w.reshape(lead + (HP,))


def _prep_attn_weights(w_in, w_q_up, w_kv_up, w_out):
    na_w = N_HEADS * NA_HEAD_DIM
    qa, ka, va = (w_in[:, k * na_w:(k + 1) * na_w] for k in range(3))
    o = 3 * na_w
    w_qc = w_in[:, o:o + MLA_Q_LORA]
    o += MLA_Q_LORA
    w_kvc = w_in[:, o:o + MLA_KV_LORA]
    o += MLA_KV_LORA
    w_kr = w_in[:, o:o + MLA_ROPE]
    rope_pad = [(0, 0), (MLA_NOPE, HEAD_PAD - MLA_NOPE - MLA_ROPE)]
    w_in_aug = jnp.concatenate(
        [_pad_heads(qa, NA_HEAD_DIM), _pad_heads(ka, NA_HEAD_DIM), _pad_heads(va, NA_HEAD_DIM),
         w_qc, w_kvc, jnp.pad(w_kr, rope_pad), jnp.pad(w_kr[:, _ROPE_SWAP], rope_pad)], axis=1).astype(BF16)

    wq = w_q_up.reshape(MLA_Q_LORA, N_HEADS, MLA_NOPE + MLA_ROPE)
    wq_full = jnp.pad(wq, [(0, 0), (0, 0), (0, HEAD_PAD - MLA_NOPE - MLA_ROPE)])
    wq_swap = jnp.pad(wq[:, :, MLA_NOPE:][:, :, _ROPE_SWAP], [(0, 0), (0, 0), rope_pad[1]])
    w_q = jnp.concatenate([wq_full.reshape(MLA_Q_LORA, HP), wq_swap.reshape(MLA_Q_LORA, HP)], axis=1).astype(BF16)

    wkv = w_kv_up.reshape(MLA_KV_LORA, N_HEADS, MLA_NOPE + MLA_V)
    wkn = jnp.pad(wkv[:, :, :MLA_NOPE], [(0, 0), (0, 0), (0, HEAD_PAD - MLA_NOPE)])
    wv = jnp.pad(wkv[:, :, MLA_NOPE:], [(0, 0), (0, 0), (0, HEAD_PAD - MLA_V)])
    w_kv = jnp.concatenate([wkn.reshape(MLA_KV_LORA, HP), wv.reshape(MLA_KV_LORA, HP)], axis=1).astype(BF16)

    def pad_rows(w, width):
        w = w.reshape(N_HEADS, width, D_MODEL)
        return jnp.pad(w, [(0, 0), (0, HEAD_PAD - width), (0, 0)]).reshape(HP, D_MODEL).astype(BF16)

    wa = pad_rows(w_out[:na_w], NA_HEAD_DIM)
    wb = pad_rows(w_out[na_w:], MLA_V)
    return w_in_aug, w_q, w_kv, wa, wb


def _rope_tables(s, n_ctx):
    t = jnp.arange(s, dtype=jnp.int32)
    row = (t // GRID_W).astype(F32)
    col = (t % GRID_W).astype(F32)
    n_freq = MLA_ROPE // 4
    inv = 1.0 / (ROPE_THETA ** (jnp.arange(n_freq, dtype=F32) / n_freq))
    ar = row[:, None] * inv
    ac = col[:, None] * inv
    cos = jnp.concatenate([jnp.cos(ar), jnp.cos(ar), jnp.cos(ac), jnp.cos(ac)], axis=1)
    sin = jnp.concatenate([-jnp.sin(ar), jnp.sin(ar), -jnp.sin(ac), jnp.sin(ac)], axis=1)
    right = HEAD_PAD - MLA_NOPE - MLA_ROPE
    cos = jnp.pad(cos, [(0, n_ctx), (MLA_NOPE, right)], constant_values=1.0)
    sin = jnp.pad(sin, [(0, n_ctx), (MLA_NOPE, right)])
    return cos, sin


def _na_bias_mask(rel_bias, n_rows):
    nb = n_rows // NA_QROWS
    c = np.arange(GRID_W)
    cs = np.clip(c - NA_KW // 2, 0, GRID_W - NA_KW)
    kc = np.arange(GRID_W)
    ok_c = (kc[None, :] >= cs[:, None]) & (kc[None, :] < cs[:, None] + NA_KW)
    n_dcol = 2 * NA_KW - 1
    left = GRID_W - NA_KW
    period = 2 * GRID_W
    u = jnp.pad(rel_bias, [(0, 0), (0, 0), (left, period - n_dcol - left)])
    flat = jnp.tile(u, (1, 1, GRID_W + 1))[:, :, :GRID_W * (period - 1)]
    toep = flat.reshape(N_HEADS, 2 * NA_KH - 1, GRID_W, period - 1)[:, :, :, GRID_W - 1:]
    toep = jnp.where(ok_c, toep, NEG_BIG)
    masked = jnp.full((N_HEADS, GRID_W, GRID_W), NEG_BIG, F32)
    out = []
    for b in (0, 1, nb - 1):
        kr0 = int(np.clip(b * NA_QROWS - NA_KH // 2, 0, n_rows - NA_KROWS))
        q_blocks = []
        for qr in range(NA_QROWS):
            r = b * NA_QROWS + qr
            rs = int(np.clip(r - NA_KH // 2, 0, n_rows - NA_KH))
            k_blocks = []
            for klr in range(NA_KROWS):
                kr = kr0 + klr
                k_blocks.append(toep[:, kr - r + NA_KH - 1] if rs <= kr < rs + NA_KH else masked)
            q_blocks.append(jnp.concatenate(k_blocks, axis=-1))
        out.append(jnp.concatenate(q_blocks, axis=1))
    return jnp.stack(out, axis=0)


def kernel(x, c, ctx, c_ctx, mod_w, mod_b, ln1_g, ln1_b, ln2_g, ln2_b, attn_w_in, na_rel_bias, mla_q_norm,
           mla_w_q_up, mla_kv_norm, mla_w_kv_up, attn_w_out, ffn_w_gate, ffn_w_up, ffn_w_down, pool_w,
           pool_scale, moe_router, moe_w_gate, moe_w_up, moe_w_down):
    assert x.shape[0] == 1 and c.shape[0] == 1 and ctx.shape[0] == 1
    s = x.shape[1]
    n_ctx = ctx.shape[1]
    assert n_ctx == TM and s % (NA_QROWS * GRID_W) == 0 and s % MLA_TQ == 0
    assert (s + n_ctx) % MLA_TK == 0
    n_lat_tiles = s // TM
    depth = mod_w.shape[0]

    xs = jnp.concatenate([x[0], ctx[0]], axis=0)
    ct = jnp.stack([c[0], c_ctx], axis=1)
    mods = _modulation(ct, mod_w, mod_b).reshape(depth, 2, 6, D_MODEL)
    cos_t, sin_t = _rope_tables(s, n_ctx)
    row = lambda v: v.reshape(1, -1)

    for i in range(depth):
        j = i // 2
        if i % 2 == 0:
            w_in, w_q, w_kv, wa, wb = _prep_attn_weights(attn_w_in[j], mla_w_q_up[j], mla_w_kv_up[j],
                                                         attn_w_out[j])
            qa, ka, va, qm, kmt, vm = _attn_proj(xs, mods, i, n_lat_tiles, w_in, row(mla_q_norm[j]), w_q,
                                                 row(mla_kv_norm[j]), w_kv, cos_t, sin_t)
            bias_mask = _na_bias_mask(na_rel_bias[j], s // GRID_W)
            oa_lat = _na_attention(qa, ka, va, bias_mask, s)
            oa_ctx = _ctx_attention(qa, ka, va, s, False, False, "na_ctx_attention")
            ob_lat = _mla_attention(qm, kmt, vm, s)
            ob_ctx = _ctx_attention(qm, kmt, vm, s, True, True, "mla_ctx_attention")
            xs = _out_proj(xs, mods, i, n_lat_tiles, oa_lat, oa_ctx, ob_lat, ob_ctx, wa, wb,
                           row(ln1_g[i]), row(ln1_b[i]))
            xs = _ffn(xs, mods, i, n_lat_tiles, ffn_w_gate[j].astype(BF16), ffn_w_up[j].astype(BF16),
                      ffn_w_down[j].astype(BF16), row(ln2_g[i]), row(ln2_b[i]))
        else:
            xs = _pool(xs, mods, i, n_lat_tiles, s, pool_w[j].astype(BF16), row(pool_scale[j]),
                       row(ln1_g[i]), row(ln1_b[i]))
            xs = _moe_layer(xs, mods, i, n_lat_tiles, moe_router[j], moe_w_gate[j].astype(BF16),
                            moe_w_up[j].astype(BF16), moe_w_down[j].astype(BF16),
                            row(ln2_g[i]), row(ln2_b[i]))
    return xs[:s][None]
```

```python
import functools

import numpy as np
import jax
import jax.numpy as jnp
from jax import lax
from jax.experimental import pallas as pl
from jax.experimental.pallas import tpu as pltpu

F32 = jnp.float32
BF16 = jnp.bfloat16

D_MODEL = 1024
GRID_W = 64
N_HEADS = 8
HEAD_PAD = 128
HP = N_HEADS * HEAD_PAD
NA_HEAD_DIM = 64
NA_KH = 8
NA_KW = 16
MLA_NOPE = 64
MLA_ROPE = 32
MLA_V = 64
MLA_Q_LORA = 256
MLA_KV_LORA = 128
ROPE_THETA = 10000.0
POOL_WINDOWS = (2, 4, 8, 16)
POOL_GROUP = D_MODEL // len(POOL_WINDOWS)
POOL_HALO = 8
FFN_DIM = 2816
N_EXPERTS = 8
EXPERT_DIM = 3584
DEPTH = 4
DEEPNORM_ALPHA = (2 * DEPTH) ** 0.25
LN_EPS = 1e-5
RMS_EPS = 1e-6
LOG2E = 1.4426950408889634
NEG_BIG = -1e30

TM = 256
NA_QROWS = 4
NA_KROWS = NA_QROWS + NA_KH - 1
NA_QB = 4
MLA_TQ = 1024
MLA_TK = 640
MLA_UNROLL = 8
MOE_TM = 512
MOE_TF = 1792
VMEM_LIMIT = 56 * 1024 * 1024


def _cparams(sem):
    return pltpu.CompilerParams(dimension_semantics=sem, vmem_limit_bytes=VMEM_LIMIT)


def _layer_norm(v, g, b):
    mu = jnp.mean(v, axis=-1, keepdims=True)
    d = v - mu
    var = jnp.mean(d * d, axis=-1, keepdims=True)
    return d * lax.rsqrt(var + LN_EPS) * g + b


def _silu(v):
    return v * jax.nn.sigmoid(v)


def _mod_kernel(ct_ref, w_ref, b_ref, o_ref):
    s = _silu(ct_ref[...])
    w = w_ref[...]
    b = b_ref[...]
    r0 = jnp.sum(w * s[:, 0:1], axis=0, keepdims=True) + b
    r1 = jnp.sum(w * s[:, 1:2], axis=0, keepdims=True) + b
    o_ref[...] = jnp.concatenate([r0, r1], axis=0)


def _modulation(ct, mod_w, mod_b):
    depth, d, n6 = mod_w.shape
    tn = 1536
    return pl.pallas_call(
        _mod_kernel,
        grid=(depth, n6 // tn),
        in_specs=[
            pl.BlockSpec((d, 2), lambda l, j: (0, 0)),
            pl.BlockSpec((None, d, tn), lambda l, j: (l, 0, j)),
            pl.BlockSpec((None, 1, tn), lambda l, j: (l, 0, j)),
        ],
        out_specs=pl.BlockSpec((None, 2, tn), lambda l, j: (l, 0, j)),
        out_shape=jax.ShapeDtypeStruct((depth, 2, n6), F32),
        compiler_params=_cparams(("arbitrary", "arbitrary")),
        name="modulation",
    )(ct, mod_w, mod_b.reshape(depth, 1, n6))


def _attn_proj_kernel(x_ref, mod_ref, win_ref, qn_ref, wq_ref, kvn_ref, wkv_ref, cos_ref, sin_ref,
                      qa_ref, ka_ref, va_ref, qm_ref, kmt_ref, vm_ref):
    h = x_ref[...] * (1.0 + mod_ref[1:2, :]) + mod_ref[0:1, :]
    p = jnp.dot(h.astype(BF16), win_ref[...], preferred_element_type=F32)
    qa_ref[...] = (p[:, 0:HP] * (NA_HEAD_DIM ** -0.5)).astype(BF16)
    ka_ref[...] = p[:, HP:2 * HP].astype(BF16)
    va_ref[...] = p[:, 2 * HP:3 * HP].astype(BF16)
    o = 3 * HP
    q_c = p[:, o:o + MLA_Q_LORA]
    o += MLA_Q_LORA
    kv_c = p[:, o:o + MLA_KV_LORA]
    o += MLA_KV_LORA
    krp = p[:, o:o + HEAD_PAD]
    krs = p[:, o + HEAD_PAD:o + 2 * HEAD_PAD]
    qn = q_c * lax.rsqrt(jnp.mean(q_c * q_c, axis=-1, keepdims=True) + RMS_EPS) * qn_ref[...]
    kvn = kv_c * lax.rsqrt(jnp.mean(kv_c * kv_c, axis=-1, keepdims=True) + RMS_EPS) * kvn_ref[...]
    q2 = jnp.dot(qn.astype(BF16), wq_ref[...], preferred_element_type=F32)
    kv2 = jnp.dot(kvn.astype(BF16), wkv_ref[...], preferred_element_type=F32)
    cos = cos_ref[...]
    sin = sin_ref[...]
    kr = krp * cos + krs * sin
    q_scale = (MLA_NOPE + MLA_ROPE) ** -0.5 * LOG2E
    for hd in range(N_HEADS):
        lo = hd * HEAD_PAD
        qh = (q2[:, lo:lo + HEAD_PAD] * cos + q2[:, HP + lo:HP + lo + HEAD_PAD] * sin) * q_scale
        qm_ref[:, lo:lo + HEAD_PAD] = qh.astype(BF16)
        kh = kv2[:, lo:lo + HEAD_PAD] + kr
        kmt_ref[hd] = kh.T.astype(BF16)
    lane = lax.broadcasted_iota(jnp.int32, (1, HP), 1)
    ones_col = jnp.where(lane % HEAD_PAD == MLA_V, 1.0, 0.0)
    vm_ref[...] = (kv2[:, HP:2 * HP] + ones_col).astype(BF16)


def _attn_proj(xs, mods, layer, n_lat_tiles, w_in, q_norm, w_q, kv_norm, w_kv, cos_t, sin_t):
    m = xs.shape[0]
    nt = m // TM
    tok = lambda i: (i, 0)
    full = lambda i: (0, 0)
    act = jax.ShapeDtypeStruct((m, HP), BF16)
    return pl.pallas_call(
        _attn_proj_kernel,
        grid=(nt,),
        in_specs=[
            pl.BlockSpec((TM, D_MODEL), tok),
            pl.BlockSpec((None, None, 6, D_MODEL), lambda i: (layer, i // n_lat_tiles, 0, 0)),
            pl.BlockSpec(w_in.shape, full),
            pl.BlockSpec(q_norm.shape, full),
            pl.BlockSpec(w_q.shape, full),
            pl.BlockSpec(kv_norm.shape, full),
            pl.BlockSpec(w_kv.shape, full),
            pl.BlockSpec((TM, HEAD_PAD), tok),
            pl.BlockSpec((TM, HEAD_PAD), tok),
        ],
        out_specs=[
            pl.BlockSpec((TM, HP), tok),
            pl.BlockSpec((TM, HP), tok),
            pl.BlockSpec((TM, HP), tok),
            pl.BlockSpec((TM, HP), tok),
            pl.BlockSpec((N_HEADS, HEAD_PAD, TM), lambda i: (0, 0, i)),
            pl.BlockSpec((TM, HP), tok),
        ],
        out_shape=[act, act, act, act, jax.ShapeDtypeStruct((N_HEADS, HEAD_PAD, m), BF16), act],
        compiler_params=_cparams(("arbitrary",)),
        name="attn_proj",
    )(xs, mods, w_in, q_norm, w_q, kv_norm, w_kv, cos_t, sin_t)


_NT_DIMS = (((1,), (1,)), ((), ()))


def _na_kernel(q_ref, k_ref, v_ref, kc_ref, vc_ref, *rest, n_rows):
    bm_refs, o_ref = rest[:NA_QB], rest[NA_QB]
    tq = NA_QROWS * GRID_W
    nk = NA_KROWS * GRID_W
    kc = kc_ref[...]
    vc = vc_ref[...]
    for u in range(NA_QB):
        b = pl.program_id(1) * NA_QB + u
        kr0 = jnp.clip(b * NA_QROWS - NA_KH // 2, 0, n_rows - NA_KROWS)
        start = pl.multiple_of(kr0 * GRID_W, GRID_W)
        q = q_ref[u * tq:(u + 1) * tq, :]
        kw = k_ref[pl.ds(start, nk), :]
        vw = v_ref[pl.ds(start, nk), :]
        s_loc = lax.dot_general(q, kw, _NT_DIMS, preferred_element_type=F32) + bm_refs[u][...]
        s_ctx = lax.dot_general(q, kc, _NT_DIMS, preferred_element_type=F32)
        m = jnp.maximum(jnp.max(s_loc, axis=-1, keepdims=True), jnp.max(s_ctx, axis=-1, keepdims=True))
        p_loc = jnp.exp(s_loc - m)
        p_ctx = jnp.exp(s_ctx - m)
        l = jnp.sum(p_loc, axis=-1, keepdims=True) + jnp.sum(p_ctx, axis=-1, keepdims=True)
        o = (jnp.dot(p_loc.astype(BF16), vw, preferred_element_type=F32)
             + jnp.dot(p_ctx.astype(BF16), vc, preferred_element_type=F32))
        o_ref[u * tq:(u + 1) * tq, :] = (o / l).astype(BF16)


def _na_attention(qa, ka, va, bias_mask, s):
    n_rows = s // GRID_W
    nb = n_rows // NA_QROWS
    tq = NA_QROWS * GRID_W
    assert nb % NA_QB == 0
    ctx_blk = s // tq
    pat = lambda b: jnp.where(b == 0, 0, jnp.where(b == nb - 1, 2, 1))
    bm_spec = lambda u: pl.BlockSpec((None, None, tq, NA_KROWS * GRID_W),
                                     lambda h, j: (pat(j * NA_QB + u), h, 0, 0))
    return pl.pallas_call(
        functools.partial(_na_kernel, n_rows=n_rows),
        grid=(N_HEADS, nb // NA_QB),
        in_specs=[
            pl.BlockSpec((NA_QB * tq, HEAD_PAD), lambda h, j: (j, h)),
            pl.BlockSpec((s, HEAD_PAD), lambda h, j: (0, h)),
            pl.BlockSpec((s, HEAD_PAD), lambda h, j: (0, h)),
            pl.BlockSpec((tq, HEAD_PAD), lambda h, j: (ctx_blk, h)),
            pl.BlockSpec((tq, HEAD_PAD), lambda h, j: (ctx_blk, h)),
        ] + [bm_spec(u) for u in range(NA_QB)],
        out_specs=pl.BlockSpec((NA_QB * tq, HEAD_PAD), lambda h, j: (j, h)),
        out_shape=jax.ShapeDtypeStruct((s, HP), BF16),
        compiler_params=_cparams(("arbitrary", "arbitrary")),
        name="na_attention",
    )(qa, ka, va, ka, va, *([bias_mask] * NA_QB))


def _ctx_attn_kernel(q_ref, k_ref, v_ref, o_ref, *, k_transposed, base2):
    q = q_ref[...]
    if k_transposed:
        s = jnp.dot(q, k_ref[...], preferred_element_type=F32)
    else:
        s = lax.dot_general(q, k_ref[...], _NT_DIMS, preferred_element_type=F32)
    m = jnp.max(s, axis=-1, keepdims=True)
    p = jnp.exp2(s - m) if base2 else jnp.exp(s - m)
    l = jnp.sum(p, axis=-1, keepdims=True)
    o = jnp.dot(p.astype(BF16), v_ref[...], preferred_element_type=F32)
    o_ref[...] = (o / l).astype(BF16)


def _ctx_attention(q, k, v, s, k_transposed, base2, name):
    blk = s // TM
    if k_transposed:
        k_spec = pl.BlockSpec((None, HEAD_PAD, TM), lambda h: (h, 0, blk))
    else:
        k_spec = pl.BlockSpec((TM, HEAD_PAD), lambda h: (blk, h))
    return pl.pallas_call(
        functools.partial(_ctx_attn_kernel, k_transposed=k_transposed, base2=base2),
        grid=(N_HEADS,),
        in_specs=[
            pl.BlockSpec((TM, HEAD_PAD), lambda h: (blk, h)),
            k_spec,
            pl.BlockSpec((TM, HEAD_PAD), lambda h: (blk, h)),
        ],
        out_specs=pl.BlockSpec((TM, HEAD_PAD), lambda h: (0, h)),
        out_shape=jax.ShapeDtypeStruct((TM, HP), BF16),
        compiler_params=_cparams(("arbitrary",)),
        name=name,
    )(q, k, v)


def _mla_kernel(q_ref, kt_ref, v_ref, o_ref, s0_ref, s1_ref, *, n_chunks):
    q = q_ref[...]
    tq = q.shape[0]

    def scores(c, s_ref):
        off = pl.multiple_of(c * MLA_TK, MLA_TK)
        s_ref[...] = jnp.dot(q, kt_ref[:, pl.ds(off, MLA_TK)], preferred_element_type=F32)

    def update(c, s_ref, carry):
        m, acc = carry
        off = pl.multiple_of(c * MLA_TK, MLA_TK)
        s = s_ref[...]
        m_new = jnp.maximum(m, jnp.max(s, axis=-1, keepdims=True))
        alpha = jnp.exp2(m - m_new)
        p = jnp.exp2(s - m_new).astype(BF16)
        acc = alpha * acc + jnp.dot(p, v_ref[pl.ds(off, MLA_TK), :], preferred_element_type=F32)
        return m_new, acc

    bufs = (s0_ref, s1_ref)

    def group(g, carry):
        c0 = g * MLA_UNROLL
        for u in range(MLA_UNROLL):
            scores(c0 + u + 1, bufs[(u + 1) % 2])
            carry = update(c0 + u, bufs[u % 2], carry)
        return carry

    carry = (jnp.full((tq, 1), NEG_BIG, F32), jnp.zeros((tq, HEAD_PAD), F32))
    scores(0, s0_ref)
    n_groups = (n_chunks - 2) // MLA_UNROLL
    carry = lax.fori_loop(0, n_groups, group, carry)
    for c in range(n_groups * MLA_UNROLL, n_chunks):
        if c + 1 < n_chunks:
            scores(c + 1, bufs[(c + 1) % 2])
        carry = update(c, bufs[c % 2], carry)
    _, acc = carry
    o_ref[...] = (acc / acc[:, MLA_V:MLA_V + 1]).astype(BF16)


def _mla_attention(qm, kmt, vm, s):
    m = qm.shape[0]
    n_chunks = m // MLA_TK
    assert n_chunks % 2 == 0 and n_chunks >= 2
    return pl.pallas_call(
        functools.partial(_mla_kernel, n_chunks=n_chunks),
        grid=(N_HEADS, s // MLA_TQ),
        in_specs=[
            pl.BlockSpec((MLA_TQ, HEAD_PAD), lambda h, i: (i, h)),
            pl.BlockSpec((None, HEAD_PAD, m), lambda h, i: (h, 0, 0)),
            pl.BlockSpec((m, HEAD_PAD), lambda h, i: (0, h)),
        ],
        out_specs=pl.BlockSpec((MLA_TQ, HEAD_PAD), lambda h, i: (i, h)),
        out_shape=jax.ShapeDtypeStruct((s, HP), BF16),
        scratch_shapes=[pltpu.VMEM((MLA_TQ, MLA_TK), F32), pltpu.VMEM((MLA_TQ, MLA_TK), F32)],
        compiler_params=_cparams(("arbitrary", "arbitrary")),
        name="mla_attention",
    )(qm, kmt, vm)


def _out_proj_kernel(x_ref, mod_ref, oal_ref, obl_ref, *rest, n_lat_tiles, with_ctx):
    if with_ctx:
        oac_ref, obc_ref, wa_ref, wb_ref, g_ref, b_ref, o_ref = rest
        is_ctx = pl.program_id(0) >= n_lat_tiles
        oa = jnp.where(is_ctx, oac_ref[...], oal_ref[...])
        ob = jnp.where(is_ctx, obc_ref[...], obl_ref[...])
    else:
        wa_ref, wb_ref, g_ref, b_ref, o_ref = rest
        oa = oal_ref[...]
        ob = obl_ref[...]
    y = (jnp.dot(oa, wa_ref[...], preferred_element_type=F32)
         + jnp.dot(ob, wb_ref[...], preferred_element_type=F32))
    v = DEEPNORM_ALPHA * x_ref[...] + mod_ref[2:3, :] * y
    o_ref[...] = _layer_norm(v, g_ref[...], b_ref[...])


def _out_proj(xs, mods, layer, n_lat_tiles, oa_lat, ob_lat, oa_ctx, ob_ctx, wa, wb, ln_g, ln_b):
    with_ctx = oa_ctx is not None
    m = xs.shape[0] if with_ctx else n_lat_tiles * TM
    tok = lambda i: (i, 0)
    lat = lambda i: (jnp.minimum(i, n_lat_tiles - 1), 0)
    full = lambda i: (0, 0)
    ctx_args = [oa_ctx, ob_ctx] if with_ctx else []
    return pl.pallas_call(
        functools.partial(_out_proj_kernel, n_lat_tiles=n_lat_tiles, with_ctx=with_ctx),
        grid=(m // TM,),
        in_specs=[
            pl.BlockSpec((TM, D_MODEL), tok),
            pl.BlockSpec((None, None, 6, D_MODEL), lambda i: (layer, i // n_lat_tiles, 0, 0)),
            pl.BlockSpec((TM, HP), lat),
            pl.BlockSpec((TM, HP), lat),
        ] + [pl.BlockSpec((TM, HP), full)] * len(ctx_args) + [
            pl.BlockSpec(wa.shape, full),
            pl.BlockSpec(wb.shape, full),
            pl.BlockSpec((1, D_MODEL), full),
            pl.BlockSpec((1, D_MODEL), full),
        ],
        out_specs=pl.BlockSpec((TM, D_MODEL), tok),
        out_shape=jax.ShapeDtypeStruct((m, D_MODEL), F32),
        compiler_params=_cparams(("arbitrary",)),
        name="out_proj_ln",
    )(xs, mods, oa_lat, ob_lat, *ctx_args, wa, wb, ln_g, ln_b)


def _ffn_kernel(x_ref, mod_ref, wg_ref, wu_ref, wd_ref, g_ref, b_ref, o_ref):
    x = x_ref[...]
    h = (x * (1.0 + mod_ref[4:5, :]) + mod_ref[3:4, :]).astype(BF16)
    gate = jnp.dot(h, wg_ref[...], preferred_element_type=F32)
    up = jnp.dot(h, wu_ref[...], preferred_element_type=F32)
    a = (_silu(gate) * up).astype(BF16)
    y = jnp.dot(a, wd_ref[...], preferred_element_type=F32)
    v = DEEPNORM_ALPHA * x + mod_ref[5:6, :] * y
    o_ref[...] = _layer_norm(v, g_ref[...], b_ref[...])


def _ffn(xs, mods, layer, n_lat_tiles, wg, wu, wd, ln_g, ln_b):
    m = xs.shape[0]
    tok = lambda i: (i, 0)
    full = lambda i: (0, 0)
    return pl.pallas_call(
        _ffn_kernel,
        grid=(m // TM,),
        in_specs=[
            pl.BlockSpec((TM, D_MODEL), tok),
            pl.BlockSpec((None, None, 6, D_MODEL), lambda i: (layer, i // n_lat_tiles, 0, 0)),
            pl.BlockSpec(wg.shape, full),
            pl.BlockSpec(wu.shape, full),
            pl.BlockSpec(wd.shape, full),
            pl.BlockSpec((1, D_MODEL), full),
            pl.BlockSpec((1, D_MODEL), full),
        ],
        out_specs=pl.BlockSpec((TM, D_MODEL), tok),
        out_shape=jax.ShapeDtypeStruct((m, D_MODEL), F32),
        compiler_params=_cparams(("arbitrary",)),
        name="ffn_ln",
    )(xs, mods, wg, wu, wd, ln_g, ln_b)


def _pool_kernel(x_ref, xp_ref, xn_ref, mod_ref, w_ref, sc_ref, g_ref, b_ref, o_ref, ext_ref,
                 *, n_lat_tiles, s):
    i = pl.program_id(0)
    is_ctx = i >= n_lat_tiles
    j = jnp.where(is_ctx, i - n_lat_tiles, i)
    last = jnp.where(is_ctx, 0, n_lat_tiles - 1)
    n_seq = jnp.where(is_ctx, TM, s)
    scale1 = 1.0 + mod_ref[1:2, :]
    shift1 = mod_ref[0:1, :]
    x = x_ref[...]
    h = x * scale1 + shift1
    hp = jnp.where(j != 0, xp_ref[...] * scale1 + shift1, 0.0)
    hn = jnp.where(j != last, xn_ref[...] * scale1 + shift1, 0.0)
    ext_ref[0:POOL_HALO, :] = hp
    ext_ref[POOL_HALO:POOL_HALO + TM, :] = h
    ext_ref[POOL_HALO + TM:POOL_HALO + TM + POOL_HALO, :] = hn
    t = j * TM + lax.broadcasted_iota(jnp.int32, (TM, 1), 0)
    ys = []
    for g, win in enumerate(POOL_WINDOWS):
        half = win // 2
        lo = g * POOL_GROUP
        acc = ext_ref[POOL_HALO - half:POOL_HALO - half + TM, lo:lo + POOL_GROUP]
        for dlt in range(-half + 1, half):
            acc = acc + ext_ref[POOL_HALO + dlt:POOL_HALO + dlt + TM, lo:lo + POOL_GROUP]
        cnt = (jnp.minimum(t + half, n_seq) - jnp.maximum(t - half, 0)).astype(F32)
        mixed = acc / cnt - h[:, lo:lo + POOL_GROUP]
        yg = jnp.dot(mixed.astype(BF16), w_ref[g], preferred_element_type=F32)
        ys.append(yg)
    y = jnp.concatenate(ys, axis=-1) * sc_ref[...]
    v = DEEPNORM_ALPHA * x + mod_ref[2:3, :] * y
    o_ref[...] = _layer_norm(v, g_ref[...], b_ref[...])


def _pool(xs, mods, layer, n_lat_tiles, s, pool_w, pool_scale, ln_g, ln_b):
    m = xs.shape[0]
    per = TM // POOL_HALO
    n_halo_blocks = m // POOL_HALO
    tok = lambda i: (i, 0)
    full = lambda i: (0, 0)
    return pl.pallas_call(
        functools.partial(_pool_kernel, n_lat_tiles=n_lat_tiles, s=s),
        grid=(m // TM,),
        in_specs=[
            pl.BlockSpec((TM, D_MODEL), tok),
            pl.BlockSpec((POOL_HALO, D_MODEL), lambda i: (jnp.maximum(i * per - 1, 0), 0)),
            pl.BlockSpec((POOL_HALO, D_MODEL), lambda i: (jnp.minimum((i + 1) * per, n_halo_blocks - 1), 0)),
            pl.BlockSpec((None, None, 6, D_MODEL), lambda i: (layer, i // n_lat_tiles, 0, 0)),
            pl.BlockSpec(pool_w.shape, lambda i: (0, 0, 0)),
            pl.BlockSpec((1, D_MODEL), full),
            pl.BlockSpec((1, D_MODEL), full),
            pl.BlockSpec((1, D_MODEL), full),
        ],
        out_specs=pl.BlockSpec((TM, D_MODEL), tok),
        out_shape=jax.ShapeDtypeStruct((m, D_MODEL), F32),
        scratch_shapes=[pltpu.VMEM((TM + 2 * POOL_HALO, D_MODEL), F32)],
        compiler_params=_cparams(("arbitrary",)),
        name="pool_ln",
    )(xs, xs, xs, mods, pool_w, pool_scale, ln_g, ln_b)


def _router_kernel(x_ref, mod_ref, rw_ref, tri_ref, idx_ref, wts_ref, cnt_ref, base_ref):
    @pl.when(pl.program_id(0) == 0)
    def _():
        base_ref[...] = jnp.zeros_like(base_ref)

    h = x_ref[...] * (1.0 + mod_ref[4:5, :]) + mod_ref[3:4, :]
    logits = jnp.dot(h, rw_ref[...], preferred_element_type=F32, precision=lax.Precision.HIGHEST)
    lane = lax.broadcasted_iota(jnp.int32, logits.shape, 1)
    logits = jnp.where(lane < N_EXPERTS, logits, -jnp.inf)
    m1 = jnp.max(logits, axis=-1, keepdims=True)
    i1 = jnp.min(jnp.where(logits == m1, lane, HEAD_PAD), axis=-1, keepdims=True)
    rest = jnp.where(lane == i1, -jnp.inf, logits)
    m2 = jnp.max(rest, axis=-1, keepdims=True)
    i2 = jnp.min(jnp.where(rest == m2, lane, HEAD_PAD), axis=-1, keepdims=True)
    e2 = jnp.exp(m2 - m1)
    w1 = 1.0 / (1.0 + e2)
    w2 = e2 / (1.0 + e2)
    wts_ref[...] = jnp.where(lane == 0, w1, jnp.where(lane == 1, w2, 0.0))

    chosen = jnp.where(lane == i1, 1.0, jnp.where(lane == i2, 1.0, 0.0))
    before = jnp.dot(tri_ref[...], chosen.astype(BF16), preferred_element_type=F32)
    rank = before + base_ref[0:1, :]
    r1 = jnp.sum(jnp.where(lane == i1, rank, 0.0), axis=-1, keepdims=True)
    r2 = jnp.sum(jnp.where(lane == i2, rank, 0.0), axis=-1, keepdims=True)
    base_ref[0:1, :] = base_ref[0:1, :] + jnp.sum(chosen, axis=0, keepdims=True)
    cnt_ref[...] = base_ref[...]
    packed = jnp.where(lane == 0, i1.astype(F32),
                       jnp.where(lane == 1, i2.astype(F32),
                                 jnp.where(lane == 2, r1, jnp.where(lane == 3, r2, 0.0))))
    idx_ref[...] = packed.T[0:8, :].astype(jnp.int32)


def _router(xs, mods, layer, n_lat_tiles, rw):
    m = xs.shape[0]
    nt = m // TM
    tok = lambda i: (i, 0)
    tri = jnp.asarray(np.tril(np.ones((TM, TM), np.float32), -1), dtype=BF16)
    return pl.pallas_call(
        _router_kernel,
        grid=(nt,),
        in_specs=[
            pl.BlockSpec((TM, D_MODEL), tok),
            pl.BlockSpec((None, None, 6, D_MODEL), lambda i: (layer, i // n_lat_tiles, 0, 0)),
            pl.BlockSpec(rw.shape, lambda i: (0, 0)),
            pl.BlockSpec((TM, TM), lambda i: (0, 0)),
        ],
        out_specs=[
            pl.BlockSpec((None, 8, TM), lambda i: (i, 0, 0)),
            pl.BlockSpec((TM, HEAD_PAD), tok),
            pl.BlockSpec((8, HEAD_PAD), lambda i: (0, 0)),
        ],
        out_shape=[
            jax.ShapeDtypeStruct((nt, 8, TM), jnp.int32),
            jax.ShapeDtypeStruct((m, HEAD_PAD), F32),
            jax.ShapeDtypeStruct((8, HEAD_PAD), F32),
        ],
        scratch_shapes=[pltpu.VMEM((8, HEAD_PAD), F32)],
        compiler_params=_cparams(("arbitrary",)),
        name="moe_router",
    )(xs, mods, rw, tri)


def _dispatch_kernel(rows_ref, x_ref, mod_ref, zero_ref, xs_ref, hbuf, sems):
    del zero_ref
    i = pl.program_id(0)
    n = pl.num_programs(0)
    slot = i % 2

    def wait_slot(sl):
        for _ in range(2):
            pltpu.make_async_copy(hbuf.at[sl], xs_ref.at[pl.ds(0, TM)], sems.at[sl]).wait()

    @pl.when(i >= 2)
    def _():
        wait_slot(slot)

    hbuf[slot] = x_ref[...] * (1.0 + mod_ref[4:5, :]) + mod_ref[3:4, :]

    def issue(r, carry):
        for k in range(2):
            dst = rows_ref[0, k * TM + r]
            pltpu.make_async_copy(hbuf.at[slot, pl.ds(r, 1)], xs_ref.at[pl.ds(dst, 1)], sems.at[slot]).start()
        return carry

    lax.fori_loop(0, TM, issue, 0, unroll=8)

    @pl.when(i == n - 1)
    def _():
        wait_slot(slot)

        @pl.when(n >= 2)
        def _():
            wait_slot(1 - slot)


def _dispatch(rows, xs, mods, layer, n_lat_tiles, n_sorted):
    m = xs.shape[0]
    zeros = jnp.zeros((n_sorted, D_MODEL), F32)
    return pl.pallas_call(
        _dispatch_kernel,
        grid=(m // TM,),
        in_specs=[
            pl.BlockSpec((None, 1, 2 * TM), lambda i: (i, 0, 0), memory_space=pltpu.SMEM),
            pl.BlockSpec((TM, D_MODEL), lambda i: (i, 0)),
            pl.BlockSpec((None, None, 6, D_MODEL), lambda i: (layer, i // n_lat_tiles, 0, 0)),
            pl.BlockSpec(memory_space=pl.ANY),
        ],
        out_specs=pl.BlockSpec(memory_space=pl.ANY),
        out_shape=jax.ShapeDtypeStruct((n_sorted, D_MODEL), F32),
        scratch_shapes=[pltpu.VMEM((2, TM, D_MODEL), F32), pltpu.SemaphoreType.DMA((2,))],
        input_output_aliases={3: 0},
        compiler_params=_cparams(("arbitrary",)),
        name="moe_dispatch",
    )(rows, xs, mods, zeros)


def _experts_kernel(te_ref, nu_ref, x_ref, wg_ref, wu_ref, wd_ref, o_ref, xb_ref):
    del te_ref
    f = pl.program_id(1)

    @pl.when((pl.program_id(0) >= nu_ref[0]) & (f == 0))
    def _():
        o_ref[...] = jnp.zeros_like(o_ref)

    @pl.when(pl.program_id(0) < nu_ref[0])
    def _():
        @pl.when(f == 0)
        def _():
            xb_ref[...] = x_ref[...].astype(BF16)

        xb = xb_ref[...]
        gate = jnp.dot(xb, wg_ref[...], preferred_element_type=F32)
        up = jnp.dot(xb, wu_ref[...], preferred_element_type=F32)
        a = (_silu(gate) * up).astype(BF16)
        y = jnp.dot(a, wd_ref[...], preferred_element_type=F32)

        @pl.when(f == 0)
        def _():
            o_ref[...] = y

        @pl.when(f > 0)
        def _():
            o_ref[...] += y


def _experts(tile_expert, n_used, xs_sorted, wg, wu, wd, j):
    n_sorted = xs_sorted.shape[0]
    nf = EXPERT_DIM // MOE_TF
    row_blk = lambda i, f, te, nu: (jnp.minimum(i, nu[0] - 1), 0)
    f_blk = lambda i, f, nu: jnp.where(i < nu[0], f, nf - 1)
    grid_spec = pltpu.PrefetchScalarGridSpec(
        num_scalar_prefetch=2,
        grid=(n_sorted // MOE_TM, nf),
        in_specs=[
            pl.BlockSpec((MOE_TM, D_MODEL), row_blk),
            pl.BlockSpec((None, None, D_MODEL, MOE_TF), lambda i, f, te, nu: (j, te[i], 0, f_blk(i, f, nu))),
            pl.BlockSpec((None, None, D_MODEL, MOE_TF), lambda i, f, te, nu: (j, te[i], 0, f_blk(i, f, nu))),
            pl.BlockSpec((None, None, MOE_TF, D_MODEL), lambda i, f, te, nu: (j, te[i], f_blk(i, f, nu), 0)),
        ],
        out_specs=pl.BlockSpec((MOE_TM, D_MODEL), lambda i, f, te, nu: (i, 0)),
        scratch_shapes=[pltpu.VMEM((MOE_TM, D_MODEL), BF16)],
    )
    return pl.pallas_call(
        _experts_kernel,
        grid_spec=grid_spec,
        out_shape=jax.ShapeDtypeStruct((n_sorted, D_MODEL), F32),
        compiler_params=_cparams(("arbitrary", "arbitrary")),
        name="moe_experts",
    )(tile_expert, n_used, xs_sorted, wg, wu, wd)


def _combine_kernel(rows_ref, rowsn_ref, wts_ref, x_ref, mod_ref, g_ref, b_ref, ys_ref, o_ref, ybuf, sems):
    i = pl.program_id(0)
    n = pl.num_programs(0)
    slot = i % 2

    def fetch(ids_ref, sl):
        def issue(r, carry):
            for k in range(2):
                src = ids_ref[0, k * TM + r]
                pltpu.make_async_copy(ys_ref.at[pl.ds(src, 1)], ybuf.at[sl, k, pl.ds(r, 1)], sems.at[sl]).start()
            return carry

        lax.fori_loop(0, TM, issue, 0, unroll=8)

    @pl.when(i == 0)
    def _():
        fetch(rows_ref, 0)

    @pl.when(i + 1 < n)
    def _():
        fetch(rowsn_ref, 1 - slot)

    for k in range(2):
        pltpu.make_async_copy(ys_ref.at[pl.ds(0, TM)], ybuf.at[slot, k], sems.at[slot]).wait()

    wts = wts_ref[...]
    y = wts[:, 0:1] * ybuf[slot, 0] + wts[:, 1:2] * ybuf[slot, 1]
    v = DEEPNORM_ALPHA * x_ref[...] + mod_ref[5:6, :] * y
    o_ref[...] = _layer_norm(v, g_ref[...], b_ref[...])


def _combine(rows, wts, xs, mods, layer, n_lat_tiles, ys_sorted, ln_g, ln_b):
    m = xs.shape[0]
    nt = m // TM
    tok = lambda i: (i, 0)
    full = lambda i: (0, 0)
    return pl.pallas_call(
        _combine_kernel,
        grid=(nt,),
        in_specs=[
            pl.BlockSpec((None, 1, 2 * TM), lambda i: (i, 0, 0), memory_space=pltpu.SMEM),
            pl.BlockSpec((None, 1, 2 * TM), lambda i: (jnp.minimum(i + 1, nt - 1), 0, 0),
                         memory_space=pltpu.SMEM),
            pl.BlockSpec((TM, HEAD_PAD), tok),
            pl.BlockSpec((TM, D_MODEL), tok),
            pl.BlockSpec((None, None, 6, D_MODEL), lambda i: (layer, i // n_lat_tiles, 0, 0)),
            pl.BlockSpec((1, D_MODEL), full),
            pl.BlockSpec((1, D_MODEL), full),
            pl.BlockSpec(memory_space=pl.ANY),
        ],
        out_specs=pl.BlockSpec((TM, D_MODEL), tok),
        out_shape=jax.ShapeDtypeStruct((m, D_MODEL), F32),
        scratch_shapes=[pltpu.VMEM((2, 2, TM, D_MODEL), F32), pltpu.SemaphoreType.DMA((2,))],
        compiler_params=_cparams(("arbitrary",)),
        name="moe_combine_ln",
    )(rows, rows, wts, xs, mods, ln_g, ln_b, ys_sorted)


def _moe_layer(xs, mods, layer, n_lat_tiles, router_w, wg, wu, wd, j, ln_g, ln_b):
    m = xs.shape[0]
    n_tiles = (2 * m + N_EXPERTS * (MOE_TM - 1) + MOE_TM - 1) // MOE_TM
    n_sorted = n_tiles * MOE_TM
    rw = jnp.pad(router_w, [(0, 0), (0, HEAD_PAD - N_EXPERTS)])
    idx, wts, cnt = _router(xs, mods, layer, n_lat_tiles, rw)
    counts = cnt[0, :N_EXPERTS].astype(jnp.int32)
    padded = (counts + MOE_TM - 1) // MOE_TM * MOE_TM
    ends = jnp.cumsum(padded)
    offs = ends - padded
    n_used = (ends[-1:] // MOE_TM).astype(jnp.int32)
    tile_row = jnp.arange(n_tiles, dtype=jnp.int32) * MOE_TM
    tile_row = jnp.minimum(tile_row, ends[-1] - MOE_TM)
    tile_expert = jnp.sum((tile_row[:, None] >= ends[None, :]).astype(jnp.int32), axis=1)
    expert_ids = idx[:, 0:2, :]
    group_off = jnp.sum(jnp.where(expert_ids[..., None] == jnp.arange(N_EXPERTS), offs, 0), axis=-1)
    rows = (group_off + idx[:, 2:4, :]).reshape(m // TM, 1, 2 * TM)
    xs_sorted = _dispatch(rows, xs, mods, layer, n_lat_tiles, n_sorted)
    ys_sorted = _experts(tile_expert, n_used, xs_sorted, wg, wu, wd, j)
    return _combine(rows, wts, xs, mods, layer, n_lat_tiles, ys_sorted, ln_g, ln_b)


_ROPE_SWAP = np.array(list(range(8, 16)) + list(range(0, 8)) + list(range(24, 32)) + list(range(16, 24)))


def _pad_heads(w, width):
    lead = w.shape[:-1]
    w = w.reshape(lead + (N_HEADS, width))
    w = jnp.pad(w, [(0, 0)] * len(lead) + [(0, 0), (0, HEAD_PAD - width)])
    return w.reshape(lead + (HP,))


def _prep_attn_weights(w_in, w_q_up, w_kv_up, w_out):
    na_w = N_HEADS * NA_HEAD_DIM
    qa, ka, va = (w_in[:, k * na_w:(k + 1) * na_w] for k in range(3))
    o = 3 * na_w
    w_qc = w_in[:, o:o + MLA_Q_LORA]
    o += MLA_Q_LORA
    w_kvc = w_in[:, o:o + MLA_KV_LORA]
    o += MLA_KV_LORA
    w_kr = w_in[:, o:o + MLA_ROPE]
    rope_pad = [(0, 0), (MLA_NOPE, HEAD_PAD - MLA_NOPE - MLA_ROPE)]
    w_in_aug = jnp.concatenate(
        [_pad_heads(qa, NA_HEAD_DIM), _pad_heads(ka, NA_HEAD_DIM), _pad_heads(va, NA_HEAD_DIM),
         w_qc, w_kvc, jnp.pad(w_kr, rope_pad), jnp.pad(w_kr[:, _ROPE_SWAP], rope_pad)], axis=1).astype(BF16)

    wq = w_q_up.reshape(MLA_Q_LORA, N_HEADS, MLA_NOPE + MLA_ROPE)
    wq_full = jnp.pad(wq, [(0, 0), (0, 0), (0, HEAD_PAD - MLA_NOPE - MLA_ROPE)])
    wq_swap = jnp.pad(wq[:, :, MLA_NOPE:][:, :, _ROPE_SWAP], [(0, 0), (0, 0), rope_pad[1]])
    w_q = jnp.concatenate([wq_full.reshape(MLA_Q_LORA, HP), wq_swap.reshape(MLA_Q_LORA, HP)], axis=1).astype(BF16)

    wkv = w_kv_up.reshape(MLA_KV_LORA, N_HEADS, MLA_NOPE + MLA_V)
    wkn = jnp.pad(wkv[:, :, :MLA_NOPE], [(0, 0), (0, 0), (0, HEAD_PAD - MLA_NOPE)])
    wv = jnp.pad(wkv[:, :, MLA_NOPE:], [(0, 0), (0, 0), (0, HEAD_PAD - MLA_V)])
    w_kv = jnp.concatenate([wkn.reshape(MLA_KV_LORA, HP), wv.reshape(MLA_KV_LORA, HP)], axis=1).astype(BF16)

    def pad_rows(w, width):
        w = w.reshape(N_HEADS, width, D_MODEL)
        return jnp.pad(w, [(0, 0), (0, HEAD_PAD - width), (0, 0)]).reshape(HP, D_MODEL).astype(BF16)

    wa = pad_rows(w_out[:na_w], NA_HEAD_DIM)
    wb = pad_rows(w_out[na_w:], MLA_V)
    return w_in_aug, w_q, w_kv, wa, wb


def _rope_tables(s, n_ctx):
    t = jnp.arange(s, dtype=jnp.int32)
    row = (t // GRID_W).astype(F32)
    col = (t % GRID_W).astype(F32)
    n_freq = MLA_ROPE // 4
    inv = 1.0 / (ROPE_THETA ** (jnp.arange(n_freq, dtype=F32) / n_freq))
    ar = row[:, None] * inv
    ac = col[:, None] * inv
    cos = jnp.concatenate([jnp.cos(ar), jnp.cos(ar), jnp.cos(ac), jnp.cos(ac)], axis=1)
    sin = jnp.concatenate([-jnp.sin(ar), jnp.sin(ar), -jnp.sin(ac), jnp.sin(ac)], axis=1)
    right = HEAD_PAD - MLA_NOPE - MLA_ROPE
    cos = jnp.pad(cos, [(0, n_ctx), (MLA_NOPE, right)], constant_values=1.0)
    sin = jnp.pad(sin, [(0, n_ctx), (MLA_NOPE, right)])
    return cos, sin


def _na_bias_mask(rel_bias, n_rows):
    nb = n_rows // NA_QROWS
    c = np.arange(GRID_W)
    cs = np.clip(c - NA_KW // 2, 0, GRID_W - NA_KW)
    kc = np.arange(GRID_W)
    ok_c = (kc[None, :] >= cs[:, None]) & (kc[None, :] < cs[:, None] + NA_KW)
    n_dcol = 2 * NA_KW - 1
    left = GRID_W - NA_KW
    period = 2 * GRID_W
    u = jnp.pad(rel_bias, [(0, 0), (0, 0), (left, period - n_dcol - left)])
    flat = jnp.tile(u, (1, 1, GRID_W + 1))[:, :, :GRID_W * (period - 1)]
    toep = flat.reshape(N_HEADS, 2 * NA_KH - 1, GRID_W, period - 1)[:, :, :, GRID_W - 1:]
    toep = jnp.where(ok_c, toep, NEG_BIG)
    masked = jnp.full((N_HEADS, GRID_W, GRID_W), NEG_BIG, F32)
    out = []
    for b in (0, 1, nb - 1):
        kr0 = int(np.clip(b * NA_QROWS - NA_KH // 2, 0, n_rows - NA_KROWS))
        q_blocks = []
        for qr in range(NA_QROWS):
            r = b * NA_QROWS + qr
            rs = int(np.clip(r - NA_KH // 2, 0, n_rows - NA_KH))
            k_blocks = []
            for klr in range(NA_KROWS):
                kr = kr0 + klr
                k_blocks.append(toep[:, kr - r + NA_KH - 1] if rs <= kr < rs + NA_KH else masked)
            q_blocks.append(jnp.concatenate(k_blocks, axis=-1))
        out.append(jnp.concatenate(q_blocks, axis=1))
    return jnp.stack(out, axis=0)


def kernel(x, c, ctx, c_ctx, mod_w, mod_b, ln1_g, ln1_b, ln2_g, ln2_b, attn_w_in, na_rel_bias, mla_q_norm,
           mla_w_q_up, mla_kv_norm, mla_w_kv_up, attn_w_out, ffn_w_gate, ffn_w_up, ffn_w_down, pool_w,
           pool_scale, moe_router, moe_w_gate, moe_w_up, moe_w_down):
    assert x.shape[0] == 1 and c.shape[0] == 1 and ctx.shape[0] == 1
    s = x.shape[1]
    n_ctx = ctx.shape[1]
    assert n_ctx == TM and s % (NA_QROWS * GRID_W) == 0 and s % MLA_TQ == 0
    assert (s + n_ctx) % MLA_TK == 0
    n_lat_tiles = s // TM
    depth = mod_w.shape[0]

    xs = jnp.concatenate([x[0], ctx[0]], axis=0)
    ct = jnp.stack([c[0], c_ctx], axis=1)
    mods = _modulation(ct, mod_w, mod_b).reshape(depth, 2, 6, D_MODEL)
    cos_t, sin_t = _rope_tables(s, n_ctx)
    row = lambda v: v.reshape(1, -1)
    moe_wg, moe_wu, moe_wd = moe_w_gate.astype(BF16), moe_w_up.astype(BF16), moe_w_down.astype(BF16)

    for i in range(depth):
        j = i // 2
        ctx_live = any(l % 2 == 0 for l in range(i + 1, depth))
        if i % 2 == 0:
            w_in, w_q, w_kv, wa, wb = _prep_attn_weights(attn_w_in[j], mla_w_q_up[j], mla_w_kv_up[j],
                                                         attn_w_out[j])
            qa, ka, va, qm, kmt, vm = _attn_proj(xs, mods, i, n_lat_tiles, w_in, row(mla_q_norm[j]), w_q,
                                                 row(mla_kv_norm[j]), w_kv, cos_t, sin_t)
            bias_mask = _na_bias_mask(na_rel_bias[j], s // GRID_W)
            oa_lat = _na_attention(qa, ka, va, bias_mask, s)
            ob_lat = _mla_attention(qm, kmt, vm, s)
            oa_ctx = ob_ctx = None
            if ctx_live:
                oa_ctx = _ctx_attention(qa, ka, va, s, False, False, "na_ctx_attention")
                ob_ctx = _ctx_attention(qm, kmt, vm, s, True, True, "mla_ctx_attention")
            xs = _out_proj(xs, mods, i, n_lat_tiles, oa_lat, ob_lat, oa_ctx, ob_ctx, wa, wb,
                           row(ln1_g[i]), row(ln1_b[i]))
            xs = _ffn(xs, mods, i, n_lat_tiles, ffn_w_gate[j].astype(BF16), ffn_w_up[j].astype(BF16),
                      ffn_w_down[j].astype(BF16), row(ln2_g[i]), row(ln2_b[i]))
        else:
            xs = _pool(xs, mods, i, n_lat_tiles, s, pool_w[j].astype(BF16), row(pool_scale[j]),
                       row(ln1_g[i]), row(ln1_b[i]))
            xs = _moe_layer(xs, mods, i, n_lat_tiles, moe_router[j], moe_wg, moe_wu, moe_wd, j,
                            row(ln2_g[i]), row(ln2_b[i]))
    return xs[:s][None] if xs.shape[0] != s else xs[None]
```

```python
import functools

import numpy as np
import jax
import jax.numpy as jnp
from jax import lax
from jax.experimental import pallas as pl
from jax.experimental.pallas import tpu as pltpu

F32 = jnp.float32
BF16 = jnp.bfloat16

D_MODEL = 1024
GRID_W = 64
N_HEADS = 8
HEAD_PAD = 128
HP = N_HEADS * HEAD_PAD
NA_HEAD_DIM = 64
NA_W = N_HEADS * NA_HEAD_DIM
NA_PAIRS = N_HEADS // 2
NA_KH = 8
NA_KW = 16
MLA_NOPE = 64
MLA_ROPE = 32
MLA_V = 64
MLA_Q_LORA = 256
MLA_KV_LORA = 128
ROPE_THETA = 10000.0
POOL_WINDOWS = (2, 4, 8, 16)
POOL_GROUP = D_MODEL // len(POOL_WINDOWS)
POOL_HALO = 8
FFN_DIM = 2816
N_EXPERTS = 8
EXPERT_DIM = 3584
DEPTH = 4
DEEPNORM_ALPHA = (2 * DEPTH) ** 0.25
LN_EPS = 1e-5
RMS_EPS = 1e-6
LOG2E = 1.4426950408889634
NEG_BIG = -1e30

TM = 256
NA_QROWS = 4
NA_KROWS = NA_QROWS + NA_KH - 1
NA_QB = 4
MLA_TQ = 1024
MLA_TK = 640
MLA_UNROLL = 12
MOE_TM = 512
MOE_TF = 1792
VMEM_LIMIT = 56 * 1024 * 1024


def _cparams(sem):
    return pltpu.CompilerParams(dimension_semantics=sem, vmem_limit_bytes=VMEM_LIMIT)


def _layer_norm(v, g, b):
    mu = jnp.mean(v, axis=-1, keepdims=True)
    d = v - mu
    var = jnp.mean(d * d, axis=-1, keepdims=True)
    return d * lax.rsqrt(var + LN_EPS) * g + b


def _silu(v):
    return v * jax.nn.sigmoid(v)


def _mod_kernel(ct_ref, w_ref, b_ref, o_ref):
    s = _silu(ct_ref[...])
    w = w_ref[...]
    b = b_ref[...]
    r0 = jnp.sum(w * s[:, 0:1], axis=0, keepdims=True) + b
    r1 = jnp.sum(w * s[:, 1:2], axis=0, keepdims=True) + b
    o_ref[...] = jnp.concatenate([r0, r1], axis=0)


def _modulation(ct, mod_w, mod_b):
    depth, d, n6 = mod_w.shape
    tn = 1536
    return pl.pallas_call(
        _mod_kernel,
        grid=(depth, n6 // tn),
        in_specs=[
            pl.BlockSpec((d, 2), lambda l, j: (0, 0)),
            pl.BlockSpec((None, d, tn), lambda l, j: (l, 0, j)),
            pl.BlockSpec((None, 1, tn), lambda l, j: (l, 0, j)),
        ],
        out_specs=pl.BlockSpec((None, 2, tn), lambda l, j: (l, 0, j)),
        out_shape=jax.ShapeDtypeStruct((depth, 2, n6), F32),
        compiler_params=_cparams(("arbitrary", "arbitrary")),
        name="modulation",
    )(ct, mod_w, mod_b.reshape(depth, 1, n6))


def _attn_proj_kernel(x_ref, mod_ref, win_ref, qn_ref, wq_ref, kvn_ref, wkv_ref, cos_ref, sin_ref,
                      qa_ref, ka_ref, va_ref, qm_ref, kmt_ref, vm_ref):
    h = x_ref[...] * (1.0 + mod_ref[1:2, :]) + mod_ref[0:1, :]
    p = jnp.dot(h.astype(BF16), win_ref[...], preferred_element_type=F32)
    qa_ref[...] = (p[:, 0:NA_W] * (NA_HEAD_DIM ** -0.5)).astype(BF16)
    ka_ref[...] = p[:, NA_W:2 * NA_W].astype(BF16)
    va_ref[...] = p[:, 2 * NA_W:3 * NA_W].astype(BF16)
    o = 3 * NA_W
    q_c = p[:, o:o + MLA_Q_LORA]
    o += MLA_Q_LORA
    kv_c = p[:, o:o + MLA_KV_LORA]
    o += MLA_KV_LORA
    krp = p[:, o:o + HEAD_PAD]
    krs = p[:, o + HEAD_PAD:o + 2 * HEAD_PAD]
    qn = q_c * lax.rsqrt(jnp.mean(q_c * q_c, axis=-1, keepdims=True) + RMS_EPS) * qn_ref[...]
    kvn = kv_c * lax.rsqrt(jnp.mean(kv_c * kv_c, axis=-1, keepdims=True) + RMS_EPS) * kvn_ref[...]
    q2 = jnp.dot(qn.astype(BF16), wq_ref[...], preferred_element_type=F32)
    kv2 = jnp.dot(kvn.astype(BF16), wkv_ref[...], preferred_element_type=F32)
    cos = cos_ref[...]
    sin = sin_ref[...]
    kr = krp * cos + krs * sin
    q_scale = (MLA_NOPE + MLA_ROPE) ** -0.5 * LOG2E
    for hd in range(N_HEADS):
        lo = hd * HEAD_PAD
        qh = (q2[:, lo:lo + HEAD_PAD] * cos + q2[:, HP + lo:HP + lo + HEAD_PAD] * sin) * q_scale
        qm_ref[:, lo:lo + HEAD_PAD] = qh.astype(BF16)
        kh = kv2[:, lo:lo + HEAD_PAD] + kr
        kmt_ref[hd] = kh.T.astype(BF16)
    lane = lax.broadcasted_iota(jnp.int32, (1, HP), 1)
    ones_col = jnp.where(lane % HEAD_PAD == MLA_V, 1.0, 0.0)
    vm_ref[...] = (kv2[:, HP:2 * HP] + ones_col).astype(BF16)


def _attn_proj(xs, mods, layer, n_lat_tiles, w_in, q_norm, w_q, kv_norm, w_kv, cos_t, sin_t):
    m = xs.shape[0]
    nt = m // TM
    tok = lambda i: (i, 0)
    full = lambda i: (0, 0)
    act = jax.ShapeDtypeStruct((m, HP), BF16)
    na_act = jax.ShapeDtypeStruct((m, NA_W), BF16)
    return pl.pallas_call(
        _attn_proj_kernel,
        grid=(nt,),
        in_specs=[
            pl.BlockSpec((TM, D_MODEL), tok),
            pl.BlockSpec((None, None, 6, D_MODEL), lambda i: (layer, i // n_lat_tiles, 0, 0)),
            pl.BlockSpec(w_in.shape, full),
            pl.BlockSpec(q_norm.shape, full),
            pl.BlockSpec(w_q.shape, full),
            pl.BlockSpec(kv_norm.shape, full),
            pl.BlockSpec(w_kv.shape, full),
            pl.BlockSpec((TM, HEAD_PAD), tok),
            pl.BlockSpec((TM, HEAD_PAD), tok),
        ],
        out_specs=[
            pl.BlockSpec((TM, NA_W), tok),
            pl.BlockSpec((TM, NA_W), tok),
            pl.BlockSpec((TM, NA_W), tok),
            pl.BlockSpec((TM, HP), tok),
            pl.BlockSpec((N_HEADS, HEAD_PAD, TM), lambda i: (0, 0, i)),
            pl.BlockSpec((TM, HP), tok),
        ],
        out_shape=[na_act, na_act, na_act, act, jax.ShapeDtypeStruct((N_HEADS, HEAD_PAD, m), BF16), act],
        compiler_params=_cparams(("arbitrary",)),
        name="attn_proj",
    )(xs, mods, w_in, q_norm, w_q, kv_norm, w_kv, cos_t, sin_t)


_NT_DIMS = (((1,), (1,)), ((), ()))


def _pair_masks():
    lane = lax.broadcasted_iota(jnp.int32, (1, HEAD_PAD), 1)
    return lane < NA_HEAD_DIM, lane >= NA_HEAD_DIM


def _na_kernel(q_ref, k_ref, v_ref, kc_ref, vc_ref, *rest, n_rows):
    bm_refs, o_ref = rest[:NA_QB], rest[NA_QB]
    tq = NA_QROWS * GRID_W
    nk = NA_KROWS * GRID_W
    kc = kc_ref[...]
    vc = vc_ref[...]
    head_lanes = _pair_masks()
    for u in range(NA_QB):
        b = pl.program_id(1) * NA_QB + u
        kr0 = jnp.clip(b * NA_QROWS - NA_KH // 2, 0, n_rows - NA_KROWS)
        start = pl.multiple_of(kr0 * GRID_W, GRID_W)
        q = q_ref[u * tq:(u + 1) * tq, :]
        kw = k_ref[pl.ds(start, nk), :]
        vw = v_ref[pl.ds(start, nk), :]
        outs = []
        for a in range(2):
            qh = jnp.where(head_lanes[a], q, jnp.zeros_like(q))
            s_loc = lax.dot_general(qh, kw, _NT_DIMS, preferred_element_type=F32) + bm_refs[u][a]
            s_ctx = lax.dot_general(qh, kc, _NT_DIMS, preferred_element_type=F32)
            m = jnp.maximum(jnp.max(s_loc, axis=-1, keepdims=True), jnp.max(s_ctx, axis=-1, keepdims=True))
            p_loc = jnp.exp(s_loc - m)
            p_ctx = jnp.exp(s_ctx - m)
            l = jnp.sum(p_loc, axis=-1, keepdims=True) + jnp.sum(p_ctx, axis=-1, keepdims=True)
            o = (jnp.dot(p_loc.astype(BF16), vw, preferred_element_type=F32)
                 + jnp.dot(p_ctx.astype(BF16), vc, preferred_element_type=F32))
            outs.append(o / l)
        o_ref[u * tq:(u + 1) * tq, :] = jnp.where(head_lanes[0], outs[0], outs[1]).astype(BF16)


def _na_attention(qa, ka, va, bias_mask, s):
    n_rows = s // GRID_W
    nb = n_rows // NA_QROWS
    tq = NA_QROWS * GRID_W
    assert nb % NA_QB == 0
    ctx_blk = s // tq
    pat = lambda b: jnp.where(b == 0, 0, jnp.where(b == nb - 1, 2, 1))
    bm_spec = lambda u: pl.BlockSpec((None, 2, tq, NA_KROWS * GRID_W),
                                     lambda hp, j: (pat(j * NA_QB + u), hp, 0, 0))
    return pl.pallas_call(
        functools.partial(_na_kernel, n_rows=n_rows),
        grid=(NA_PAIRS, nb // NA_QB),
        in_specs=[
            pl.BlockSpec((NA_QB * tq, HEAD_PAD), lambda hp, j: (j, hp)),
            pl.BlockSpec((s, HEAD_PAD), lambda hp, j: (0, hp)),
            pl.BlockSpec((s, HEAD_PAD), lambda hp, j: (0, hp)),
            pl.BlockSpec((tq, HEAD_PAD), lambda hp, j: (ctx_blk, hp)),
            pl.BlockSpec((tq, HEAD_PAD), lambda hp, j: (ctx_blk, hp)),
        ] + [bm_spec(u) for u in range(NA_QB)],
        out_specs=pl.BlockSpec((NA_QB * tq, HEAD_PAD), lambda hp, j: (j, hp)),
        out_shape=jax.ShapeDtypeStruct((s, NA_W), BF16),
        compiler_params=_cparams(("arbitrary", "arbitrary")),
        name="na_attention",
    )(qa, ka, va, ka, va, *([bias_mask] * NA_QB))


def _na_ctx_kernel(q_ref, k_ref, v_ref, o_ref):
    q = q_ref[...]
    k = k_ref[...]
    v = v_ref[...]
    head_lanes = _pair_masks()
    outs = []
    for a in range(2):
        qh = jnp.where(head_lanes[a], q, jnp.zeros_like(q))
        s = lax.dot_general(qh, k, _NT_DIMS, preferred_element_type=F32)
        m = jnp.max(s, axis=-1, keepdims=True)
        p = jnp.exp(s - m)
        l = jnp.sum(p, axis=-1, keepdims=True)
        outs.append(jnp.dot(p.astype(BF16), v, preferred_element_type=F32) / l)
    o_ref[...] = jnp.where(head_lanes[0], outs[0], outs[1]).astype(BF16)


def _na_ctx_attention(qa, ka, va, s):
    blk = s // TM
    spec = pl.BlockSpec((TM, HEAD_PAD), lambda hp: (blk, hp))
    return pl.pallas_call(
        _na_ctx_kernel,
        grid=(NA_PAIRS,),
        in_specs=[spec, spec, spec],
        out_specs=pl.BlockSpec((TM, HEAD_PAD), lambda hp: (0, hp)),
        out_shape=jax.ShapeDtypeStruct((TM, NA_W), BF16),
        compiler_params=_cparams(("arbitrary",)),
        name="na_ctx_attention",
    )(qa, ka, va)


def _mla_ctx_kernel(q_ref, kt_ref, v_ref, o_ref):
    s = jnp.dot(q_ref[...], kt_ref[...], preferred_element_type=F32)
    m = jnp.max(s, axis=-1, keepdims=True)
    p = jnp.exp2(s - m)
    l = jnp.sum(p, axis=-1, keepdims=True)
    o = jnp.dot(p.astype(BF16), v_ref[...], preferred_element_type=F32)
    o_ref[...] = (o / l).astype(BF16)


def _mla_ctx_attention(qm, kmt, vm, s):
    blk = s // TM
    return pl.pallas_call(
        _mla_ctx_kernel,
        grid=(N_HEADS,),
        in_specs=[
            pl.BlockSpec((TM, HEAD_PAD), lambda h: (blk, h)),
            pl.BlockSpec((None, HEAD_PAD, TM), lambda h: (h, 0, blk)),
            pl.BlockSpec((TM, HEAD_PAD), lambda h: (blk, h)),
        ],
        out_specs=pl.BlockSpec((TM, HEAD_PAD), lambda h: (0, h)),
        out_shape=jax.ShapeDtypeStruct((TM, HP), BF16),
        compiler_params=_cparams(("arbitrary",)),
        name="mla_ctx_attention",
    )(qm, kmt, vm)


def _mla_kernel(q_ref, kt_ref, v_ref, o_ref, s0_ref, s1_ref, *, n_chunks):
    q = q_ref[...]
    tq = q.shape[0]

    def scores(c, s_ref):
        off = pl.multiple_of(c * MLA_TK, MLA_TK)
        s_ref[...] = jnp.dot(q, kt_ref[:, pl.ds(off, MLA_TK)], preferred_element_type=F32)

    def update(c, s_ref, carry):
        m, acc = carry
        off = pl.multiple_of(c * MLA_TK, MLA_TK)
        s = s_ref[...]
        m_new = jnp.maximum(m, jnp.max(s, axis=-1, keepdims=True))
        alpha = jnp.exp2(m - m_new)
        p = jnp.exp2(s - m_new).astype(BF16)
        acc = alpha * acc + jnp.dot(p, v_ref[pl.ds(off, MLA_TK), :], preferred_element_type=F32)
        return m_new, acc

    bufs = (s0_ref, s1_ref)

    def group(g, carry):
        c0 = g * MLA_UNROLL
        for u in range(MLA_UNROLL):
            scores(c0 + u + 1, bufs[(u + 1) % 2])
            carry = update(c0 + u, bufs[u % 2], carry)
        return carry

    carry = (jnp.full((tq, 1), NEG_BIG, F32), jnp.zeros((tq, HEAD_PAD), F32))
    scores(0, s0_ref)
    n_groups = (n_chunks - 2) // MLA_UNROLL
    carry = lax.fori_loop(0, n_groups, group, carry)
    for c in range(n_groups * MLA_UNROLL, n_chunks):
        if c + 1 < n_chunks:
            scores(c + 1, bufs[(c + 1) % 2])
        carry = update(c, bufs[c % 2], carry)
    _, acc = carry
    o_ref[...] = (acc / acc[:, MLA_V:MLA_V + 1]).astype(BF16)


def _mla_attention(qm, kmt, vm, s):
    m = qm.shape[0]
    n_chunks = m // MLA_TK
    assert n_chunks % 2 == 0 and n_chunks >= 2
    return pl.pallas_call(
        functools.partial(_mla_kernel, n_chunks=n_chunks),
        grid=(N_HEADS, s // MLA_TQ),
        in_specs=[
            pl.BlockSpec((MLA_TQ, HEAD_PAD), lambda h, i: (i, h)),
            pl.BlockSpec((None, HEAD_PAD, m), lambda h, i: (h, 0, 0)),
            pl.BlockSpec((m, HEAD_PAD), lambda h, i: (0, h)),
        ],
        out_specs=pl.BlockSpec((MLA_TQ, HEAD_PAD), lambda h, i: (i, h)),
        out_shape=jax.ShapeDtypeStruct((s, HP), BF16),
        scratch_shapes=[pltpu.VMEM((MLA_TQ, MLA_TK), F32), pltpu.VMEM((MLA_TQ, MLA_TK), F32)],
        compiler_params=_cparams(("arbitrary", "arbitrary")),
        name="mla_attention",
    )(qm, kmt, vm)


def _out_proj_kernel(x_ref, mod_ref, oal_ref, obl_ref, *rest, n_lat_tiles, with_ctx):
    if with_ctx:
        oac_ref, obc_ref, wa_ref, wb_ref, g_ref, b_ref, o_ref = rest
        is_ctx = pl.program_id(0) >= n_lat_tiles
        oa = jnp.where(is_ctx, oac_ref[...], oal_ref[...])
        ob = jnp.where(is_ctx, obc_ref[...], obl_ref[...])
    else:
        wa_ref, wb_ref, g_ref, b_ref, o_ref = rest
        oa = oal_ref[...]
        ob = obl_ref[...]
    y = (jnp.dot(oa, wa_ref[...], preferred_element_type=F32)
         + jnp.dot(ob, wb_ref[...], preferred_element_type=F32))
    v = DEEPNORM_ALPHA * x_ref[...] + mod_ref[2:3, :] * y
    o_ref[...] = _layer_norm(v, g_ref[...], b_ref[...])


def _out_proj(xs, mods, layer, n_lat_tiles, oa_lat, ob_lat, oa_ctx, ob_ctx, wa, wb, ln_g, ln_b):
    with_ctx = oa_ctx is not None
    m = xs.shape[0] if with_ctx else n_lat_tiles * TM
    tok = lambda i: (i, 0)
    lat = lambda i: (jnp.minimum(i, n_lat_tiles - 1), 0)
    full = lambda i: (0, 0)
    ctx_args = [oa_ctx, ob_ctx] if with_ctx else []
    return pl.pallas_call(
        functools.partial(_out_proj_kernel, n_lat_tiles=n_lat_tiles, with_ctx=with_ctx),
        grid=(m // TM,),
        in_specs=[
            pl.BlockSpec((TM, D_MODEL), tok),
            pl.BlockSpec((None, None, 6, D_MODEL), lambda i: (layer, i // n_lat_tiles, 0, 0)),
            pl.BlockSpec((TM, NA_W), lat),
            pl.BlockSpec((TM, HP), lat),
        ] + ([pl.BlockSpec((TM, NA_W), full), pl.BlockSpec((TM, HP), full)] if with_ctx else []) + [
            pl.BlockSpec(wa.shape, full),
            pl.BlockSpec(wb.shape, full),
            pl.BlockSpec((1, D_MODEL), full),
            pl.BlockSpec((1, D_MODEL), full),
        ],
        out_specs=pl.BlockSpec((TM, D_MODEL), tok),
        out_shape=jax.ShapeDtypeStruct((m, D_MODEL), F32),
        compiler_params=_cparams(("arbitrary",)),
        name="out_proj_ln",
    )(xs, mods, oa_lat, ob_lat, *ctx_args, wa, wb, ln_g, ln_b)


def _ffn_kernel(x_ref, mod_ref, wg_ref, wu_ref, wd_ref, g_ref, b_ref, o_ref):
    x = x_ref[...]
    h = (x * (1.0 + mod_ref[4:5, :]) + mod_ref[3:4, :]).astype(BF16)
    gate = jnp.dot(h, wg_ref[...], preferred_element_type=F32)
    up = jnp.dot(h, wu_ref[...], preferred_element_type=F32)
    a = (_silu(gate) * up).astype(BF16)
    y = jnp.dot(a, wd_ref[...], preferred_element_type=F32)
    v = DEEPNORM_ALPHA * x + mod_ref[5:6, :] * y
    o_ref[...] = _layer_norm(v, g_ref[...], b_ref[...])


def _ffn(xs, mods, layer, n_lat_tiles, wg, wu, wd, ln_g, ln_b):
    m = xs.shape[0]
    tok = lambda i: (i, 0)
    full = lambda i: (0, 0)
    return pl.pallas_call(
        _ffn_kernel,
        grid=(m // TM,),
        in_specs=[
            pl.BlockSpec((TM, D_MODEL), tok),
            pl.BlockSpec((None, None, 6, D_MODEL), lambda i: (layer, i // n_lat_tiles, 0, 0)),
            pl.BlockSpec(wg.shape, full),
            pl.BlockSpec(wu.shape, full),
            pl.BlockSpec(wd.shape, full),
            pl.BlockSpec((1, D_MODEL), full),
            pl.BlockSpec((1, D_MODEL), full),
        ],
        out_specs=pl.BlockSpec((TM, D_MODEL), tok),
        out_shape=jax.ShapeDtypeStruct((m, D_MODEL), F32),
        compiler_params=_cparams(("arbitrary",)),
        name="ffn_ln",
    )(xs, mods, wg, wu, wd, ln_g, ln_b)


def _pool_kernel(x_ref, xp_ref, xn_ref, mod_ref, w_ref, sc_ref, g_ref, b_ref, o_ref, ext_ref,
                 *, n_lat_tiles, s):
    i = pl.program_id(0)
    is_ctx = i >= n_lat_tiles
    j = jnp.where(is_ctx, i - n_lat_tiles, i)
    last = jnp.where(is_ctx, 0, n_lat_tiles - 1)
    n_seq = jnp.where(is_ctx, TM, s)
    scale1 = 1.0 + mod_ref[1:2, :]
    shift1 = mod_ref[0:1, :]
    x = x_ref[...]
    h = x * scale1 + shift1
    hp = jnp.where(j != 0, xp_ref[...] * scale1 + shift1, 0.0)
    hn = jnp.where(j != last, xn_ref[...] * scale1 + shift1, 0.0)
    ext_ref[0:POOL_HALO, :] = hp
    ext_ref[POOL_HALO:POOL_HALO + TM, :] = h
    ext_ref[POOL_HALO + TM:POOL_HALO + TM + POOL_HALO, :] = hn
    t = j * TM + lax.broadcasted_iota(jnp.int32, (TM, 1), 0)
    ys = []
    for g, win in enumerate(POOL_WINDOWS):
        half = win // 2
        lo = g * POOL_GROUP
        acc = ext_ref[POOL_HALO - half:POOL_HALO - half + TM, lo:lo + POOL_GROUP]
        for dlt in range(-half + 1, half):
            acc = acc + ext_ref[POOL_HALO + dlt:POOL_HALO + dlt + TM, lo:lo + POOL_GROUP]
        cnt = (jnp.minimum(t + half, n_seq) - jnp.maximum(t - half, 0)).astype(F32)
        mixed = acc / cnt - h[:, lo:lo + POOL_GROUP]
        yg = jnp.dot(mixed.astype(BF16), w_ref[g], preferred_element_type=F32)
        ys.append(yg)
    y = jnp.concatenate(ys, axis=-1) * sc_ref[...]
    v = DEEPNORM_ALPHA * x + mod_ref[2:3, :] * y
    o_ref[...] = _layer_norm(v, g_ref[...], b_ref[...])


def _pool(xs, mods, layer, n_lat_tiles, s, pool_w, pool_scale, ln_g, ln_b):
    m = xs.shape[0]
    per = TM // POOL_HALO
    n_halo_blocks = m // POOL_HALO
    tok = lambda i: (i, 0)
    full = lambda i: (0, 0)
    return pl.pallas_call(
        functools.partial(_pool_kernel, n_lat_tiles=n_lat_tiles, s=s),
        grid=(m // TM,),
        in_specs=[
            pl.BlockSpec((TM, D_MODEL), tok),
            pl.BlockSpec((POOL_HALO, D_MODEL), lambda i: (jnp.maximum(i * per - 1, 0), 0)),
            pl.BlockSpec((POOL_HALO, D_MODEL), lambda i: (jnp.minimum((i + 1) * per, n_halo_blocks - 1), 0)),
            pl.BlockSpec((None, None, 6, D_MODEL), lambda i: (layer, i // n_lat_tiles, 0, 0)),
            pl.BlockSpec(pool_w.shape, lambda i: (0, 0, 0)),
            pl.BlockSpec((1, D_MODEL), full),
            pl.BlockSpec((1, D_MODEL), full),
            pl.BlockSpec((1, D_MODEL), full),
        ],
        out_specs=pl.BlockSpec((TM, D_MODEL), tok),
        out_shape=jax.ShapeDtypeStruct((m, D_MODEL), F32),
        scratch_shapes=[pltpu.VMEM((TM + 2 * POOL_HALO, D_MODEL), F32)],
        compiler_params=_cparams(("arbitrary",)),
        name="pool_ln",
    )(xs, xs, xs, mods, pool_w, pool_scale, ln_g, ln_b)


def _router_kernel(x_ref, mod_ref, rw_ref, tri_ref, idx_ref, wts_ref, cnt_ref, base_ref):
    @pl.when(pl.program_id(0) == 0)
    def _():
        base_ref[...] = jnp.zeros_like(base_ref)

    h = x_ref[...] * (1.0 + mod_ref[4:5, :]) + mod_ref[3:4, :]
    h_hi = h.astype(BF16)
    h_lo = (h - h_hi.astype(F32)).astype(BF16)
    logits = (jnp.dot(h_hi, rw_ref[0], preferred_element_type=F32)
              + jnp.dot(h_hi, rw_ref[1], preferred_element_type=F32)
              + jnp.dot(h_lo, rw_ref[0], preferred_element_type=F32))
    lane = lax.broadcasted_iota(jnp.int32, logits.shape, 1)
    logits = jnp.where(lane < N_EXPERTS, logits, -jnp.inf)
    m1 = jnp.max(logits, axis=-1, keepdims=True)
    i1 = jnp.min(jnp.where(logits == m1, lane, HEAD_PAD), axis=-1, keepdims=True)
    rest = jnp.where(lane == i1, -jnp.inf, logits)
    m2 = jnp.max(rest, axis=-1, keepdims=True)
    i2 = jnp.min(jnp.where(rest == m2, lane, HEAD_PAD), axis=-1, keepdims=True)
    e2 = jnp.exp(m2 - m1)
    w1 = 1.0 / (1.0 + e2)
    w2 = e2 / (1.0 + e2)
    wts_ref[...] = jnp.where(lane == 0, w1, jnp.where(lane == 1, w2, 0.0))

    chosen = jnp.where(lane == i1, 1.0, jnp.where(lane == i2, 1.0, 0.0))
    before = jnp.dot(tri_ref[...], chosen.astype(BF16), preferred_element_type=F32)
    rank = before + base_ref[0:1, :]
    r1 = jnp.sum(jnp.where(lane == i1, rank, 0.0), axis=-1, keepdims=True)
    r2 = jnp.sum(jnp.where(lane == i2, rank, 0.0), axis=-1, keepdims=True)
    base_ref[0:1, :] = base_ref[0:1, :] + jnp.sum(chosen, axis=0, keepdims=True)
    cnt_ref[...] = base_ref[...]
    packed = jnp.where(lane == 0, i1.astype(F32),
                       jnp.where(lane == 1, i2.astype(F32),
                                 jnp.where(lane == 2, r1, jnp.where(lane == 3, r2, 0.0))))
    idx_ref[...] = packed.T[0:8, :].astype(jnp.int32)


def _router(xs, mods, layer, n_lat_tiles, rw):
    m = xs.shape[0]
    nt = m // TM
    tok = lambda i: (i, 0)
    tri = jnp.asarray(np.tril(np.ones((TM, TM), np.float32), -1), dtype=BF16)
    return pl.pallas_call(
        _router_kernel,
        grid=(nt,),
        in_specs=[
            pl.BlockSpec((TM, D_MODEL), tok),
            pl.BlockSpec((None, None, 6, D_MODEL), lambda i: (layer, i // n_lat_tiles, 0, 0)),
            pl.BlockSpec(rw.shape, lambda i: (0, 0, 0)),
            pl.BlockSpec((TM, TM), lambda i: (0, 0)),
        ],
        out_specs=[
            pl.BlockSpec((None, 8, TM), lambda i: (i, 0, 0)),
            pl.BlockSpec((TM, HEAD_PAD), tok),
            pl.BlockSpec((8, HEAD_PAD), lambda i: (0, 0)),
        ],
        out_shape=[
            jax.ShapeDtypeStruct((nt, 8, TM), jnp.int32),
            jax.ShapeDtypeStruct((m, HEAD_PAD), F32),
            jax.ShapeDtypeStruct((8, HEAD_PAD), F32),
        ],
        scratch_shapes=[pltpu.VMEM((8, HEAD_PAD), F32)],
        compiler_params=_cparams(("arbitrary",)),
        name="moe_router",
    )(xs, mods, rw, tri)


def _dispatch_kernel(rows_ref, x_ref, mod_ref, zero_ref, xs_ref, hbuf, sems):
    del zero_ref
    i = pl.program_id(0)
    n = pl.num_programs(0)
    slot = i % 2

    def wait_slot(sl):
        for _ in range(2):
            pltpu.make_async_copy(hbuf.at[sl], xs_ref.at[pl.ds(0, TM)], sems.at[sl]).wait()

    @pl.when(i >= 2)
    def _():
        wait_slot(slot)

    hbuf[slot] = x_ref[...] * (1.0 + mod_ref[4:5, :]) + mod_ref[3:4, :]

    def issue(r, carry):
        for k in range(2):
            dst = rows_ref[0, k * TM + r]
            pltpu.make_async_copy(hbuf.at[slot, pl.ds(r, 1)], xs_ref.at[pl.ds(dst, 1)], sems.at[slot]).start()
        return carry

    lax.fori_loop(0, TM, issue, 0, unroll=8)

    @pl.when(i == n - 1)
    def _():
        wait_slot(slot)

        @pl.when(n >= 2)
        def _():
            wait_slot(1 - slot)


def _dispatch(rows, xs, mods, layer, n_lat_tiles, n_sorted):
    m = xs.shape[0]
    zeros = jnp.zeros((n_sorted, D_MODEL), F32)
    return pl.pallas_call(
        _dispatch_kernel,
        grid=(m // TM,),
        in_specs=[
            pl.BlockSpec((None, 1, 2 * TM), lambda i: (i, 0, 0), memory_space=pltpu.SMEM),
            pl.BlockSpec((TM, D_MODEL), lambda i: (i, 0)),
            pl.BlockSpec((None, None, 6, D_MODEL), lambda i: (layer, i // n_lat_tiles, 0, 0)),
            pl.BlockSpec(memory_space=pl.ANY),
        ],
        out_specs=pl.BlockSpec(memory_space=pl.ANY),
        out_shape=jax.ShapeDtypeStruct((n_sorted, D_MODEL), F32),
        scratch_shapes=[pltpu.VMEM((2, TM, D_MODEL), F32), pltpu.SemaphoreType.DMA((2,))],
        input_output_aliases={3: 0},
        compiler_params=_cparams(("arbitrary",)),
        name="moe_dispatch",
    )(rows, xs, mods, zeros)


def _experts_kernel(te_ref, nu_ref, x_ref, wg_ref, wu_ref, wd_ref, o_ref, xb_ref):
    del te_ref
    f = pl.program_id(1)

    @pl.when((pl.program_id(0) >= nu_ref[0]) & (f == 0))
    def _():
        o_ref[...] = jnp.zeros_like(o_ref)

    @pl.when(pl.program_id(0) < nu_ref[0])
    def _():
        @pl.when(f == 0)
        def _():
            xb_ref[...] = x_ref[...].astype(BF16)

        xb = xb_ref[...]
        gate = jnp.dot(xb, wg_ref[...], preferred_element_type=F32)
        up = jnp.dot(xb, wu_ref[...], preferred_element_type=F32)
        a = (_silu(gate) * up).astype(BF16)
        y = jnp.dot(a, wd_ref[...], preferred_element_type=F32)

        @pl.when(f == 0)
        def _():
            o_ref[...] = y

        @pl.when(f > 0)
        def _():
            o_ref[...] += y


def _experts(tile_expert, n_used, xs_sorted, wg, wu, wd, j):
    n_sorted = xs_sorted.shape[0]
    nf = EXPERT_DIM // MOE_TF
    row_blk = lambda i, f, te, nu: (jnp.minimum(i, nu[0] - 1), 0)
    f_blk = lambda i, f, nu: jnp.where(i < nu[0], f, nf - 1)
    grid_spec = pltpu.PrefetchScalarGridSpec(
        num_scalar_prefetch=2,
        grid=(n_sorted // MOE_TM, nf),
        in_specs=[
            pl.BlockSpec((MOE_TM, D_MODEL), row_blk),
            pl.BlockSpec((None, None, D_MODEL, MOE_TF), lambda i, f, te, nu: (j, te[i], 0, f_blk(i, f, nu))),
            pl.BlockSpec((None, None, D_MODEL, MOE_TF), lambda i, f, te, nu: (j, te[i], 0, f_blk(i, f, nu))),
            pl.BlockSpec((None, None, MOE_TF, D_MODEL), lambda i, f, te, nu: (j, te[i], f_blk(i, f, nu), 0)),
        ],
        out_specs=pl.BlockSpec((MOE_TM, D_MODEL), lambda i, f, te, nu: (i, 0)),
        scratch_shapes=[pltpu.VMEM((MOE_TM, D_MODEL), BF16)],
    )
    return pl.pallas_call(
        _experts_kernel,
        grid_spec=grid_spec,
        out_shape=jax.ShapeDtypeStruct((n_sorted, D_MODEL), F32),
        compiler_params=_cparams(("arbitrary", "arbitrary")),
        name="moe_experts",
    )(tile_expert, n_used, xs_sorted, wg, wu, wd)


def _combine_kernel(rows_ref, rowsn_ref, wts_ref, x_ref, mod_ref, g_ref, b_ref, ys_ref, o_ref, ybuf, sems):
    i = pl.program_id(0)
    n = pl.num_programs(0)
    slot = i % 2

    def fetch(ids_ref, sl):
        def issue(r, carry):
            for k in range(2):
                src = ids_ref[0, k * TM + r]
                pltpu.make_async_copy(ys_ref.at[pl.ds(src, 1)], ybuf.at[sl, k, pl.ds(r, 1)], sems.at[sl]).start()
            return carry

        lax.fori_loop(0, TM, issue, 0, unroll=8)

    @pl.when(i == 0)
    def _():
        fetch(rows_ref, 0)

    @pl.when(i + 1 < n)
    def _():
        fetch(rowsn_ref, 1 - slot)

    for k in range(2):
        pltpu.make_async_copy(ys_ref.at[pl.ds(0, TM)], ybuf.at[slot, k], sems.at[slot]).wait()

    wts = wts_ref[...]
    y = wts[:, 0:1] * ybuf[slot, 0] + wts[:, 1:2] * ybuf[slot, 1]
    v = DEEPNORM_ALPHA * x_ref[...] + mod_ref[5:6, :] * y
    o_ref[...] = _layer_norm(v, g_ref[...], b_ref[...])


def _combine(rows, wts, xs, mods, layer, n_lat_tiles, ys_sorted, ln_g, ln_b):
    m = xs.shape[0]
    nt = m // TM
    tok = lambda i: (i, 0)
    full = lambda i: (0, 0)
    return pl.pallas_call(
        _combine_kernel,
        grid=(nt,),
        in_specs=[
            pl.BlockSpec((None, 1, 2 * TM), lambda i: (i, 0, 0), memory_space=pltpu.SMEM),
            pl.BlockSpec((None, 1, 2 * TM), lambda i: (jnp.minimum(i + 1, nt - 1), 0, 0),
                         memory_space=pltpu.SMEM),
            pl.BlockSpec((TM, HEAD_PAD), tok),
            pl.BlockSpec((TM, D_MODEL), tok),
            pl.BlockSpec((None, None, 6, D_MODEL), lambda i: (layer, i // n_lat_tiles, 0, 0)),
            pl.BlockSpec((1, D_MODEL), full),
            pl.BlockSpec((1, D_MODEL), full),
            pl.BlockSpec(memory_space=pl.ANY),
        ],
        out_specs=pl.BlockSpec((TM, D_MODEL), tok),
        out_shape=jax.ShapeDtypeStruct((m, D_MODEL), F32),
        scratch_shapes=[pltpu.VMEM((2, 2, TM, D_MODEL), F32), pltpu.SemaphoreType.DMA((2,))],
        compiler_params=_cparams(("arbitrary",)),
        name="moe_combine_ln",
    )(rows, rows, wts, xs, mods, ln_g, ln_b, ys_sorted)


def _moe_layer(xs, mods, layer, n_lat_tiles, router_w, wg, wu, wd, j, ln_g, ln_b):
    m = xs.shape[0]
    n_tiles = (2 * m + N_EXPERTS * (MOE_TM - 1) + MOE_TM - 1) // MOE_TM
    n_sorted = n_tiles * MOE_TM
    rw = jnp.pad(router_w, [(0, 0), (0, HEAD_PAD - N_EXPERTS)])
    rw_hi = rw.astype(BF16)
    rw = jnp.stack([rw_hi, (rw - rw_hi.astype(F32)).astype(BF16)])
    idx, wts, cnt = _router(xs, mods, layer, n_lat_tiles, rw)
    counts = cnt[0, :N_EXPERTS].astype(jnp.int32)
    padded = (counts + MOE_TM - 1) // MOE_TM * MOE_TM
    ends = jnp.cumsum(padded)
    offs = ends - padded
    n_used = (ends[-1:] // MOE_TM).astype(jnp.int32)
    tile_row = jnp.arange(n_tiles, dtype=jnp.int32) * MOE_TM
    tile_row = jnp.minimum(tile_row, ends[-1] - MOE_TM)
    tile_expert = jnp.sum((tile_row[:, None] >= ends[None, :]).astype(jnp.int32), axis=1)
    expert_ids = idx[:, 0:2, :]
    group_off = jnp.sum(jnp.where(expert_ids[..., None] == jnp.arange(N_EXPERTS), offs, 0), axis=-1)
    rows = (group_off + idx[:, 2:4, :]).reshape(m // TM, 1, 2 * TM)
    xs_sorted = _dispatch(rows, xs, mods, layer, n_lat_tiles, n_sorted)
    ys_sorted = _experts(tile_expert, n_used, xs_sorted, wg, wu, wd, j)
    return _combine(rows, wts, xs, mods, layer, n_lat_tiles, ys_sorted, ln_g, ln_b)


_ROPE_SWAP = np.array(list(range(8, 16)) + list(range(0, 8)) + list(range(24, 32)) + list(range(16, 24)))


def _prep_attn_weights(w_in, w_q_up, w_kv_up, w_out):
    o = 3 * NA_W
    w_qc = w_in[:, o:o + MLA_Q_LORA]
    o += MLA_Q_LORA
    w_kvc = w_in[:, o:o + MLA_KV_LORA]
    o += MLA_KV_LORA
    w_kr = w_in[:, o:o + MLA_ROPE]
    rope_pad = [(0, 0), (MLA_NOPE, HEAD_PAD - MLA_NOPE - MLA_ROPE)]
    w_in_aug = jnp.concatenate(
        [w_in[:, :3 * NA_W], w_qc, w_kvc, jnp.pad(w_kr, rope_pad), jnp.pad(w_kr[:, _ROPE_SWAP], rope_pad)],
        axis=1).astype(BF16)

    wq = w_q_up.reshape(MLA_Q_LORA, N_HEADS, MLA_NOPE + MLA_ROPE)
    wq_full = jnp.pad(wq, [(0, 0), (0, 0), (0, HEAD_PAD - MLA_NOPE - MLA_ROPE)])
    wq_swap = jnp.pad(wq[:, :, MLA_NOPE:][:, :, _ROPE_SWAP], [(0, 0), (0, 0), rope_pad[1]])
    w_q = jnp.concatenate([wq_full.reshape(MLA_Q_LORA, HP), wq_swap.reshape(MLA_Q_LORA, HP)], axis=1).astype(BF16)

    wkv = w_kv_up.reshape(MLA_KV_LORA, N_HEADS, MLA_NOPE + MLA_V)
    wkn = jnp.pad(wkv[:, :, :MLA_NOPE], [(0, 0), (0, 0), (0, HEAD_PAD - MLA_NOPE)])
    wv = jnp.pad(wkv[:, :, MLA_NOPE:], [(0, 0), (0, 0), (0, HEAD_PAD - MLA_V)])
    w_kv = jnp.concatenate([wkn.reshape(MLA_KV_LORA, HP), wv.reshape(MLA_KV_LORA, HP)], axis=1).astype(BF16)

    wa = w_out[:NA_W].astype(BF16)
    wb = w_out[NA_W:].reshape(N_HEADS, MLA_V, D_MODEL)
    wb = jnp.pad(wb, [(0, 0), (0, HEAD_PAD - MLA_V), (0, 0)]).reshape(HP, D_MODEL).astype(BF16)
    return w_in_aug, w_q, w_kv, wa, wb


def _rope_tables(s, n_ctx):
    t = jnp.arange(s, dtype=jnp.int32)
    row = (t // GRID_W).astype(F32)
    col = (t % GRID_W).astype(F32)
    n_freq = MLA_ROPE // 4
    inv = 1.0 / (ROPE_THETA ** (jnp.arange(n_freq, dtype=F32) / n_freq))
    ar = row[:, None] * inv
    ac = col[:, None] * inv
    cos = jnp.concatenate([jnp.cos(ar), jnp.cos(ar), jnp.cos(ac), jnp.cos(ac)], axis=1)
    sin = jnp.concatenate([-jnp.sin(ar), jnp.sin(ar), -jnp.sin(ac), jnp.sin(ac)], axis=1)
    right = HEAD_PAD - MLA_NOPE - MLA_ROPE
    cos = jnp.pad(cos, [(0, n_ctx), (MLA_NOPE, right)], constant_values=1.0)
    sin = jnp.pad(sin, [(0, n_ctx), (MLA_NOPE, right)])
    return cos, sin


def _na_bias_mask(rel_bias, n_rows):
    nb = n_rows // NA_QROWS
    c = np.arange(GRID_W)
    cs = np.clip(c - NA_KW // 2, 0, GRID_W - NA_KW)
    kc = np.arange(GRID_W)
    ok_c = (kc[None, :] >= cs[:, None]) & (kc[None, :] < cs[:, None] + NA_KW)
    n_dcol = 2 * NA_KW - 1
    left = GRID_W - NA_KW
    period = 2 * GRID_W
    u = jnp.pad(rel_bias, [(0, 0), (0, 0), (left, period - n_dcol - left)])
    flat = jnp.tile(u, (1, 1, GRID_W + 1))[:, :, :GRID_W * (period - 1)]
    toep = flat.reshape(N_HEADS, 2 * NA_KH - 1, GRID_W, period - 1)[:, :, :, GRID_W - 1:]
    toep = jnp.where(ok_c, toep, NEG_BIG)
    masked = jnp.full((N_HEADS, GRID_W, GRID_W), NEG_BIG, F32)
    out = []
    for b in (0, 1, nb - 1):
        kr0 = int(np.clip(b * NA_QROWS - NA_KH // 2, 0, n_rows - NA_KROWS))
        q_blocks = []
        for qr in range(NA_QROWS):
            r = b * NA_QROWS + qr
            rs = int(np.clip(r - NA_KH // 2, 0, n_rows - NA_KH))
            k_blocks = []
            for klr in range(NA_KROWS):
                kr = kr0 + klr
                k_blocks.append(toep[:, kr - r + NA_KH - 1] if rs <= kr < rs + NA_KH else masked)
            q_blocks.append(jnp.concatenate(k_blocks, axis=-1))
        out.append(jnp.concatenate(q_blocks, axis=1))
    return jnp.stack(out, axis=0)


def kernel(x, c, ctx, c_ctx, mod_w, mod_b, ln1_g, ln1_b, ln2_g, ln2_b, attn_w_in, na_rel_bias, mla_q_norm,
           mla_w_q_up, mla_kv_norm, mla_w_kv_up, attn_w_out, ffn_w_gate, ffn_w_up, ffn_w_down, pool_w,
           pool_scale, moe_router, moe_w_gate, moe_w_up, moe_w_down):
    assert x.shape[0] == 1 and c.shape[0] == 1 and ctx.shape[0] == 1
    s = x.shape[1]
    n_ctx = ctx.shape[1]
    assert n_ctx == TM and s % (NA_QROWS * GRID_W) == 0 and s % MLA_TQ == 0
    assert (s + n_ctx) % MLA_TK == 0
    n_lat_tiles = s // TM
    depth = mod_w.shape[0]

    xs = jnp.concatenate([x[0], ctx[0]], axis=0)
    ct = jnp.stack([c[0], c_ctx], axis=1)
    mods = _modulation(ct, mod_w, mod_b).reshape(depth, 2, 6, D_MODEL)
    cos_t, sin_t = _rope_tables(s, n_ctx)
    row = lambda v: v.reshape(1, -1)
    moe_wg, moe_wu, moe_wd = moe_w_gate.astype(BF16), moe_w_up.astype(BF16), moe_w_down.astype(BF16)

    for i in range(depth):
        j = i // 2
        ctx_live = any(l % 2 == 0 for l in range(i + 1, depth))
        if i % 2 == 0:
            w_in, w_q, w_kv, wa, wb = _prep_attn_weights(attn_w_in[j], mla_w_q_up[j], mla_w_kv_up[j],
                                                         attn_w_out[j])
            qa, ka, va, qm, kmt, vm = _attn_proj(xs, mods, i, n_lat_tiles, w_in, row(mla_q_norm[j]), w_q,
                                                 row(mla_kv_norm[j]), w_kv, cos_t, sin_t)
            bias_mask = _na_bias_mask(na_rel_bias[j], s // GRID_W)
            oa_lat = _na_attention(qa, ka, va, bias_mask, s)
            ob_lat = _mla_attention(qm, kmt, vm, s)
            oa_ctx = ob_ctx = None
            if ctx_live:
                oa_ctx = _na_ctx_attention(qa, ka, va, s)
                ob_ctx = _mla_ctx_attention(qm, kmt, vm, s)
            xs = _out_proj(xs, mods, i, n_lat_tiles, oa_lat, ob_lat, oa_ctx, ob_ctx, wa, wb,
                           row(ln1_g[i]), row(ln1_b[i]))
            xs = _ffn(xs, mods, i, n_lat_tiles, ffn_w_gate[j].astype(BF16), ffn_w_up[j].astype(BF16),
                      ffn_w_down[j].astype(BF16), row(ln2_g[i]), row(ln2_b[i]))
        else:
            xs = _pool(xs, mods, i, n_lat_tiles, s, pool_w[j].astype(BF16), row(pool_scale[j]),
                       row(ln1_g[i]), row(ln1_b[i]))
            xs = _moe_layer(xs, mods, i, n_lat_tiles, moe_router[j], moe_wg, moe_wu, moe_wd, j,
                            row(ln2_g[i]), row(ln2_b[i]))
    return xs[:s][None] if xs.shape[0] != s else xs[None]
```

```python
import functools

import numpy as np
import jax
import jax.numpy as jnp
from jax import lax
from jax.experimental import pallas as pl
from jax.experimental.pallas import tpu as pltpu

F32 = jnp.float32
BF16 = jnp.bfloat16

D_MODEL = 1024
GRID_W = 64
N_HEADS = 8
HEAD_PAD = 128
HP = N_HEADS * HEAD_PAD
NA_HEAD_DIM = 64
NA_W = N_HEADS * NA_HEAD_DIM
NA_PAIRS = N_HEADS // 2
NA_KH = 8
NA_KW = 16
MLA_NOPE = 64
MLA_ROPE = 32
MLA_V = 64
MLA_Q_LORA = 256
MLA_KV_LORA = 128
ROPE_THETA = 10000.0
POOL_WINDOWS = (2, 4, 8, 16)
POOL_GROUP = D_MODEL // len(POOL_WINDOWS)
POOL_HALO = 8
FFN_DIM = 2816
N_EXPERTS = 8
EXPERT_DIM = 3584
DEPTH = 4
DEEPNORM_ALPHA = (2 * DEPTH) ** 0.25
LN_EPS = 1e-5
RMS_EPS = 1e-6
LOG2E = 1.4426950408889634
NEG_BIG = -1e30

TM = 256
NA_QROWS = 4
NA_KROWS = NA_QROWS + NA_KH - 1
NA_QB = 4
MLA_TQ = 1024
MLA_TK = 640
MLA_UNROLL = 12
MOE_TM = 512
MOE_TF = 1792
VMEM_LIMIT = 56 * 1024 * 1024


def _cparams(sem):
    return pltpu.CompilerParams(dimension_semantics=sem, vmem_limit_bytes=VMEM_LIMIT)


def _layer_norm(v, g, b):
    mu = jnp.mean(v, axis=-1, keepdims=True)
    d = v - mu
    var = jnp.mean(d * d, axis=-1, keepdims=True)
    return d * lax.rsqrt(var + LN_EPS) * g + b


def _silu(v):
    return v * jax.nn.sigmoid(v)


def _mod_kernel(ct_ref, w_ref, b_ref, o_ref):
    s = _silu(ct_ref[...])
    w = w_ref[...]
    b = b_ref[...]
    r0 = jnp.sum(w * s[:, 0:1], axis=0, keepdims=True) + b
    r1 = jnp.sum(w * s[:, 1:2], axis=0, keepdims=True) + b
    o_ref[...] = jnp.concatenate([r0, r1], axis=0)


def _modulation(ct, mod_w, mod_b):
    depth, d, n6 = mod_w.shape
    tn = 1536
    return pl.pallas_call(
        _mod_kernel,
        grid=(depth, n6 // tn),
        in_specs=[
            pl.BlockSpec((d, 2), lambda l, j: (0, 0)),
            pl.BlockSpec((None, d, tn), lambda l, j: (l, 0, j)),
            pl.BlockSpec((None, 1, tn), lambda l, j: (l, 0, j)),
        ],
        out_specs=pl.BlockSpec((None, 2, tn), lambda l, j: (l, 0, j)),
        out_shape=jax.ShapeDtypeStruct((depth, 2, n6), F32),
        compiler_params=_cparams(("arbitrary", "arbitrary")),
        name="modulation",
    )(ct, mod_w, mod_b.reshape(depth, 1, n6))


def _attn_proj_kernel(x_ref, mod_ref, win_ref, qn_ref, wq_ref, kvn_ref, wkv_ref, cos_ref, sin_ref,
                      qa_ref, ka_ref, va_ref, qm_ref, kmt_ref, vm_ref):
    h = x_ref[...] * (1.0 + mod_ref[1:2, :]) + mod_ref[0:1, :]
    p = jnp.dot(h.astype(BF16), win_ref[...], preferred_element_type=F32)
    qa_ref[...] = (p[:, 0:NA_W] * (NA_HEAD_DIM ** -0.5)).astype(BF16)
    ka_ref[...] = p[:, NA_W:2 * NA_W].astype(BF16)
    va_ref[...] = p[:, 2 * NA_W:3 * NA_W].astype(BF16)
    o = 3 * NA_W
    q_c = p[:, o:o + MLA_Q_LORA]
    o += MLA_Q_LORA
    kv_c = p[:, o:o + MLA_KV_LORA]
    o += MLA_KV_LORA
    krp = p[:, o:o + HEAD_PAD]
    krs = p[:, o + HEAD_PAD:o + 2 * HEAD_PAD]
    qn = q_c * lax.rsqrt(jnp.mean(q_c * q_c, axis=-1, keepdims=True) + RMS_EPS) * qn_ref[...]
    kvn = kv_c * lax.rsqrt(jnp.mean(kv_c * kv_c, axis=-1, keepdims=True) + RMS_EPS) * kvn_ref[...]
    q2 = jnp.dot(qn.astype(BF16), wq_ref[...], preferred_element_type=F32)
    kv2 = jnp.dot(kvn.astype(BF16), wkv_ref[...], preferred_element_type=F32)
    cos = cos_ref[...]
    sin = sin_ref[...]
    kr = krp * cos + krs * sin
    q_scale = (MLA_NOPE + MLA_ROPE) ** -0.5 * LOG2E
    for hd in range(N_HEADS):
        lo = hd * HEAD_PAD
        qh = (q2[:, lo:lo + HEAD_PAD] * cos + q2[:, HP + lo:HP + lo + HEAD_PAD] * sin) * q_scale
        qm_ref[:, lo:lo + HEAD_PAD] = qh.astype(BF16)
        kh = kv2[:, lo:lo + HEAD_PAD] + kr
        kmt_ref[hd] = kh.T.astype(BF16)
    lane = lax.broadcasted_iota(jnp.int32, (1, HP), 1)
    ones_col = jnp.where(lane % HEAD_PAD == MLA_V, 1.0, 0.0)
    vm_ref[...] = (kv2[:, HP:2 * HP] + ones_col).astype(BF16)


def _attn_proj(xs, mods, layer, n_lat_tiles, w_in, q_norm, w_q, kv_norm, w_kv, cos_t, sin_t):
    m = xs.shape[0]
    nt = m // TM
    tok = lambda i: (i, 0)
    full = lambda i: (0, 0)
    act = jax.ShapeDtypeStruct((m, HP), BF16)
    na_act = jax.ShapeDtypeStruct((m, NA_W), BF16)
    return pl.pallas_call(
        _attn_proj_kernel,
        grid=(nt,),
        in_specs=[
            pl.BlockSpec((TM, D_MODEL), tok),
            pl.BlockSpec((None, None, 6, D_MODEL), lambda i: (layer, i // n_lat_tiles, 0, 0)),
            pl.BlockSpec(w_in.shape, full),
            pl.BlockSpec(q_norm.shape, full),
            pl.BlockSpec(w_q.shape, full),
            pl.BlockSpec(kv_norm.shape, full),
            pl.BlockSpec(w_kv.shape, full),
            pl.BlockSpec((TM, HEAD_PAD), tok),
            pl.BlockSpec((TM, HEAD_PAD), tok),
        ],
        out_specs=[
            pl.BlockSpec((TM, NA_W), tok),
            pl.BlockSpec((TM, NA_W), tok),
            pl.BlockSpec((TM, NA_W), tok),
            pl.BlockSpec((TM, HP), tok),
            pl.BlockSpec((N_HEADS, HEAD_PAD, TM), lambda i: (0, 0, i)),
            pl.BlockSpec((TM, HP), tok),
        ],
        out_shape=[na_act, na_act, na_act, act, jax.ShapeDtypeStruct((N_HEADS, HEAD_PAD, m), BF16), act],
        compiler_params=_cparams(("arbitrary",)),
        name="attn_proj",
    )(xs, mods, w_in, q_norm, w_q, kv_norm, w_kv, cos_t, sin_t)


_NT_DIMS = (((1,), (1,)), ((), ()))


def _pair_masks():
    lane = lax.broadcasted_iota(jnp.int32, (1, HEAD_PAD), 1)
    return lane < NA_HEAD_DIM, lane >= NA_HEAD_DIM


def _na_kernel(q_ref, k_ref, v_ref, kc_ref, vc_ref, *rest, n_rows):
    bm_refs, o_ref = rest[:NA_QB], rest[NA_QB]
    tq = NA_QROWS * GRID_W
    nk = NA_KROWS * GRID_W
    kc = kc_ref[...]
    vc = vc_ref[...]
    head_lanes = _pair_masks()
    for u in range(NA_QB):
        b = pl.program_id(1) * NA_QB + u
        kr0 = jnp.clip(b * NA_QROWS - NA_KH // 2, 0, n_rows - NA_KROWS)
        start = pl.multiple_of(kr0 * GRID_W, GRID_W)
        q = q_ref[u * tq:(u + 1) * tq, :]
        kw = k_ref[pl.ds(start, nk), :]
        vw = v_ref[pl.ds(start, nk), :]
        outs = []
        for a in range(2):
            qh = jnp.where(head_lanes[a], q, jnp.zeros_like(q))
            s_loc = lax.dot_general(qh, kw, _NT_DIMS, preferred_element_type=F32) + bm_refs[u][a]
            s_ctx = lax.dot_general(qh, kc, _NT_DIMS, preferred_element_type=F32)
            m = jnp.maximum(jnp.max(s_loc, axis=-1, keepdims=True), jnp.max(s_ctx, axis=-1, keepdims=True))
            p_loc = jnp.exp(s_loc - m)
            p_ctx = jnp.exp(s_ctx - m)
            l = jnp.sum(p_loc, axis=-1, keepdims=True) + jnp.sum(p_ctx, axis=-1, keepdims=True)
            o = (jnp.dot(p_loc.astype(BF16), vw, preferred_element_type=F32)
                 + jnp.dot(p_ctx.astype(BF16), vc, preferred_element_type=F32))
            outs.append(o / l)
        o_ref[u * tq:(u + 1) * tq, :] = jnp.where(head_lanes[0], outs[0], outs[1]).astype(BF16)


def _na_attention(qa, ka, va, bias_mask, s):
    n_rows = s // GRID_W
    nb = n_rows // NA_QROWS
    tq = NA_QROWS * GRID_W
    assert nb % NA_QB == 0
    ctx_blk = s // tq
    pat = lambda b: jnp.where(b == 0, 0, jnp.where(b == nb - 1, 2, 1))
    bm_spec = lambda u: pl.BlockSpec((None, 2, tq, NA_KROWS * GRID_W),
                                     lambda hp, j: (pat(j * NA_QB + u), hp, 0, 0))
    return pl.pallas_call(
        functools.partial(_na_kernel, n_rows=n_rows),
        grid=(NA_PAIRS, nb // NA_QB),
        in_specs=[
            pl.BlockSpec((NA_QB * tq, HEAD_PAD), lambda hp, j: (j, hp)),
            pl.BlockSpec((s, HEAD_PAD), lambda hp, j: (0, hp)),
            pl.BlockSpec((s, HEAD_PAD), lambda hp, j: (0, hp)),
            pl.BlockSpec((tq, HEAD_PAD), lambda hp, j: (ctx_blk, hp)),
            pl.BlockSpec((tq, HEAD_PAD), lambda hp, j: (ctx_blk, hp)),
        ] + [bm_spec(u) for u in range(NA_QB)],
        out_specs=pl.BlockSpec((NA_QB * tq, HEAD_PAD), lambda hp, j: (j, hp)),
        out_shape=jax.ShapeDtypeStruct((s, NA_W), BF16),
        compiler_params=_cparams(("arbitrary", "arbitrary")),
        name="na_attention",
    )(qa, ka, va, ka, va, *([bias_mask] * NA_QB))


def _na_ctx_kernel(q_ref, k_ref, v_ref, o_ref):
    q = q_ref[...]
    k = k_ref[...]
    v = v_ref[...]
    head_lanes = _pair_masks()
    outs = []
    for a in range(2):
        qh = jnp.where(head_lanes[a], q, jnp.zeros_like(q))
        s = lax.dot_general(qh, k, _NT_DIMS, preferred_element_type=F32)
        m = jnp.max(s, axis=-1, keepdims=True)
        p = jnp.exp(s - m)
        l = jnp.sum(p, axis=-1, keepdims=True)
        outs.append(jnp.dot(p.astype(BF16), v, preferred_element_type=F32) / l)
    o_ref[...] = jnp.where(head_lanes[0], outs[0], outs[1]).astype(BF16)


def _na_ctx_attention(qa, ka, va, s):
    blk = s // TM
    spec = pl.BlockSpec((TM, HEAD_PAD), lambda hp: (blk, hp))
    return pl.pallas_call(
        _na_ctx_kernel,
        grid=(NA_PAIRS,),
        in_specs=[spec, spec, spec],
        out_specs=pl.BlockSpec((TM, HEAD_PAD), lambda hp: (0, hp)),
        out_shape=jax.ShapeDtypeStruct((TM, NA_W), BF16),
        compiler_params=_cparams(("arbitrary",)),
        name="na_ctx_attention",
    )(qa, ka, va)


def _mla_ctx_kernel(q_ref, kt_ref, v_ref, o_ref):
    s = jnp.dot(q_ref[...], kt_ref[...], preferred_element_type=F32)
    m = jnp.max(s, axis=-1, keepdims=True)
    p = jnp.exp2(s - m)
    l = jnp.sum(p, axis=-1, keepdims=True)
    o = jnp.dot(p.astype(BF16), v_ref[...], preferred_element_type=F32)
    o_ref[...] = (o / l).astype(BF16)


def _mla_ctx_attention(qm, kmt, vm, s):
    blk = s // TM
    return pl.pallas_call(
        _mla_ctx_kernel,
        grid=(N_HEADS,),
        in_specs=[
            pl.BlockSpec((TM, HEAD_PAD), lambda h: (blk, h)),
            pl.BlockSpec((None, HEAD_PAD, TM), lambda h: (h, 0, blk)),
            pl.BlockSpec((TM, HEAD_PAD), lambda h: (blk, h)),
        ],
        out_specs=pl.BlockSpec((TM, HEAD_PAD), lambda h: (0, h)),
        out_shape=jax.ShapeDtypeStruct((TM, HP), BF16),
        compiler_params=_cparams(("arbitrary",)),
        name="mla_ctx_attention",
    )(qm, kmt, vm)


def _mla_kernel(q_ref, kt_ref, v_ref, *rest, n_chunks, n_cast):
    cast_in, o_ref, cast_out = rest[:n_cast], rest[n_cast], rest[n_cast + 1:2 * n_cast + 1]
    s0_ref, s1_ref = rest[2 * n_cast + 1:]
    for w_ref, wb_ref in zip(cast_in, cast_out):
        wb_ref[...] = w_ref[...].astype(BF16)
    q = q_ref[...]
    tq = q.shape[0]

    def scores(c, s_ref):
        off = pl.multiple_of(c * MLA_TK, MLA_TK)
        s_ref[...] = jnp.dot(q, kt_ref[:, pl.ds(off, MLA_TK)], preferred_element_type=F32)

    def update(c, s_ref, carry):
        m, acc = carry
        off = pl.multiple_of(c * MLA_TK, MLA_TK)
        s = s_ref[...]
        m_new = jnp.maximum(m, jnp.max(s, axis=-1, keepdims=True))
        alpha = jnp.exp2(m - m_new)
        p = jnp.exp2(s - m_new).astype(BF16)
        acc = alpha * acc + jnp.dot(p, v_ref[pl.ds(off, MLA_TK), :], preferred_element_type=F32)
        return m_new, acc

    bufs = (s0_ref, s1_ref)

    def group(g, carry):
        c0 = g * MLA_UNROLL
        for u in range(MLA_UNROLL):
            scores(c0 + u + 1, bufs[(u + 1) % 2])
            carry = update(c0 + u, bufs[u % 2], carry)
        return carry

    carry = (jnp.full((tq, 1), NEG_BIG, F32), jnp.zeros((tq, HEAD_PAD), F32))
    scores(0, s0_ref)
    n_groups = (n_chunks - 2) // MLA_UNROLL
    carry = lax.fori_loop(0, n_groups, group, carry)
    for c in range(n_groups * MLA_UNROLL, n_chunks):
        if c + 1 < n_chunks:
            scores(c + 1, bufs[(c + 1) % 2])
        carry = update(c, bufs[c % 2], carry)
    _, acc = carry
    o_ref[...] = (acc / acc[:, MLA_V:MLA_V + 1]).astype(BF16)


def _mla_attention(qm, kmt, vm, s, cast_weights=(), cast_layer=0):
    m = qm.shape[0]
    n_chunks = m // MLA_TK
    assert n_chunks % 2 == 0 and n_chunks >= 2
    n_q = s // MLA_TQ
    assert N_HEADS == N_EXPERTS or not cast_weights
    cast_in_specs, cast_out_specs, cast_shapes = [], [], []
    for w in cast_weights:
        rows, cols = w.shape[2], w.shape[3]
        slab = rows // n_q
        assert rows % n_q == 0 and slab % 16 == 0
        cast_in_specs.append(pl.BlockSpec((None, None, slab, cols), lambda h, i: (cast_layer, h, i, 0)))
        cast_out_specs.append(pl.BlockSpec((None, slab, cols), lambda h, i: (h, i, 0)))
        cast_shapes.append(jax.ShapeDtypeStruct((N_EXPERTS, rows, cols), BF16))
    outs = pl.pallas_call(
        functools.partial(_mla_kernel, n_chunks=n_chunks, n_cast=len(cast_weights)),
        grid=(N_HEADS, n_q),
        in_specs=[
            pl.BlockSpec((MLA_TQ, HEAD_PAD), lambda h, i: (i, h)),
            pl.BlockSpec((None, HEAD_PAD, m), lambda h, i: (h, 0, 0)),
            pl.BlockSpec((m, HEAD_PAD), lambda h, i: (0, h)),
        ] + cast_in_specs,
        out_specs=[pl.BlockSpec((MLA_TQ, HEAD_PAD), lambda h, i: (i, h))] + cast_out_specs,
        out_shape=[jax.ShapeDtypeStruct((s, HP), BF16)] + cast_shapes,
        scratch_shapes=[pltpu.VMEM((MLA_TQ, MLA_TK), F32), pltpu.VMEM((MLA_TQ, MLA_TK), F32)],
        compiler_params=_cparams(("arbitrary", "arbitrary")),
        name="mla_attention",
    )(qm, kmt, vm, *cast_weights)
    return outs[0], tuple(outs[1:])


def _out_proj_kernel(x_ref, mod_ref, oal_ref, obl_ref, *rest, n_lat_tiles, with_ctx):
    if with_ctx:
        oac_ref, obc_ref, wa_ref, wb_ref, g_ref, b_ref, o_ref = rest
        is_ctx = pl.program_id(0) >= n_lat_tiles
        oa = jnp.where(is_ctx, oac_ref[...], oal_ref[...])
        ob = jnp.where(is_ctx, obc_ref[...], obl_ref[...])
    else:
        wa_ref, wb_ref, g_ref, b_ref, o_ref = rest
        oa = oal_ref[...]
        ob = obl_ref[...]
    y = (jnp.dot(oa, wa_ref[...], preferred_element_type=F32)
         + jnp.dot(ob, wb_ref[...], preferred_element_type=F32))
    v = DEEPNORM_ALPHA * x_ref[...] + mod_ref[2:3, :] * y
    o_ref[...] = _layer_norm(v, g_ref[...], b_ref[...])


def _out_proj(xs, mods, layer, n_lat_tiles, oa_lat, ob_lat, oa_ctx, ob_ctx, wa, wb, ln_g, ln_b):
    with_ctx = oa_ctx is not None
    m = xs.shape[0] if with_ctx else n_lat_tiles * TM
    tok = lambda i: (i, 0)
    lat = lambda i: (jnp.minimum(i, n_lat_tiles - 1), 0)
    full = lambda i: (0, 0)
    ctx_args = [oa_ctx, ob_ctx] if with_ctx else []
    return pl.pallas_call(
        functools.partial(_out_proj_kernel, n_lat_tiles=n_lat_tiles, with_ctx=with_ctx),
        grid=(m // TM,),
        in_specs=[
            pl.BlockSpec((TM, D_MODEL), tok),
            pl.BlockSpec((None, None, 6, D_MODEL), lambda i: (layer, i // n_lat_tiles, 0, 0)),
            pl.BlockSpec((TM, NA_W), lat),
            pl.BlockSpec((TM, HP), lat),
        ] + ([pl.BlockSpec((TM, NA_W), full), pl.BlockSpec((TM, HP), full)] if with_ctx else []) + [
            pl.BlockSpec(wa.shape, full),
            pl.BlockSpec(wb.shape, full),
            pl.BlockSpec((1, D_MODEL), full),
            pl.BlockSpec((1, D_MODEL), full),
        ],
        out_specs=pl.BlockSpec((TM, D_MODEL), tok),
        out_shape=jax.ShapeDtypeStruct((m, D_MODEL), F32),
        compiler_params=_cparams(("arbitrary",)),
        name="out_proj_ln",
    )(xs, mods, oa_lat, ob_lat, *ctx_args, wa, wb, ln_g, ln_b)


def _ffn_kernel(x_ref, mod_ref, wg_ref, wu_ref, wd_ref, g_ref, b_ref, o_ref):
    x = x_ref[...]
    h = (x * (1.0 + mod_ref[4:5, :]) + mod_ref[3:4, :]).astype(BF16)
    gate = jnp.dot(h, wg_ref[...], preferred_element_type=F32)
    up = jnp.dot(h, wu_ref[...], preferred_element_type=F32)
    a = (_silu(gate) * up).astype(BF16)
    y = jnp.dot(a, wd_ref[...], preferred_element_type=F32)
    v = DEEPNORM_ALPHA * x + mod_ref[5:6, :] * y
    o_ref[...] = _layer_norm(v, g_ref[...], b_ref[...])


def _ffn(xs, mods, layer, n_lat_tiles, wg, wu, wd, ln_g, ln_b):
    m = xs.shape[0]
    tok = lambda i: (i, 0)
    full = lambda i: (0, 0)
    return pl.pallas_call(
        _ffn_kernel,
        grid=(m // TM,),
        in_specs=[
            pl.BlockSpec((TM, D_MODEL), tok),
            pl.BlockSpec((None, None, 6, D_MODEL), lambda i: (layer, i // n_lat_tiles, 0, 0)),
            pl.BlockSpec(wg.shape, full),
            pl.BlockSpec(wu.shape, full),
            pl.BlockSpec(wd.shape, full),
            pl.BlockSpec((1, D_MODEL), full),
            pl.BlockSpec((1, D_MODEL), full),
        ],
        out_specs=pl.BlockSpec((TM, D_MODEL), tok),
        out_shape=jax.ShapeDtypeStruct((m, D_MODEL), F32),
        compiler_params=_cparams(("arbitrary",)),
        name="ffn_ln",
    )(xs, mods, wg, wu, wd, ln_g, ln_b)


def _pool_kernel(x_ref, xp_ref, xn_ref, mod_ref, w_ref, sc_ref, g_ref, b_ref, o_ref, ext_ref,
                 *, n_lat_tiles, s):
    i = pl.program_id(0)
    is_ctx = i >= n_lat_tiles
    j = jnp.where(is_ctx, i - n_lat_tiles, i)
    last = jnp.where(is_ctx, 0, n_lat_tiles - 1)
    n_seq = jnp.where(is_ctx, TM, s)
    scale1 = 1.0 + mod_ref[1:2, :]
    shift1 = mod_ref[0:1, :]
    x = x_ref[...]
    h = x * scale1 + shift1
    hp = jnp.where(j != 0, xp_ref[...] * scale1 + shift1, 0.0)
    hn = jnp.where(j != last, xn_ref[...] * scale1 + shift1, 0.0)
    ext_ref[0:POOL_HALO, :] = hp
    ext_ref[POOL_HALO:POOL_HALO + TM, :] = h
    ext_ref[POOL_HALO + TM:POOL_HALO + TM + POOL_HALO, :] = hn
    t = j * TM + lax.broadcasted_iota(jnp.int32, (TM, 1), 0)
    ys = []
    for g, win in enumerate(POOL_WINDOWS):
        half = win // 2
        lo = g * POOL_GROUP
        acc = ext_ref[POOL_HALO - half:POOL_HALO - half + TM, lo:lo + POOL_GROUP]
        for dlt in range(-half + 1, half):
            acc = acc + ext_ref[POOL_HALO + dlt:POOL_HALO + dlt + TM, lo:lo + POOL_GROUP]
        cnt = (jnp.minimum(t + half, n_seq) - jnp.maximum(t - half, 0)).astype(F32)
        mixed = acc / cnt - h[:, lo:lo + POOL_GROUP]
        yg = jnp.dot(mixed.astype(BF16), w_ref[g], preferred_element_type=F32)
        ys.append(yg)
    y = jnp.concatenate(ys, axis=-1) * sc_ref[...]
    v = DEEPNORM_ALPHA * x + mod_ref[2:3, :] * y
    o_ref[...] = _layer_norm(v, g_ref[...], b_ref[...])


def _pool(xs, mods, layer, n_lat_tiles, s, pool_w, pool_scale, ln_g, ln_b):
    m = xs.shape[0]
    per = TM // POOL_HALO
    n_halo_blocks = m // POOL_HALO
    tok = lambda i: (i, 0)
    full = lambda i: (0, 0)
    return pl.pallas_call(
        functools.partial(_pool_kernel, n_lat_tiles=n_lat_tiles, s=s),
        grid=(m // TM,),
        in_specs=[
            pl.BlockSpec((TM, D_MODEL), tok),
            pl.BlockSpec((POOL_HALO, D_MODEL), lambda i: (jnp.maximum(i * per - 1, 0), 0)),
            pl.BlockSpec((POOL_HALO, D_MODEL), lambda i: (jnp.minimum((i + 1) * per, n_halo_blocks - 1), 0)),
            pl.BlockSpec((None, None, 6, D_MODEL), lambda i: (layer, i // n_lat_tiles, 0, 0)),
            pl.BlockSpec(pool_w.shape, lambda i: (0, 0, 0)),
            pl.BlockSpec((1, D_MODEL), full),
            pl.BlockSpec((1, D_MODEL), full),
            pl.BlockSpec((1, D_MODEL), full),
        ],
        out_specs=pl.BlockSpec((TM, D_MODEL), tok),
        out_shape=jax.ShapeDtypeStruct((m, D_MODEL), F32),
        scratch_shapes=[pltpu.VMEM((TM + 2 * POOL_HALO, D_MODEL), F32)],
        compiler_params=_cparams(("arbitrary",)),
        name="pool_ln",
    )(xs, xs, xs, mods, pool_w, pool_scale, ln_g, ln_b)


def _router_kernel(x_ref, mod_ref, rw_ref, tri_ref, idx_ref, wts_ref, cnt_ref, base_ref):
    @pl.when(pl.program_id(0) == 0)
    def _():
        base_ref[...] = jnp.zeros_like(base_ref)

    h = x_ref[...] * (1.0 + mod_ref[4:5, :]) + mod_ref[3:4, :]
    h_hi = h.astype(BF16)
    h_lo = (h - h_hi.astype(F32)).astype(BF16)
    logits = (jnp.dot(h_hi, rw_ref[0], preferred_element_type=F32)
              + jnp.dot(h_hi, rw_ref[1], preferred_element_type=F32)
              + jnp.dot(h_lo, rw_ref[0], preferred_element_type=F32))
    lane = lax.broadcasted_iota(jnp.int32, logits.shape, 1)
    logits = jnp.where(lane < N_EXPERTS, logits, -jnp.inf)
    m1 = jnp.max(logits, axis=-1, keepdims=True)
    i1 = jnp.min(jnp.where(logits == m1, lane, HEAD_PAD), axis=-1, keepdims=True)
    rest = jnp.where(lane == i1, -jnp.inf, logits)
    m2 = jnp.max(rest, axis=-1, keepdims=True)
    i2 = jnp.min(jnp.where(rest == m2, lane, HEAD_PAD), axis=-1, keepdims=True)
    e2 = jnp.exp(m2 - m1)
    w1 = 1.0 / (1.0 + e2)
    w2 = e2 / (1.0 + e2)
    wts_ref[...] = jnp.where(lane == 0, w1, jnp.where(lane == 1, w2, 0.0))

    chosen = jnp.where(lane == i1, 1.0, jnp.where(lane == i2, 1.0, 0.0))
    before = jnp.dot(tri_ref[...], chosen.astype(BF16), preferred_element_type=F32)
    rank = before + base_ref[0:1, :]
    r1 = jnp.sum(jnp.where(lane == i1, rank, 0.0), axis=-1, keepdims=True)
    r2 = jnp.sum(jnp.where(lane == i2, rank, 0.0), axis=-1, keepdims=True)
    base_ref[0:1, :] = base_ref[0:1, :] + jnp.sum(chosen, axis=0, keepdims=True)
    cnt_ref[...] = base_ref[...]
    packed = jnp.where(lane == 0, i1.astype(F32),
                       jnp.where(lane == 1, i2.astype(F32),
                                 jnp.where(lane == 2, r1, jnp.where(lane == 3, r2, 0.0))))
    idx_ref[...] = packed.T[0:8, :].astype(jnp.int32)


def _router(xs, mods, layer, n_lat_tiles, rw):
    m = xs.shape[0]
    nt = m // TM
    tok = lambda i: (i, 0)
    tri = jnp.asarray(np.tril(np.ones((TM, TM), np.float32), -1), dtype=BF16)
    return pl.pallas_call(
        _router_kernel,
        grid=(nt,),
        in_specs=[
            pl.BlockSpec((TM, D_MODEL), tok),
            pl.BlockSpec((None, None, 6, D_MODEL), lambda i: (layer, i // n_lat_tiles, 0, 0)),
            pl.BlockSpec(rw.shape, lambda i: (0, 0, 0)),
            pl.BlockSpec((TM, TM), lambda i: (0, 0)),
        ],
        out_specs=[
            pl.BlockSpec((None, 8, TM), lambda i: (i, 0, 0)),
            pl.BlockSpec((TM, HEAD_PAD), tok),
            pl.BlockSpec((8, HEAD_PAD), lambda i: (0, 0)),
        ],
        out_shape=[
            jax.ShapeDtypeStruct((nt, 8, TM), jnp.int32),
            jax.ShapeDtypeStruct((m, HEAD_PAD), F32),
            jax.ShapeDtypeStruct((8, HEAD_PAD), F32),
        ],
        scratch_shapes=[pltpu.VMEM((8, HEAD_PAD), F32)],
        compiler_params=_cparams(("arbitrary",)),
        name="moe_router",
    )(xs, mods, rw, tri)


def _dispatch_kernel(fill_ref, nu_ref, rows_ref, x_ref, mod_ref, xs_ref, hbuf, zbuf, sems, zsem):
    i = pl.program_id(0)
    n = pl.num_programs(0)
    slot = i % 2
    n_tiles = xs_ref.shape[0] // MOE_TM

    @pl.when(i == 0)
    def _():
        zbuf[...] = jnp.zeros_like(zbuf)
        fills = [pltpu.make_async_copy(zbuf, xs_ref.at[pl.ds(pl.multiple_of(fill_ref[e], 8), MOE_TM)],
                                       zsem.at[0]) for e in range(N_EXPERTS)]
        for cp in fills:
            cp.start()
        for cp in fills:
            cp.wait()

        def zero_tail(t, carry):
            row0 = pl.multiple_of(t * MOE_TM, MOE_TM)
            cp = pltpu.make_async_copy(zbuf, xs_ref.at[pl.ds(row0, MOE_TM)], zsem.at[0])
            cp.start()
            cp.wait()
            return carry

        lax.fori_loop(nu_ref[0], n_tiles, zero_tail, 0)

    def wait_slot(sl):
        for _ in range(2):
            pltpu.make_async_copy(hbuf.at[sl], xs_ref.at[pl.ds(0, TM)], sems.at[sl]).wait()

    @pl.when(i >= 2)
    def _():
        wait_slot(slot)

    hbuf[slot] = x_ref[...] * (1.0 + mod_ref[4:5, :]) + mod_ref[3:4, :]

    def issue(r, carry):
        for k in range(2):
            dst = rows_ref[0, k * TM + r]
            pltpu.make_async_copy(hbuf.at[slot, pl.ds(r, 1)], xs_ref.at[pl.ds(dst, 1)], sems.at[slot]).start()
        return carry

    lax.fori_loop(0, TM, issue, 0, unroll=8)

    @pl.when(i == n - 1)
    def _():
        wait_slot(slot)

        @pl.when(n >= 2)
        def _():
            wait_slot(1 - slot)


def _dispatch(fill, n_used, rows, xs, mods, layer, n_lat_tiles, n_tiles):
    m = xs.shape[0]
    grid_spec = pltpu.PrefetchScalarGridSpec(
        num_scalar_prefetch=2,
        grid=(m // TM,),
        in_specs=[
            pl.BlockSpec((None, 1, 2 * TM), lambda i, fill, nu: (i, 0, 0), memory_space=pltpu.SMEM),
            pl.BlockSpec((TM, D_MODEL), lambda i, fill, nu: (i, 0)),
            pl.BlockSpec((None, None, 6, D_MODEL), lambda i, fill, nu: (layer, i // n_lat_tiles, 0, 0)),
        ],
        out_specs=pl.BlockSpec(memory_space=pl.ANY),
        scratch_shapes=[pltpu.VMEM((2, TM, D_MODEL), F32), pltpu.VMEM((MOE_TM, D_MODEL), F32),
                        pltpu.SemaphoreType.DMA((2,)), pltpu.SemaphoreType.DMA((1,))],
    )
    return pl.pallas_call(
        _dispatch_kernel,
        grid_spec=grid_spec,
        out_shape=jax.ShapeDtypeStruct(((n_tiles + 1) * MOE_TM, D_MODEL), F32),
        compiler_params=_cparams(("arbitrary",)),
        name="moe_dispatch",
    )(fill, n_used, rows, xs, mods)


def _experts_kernel(te_ref, nu_ref, x_ref, wg_ref, wu_ref, wd_ref, o_ref, xb_ref):
    del te_ref
    f = pl.program_id(1)

    @pl.when((pl.program_id(0) >= nu_ref[0]) & (f == 0))
    def _():
        o_ref[...] = jnp.zeros_like(o_ref)

    @pl.when(pl.program_id(0) < nu_ref[0])
    def _():
        @pl.when(f == 0)
        def _():
            xb_ref[...] = x_ref[...].astype(BF16)

        xb = xb_ref[...]
        gate = jnp.dot(xb, wg_ref[...], preferred_element_type=F32)
        up = jnp.dot(xb, wu_ref[...], preferred_element_type=F32)
        a = (_silu(gate) * up).astype(BF16)
        y = jnp.dot(a, wd_ref[...], preferred_element_type=F32)

        @pl.when(f == 0)
        def _():
            o_ref[...] = y

        @pl.when(f > 0)
        def _():
            o_ref[...] += y


def _experts(tile_expert, n_used, xs_sorted, wg, wu, wd, n_tiles):
    nf = EXPERT_DIM // MOE_TF
    row_blk = lambda i, f, te, nu: (jnp.minimum(i, nu[0] - 1), 0)
    f_blk = lambda i, f, nu: jnp.where(i < nu[0], f, nf - 1)
    grid_spec = pltpu.PrefetchScalarGridSpec(
        num_scalar_prefetch=2,
        grid=(n_tiles, nf),
        in_specs=[
            pl.BlockSpec((MOE_TM, D_MODEL), row_blk),
            pl.BlockSpec((None, D_MODEL, MOE_TF), lambda i, f, te, nu: (te[i], 0, f_blk(i, f, nu))),
            pl.BlockSpec((None, D_MODEL, MOE_TF), lambda i, f, te, nu: (te[i], 0, f_blk(i, f, nu))),
            pl.BlockSpec((None, MOE_TF, D_MODEL), lambda i, f, te, nu: (te[i], f_blk(i, f, nu), 0)),
        ],
        out_specs=pl.BlockSpec((MOE_TM, D_MODEL), lambda i, f, te, nu: (i, 0)),
        scratch_shapes=[pltpu.VMEM((MOE_TM, D_MODEL), BF16)],
    )
    return pl.pallas_call(
        _experts_kernel,
        grid_spec=grid_spec,
        out_shape=jax.ShapeDtypeStruct((n_tiles * MOE_TM, D_MODEL), F32),
        compiler_params=_cparams(("arbitrary", "arbitrary")),
        name="moe_experts",
    )(tile_expert, n_used, xs_sorted, wg, wu, wd)


def _combine_kernel(rows_ref, rowsn_ref, wts_ref, x_ref, mod_ref, g_ref, b_ref, ys_ref, o_ref, ybuf, sems):
    i = pl.program_id(0)
    n = pl.num_programs(0)
    slot = i % 2

    def fetch(ids_ref, sl):
        def issue(r, carry):
            for k in range(2):
                src = ids_ref[0, k * TM + r]
                pltpu.make_async_copy(ys_ref.at[pl.ds(src, 1)], ybuf.at[sl, k, pl.ds(r, 1)], sems.at[sl]).start()
            return carry

        lax.fori_loop(0, TM, issue, 0, unroll=8)

    @pl.when(i == 0)
    def _():
        fetch(rows_ref, 0)

    @pl.when(i + 1 < n)
    def _():
        fetch(rowsn_ref, 1 - slot)

    for k in range(2):
        pltpu.make_async_copy(ys_ref.at[pl.ds(0, TM)], ybuf.at[slot, k], sems.at[slot]).wait()

    wts = wts_ref[...]
    y = wts[:, 0:1] * ybuf[slot, 0] + wts[:, 1:2] * ybuf[slot, 1]
    v = DEEPNORM_ALPHA * x_ref[...] + mod_ref[5:6, :] * y
    o_ref[...] = _layer_norm(v, g_ref[...], b_ref[...])


def _combine(rows, wts, xs, mods, layer, n_lat_tiles, ys_sorted, ln_g, ln_b):
    m = xs.shape[0]
    nt = m // TM
    tok = lambda i: (i, 0)
    full = lambda i: (0, 0)
    return pl.pallas_call(
        _combine_kernel,
        grid=(nt,),
        in_specs=[
            pl.BlockSpec((None, 1, 2 * TM), lambda i: (i, 0, 0), memory_space=pltpu.SMEM),
            pl.BlockSpec((None, 1, 2 * TM), lambda i: (jnp.minimum(i + 1, nt - 1), 0, 0),
                         memory_space=pltpu.SMEM),
            pl.BlockSpec((TM, HEAD_PAD), tok),
            pl.BlockSpec((TM, D_MODEL), tok),
            pl.BlockSpec((None, None, 6, D_MODEL), lambda i: (layer, i // n_lat_tiles, 0, 0)),
            pl.BlockSpec((1, D_MODEL), full),
            pl.BlockSpec((1, D_MODEL), full),
            pl.BlockSpec(memory_space=pl.ANY),
        ],
        out_specs=pl.BlockSpec((TM, D_MODEL), tok),
        out_shape=jax.ShapeDtypeStruct((m, D_MODEL), F32),
        scratch_shapes=[pltpu.VMEM((2, 2, TM, D_MODEL), F32), pltpu.SemaphoreType.DMA((2,))],
        compiler_params=_cparams(("arbitrary",)),
        name="moe_combine_ln",
    )(rows, rows, wts, xs, mods, ln_g, ln_b, ys_sorted)


def _moe_layer(xs, mods, layer, n_lat_tiles, router_w, wg, wu, wd, ln_g, ln_b):
    m = xs.shape[0]
    n_tiles = (2 * m + N_EXPERTS * (MOE_TM - 1) + MOE_TM - 1) // MOE_TM
    rw = jnp.pad(router_w, [(0, 0), (0, HEAD_PAD - N_EXPERTS)])
    rw_hi = rw.astype(BF16)
    rw = jnp.stack([rw_hi, (rw - rw_hi.astype(F32)).astype(BF16)])
    idx, wts, cnt = _router(xs, mods, layer, n_lat_tiles, rw)
    counts = cnt[0, :N_EXPERTS].astype(jnp.int32)
    padded = (counts + MOE_TM - 1) // MOE_TM * MOE_TM
    ends = jnp.cumsum(padded)
    offs = ends - padded
    n_used = (ends[-1:] // MOE_TM).astype(jnp.int32)
    tile_row = jnp.arange(n_tiles, dtype=jnp.int32) * MOE_TM
    tile_row = jnp.minimum(tile_row, ends[-1] - MOE_TM)
    tile_expert = jnp.sum((tile_row[:, None] >= ends[None, :]).astype(jnp.int32), axis=1)
    expert_ids = idx[:, 0:2, :]
    group_off = jnp.sum(jnp.where(expert_ids[..., None] == jnp.arange(N_EXPERTS), offs, 0), axis=-1)
    rows = (group_off + idx[:, 2:4, :]).reshape(m // TM, 1, 2 * TM)
    fill = (offs + counts) // 8 * 8
    xs_sorted = _dispatch(fill, n_used, rows, xs, mods, layer, n_lat_tiles, n_tiles)
    ys_sorted = _experts(tile_expert, n_used, xs_sorted, wg, wu, wd, n_tiles)
    return _combine(rows, wts, xs, mods, layer, n_lat_tiles, ys_sorted, ln_g, ln_b)


_ROPE_SWAP = np.array(list(range(8, 16)) + list(range(0, 8)) + list(range(24, 32)) + list(range(16, 24)))


def _prep_attn_weights(w_in, w_q_up, w_kv_up, w_out):
    o = 3 * NA_W
    w_qc = w_in[:, o:o + MLA_Q_LORA]
    o += MLA_Q_LORA
    w_kvc = w_in[:, o:o + MLA_KV_LORA]
    o += MLA_KV_LORA
    w_kr = w_in[:, o:o + MLA_ROPE]
    rope_pad = [(0, 0), (MLA_NOPE, HEAD_PAD - MLA_NOPE - MLA_ROPE)]
    w_in_aug = jnp.concatenate(
        [w_in[:, :3 * NA_W], w_qc, w_kvc, jnp.pad(w_kr, rope_pad), jnp.pad(w_kr[:, _ROPE_SWAP], rope_pad)],
        axis=1).astype(BF16)

    wq = w_q_up.reshape(MLA_Q_LORA, N_HEADS, MLA_NOPE + MLA_ROPE)
    wq_full = jnp.pad(wq, [(0, 0), (0, 0), (0, HEAD_PAD - MLA_NOPE - MLA_ROPE)])
    wq_swap = jnp.pad(wq[:, :, MLA_NOPE:][:, :, _ROPE_SWAP], [(0, 0), (0, 0), rope_pad[1]])
    w_q = jnp.concatenate([wq_full.reshape(MLA_Q_LORA, HP), wq_swap.reshape(MLA_Q_LORA, HP)], axis=1).astype(BF16)

    wkv = w_kv_up.reshape(MLA_KV_LORA, N_HEADS, MLA_NOPE + MLA_V)
    wkn = jnp.pad(wkv[:, :, :MLA_NOPE], [(0, 0), (0, 0), (0, HEAD_PAD - MLA_NOPE)])
    wv = jnp.pad(wkv[:, :, MLA_NOPE:], [(0, 0), (0, 0), (0, HEAD_PAD - MLA_V)])
    w_kv = jnp.concatenate([wkn.reshape(MLA_KV_LORA, HP), wv.reshape(MLA_KV_LORA, HP)], axis=1).astype(BF16)

    wa = w_out[:NA_W].astype(BF16)
    wb = w_out[NA_W:].reshape(N_HEADS, MLA_V, D_MODEL)
    wb = jnp.pad(wb, [(0, 0), (0, HEAD_PAD - MLA_V), (0, 0)]).reshape(HP, D_MODEL).astype(BF16)
    return w_in_aug, w_q, w_kv, wa, wb


def _rope_tables(s, n_ctx):
    t = jnp.arange(s, dtype=jnp.int32)
    row = (t // GRID_W).astype(F32)
    col = (t % GRID_W).astype(F32)
    n_freq = MLA_ROPE // 4
    inv = 1.0 / (ROPE_THETA ** (jnp.arange(n_freq, dtype=F32) / n_freq))
    ar = row[:, None] * inv
    ac = col[:, None] * inv
    cos = jnp.concatenate([jnp.cos(ar), jnp.cos(ar), jnp.cos(ac), jnp.cos(ac)], axis=1)
    sin = jnp.concatenate([-jnp.sin(ar), jnp.sin(ar), -jnp.sin(ac), jnp.sin(ac)], axis=1)
    right = HEAD_PAD - MLA_NOPE - MLA_ROPE
    cos = jnp.pad(cos, [(0, n_ctx), (MLA_NOPE, right)], constant_values=1.0)
    sin = jnp.pad(sin, [(0, n_ctx), (MLA_NOPE, right)])
    return cos, sin


def _na_bias_mask(rel_bias, n_rows):
    nb = n_rows // NA_QROWS
    c = np.arange(GRID_W)
    cs = np.clip(c - NA_KW // 2, 0, GRID_W - NA_KW)
    kc = np.arange(GRID_W)
    ok_c = (kc[None, :] >= cs[:, None]) & (kc[None, :] < cs[:, None] + NA_KW)
    n_dcol = 2 * NA_KW - 1
    left = GRID_W - NA_KW
    period = 2 * GRID_W
    u = jnp.pad(rel_bias, [(0, 0), (0, 0), (left, period - n_dcol - left)])
    flat = jnp.tile(u, (1, 1, GRID_W + 1))[:, :, :GRID_W * (period - 1)]
    toep = flat.reshape(N_HEADS, 2 * NA_KH - 1, GRID_W, period - 1)[:, :, :, GRID_W - 1:]
    toep = jnp.where(ok_c, toep, NEG_BIG)
    masked = jnp.full((N_HEADS, GRID_W, GRID_W), NEG_BIG, F32)
    out = []
    for b in (0, 1, nb - 1):
        kr0 = int(np.clip(b * NA_QROWS - NA_KH // 2, 0, n_rows - NA_KROWS))
        q_blocks = []
        for qr in range(NA_QROWS):
            r = b * NA_QROWS + qr
            rs = int(np.clip(r - NA_KH // 2, 0, n_rows - NA_KH))
            k_blocks = []
            for klr in range(NA_KROWS):
                kr = kr0 + klr
                k_blocks.append(toep[:, kr - r + NA_KH - 1] if rs <= kr < rs + NA_KH else masked)
            q_blocks.append(jnp.concatenate(k_blocks, axis=-1))
        out.append(jnp.concatenate(q_blocks, axis=1))
    return jnp.stack(out, axis=0)


def kernel(x, c, ctx, c_ctx, mod_w, mod_b, ln1_g, ln1_b, ln2_g, ln2_b, attn_w_in, na_rel_bias, mla_q_norm,
           mla_w_q_up, mla_kv_norm, mla_w_kv_up, attn_w_out, ffn_w_gate, ffn_w_up, ffn_w_down, pool_w,
           pool_scale, moe_router, moe_w_gate, moe_w_up, moe_w_down):
    assert x.shape[0] == 1 and c.shape[0] == 1 and ctx.shape[0] == 1
    s = x.shape[1]
    n_ctx = ctx.shape[1]
    assert n_ctx == TM and s % (NA_QROWS * GRID_W) == 0 and s % MLA_TQ == 0
    assert (s + n_ctx) % MLA_TK == 0
    n_lat_tiles = s // TM
    depth = mod_w.shape[0]

    xs = jnp.concatenate([x[0], ctx[0]], axis=0)
    ct = jnp.stack([c[0], c_ctx], axis=1)
    mods = _modulation(ct, mod_w, mod_b).reshape(depth, 2, 6, D_MODEL)
    cos_t, sin_t = _rope_tables(s, n_ctx)
    row = lambda v: v.reshape(1, -1)
    moe_bf16 = None

    for i in range(depth):
        j = i // 2
        ctx_live = any(l % 2 == 0 for l in range(i + 1, depth))
        if i % 2 == 0:
            w_in, w_q, w_kv, wa, wb = _prep_attn_weights(attn_w_in[j], mla_w_q_up[j], mla_w_kv_up[j],
                                                         attn_w_out[j])
            qa, ka, va, qm, kmt, vm = _attn_proj(xs, mods, i, n_lat_tiles, w_in, row(mla_q_norm[j]), w_q,
                                                 row(mla_kv_norm[j]), w_kv, cos_t, sin_t)
            bias_mask = _na_bias_mask(na_rel_bias[j], s // GRID_W)
            oa_lat = _na_attention(qa, ka, va, bias_mask, s)
            cast = (moe_w_gate, moe_w_up, moe_w_down) if i + 1 < depth else ()
            ob_lat, moe_bf16 = _mla_attention(qm, kmt, vm, s, cast, j)
            oa_ctx = ob_ctx = None
            if ctx_live:
                oa_ctx = _na_ctx_attention(qa, ka, va, s)
                ob_ctx = _mla_ctx_attention(qm, kmt, vm, s)
            xs = _out_proj(xs, mods, i, n_lat_tiles, oa_lat, ob_lat, oa_ctx, ob_ctx, wa, wb,
                           row(ln1_g[i]), row(ln1_b[i]))
            xs = _ffn(xs, mods, i, n_lat_tiles, ffn_w_gate[j].astype(BF16), ffn_w_up[j].astype(BF16),
                      ffn_w_down[j].astype(BF16), row(ln2_g[i]), row(ln2_b[i]))
        else:
            xs = _pool(xs, mods, i, n_lat_tiles, s, pool_w[j].astype(BF16), row(pool_scale[j]),
                       row(ln1_g[i]), row(ln1_b[i]))
            xs = _moe_layer(xs, mods, i, n_lat_tiles, moe_router[j], *moe_bf16, row(ln2_g[i]), row(ln2_b[i]))
    return xs[:s][None] if xs.shape[0] != s else xs[None]
```

```python
import functools

import numpy as np
import jax
import jax.numpy as jnp
from jax import lax
from jax.experimental import pallas as pl
from jax.experimental.pallas import tpu as pltpu

F32 = jnp.float32
BF16 = jnp.bfloat16

D_MODEL = 1024
GRID_W = 64
N_HEADS = 8
HEAD_PAD = 128
HP = N_HEADS * HEAD_PAD
NA_HEAD_DIM = 64
NA_W = N_HEADS * NA_HEAD_DIM
NA_PAIRS = N_HEADS // 2
NA_KH = 8
NA_KW = 16
MLA_NOPE = 64
MLA_ROPE = 32
MLA_V = 64
MLA_Q_LORA = 256
MLA_KV_LORA = 128
ROPE_THETA = 10000.0
POOL_WINDOWS = (2, 4, 8, 16)
POOL_GROUP = D_MODEL // len(POOL_WINDOWS)
POOL_HALO = 8
FFN_DIM = 2816
N_EXPERTS = 8
EXPERT_DIM = 3584
DEPTH = 4
DEEPNORM_ALPHA = (2 * DEPTH) ** 0.25
LN_EPS = 1e-5
RMS_EPS = 1e-6
LOG2E = 1.4426950408889634
NEG_BIG = -1e30

TM = 256
NA_QROWS = 4
NA_KROWS = NA_QROWS + NA_KH - 1
NA_QB = 4
MLA_TQ = 1024
MLA_TK = 640
MLA_UNROLL = 12
MOE_TM = 512
MOE_TF = 1792
VMEM_LIMIT = 56 * 1024 * 1024


def _cparams(sem):
    return pltpu.CompilerParams(dimension_semantics=sem, vmem_limit_bytes=VMEM_LIMIT)


def _layer_norm(v, g, b):
    mu = jnp.mean(v, axis=-1, keepdims=True)
    d = v - mu
    var = jnp.mean(d * d, axis=-1, keepdims=True)
    return d * lax.rsqrt(var + LN_EPS) * g + b


def _silu(v):
    return v * jax.nn.sigmoid(v)


def _token_specs(xs, n_lat_tiles):
    if isinstance(xs, tuple):
        return list(xs), [pl.BlockSpec((TM, D_MODEL), lambda i: (jnp.minimum(i, n_lat_tiles - 1), 0)),
                          pl.BlockSpec((TM, D_MODEL), lambda i: (0, 0))]
    return [xs], [pl.BlockSpec((TM, D_MODEL), lambda i: (i, 0))]


def _load_tokens(x_refs, n_lat_tiles):
    if len(x_refs) == 2:
        return jnp.where(pl.program_id(0) >= n_lat_tiles, x_refs[1][...], x_refs[0][...])
    return x_refs[0][...]


def _token_rows(xs):
    return sum(a.shape[0] for a in xs) if isinstance(xs, tuple) else xs.shape[0]


def _mod_kernel(ct_ref, w_ref, b_ref, o_ref):
    s = _silu(ct_ref[...])
    w = w_ref[...]
    b = b_ref[...]
    r0 = jnp.sum(w * s[:, 0:1], axis=0, keepdims=True) + b
    r1 = jnp.sum(w * s[:, 1:2], axis=0, keepdims=True) + b
    o_ref[...] = jnp.concatenate([r0, r1], axis=0)


def _modulation(ct, mod_w, mod_b):
    depth, d, n6 = mod_w.shape
    tn = 1536
    return pl.pallas_call(
        _mod_kernel,
        grid=(depth, n6 // tn),
        in_specs=[
            pl.BlockSpec((d, 2), lambda l, j: (0, 0)),
            pl.BlockSpec((None, d, tn), lambda l, j: (l, 0, j)),
            pl.BlockSpec((None, 1, tn), lambda l, j: (l, 0, j)),
        ],
        out_specs=pl.BlockSpec((None, 2, tn), lambda l, j: (l, 0, j)),
        out_shape=jax.ShapeDtypeStruct((depth, 2, n6), F32),
        compiler_params=_cparams(("arbitrary", "arbitrary")),
        name="modulation",
    )(ct, mod_w, mod_b.reshape(depth, 1, n6))


def _attn_proj_kernel(*refs, n_x, n_lat_tiles):
    x_refs = refs[:n_x]
    (mod_ref, win_ref, qn_ref, wq_ref, kvn_ref, wkv_ref, cos_ref, sin_ref,
     qa_ref, ka_ref, va_ref, qm_ref, kmt_ref, vm_ref) = refs[n_x:]
    h = _load_tokens(x_refs, n_lat_tiles) * (1.0 + mod_ref[1:2, :]) + mod_ref[0:1, :]
    p = jnp.dot(h.astype(BF16), win_ref[...], preferred_element_type=F32)
    qa_ref[...] = (p[:, 0:NA_W] * (NA_HEAD_DIM ** -0.5)).astype(BF16)
    ka_ref[...] = p[:, NA_W:2 * NA_W].astype(BF16)
    va_ref[...] = p[:, 2 * NA_W:3 * NA_W].astype(BF16)
    o = 3 * NA_W
    q_c = p[:, o:o + MLA_Q_LORA]
    o += MLA_Q_LORA
    kv_c = p[:, o:o + MLA_KV_LORA]
    o += MLA_KV_LORA
    krp = p[:, o:o + HEAD_PAD]
    krs = p[:, o + HEAD_PAD:o + 2 * HEAD_PAD]
    qn = q_c * lax.rsqrt(jnp.mean(q_c * q_c, axis=-1, keepdims=True) + RMS_EPS) * qn_ref[...]
    kvn = kv_c * lax.rsqrt(jnp.mean(kv_c * kv_c, axis=-1, keepdims=True) + RMS_EPS) * kvn_ref[...]
    q2 = jnp.dot(qn.astype(BF16), wq_ref[...], preferred_element_type=F32)
    kv2 = jnp.dot(kvn.astype(BF16), wkv_ref[...], preferred_element_type=F32)
    cos = cos_ref[...]
    sin = sin_ref[...]
    kr = krp * cos + krs * sin
    q_scale = (MLA_NOPE + MLA_ROPE) ** -0.5 * LOG2E
    for hd in range(N_HEADS):
        lo = hd * HEAD_PAD
        qh = (q2[:, lo:lo + HEAD_PAD] * cos + q2[:, HP + lo:HP + lo + HEAD_PAD] * sin) * q_scale
        qm_ref[:, lo:lo + HEAD_PAD] = qh.astype(BF16)
        kh = kv2[:, lo:lo + HEAD_PAD] + kr
        kmt_ref[hd] = kh.T.astype(BF16)
    lane = lax.broadcasted_iota(jnp.int32, (1, HP), 1)
    ones_col = jnp.where(lane % HEAD_PAD == MLA_V, 1.0, 0.0)
    vm_ref[...] = (kv2[:, HP:2 * HP] + ones_col).astype(BF16)


def _attn_proj(xs, mods, layer, n_lat_tiles, w_in, q_norm, w_q, kv_norm, w_kv, cos_t, sin_t):
    m = _token_rows(xs)
    nt = m // TM
    tok = lambda i: (i, 0)
    full = lambda i: (0, 0)
    act = jax.ShapeDtypeStruct((m, HP), BF16)
    na_act = jax.ShapeDtypeStruct((m, NA_W), BF16)
    x_args, x_specs = _token_specs(xs, n_lat_tiles)
    return pl.pallas_call(
        functools.partial(_attn_proj_kernel, n_x=len(x_args), n_lat_tiles=n_lat_tiles),
        grid=(nt,),
        in_specs=x_specs + [
            pl.BlockSpec((None, None, 6, D_MODEL), lambda i: (layer, i // n_lat_tiles, 0, 0)),
            pl.BlockSpec(w_in.shape, full),
            pl.BlockSpec(q_norm.shape, full),
            pl.BlockSpec(w_q.shape, full),
            pl.BlockSpec(kv_norm.shape, full),
            pl.BlockSpec(w_kv.shape, full),
            pl.BlockSpec((TM, HEAD_PAD), tok),
            pl.BlockSpec((TM, HEAD_PAD), tok),
        ],
        out_specs=[
            pl.BlockSpec((TM, NA_W), tok),
            pl.BlockSpec((TM, NA_W), tok),
            pl.BlockSpec((TM, NA_W), tok),
            pl.BlockSpec((TM, HP), tok),
            pl.BlockSpec((N_HEADS, HEAD_PAD, TM), lambda i: (0, 0, i)),
            pl.BlockSpec((TM, HP), tok),
        ],
        out_shape=[na_act, na_act, na_act, act, jax.ShapeDtypeStruct((N_HEADS, HEAD_PAD, m), BF16), act],
        compiler_params=_cparams(("arbitrary",)),
        name="attn_proj",
    )(*x_args, mods, w_in, q_norm, w_q, kv_norm, w_kv, cos_t, sin_t)


_NT_DIMS = (((1,), (1,)), ((), ()))


def _pair_masks():
    lane = lax.broadcasted_iota(jnp.int32, (1, HEAD_PAD), 1)
    return lane < NA_HEAD_DIM, lane >= NA_HEAD_DIM


def _na_kernel(q_ref, k_ref, v_ref, kc_ref, vc_ref, *rest, n_rows):
    bm_refs, o_ref = rest[:NA_QB], rest[NA_QB]
    tq = NA_QROWS * GRID_W
    nk = NA_KROWS * GRID_W
    kc = kc_ref[...]
    vc = vc_ref[...]
    head_lanes = _pair_masks()
    for u in range(NA_QB):
        b = pl.program_id(1) * NA_QB + u
        kr0 = jnp.clip(b * NA_QROWS - NA_KH // 2, 0, n_rows - NA_KROWS)
        start = pl.multiple_of(kr0 * GRID_W, GRID_W)
        q = q_ref[u * tq:(u + 1) * tq, :]
        kw = k_ref[pl.ds(start, nk), :]
        vw = v_ref[pl.ds(start, nk), :]
        outs = []
        for a in range(2):
            qh = jnp.where(head_lanes[a], q, jnp.zeros_like(q))
            s_loc = lax.dot_general(qh, kw, _NT_DIMS, preferred_element_type=F32) + bm_refs[u][a]
            s_ctx = lax.dot_general(qh, kc, _NT_DIMS, preferred_element_type=F32)
            m = jnp.maximum(jnp.max(s_loc, axis=-1, keepdims=True), jnp.max(s_ctx, axis=-1, keepdims=True))
            p_loc = jnp.exp(s_loc - m)
            p_ctx = jnp.exp(s_ctx - m)
            l = jnp.sum(p_loc, axis=-1, keepdims=True) + jnp.sum(p_ctx, axis=-1, keepdims=True)
            o = (jnp.dot(p_loc.astype(BF16), vw, preferred_element_type=F32)
                 + jnp.dot(p_ctx.astype(BF16), vc, preferred_element_type=F32))
            outs.append(o / l)
        o_ref[u * tq:(u + 1) * tq, :] = jnp.where(head_lanes[0], outs[0], outs[1]).astype(BF16)


def _na_attention(qa, ka, va, bias_mask, s):
    n_rows = s // GRID_W
    nb = n_rows // NA_QROWS
    tq = NA_QROWS * GRID_W
    assert nb % NA_QB == 0
    ctx_blk = s // tq
    pat = lambda b: jnp.where(b == 0, 0, jnp.where(b == nb - 1, 2, 1))
    bm_spec = lambda u: pl.BlockSpec((None, 2, tq, NA_KROWS * GRID_W),
                                     lambda hp, j: (pat(j * NA_QB + u), hp, 0, 0))
    return pl.pallas_call(
        functools.partial(_na_kernel, n_rows=n_rows),
        grid=(NA_PAIRS, nb // NA_QB),
        in_specs=[
            pl.BlockSpec((NA_QB * tq, HEAD_PAD), lambda hp, j: (j, hp)),
            pl.BlockSpec((s, HEAD_PAD), lambda hp, j: (0, hp)),
            pl.BlockSpec((s, HEAD_PAD), lambda hp, j: (0, hp)),
            pl.BlockSpec((tq, HEAD_PAD), lambda hp, j: (ctx_blk, hp)),
            pl.BlockSpec((tq, HEAD_PAD), lambda hp, j: (ctx_blk, hp)),
        ] + [bm_spec(u) for u in range(NA_QB)],
        out_specs=pl.BlockSpec((NA_QB * tq, HEAD_PAD), lambda hp, j: (j, hp)),
        out_shape=jax.ShapeDtypeStruct((s, NA_W), BF16),
        compiler_params=_cparams(("arbitrary", "arbitrary")),
        name="na_attention",
    )(qa, ka, va, ka, va, *([bias_mask] * NA_QB))


def _na_ctx_kernel(q_ref, k_ref, v_ref, o_ref):
    q = q_ref[...]
    k = k_ref[...]
    v = v_ref[...]
    head_lanes = _pair_masks()
    outs = []
    for a in range(2):
        qh = jnp.where(head_lanes[a], q, jnp.zeros_like(q))
        s = lax.dot_general(qh, k, _NT_DIMS, preferred_element_type=F32)
        m = jnp.max(s, axis=-1, keepdims=True)
        p = jnp.exp(s - m)
        l = jnp.sum(p, axis=-1, keepdims=True)
        outs.append(jnp.dot(p.astype(BF16), v, preferred_element_type=F32) / l)
    o_ref[...] = jnp.where(head_lanes[0], outs[0], outs[1]).astype(BF16)


def _na_ctx_attention(qa, ka, va, s):
    blk = s // TM
    spec = pl.BlockSpec((TM, HEAD_PAD), lambda hp: (blk, hp))
    return pl.pallas_call(
        _na_ctx_kernel,
        grid=(NA_PAIRS,),
        in_specs=[spec, spec, spec],
        out_specs=pl.BlockSpec((TM, HEAD_PAD), lambda hp: (0, hp)),
        out_shape=jax.ShapeDtypeStruct((TM, NA_W), BF16),
        compiler_params=_cparams(("arbitrary",)),
        name="na_ctx_attention",
    )(qa, ka, va)


def _mla_ctx_kernel(q_ref, kt_ref, v_ref, o_ref):
    s = jnp.dot(q_ref[...], kt_ref[...], preferred_element_type=F32)
    m = jnp.max(s, axis=-1, keepdims=True)
    p = jnp.exp2(s - m)
    l = jnp.sum(p, axis=-1, keepdims=True)
    o = jnp.dot(p.astype(BF16), v_ref[...], preferred_element_type=F32)
    o_ref[...] = (o / l).astype(BF16)


def _mla_ctx_attention(qm, kmt, vm, s):
    blk = s // TM
    return pl.pallas_call(
        _mla_ctx_kernel,
        grid=(N_HEADS,),
        in_specs=[
            pl.BlockSpec((TM, HEAD_PAD), lambda h: (blk, h)),
            pl.BlockSpec((None, HEAD_PAD, TM), lambda h: (h, 0, blk)),
            pl.BlockSpec((TM, HEAD_PAD), lambda h: (blk, h)),
        ],
        out_specs=pl.BlockSpec((TM, HEAD_PAD), lambda h: (0, h)),
        out_shape=jax.ShapeDtypeStruct((TM, HP), BF16),
        compiler_params=_cparams(("arbitrary",)),
        name="mla_ctx_attention",
    )(qm, kmt, vm)


def _mla_kernel(q_ref, qn_ref, kt_ref, v_ref, *rest, n_chunks, n_cast):
    cast_in, o_ref, cast_out = rest[:n_cast], rest[n_cast], rest[n_cast + 1:2 * n_cast + 1]
    s0_ref, s1_ref = rest[2 * n_cast + 1:]
    for w_ref, wb_ref in zip(cast_in, cast_out):
        wb_ref[...] = w_ref[...].astype(BF16)
    q = q_ref[...]
    tq = q.shape[0]

    def scores(c, s_ref, q_tile=q):
        off = pl.multiple_of(c * MLA_TK, MLA_TK)
        s_ref[...] = jnp.dot(q_tile, kt_ref[:, pl.ds(off, MLA_TK)], preferred_element_type=F32)

    def update(c, s_ref, carry):
        m, acc = carry
        off = pl.multiple_of(c * MLA_TK, MLA_TK)
        s = s_ref[...]
        m_new = jnp.maximum(m, jnp.max(s, axis=-1, keepdims=True))
        alpha = jnp.exp2(m - m_new)
        p = jnp.exp2(s - m_new).astype(BF16)
        acc = alpha * acc + jnp.dot(p, v_ref[pl.ds(off, MLA_TK), :], preferred_element_type=F32)
        return m_new, acc

    bufs = (s0_ref, s1_ref)

    def group(g, carry):
        c0 = g * MLA_UNROLL
        for u in range(MLA_UNROLL):
            scores(c0 + u + 1, bufs[(u + 1) % 2])
            carry = update(c0 + u, bufs[u % 2], carry)
        return carry

    carry = (jnp.full((tq, 1), NEG_BIG, F32), jnp.zeros((tq, HEAD_PAD), F32))

    @pl.when(pl.program_id(1) == 0)
    def _():
        scores(0, s0_ref)

    n_groups = (n_chunks - 2) // MLA_UNROLL
    carry = lax.fori_loop(0, n_groups, group, carry)
    for c in range(n_groups * MLA_UNROLL, n_chunks):
        if c + 1 < n_chunks:
            scores(c + 1, bufs[(c + 1) % 2])
        else:
            scores(0, s0_ref, qn_ref[...])
        carry = update(c, bufs[c % 2], carry)
    _, acc = carry
    o_ref[...] = (acc / acc[:, MLA_V:MLA_V + 1]).astype(BF16)


def _mla_attention(qm, kmt, vm, s, cast_weights=(), cast_layer=0):
    m = qm.shape[0]
    n_chunks = m // MLA_TK
    assert n_chunks % 2 == 0 and n_chunks >= 2
    n_q = s // MLA_TQ
    assert N_HEADS == N_EXPERTS or not cast_weights
    cast_in_specs, cast_out_specs, cast_shapes = [], [], []
    for w in cast_weights:
        rows, cols = w.shape[2], w.shape[3]
        slab = rows // n_q
        assert rows % n_q == 0 and slab % 16 == 0
        cast_in_specs.append(pl.BlockSpec((None, None, slab, cols), lambda h, i: (cast_layer, h, i, 0)))
        cast_out_specs.append(pl.BlockSpec((None, slab, cols), lambda h, i: (h, i, 0)))
        cast_shapes.append(jax.ShapeDtypeStruct((N_EXPERTS, rows, cols), BF16))
    outs = pl.pallas_call(
        functools.partial(_mla_kernel, n_chunks=n_chunks, n_cast=len(cast_weights)),
        grid=(N_HEADS, n_q),
        in_specs=[
            pl.BlockSpec((MLA_TQ, HEAD_PAD), lambda h, i: (i, h)),
            pl.BlockSpec((MLA_TQ, HEAD_PAD), lambda h, i: (jnp.minimum(i + 1, n_q - 1), h)),
            pl.BlockSpec((None, HEAD_PAD, m), lambda h, i: (h, 0, 0)),
            pl.BlockSpec((m, HEAD_PAD), lambda h, i: (0, h)),
        ] + cast_in_specs,
        out_specs=[pl.BlockSpec((MLA_TQ, HEAD_PAD), lambda h, i: (i, h))] + cast_out_specs,
        out_shape=[jax.ShapeDtypeStruct((s, HP), BF16)] + cast_shapes,
        scratch_shapes=[pltpu.VMEM((MLA_TQ, MLA_TK), F32), pltpu.VMEM((MLA_TQ, MLA_TK), F32)],
        compiler_params=_cparams(("arbitrary", "arbitrary")),
        name="mla_attention",
    )(qm, qm, kmt, vm, *cast_weights)
    return outs[0], tuple(outs[1:])


def _out_ffn_kernel(*refs, n_x, n_lat_tiles, with_ctx):
    x_refs = refs[:n_x]
    mod_ref, oal_ref, obl_ref = refs[n_x:n_x + 3]
    rest = refs[n_x + 3:]
    if with_ctx:
        oac_ref, obc_ref = rest[:2]
        rest = rest[2:]
        is_ctx = pl.program_id(0) >= n_lat_tiles
        oa = jnp.where(is_ctx, oac_ref[...], oal_ref[...])
        ob = jnp.where(is_ctx, obc_ref[...], obl_ref[...])
    else:
        oa = oal_ref[...]
        ob = obl_ref[...]
    wa_ref, wb_ref, g1_ref, b1_ref, wg_ref, wu_ref, wd_ref, g2_ref, b2_ref, o_ref = rest
    y = (jnp.dot(oa, wa_ref[...], preferred_element_type=F32)
         + jnp.dot(ob, wb_ref[...], preferred_element_type=F32))
    v = DEEPNORM_ALPHA * _load_tokens(x_refs, n_lat_tiles) + mod_ref[2:3, :] * y
    x1 = _layer_norm(v, g1_ref[...], b1_ref[...])
    h = (x1 * (1.0 + mod_ref[4:5, :]) + mod_ref[3:4, :]).astype(BF16)
    gate = jnp.dot(h, wg_ref[...], preferred_element_type=F32)
    up = jnp.dot(h, wu_ref[...], preferred_element_type=F32)
    a = (_silu(gate) * up).astype(BF16)
    y2 = jnp.dot(a, wd_ref[...], preferred_element_type=F32)
    v2 = DEEPNORM_ALPHA * x1 + mod_ref[5:6, :] * y2
    o_ref[...] = _layer_norm(v2, g2_ref[...], b2_ref[...])


def _out_ffn(xs, mods, layer, n_lat_tiles, oa_lat, ob_lat, oa_ctx, ob_ctx, wa, wb, ln1_g, ln1_b,
             wg, wu, wd, ln2_g, ln2_b):
    with_ctx = oa_ctx is not None
    m = _token_rows(xs) if with_ctx else n_lat_tiles * TM
    tok = lambda i: (i, 0)
    lat = lambda i: (jnp.minimum(i, n_lat_tiles - 1), 0)
    full = lambda i: (0, 0)
    ctx_args = [oa_ctx, ob_ctx] if with_ctx else []
    x_args, x_specs = _token_specs(xs, n_lat_tiles)
    resident = lambda w: pl.BlockSpec(w.shape, full, pipeline_mode=pl.Buffered(1))
    vec = pl.BlockSpec((1, D_MODEL), full)
    return pl.pallas_call(
        functools.partial(_out_ffn_kernel, n_x=len(x_args), n_lat_tiles=n_lat_tiles, with_ctx=with_ctx),
        grid=(m // TM,),
        in_specs=x_specs + [
            pl.BlockSpec((None, None, 6, D_MODEL), lambda i: (layer, i // n_lat_tiles, 0, 0)),
            pl.BlockSpec((TM, NA_W), lat),
            pl.BlockSpec((TM, HP), lat),
        ] + ([pl.BlockSpec((TM, NA_W), full), pl.BlockSpec((TM, HP), full)] if with_ctx else []) + [
            resident(wa), resident(wb), vec, vec, resident(wg), resident(wu), resident(wd), vec, vec,
        ],
        out_specs=pl.BlockSpec((TM, D_MODEL), tok),
        out_shape=jax.ShapeDtypeStruct((m, D_MODEL), F32),
        compiler_params=_cparams(("arbitrary",)),
        name="out_proj_ffn_ln",
    )(*x_args, mods, oa_lat, ob_lat, *ctx_args, wa, wb, ln1_g, ln1_b, wg, wu, wd, ln2_g, ln2_b)


def _pool_kernel(x_ref, xp_ref, xn_ref, mod_ref, w_ref, sc_ref, g_ref, b_ref, o_ref, ext_ref,
                 *, n_lat_tiles, s):
    i = pl.program_id(0)
    is_ctx = i >= n_lat_tiles
    j = jnp.where(is_ctx, i - n_lat_tiles, i)
    last = jnp.where(is_ctx, 0, n_lat_tiles - 1)
    n_seq = jnp.where(is_ctx, TM, s)
    scale1 = 1.0 + mod_ref[1:2, :]
    shift1 = mod_ref[0:1, :]
    x = x_ref[...]
    h = x * scale1 + shift1
    hp = jnp.where(j != 0, xp_ref[...] * scale1 + shift1, 0.0)
    hn = jnp.where(j != last, xn_ref[...] * scale1 + shift1, 0.0)
    ext_ref[0:POOL_HALO, :] = hp
    ext_ref[POOL_HALO:POOL_HALO + TM, :] = h
    ext_ref[POOL_HALO + TM:POOL_HALO + TM + POOL_HALO, :] = hn
    t = j * TM + lax.broadcasted_iota(jnp.int32, (TM, 1), 0)
    ys = []
    for g, win in enumerate(POOL_WINDOWS):
        half = win // 2
        lo = g * POOL_GROUP
        acc = ext_ref[POOL_HALO - half:POOL_HALO - half + TM, lo:lo + POOL_GROUP]
        for dlt in range(-half + 1, half):
            acc = acc + ext_ref[POOL_HALO + dlt:POOL_HALO + dlt + TM, lo:lo + POOL_GROUP]
        cnt = (jnp.minimum(t + half, n_seq) - jnp.maximum(t - half, 0)).astype(F32)
        mixed = acc / cnt - h[:, lo:lo + POOL_GROUP]
        yg = jnp.dot(mixed.astype(BF16), w_ref[g], preferred_element_type=F32)
        ys.append(yg)
    y = jnp.concatenate(ys, axis=-1) * sc_ref[...]
    v = DEEPNORM_ALPHA * x + mod_ref[2:3, :] * y
    o_ref[...] = _layer_norm(v, g_ref[...], b_ref[...])


def _pool(xs, mods, layer, n_lat_tiles, s, pool_w, pool_scale, ln_g, ln_b):
    m = xs.shape[0]
    per = TM // POOL_HALO
    n_halo_blocks = m // POOL_HALO
    tok = lambda i: (i, 0)
    full = lambda i: (0, 0)
    return pl.pallas_call(
        functools.partial(_pool_kernel, n_lat_tiles=n_lat_tiles, s=s),
        grid=(m // TM,),
        in_specs=[
            pl.BlockSpec((TM, D_MODEL), tok),
            pl.BlockSpec((POOL_HALO, D_MODEL), lambda i: (jnp.maximum(i * per - 1, 0), 0)),
            pl.BlockSpec((POOL_HALO, D_MODEL), lambda i: (jnp.minimum((i + 1) * per, n_halo_blocks - 1), 0)),
            pl.BlockSpec((None, None, 6, D_MODEL), lambda i: (layer, i // n_lat_tiles, 0, 0)),
            pl.BlockSpec(pool_w.shape, lambda i: (0, 0, 0)),
            pl.BlockSpec((1, D_MODEL), full),
            pl.BlockSpec((1, D_MODEL), full),
            pl.BlockSpec((1, D_MODEL), full),
        ],
        out_specs=pl.BlockSpec((TM, D_MODEL), tok),
        out_shape=jax.ShapeDtypeStruct((m, D_MODEL), F32),
        scratch_shapes=[pltpu.VMEM((TM + 2 * POOL_HALO, D_MODEL), F32)],
        compiler_params=_cparams(("arbitrary",)),
        name="pool_ln",
    )(xs, xs, xs, mods, pool_w, pool_scale, ln_g, ln_b)


def _router_kernel(x_ref, mod_ref, rw_ref, tri_ref, idx_ref, wts_ref, cnt_ref, base_ref):
    @pl.when(pl.program_id(0) == 0)
    def _():
        base_ref[...] = jnp.zeros_like(base_ref)

    h = x_ref[...] * (1.0 + mod_ref[4:5, :]) + mod_ref[3:4, :]
    h_hi = h.astype(BF16)
    h_lo = (h - h_hi.astype(F32)).astype(BF16)
    logits = (jnp.dot(h_hi, rw_ref[0], preferred_element_type=F32)
              + jnp.dot(h_hi, rw_ref[1], preferred_element_type=F32)
              + jnp.dot(h_lo, rw_ref[0], preferred_element_type=F32))
    lane = lax.broadcasted_iota(jnp.int32, logits.shape, 1)
    logits = jnp.where(lane < N_EXPERTS, logits, -jnp.inf)
    m1 = jnp.max(logits, axis=-1, keepdims=True)
    i1 = jnp.min(jnp.where(logits == m1, lane, HEAD_PAD), axis=-1, keepdims=True)
    rest = jnp.where(lane == i1, -jnp.inf, logits)
    m2 = jnp.max(rest, axis=-1, keepdims=True)
    i2 = jnp.min(jnp.where(rest == m2, lane, HEAD_PAD), axis=-1, keepdims=True)
    e2 = jnp.exp(m2 - m1)
    w1 = 1.0 / (1.0 + e2)
    w2 = e2 / (1.0 + e2)
    wts_ref[...] = jnp.where(lane == 0, w1, jnp.where(lane == 1, w2, 0.0))

    chosen = jnp.where(lane == i1, 1.0, jnp.where(lane == i2, 1.0, 0.0))
    before = jnp.dot(tri_ref[...], chosen.astype(BF16), preferred_element_type=F32)
    rank = before + base_ref[0:1, :]
    r1 = jnp.sum(jnp.where(lane == i1, rank, 0.0), axis=-1, keepdims=True)
    r2 = jnp.sum(jnp.where(lane == i2, rank, 0.0), axis=-1, keepdims=True)
    base_ref[0:1, :] = base_ref[0:1, :] + jnp.sum(chosen, axis=0, keepdims=True)
    cnt_ref[...] = base_ref[...]
    packed = jnp.where(lane == 0, i1.astype(F32),
                       jnp.where(lane == 1, i2.astype(F32),
                                 jnp.where(lane == 2, r1, jnp.where(lane == 3, r2, 0.0))))
    idx_ref[...] = packed.T[0:8, :].astype(jnp.int32)


def _router(xs, mods, layer, n_lat_tiles, rw):
    m = xs.shape[0]
    nt = m // TM
    tok = lambda i: (i, 0)
    tri = jnp.asarray(np.tril(np.ones((TM, TM), np.float32), -1), dtype=BF16)
    return pl.pallas_call(
        _router_kernel,
        grid=(nt,),
        in_specs=[
            pl.BlockSpec((TM, D_MODEL), tok),
            pl.BlockSpec((None, None, 6, D_MODEL), lambda i: (layer, i // n_lat_tiles, 0, 0)),
            pl.BlockSpec(rw.shape, lambda i: (0, 0, 0)),
            pl.BlockSpec((TM, TM), lambda i: (0, 0)),
        ],
        out_specs=[
            pl.BlockSpec((None, 8, TM), lambda i: (i, 0, 0)),
            pl.BlockSpec((TM, HEAD_PAD), tok),
            pl.BlockSpec((8, HEAD_PAD), lambda i: (0, 0)),
        ],
        out_shape=[
            jax.ShapeDtypeStruct((nt, 8, TM), jnp.int32),
            jax.ShapeDtypeStruct((m, HEAD_PAD), F32),
            jax.ShapeDtypeStruct((8, HEAD_PAD), F32),
        ],
        scratch_shapes=[pltpu.VMEM((8, HEAD_PAD), F32)],
        compiler_params=_cparams(("arbitrary",)),
        name="moe_router",
    )(xs, mods, rw, tri)


def _dispatch_kernel(fill_ref, nu_ref, rows_ref, x_ref, mod_ref, xs_ref, hbuf, zbuf, sems, zsem):
    i = pl.program_id(0)
    n = pl.num_programs(0)
    slot = i % 2
    n_tiles = xs_ref.shape[0] // MOE_TM

    @pl.when(i == 0)
    def _():
        zbuf[...] = jnp.zeros_like(zbuf)
        fills = [pltpu.make_async_copy(zbuf, xs_ref.at[pl.ds(pl.multiple_of(fill_ref[e], 8), MOE_TM)],
                                       zsem.at[0]) for e in range(N_EXPERTS)]
        for cp in fills:
            cp.start()
        for cp in fills:
            cp.wait()

        def zero_tail(t, carry):
            row0 = pl.multiple_of(t * MOE_TM, MOE_TM)
            cp = pltpu.make_async_copy(zbuf, xs_ref.at[pl.ds(row0, MOE_TM)], zsem.at[0])
            cp.start()
            cp.wait()
            return carry

        lax.fori_loop(nu_ref[0], n_tiles, zero_tail, 0)

    def wait_slot(sl):
        for _ in range(2):
            pltpu.make_async_copy(hbuf.at[sl], xs_ref.at[pl.ds(0, TM)], sems.at[sl]).wait()

    @pl.when(i >= 2)
    def _():
        wait_slot(slot)

    hbuf[slot] = x_ref[...] * (1.0 + mod_ref[4:5, :]) + mod_ref[3:4, :]

    def issue(r, carry):
        for k in range(2):
            dst = rows_ref[0, k * TM + r]
            pltpu.make_async_copy(hbuf.at[slot, pl.ds(r, 1)], xs_ref.at[pl.ds(dst, 1)], sems.at[slot]).start()
        return carry

    lax.fori_loop(0, TM, issue, 0, unroll=8)

    @pl.when(i == n - 1)
    def _():
        wait_slot(slot)

        @pl.when(n >= 2)
        def _():
            wait_slot(1 - slot)


def _dispatch(fill, n_used, rows, xs, mods, layer, n_lat_tiles, n_tiles):
    m = xs.shape[0]
    grid_spec = pltpu.PrefetchScalarGridSpec(
        num_scalar_prefetch=2,
        grid=(m // TM,),
        in_specs=[
            pl.BlockSpec((None, 1, 2 * TM), lambda i, fill, nu: (i, 0, 0), memory_space=pltpu.SMEM),
            pl.BlockSpec((TM, D_MODEL), lambda i, fill, nu: (i, 0)),
            pl.BlockSpec((None, None, 6, D_MODEL), lambda i, fill, nu: (layer, i // n_lat_tiles, 0, 0)),
        ],
        out_specs=pl.BlockSpec(memory_space=pl.ANY),
        scratch_shapes=[pltpu.VMEM((2, TM, D_MODEL), F32), pltpu.VMEM((MOE_TM, D_MODEL), F32),
                        pltpu.SemaphoreType.DMA((2,)), pltpu.SemaphoreType.DMA((1,))],
    )
    return pl.pallas_call(
        _dispatch_kernel,
        grid_spec=grid_spec,
        out_shape=jax.ShapeDtypeStruct(((n_tiles + 1) * MOE_TM, D_MODEL), F32),
        compiler_params=_cparams(("arbitrary",)),
        name="moe_dispatch",
    )(fill, n_used, rows, xs, mods)


def _experts_kernel(te_ref, nu_ref, x_ref, wg_ref, wu_ref, wd_ref, o_ref, xb_ref):
    del te_ref
    f = pl.program_id(1)

    @pl.when((pl.program_id(0) >= nu_ref[0]) & (f == 0))
    def _():
        o_ref[...] = jnp.zeros_like(o_ref)

    @pl.when(pl.program_id(0) < nu_ref[0])
    def _():
        @pl.when(f == 0)
        def _():
            xb_ref[...] = x_ref[...].astype(BF16)

        xb = xb_ref[...]
        gate = jnp.dot(xb, wg_ref[...], preferred_element_type=F32)
        up = jnp.dot(xb, wu_ref[...], preferred_element_type=F32)
        a = (_silu(gate) * up).astype(BF16)
        y = jnp.dot(a, wd_ref[...], preferred_element_type=F32)

        @pl.when(f == 0)
        def _():
            o_ref[...] = y

        @pl.when(f > 0)
        def _():
            o_ref[...] += y


def _experts(tile_expert, n_used, xs_sorted, wg, wu, wd, n_tiles):
    nf = EXPERT_DIM // MOE_TF
    row_blk = lambda i, f, te, nu: (jnp.minimum(i, nu[0] - 1), 0)
    f_blk = lambda i, f, nu: jnp.where(i < nu[0], f, nf - 1)
    grid_spec = pltpu.PrefetchScalarGridSpec(
        num_scalar_prefetch=2,
        grid=(n_tiles, nf),
        in_specs=[
            pl.BlockSpec((MOE_TM, D_MODEL), row_blk),
            pl.BlockSpec((None, D_MODEL, MOE_TF), lambda i, f, te, nu: (te[i], 0, f_blk(i, f, nu))),
            pl.BlockSpec((None, D_MODEL, MOE_TF), lambda i, f, te, nu: (te[i], 0, f_blk(i, f, nu))),
            pl.BlockSpec((None, MOE_TF, D_MODEL), lambda i, f, te, nu: (te[i], f_blk(i, f, nu), 0)),
        ],
        out_specs=pl.BlockSpec((MOE_TM, D_MODEL), lambda i, f, te, nu: (i, 0)),
        scratch_shapes=[pltpu.VMEM((MOE_TM, D_MODEL), BF16)],
    )
    return pl.pallas_call(
        _experts_kernel,
        grid_spec=grid_spec,
        out_shape=jax.ShapeDtypeStruct((n_tiles * MOE_TM, D_MODEL), F32),
        compiler_params=_cparams(("arbitrary", "arbitrary")),
        name="moe_experts",
    )(tile_expert, n_used, xs_sorted, wg, wu, wd)


def _combine_kernel(rows_ref, rowsn_ref, wts_ref, x_ref, mod_ref, g_ref, b_ref, ys_ref, o_ref, ybuf, sems):
    i = pl.program_id(0)
    n = pl.num_programs(0)
    slot = i % 2

    def fetch(ids_ref, sl):
        def issue(r, carry):
            for k in range(2):
                src = ids_ref[0, k * TM + r]
                pltpu.make_async_copy(ys_ref.at[pl.ds(src, 1)], ybuf.at[sl, k, pl.ds(r, 1)], sems.at[sl]).start()
            return carry

        lax.fori_loop(0, TM, issue, 0, unroll=8)

    @pl.when(i == 0)
    def _():
        fetch(rows_ref, 0)

    @pl.when(i + 1 < n)
    def _():
        fetch(rowsn_ref, 1 - slot)

    for k in range(2):
        pltpu.make_async_copy(ys_ref.at[pl.ds(0, TM)], ybuf.at[slot, k], sems.at[slot]).wait()

    wts = wts_ref[...]
    y = wts[:, 0:1] * ybuf[slot, 0] + wts[:, 1:2] * ybuf[slot, 1]
    v = DEEPNORM_ALPHA * x_ref[...] + mod_ref[5:6, :] * y
    o_ref[...] = _layer_norm(v, g_ref[...], b_ref[...])


def _combine(rows, wts, xs, mods, layer, n_lat_tiles, ys_sorted, ln_g, ln_b):
    m = xs.shape[0]
    nt = m // TM
    tok = lambda i: (i, 0)
    full = lambda i: (0, 0)
    return pl.pallas_call(
        _combine_kernel,
        grid=(nt,),
        in_specs=[
            pl.BlockSpec((None, 1, 2 * TM), lambda i: (i, 0, 0), memory_space=pltpu.SMEM),
            pl.BlockSpec((None, 1, 2 * TM), lambda i: (jnp.minimum(i + 1, nt - 1), 0, 0),
                         memory_space=pltpu.SMEM),
            pl.BlockSpec((TM, HEAD_PAD), tok),
            pl.BlockSpec((TM, D_MODEL), tok),
            pl.BlockSpec((None, None, 6, D_MODEL), lambda i: (layer, i // n_lat_tiles, 0, 0)),
            pl.BlockSpec((1, D_MODEL), full),
            pl.BlockSpec((1, D_MODEL), full),
            pl.BlockSpec(memory_space=pl.ANY),
        ],
        out_specs=pl.BlockSpec((TM, D_MODEL), tok),
        out_shape=jax.ShapeDtypeStruct((m, D_MODEL), F32),
        scratch_shapes=[pltpu.VMEM((2, 2, TM, D_MODEL), F32), pltpu.SemaphoreType.DMA((2,))],
        compiler_params=_cparams(("arbitrary",)),
        name="moe_combine_ln",
    )(rows, rows, wts, xs, mods, ln_g, ln_b, ys_sorted)


def _moe_layer(xs, mods, layer, n_lat_tiles, router_w, wg, wu, wd, ln_g, ln_b):
    m = xs.shape[0]
    n_tiles = (2 * m + N_EXPERTS * (MOE_TM - 1) + MOE_TM - 1) // MOE_TM
    rw = jnp.pad(router_w, [(0, 0), (0, HEAD_PAD - N_EXPERTS)])
    rw_hi = rw.astype(BF16)
    rw = jnp.stack([rw_hi, (rw - rw_hi.astype(F32)).astype(BF16)])
    idx, wts, cnt = _router(xs, mods, layer, n_lat_tiles, rw)
    counts = cnt[0, :N_EXPERTS].astype(jnp.int32)
    padded = (counts + MOE_TM - 1) // MOE_TM * MOE_TM
    ends = jnp.cumsum(padded)
    offs = ends - padded
    n_used = (ends[-1:] // MOE_TM).astype(jnp.int32)
    tile_row = jnp.arange(n_tiles, dtype=jnp.int32) * MOE_TM
    tile_row = jnp.minimum(tile_row, ends[-1] - MOE_TM)
    tile_expert = jnp.sum((tile_row[:, None] >= ends[None, :]).astype(jnp.int32), axis=1)
    expert_ids = idx[:, 0:2, :]
    group_off = jnp.sum(jnp.where(expert_ids[..., None] == jnp.arange(N_EXPERTS), offs, 0), axis=-1)
    rows = (group_off + idx[:, 2:4, :]).reshape(m // TM, 1, 2 * TM)
    fill = (offs + counts) // 8 * 8
    xs_sorted = _dispatch(fill, n_used, rows, xs, mods, layer, n_lat_tiles, n_tiles)
    ys_sorted = _experts(tile_expert, n_used, xs_sorted, wg, wu, wd, n_tiles)
    return _combine(rows, wts, xs, mods, layer, n_lat_tiles, ys_sorted, ln_g, ln_b)


_ROPE_SWAP = np.array(list(range(8, 16)) + list(range(0, 8)) + list(range(24, 32)) + list(range(16, 24)))


def _prep_attn_weights(w_in, w_q_up, w_kv_up, w_out):
    o = 3 * NA_W
    w_qc = w_in[:, o:o + MLA_Q_LORA]
    o += MLA_Q_LORA
    w_kvc = w_in[:, o:o + MLA_KV_LORA]
    o += MLA_KV_LORA
    w_kr = w_in[:, o:o + MLA_ROPE]
    rope_pad = [(0, 0), (MLA_NOPE, HEAD_PAD - MLA_NOPE - MLA_ROPE)]
    w_in_aug = jnp.concatenate(
        [w_in[:, :3 * NA_W], w_qc, w_kvc, jnp.pad(w_kr, rope_pad), jnp.pad(w_kr[:, _ROPE_SWAP], rope_pad)],
        axis=1).astype(BF16)

    wq = w_q_up.reshape(MLA_Q_LORA, N_HEADS, MLA_NOPE + MLA_ROPE)
    wq_full = jnp.pad(wq, [(0, 0), (0, 0), (0, HEAD_PAD - MLA_NOPE - MLA_ROPE)])
    wq_swap = jnp.pad(wq[:, :, MLA_NOPE:][:, :, _ROPE_SWAP], [(0, 0), (0, 0), rope_pad[1]])
    w_q = jnp.concatenate([wq_full.reshape(MLA_Q_LORA, HP), wq_swap.reshape(MLA_Q_LORA, HP)], axis=1).astype(BF16)

    wkv = w_kv_up.reshape(MLA_KV_LORA, N_HEADS, MLA_NOPE + MLA_V)
    wkn = jnp.pad(wkv[:, :, :MLA_NOPE], [(0, 0), (0, 0), (0, HEAD_PAD - MLA_NOPE)])
    wv = jnp.pad(wkv[:, :, MLA_NOPE:], [(0, 0), (0, 0), (0, HEAD_PAD - MLA_V)])
    w_kv = jnp.concatenate([wkn.reshape(MLA_KV_LORA, HP), wv.reshape(MLA_KV_LORA, HP)], axis=1).astype(BF16)

    wa = w_out[:NA_W].astype(BF16)
    wb = w_out[NA_W:].reshape(N_HEADS, MLA_V, D_MODEL)
    wb = jnp.pad(wb, [(0, 0), (0, HEAD_PAD - MLA_V), (0, 0)]).reshape(HP, D_MODEL).astype(BF16)
    return w_in_aug, w_q, w_kv, wa, wb


def _rope_tables(s, n_ctx):
    t = jnp.arange(s, dtype=jnp.int32)
    row = (t // GRID_W).astype(F32)
    col = (t % GRID_W).astype(F32)
    n_freq = MLA_ROPE // 4
    inv = 1.0 / (ROPE_THETA ** (jnp.arange(n_freq, dtype=F32) / n_freq))
    ar = row[:, None] * inv
    ac = col[:, None] * inv
    cos = jnp.concatenate([jnp.cos(ar), jnp.cos(ar), jnp.cos(ac), jnp.cos(ac)], axis=1)
    sin = jnp.concatenate([-jnp.sin(ar), jnp.sin(ar), -jnp.sin(ac), jnp.sin(ac)], axis=1)
    right = HEAD_PAD - MLA_NOPE - MLA_ROPE
    cos = jnp.pad(cos, [(0, n_ctx), (MLA_NOPE, right)], constant_values=1.0)
    sin = jnp.pad(sin, [(0, n_ctx), (MLA_NOPE, right)])
    return cos, sin


def _na_bias_mask(rel_bias, n_rows):
    nb = n_rows // NA_QROWS
    c = np.arange(GRID_W)
    cs = np.clip(c - NA_KW // 2, 0, GRID_W - NA_KW)
    kc = np.arange(GRID_W)
    ok_c = (kc[None, :] >= cs[:, None]) & (kc[None, :] < cs[:, None] + NA_KW)
    n_dcol = 2 * NA_KW - 1
    left = GRID_W - NA_KW
    period = 2 * GRID_W
    u = jnp.pad(rel_bias, [(0, 0), (0, 0), (left, period - n_dcol - left)])
    flat = jnp.tile(u, (1, 1, GRID_W + 1))[:, :, :GRID_W * (period - 1)]
    toep = flat.reshape(N_HEADS, 2 * NA_KH - 1, GRID_W, period - 1)[:, :, :, GRID_W - 1:]
    toep = jnp.where(ok_c, toep, NEG_BIG)
    masked = jnp.full((N_HEADS, GRID_W, GRID_W), NEG_BIG, F32)
    out = []
    for b in (0, 1, nb - 1):
        kr0 = int(np.clip(b * NA_QROWS - NA_KH // 2, 0, n_rows - NA_KROWS))
        q_blocks = []
        for qr in range(NA_QROWS):
            r = b * NA_QROWS + qr
            rs = int(np.clip(r - NA_KH // 2, 0, n_rows - NA_KH))
            k_blocks = []
            for klr in range(NA_KROWS):
                kr = kr0 + klr
                k_blocks.append(toep[:, kr - r + NA_KH - 1] if rs <= kr < rs + NA_KH else masked)
            q_blocks.append(jnp.concatenate(k_blocks, axis=-1))
        out.append(jnp.concatenate(q_blocks, axis=1))
    return jnp.stack(out, axis=0)


def kernel(x, c, ctx, c_ctx, mod_w, mod_b, ln1_g, ln1_b, ln2_g, ln2_b, attn_w_in, na_rel_bias, mla_q_norm,
           mla_w_q_up, mla_kv_norm, mla_w_kv_up, attn_w_out, ffn_w_gate, ffn_w_up, ffn_w_down, pool_w,
           pool_scale, moe_router, moe_w_gate, moe_w_up, moe_w_down):
    assert x.shape[0] == 1 and c.shape[0] == 1 and ctx.shape[0] == 1
    s = x.shape[1]
    n_ctx = ctx.shape[1]
    assert n_ctx == TM and s % (NA_QROWS * GRID_W) == 0 and s % MLA_TQ == 0
    assert (s + n_ctx) % MLA_TK == 0
    n_lat_tiles = s // TM
    depth = mod_w.shape[0]

    xs = (x[0], ctx[0])
    ct = jnp.stack([c[0], c_ctx], axis=1)
    mods = _modulation(ct, mod_w, mod_b).reshape(depth, 2, 6, D_MODEL)
    cos_t, sin_t = _rope_tables(s, n_ctx)
    row = lambda v: v.reshape(1, -1)
    moe_bf16 = None

    for i in range(depth):
        j = i // 2
        ctx_live = any(l % 2 == 0 for l in range(i + 1, depth))
        if i % 2 == 0:
            w_in, w_q, w_kv, wa, wb = _prep_attn_weights(attn_w_in[j], mla_w_q_up[j], mla_w_kv_up[j],
                                                         attn_w_out[j])
            qa, ka, va, qm, kmt, vm = _attn_proj(xs, mods, i, n_lat_tiles, w_in, row(mla_q_norm[j]), w_q,
                                                 row(mla_kv_norm[j]), w_kv, cos_t, sin_t)
            bias_mask = _na_bias_mask(na_rel_bias[j], s // GRID_W)
            oa_lat = _na_attention(qa, ka, va, bias_mask, s)
            cast = (moe_w_gate, moe_w_up, moe_w_down) if i + 1 < depth else ()
            ob_lat, moe_bf16 = _mla_attention(qm, kmt, vm, s, cast, j)
            oa_ctx = ob_ctx = None
            if ctx_live:
                oa_ctx = _na_ctx_attention(qa, ka, va, s)
                ob_ctx = _mla_ctx_attention(qm, kmt, vm, s)
            xs = _out_ffn(xs, mods, i, n_lat_tiles, oa_lat, ob_lat, oa_ctx, ob_ctx, wa, wb,
                          row(ln1_g[i]), row(ln1_b[i]), ffn_w_gate[j].astype(BF16), ffn_w_up[j].astype(BF16),
                          ffn_w_down[j].astype(BF16), row(ln2_g[i]), row(ln2_b[i]))
        else:
            xs = _pool(xs, mods, i, n_lat_tiles, s, pool_w[j].astype(BF16), row(pool_scale[j]),
                       row(ln1_g[i]), row(ln1_b[i]))
            xs = _moe_layer(xs, mods, i, n_lat_tiles, moe_router[j], *moe_bf16, row(ln2_g[i]), row(ln2_b[i]))
    return xs[:s][None] if xs.shape[0] != s else xs[None]
```

```python
import functools

import numpy as np
import jax
import jax.numpy as jnp
from jax import lax
from jax.experimental import pallas as pl
from jax.experimental.pallas import tpu as pltpu

F32 = jnp.float32
BF16 = jnp.bfloat16

D_MODEL = 1024
GRID_W = 64
N_HEADS = 8
HEAD_PAD = 128
HP = N_HEADS * HEAD_PAD
NA_HEAD_DIM = 64
NA_W = N_HEADS * NA_HEAD_DIM
NA_PAIRS = N_HEADS // 2
NA_KH = 8
NA_KW = 16
MLA_NOPE = 64
MLA_ROPE = 32
MLA_V = 64
MLA_Q_LORA = 256
MLA_KV_LORA = 128
ROPE_THETA = 10000.0
POOL_WINDOWS = (2, 4, 8, 16)
POOL_GROUP = D_MODEL // len(POOL_WINDOWS)
POOL_HALO = 8
FFN_DIM = 2816
N_EXPERTS = 8
EXPERT_DIM = 3584
DEPTH = 4
DEEPNORM_ALPHA = (2 * DEPTH) ** 0.25
LN_EPS = 1e-5
RMS_EPS = 1e-6
LOG2E = 1.4426950408889634
NEG_BIG = -1e30

TM = 256
NA_QROWS = 4
NA_KROWS = NA_QROWS + NA_KH - 1
NA_QB = 4
MLA_TQ = 1024
MLA_TK = 640
MLA_UNROLL = 12
MOE_TM = 512
MOE_TF = 1792
VMEM_LIMIT = 56 * 1024 * 1024


def _cparams(sem):
    return pltpu.CompilerParams(dimension_semantics=sem, vmem_limit_bytes=VMEM_LIMIT)


def _layer_norm(v, g, b):
    mu = jnp.mean(v, axis=-1, keepdims=True)
    d = v - mu
    var = jnp.mean(d * d, axis=-1, keepdims=True)
    return d * lax.rsqrt(var + LN_EPS) * g + b


def _silu(v):
    return v * jax.nn.sigmoid(v)


def _token_specs(xs, n_lat_tiles):
    if isinstance(xs, tuple):
        return list(xs), [pl.BlockSpec((TM, D_MODEL), lambda i: (jnp.minimum(i, n_lat_tiles - 1), 0)),
                          pl.BlockSpec((TM, D_MODEL), lambda i: (0, 0))]
    return [xs], [pl.BlockSpec((TM, D_MODEL), lambda i: (i, 0))]


def _load_tokens(x_refs, n_lat_tiles):
    if len(x_refs) == 2:
        return jnp.where(pl.program_id(0) >= n_lat_tiles, x_refs[1][...], x_refs[0][...])
    return x_refs[0][...]


def _token_rows(xs):
    return sum(a.shape[0] for a in xs) if isinstance(xs, tuple) else xs.shape[0]


def _mod_kernel(ct_ref, w_ref, b_ref, o_ref):
    s = _silu(ct_ref[...])
    w = w_ref[...]
    b = b_ref[...]
    r0 = jnp.sum(w * s[:, 0:1], axis=0, keepdims=True) + b
    r1 = jnp.sum(w * s[:, 1:2], axis=0, keepdims=True) + b
    o_ref[...] = jnp.concatenate([r0, r1], axis=0)


def _modulation(ct, mod_w, mod_b):
    depth, d, n6 = mod_w.shape
    tn = 1536
    return pl.pallas_call(
        _mod_kernel,
        grid=(depth, n6 // tn),
        in_specs=[
            pl.BlockSpec((d, 2), lambda l, j: (0, 0)),
            pl.BlockSpec((None, d, tn), lambda l, j: (l, 0, j)),
            pl.BlockSpec((None, 1, tn), lambda l, j: (l, 0, j)),
        ],
        out_specs=pl.BlockSpec((None, 2, tn), lambda l, j: (l, 0, j)),
        out_shape=jax.ShapeDtypeStruct((depth, 2, n6), F32),
        compiler_params=_cparams(("arbitrary", "arbitrary")),
        name="modulation",
    )(ct, mod_w, mod_b.reshape(depth, 1, n6))


def _attn_proj_kernel(*refs, n_x, n_lat_tiles):
    x_refs = refs[:n_x]
    (mod_ref, win_ref, qn_ref, wq_ref, kvn_ref, wkv_ref, cos_ref, sin_ref,
     qa_ref, ka_ref, va_ref, qm_ref, kmt_ref, vm_ref) = refs[n_x:]
    h = _load_tokens(x_refs, n_lat_tiles) * (1.0 + mod_ref[1:2, :]) + mod_ref[0:1, :]
    p = jnp.dot(h.astype(BF16), win_ref[...], preferred_element_type=F32)
    qa_ref[...] = (p[:, 0:NA_W] * (NA_HEAD_DIM ** -0.5)).astype(BF16)
    ka_ref[...] = p[:, NA_W:2 * NA_W].astype(BF16)
    va_ref[...] = p[:, 2 * NA_W:3 * NA_W].astype(BF16)
    o = 3 * NA_W
    q_c = p[:, o:o + MLA_Q_LORA]
    o += MLA_Q_LORA
    kv_c = p[:, o:o + MLA_KV_LORA]
    o += MLA_KV_LORA
    krp = p[:, o:o + HEAD_PAD]
    krs = p[:, o + HEAD_PAD:o + 2 * HEAD_PAD]
    qn = q_c * lax.rsqrt(jnp.mean(q_c * q_c, axis=-1, keepdims=True) + RMS_EPS) * qn_ref[...]
    kvn = kv_c * lax.rsqrt(jnp.mean(kv_c * kv_c, axis=-1, keepdims=True) + RMS_EPS) * kvn_ref[...]
    q2 = jnp.dot(qn.astype(BF16), wq_ref[...], preferred_element_type=F32)
    kv2 = jnp.dot(kvn.astype(BF16), wkv_ref[...], preferred_element_type=F32)
    cos = cos_ref[...]
    sin = sin_ref[...]
    kr = krp * cos + krs * sin
    q_scale = (MLA_NOPE + MLA_ROPE) ** -0.5 * LOG2E
    for hd in range(N_HEADS):
        lo = hd * HEAD_PAD
        qh = (q2[:, lo:lo + HEAD_PAD] * cos + q2[:, HP + lo:HP + lo + HEAD_PAD] * sin) * q_scale
        qm_ref[:, lo:lo + HEAD_PAD] = qh.astype(BF16)
        kh = kv2[:, lo:lo + HEAD_PAD] + kr
        kmt_ref[hd] = kh.T.astype(BF16)
    lane = lax.broadcasted_iota(jnp.int32, (1, HP), 1)
    ones_col = jnp.where(lane % HEAD_PAD == MLA_V, 1.0, 0.0)
    vm_ref[...] = (kv2[:, HP:2 * HP] + ones_col).astype(BF16)


def _attn_proj(xs, mods, layer, n_lat_tiles, w_in, q_norm, w_q, kv_norm, w_kv, cos_t, sin_t):
    m = _token_rows(xs)
    nt = m // TM
    tok = lambda i: (i, 0)
    full = lambda i: (0, 0)
    act = jax.ShapeDtypeStruct((m, HP), BF16)
    na_act = jax.ShapeDtypeStruct((m, NA_W), BF16)
    x_args, x_specs = _token_specs(xs, n_lat_tiles)
    return pl.pallas_call(
        functools.partial(_attn_proj_kernel, n_x=len(x_args), n_lat_tiles=n_lat_tiles),
        grid=(nt,),
        in_specs=x_specs + [
            pl.BlockSpec((None, None, 6, D_MODEL), lambda i: (layer, i // n_lat_tiles, 0, 0)),
            pl.BlockSpec(w_in.shape, full),
            pl.BlockSpec(q_norm.shape, full),
            pl.BlockSpec(w_q.shape, full),
            pl.BlockSpec(kv_norm.shape, full),
            pl.BlockSpec(w_kv.shape, full),
            pl.BlockSpec((TM, HEAD_PAD), tok),
            pl.BlockSpec((TM, HEAD_PAD), tok),
        ],
        out_specs=[
            pl.BlockSpec((TM, NA_W), tok),
            pl.BlockSpec((TM, NA_W), tok),
            pl.BlockSpec((TM, NA_W), tok),
            pl.BlockSpec((TM, HP), tok),
            pl.BlockSpec((N_HEADS, HEAD_PAD, TM), lambda i: (0, 0, i)),
            pl.BlockSpec((TM, HP), tok),
        ],
        out_shape=[na_act, na_act, na_act, act, jax.ShapeDtypeStruct((N_HEADS, HEAD_PAD, m), BF16), act],
        compiler_params=_cparams(("arbitrary",)),
        name="attn_proj",
    )(*x_args, mods, w_in, q_norm, w_q, kv_norm, w_kv, cos_t, sin_t)


_NT_DIMS = (((1,), (1,)), ((), ()))


def _pair_masks():
    lane = lax.broadcasted_iota(jnp.int32, (1, HEAD_PAD), 1)
    return lane < NA_HEAD_DIM, lane >= NA_HEAD_DIM


def _na_kernel(q_ref, k_ref, v_ref, kc_ref, vc_ref, *rest, n_rows):
    bm_refs, o_ref = rest[:NA_QB], rest[NA_QB]
    tq = NA_QROWS * GRID_W
    nk = NA_KROWS * GRID_W
    kc = kc_ref[...]
    vc = vc_ref[...]
    head_lanes = _pair_masks()
    for u in range(NA_QB):
        b = pl.program_id(1) * NA_QB + u
        kr0 = jnp.clip(b * NA_QROWS - NA_KH // 2, 0, n_rows - NA_KROWS)
        start = pl.multiple_of(kr0 * GRID_W, GRID_W)
        q = q_ref[u * tq:(u + 1) * tq, :]
        kw = k_ref[pl.ds(start, nk), :]
        vw = v_ref[pl.ds(start, nk), :]
        outs = []
        for a in range(2):
            qh = jnp.where(head_lanes[a], q, jnp.zeros_like(q))
            s_loc = lax.dot_general(qh, kw, _NT_DIMS, preferred_element_type=F32) + bm_refs[u][a]
            s_ctx = lax.dot_general(qh, kc, _NT_DIMS, preferred_element_type=F32)
            m = jnp.maximum(jnp.max(s_loc, axis=-1, keepdims=True), jnp.max(s_ctx, axis=-1, keepdims=True))
            p_loc = jnp.exp(s_loc - m)
            p_ctx = jnp.exp(s_ctx - m)
            l = jnp.sum(p_loc, axis=-1, keepdims=True) + jnp.sum(p_ctx, axis=-1, keepdims=True)
            o = (jnp.dot(p_loc.astype(BF16), vw, preferred_element_type=F32)
                 + jnp.dot(p_ctx.astype(BF16), vc, preferred_element_type=F32))
            outs.append(o / l)
        o_ref[u * tq:(u + 1) * tq, :] = jnp.where(head_lanes[0], outs[0], outs[1]).astype(BF16)


def _na_attention(qa, ka, va, bias_mask, s):
    n_rows = s // GRID_W
    nb = n_rows // NA_QROWS
    tq = NA_QROWS * GRID_W
    assert nb % NA_QB == 0
    ctx_blk = s // tq
    pat = lambda b: jnp.where(b == 0, 0, jnp.where(b == nb - 1, 2, 1))
    bm_spec = lambda u: pl.BlockSpec((None, 2, tq, NA_KROWS * GRID_W),
                                     lambda hp, j: (pat(j * NA_QB + u), hp, 0, 0))
    return pl.pallas_call(
        functools.partial(_na_kernel, n_rows=n_rows),
        grid=(NA_PAIRS, nb // NA_QB),
        in_specs=[
            pl.BlockSpec((NA_QB * tq, HEAD_PAD), lambda hp, j: (j, hp)),
            pl.BlockSpec((s, HEAD_PAD), lambda hp, j: (0, hp)),
            pl.BlockSpec((s, HEAD_PAD), lambda hp, j: (0, hp)),
            pl.BlockSpec((tq, HEAD_PAD), lambda hp, j: (ctx_blk, hp)),
            pl.BlockSpec((tq, HEAD_PAD), lambda hp, j: (ctx_blk, hp)),
        ] + [bm_spec(u) for u in range(NA_QB)],
        out_specs=pl.BlockSpec((NA_QB * tq, HEAD_PAD), lambda hp, j: (j, hp)),
        out_shape=jax.ShapeDtypeStruct((s, NA_W), BF16),
        compiler_params=_cparams(("arbitrary", "arbitrary")),
        name="na_attention",
    )(qa, ka, va, ka, va, *([bias_mask] * NA_QB))


def _na_ctx_kernel(q_ref, k_ref, v_ref, o_ref):
    q = q_ref[...]
    k = k_ref[...]
    v = v_ref[...]
    head_lanes = _pair_masks()
    outs = []
    for a in range(2):
        qh = jnp.where(head_lanes[a], q, jnp.zeros_like(q))
        s = lax.dot_general(qh, k, _NT_DIMS, preferred_element_type=F32)
        m = jnp.max(s, axis=-1, keepdims=True)
        p = jnp.exp(s - m)
        l = jnp.sum(p, axis=-1, keepdims=True)
        outs.append(jnp.dot(p.astype(BF16), v, preferred_element_type=F32) / l)
    o_ref[...] = jnp.where(head_lanes[0], outs[0], outs[1]).astype(BF16)


def _na_ctx_attention(qa, ka, va, s):
    blk = s // TM
    spec = pl.BlockSpec((TM, HEAD_PAD), lambda hp: (blk, hp))
    return pl.pallas_call(
        _na_ctx_kernel,
        grid=(NA_PAIRS,),
        in_specs=[spec, spec, spec],
        out_specs=pl.BlockSpec((TM, HEAD_PAD), lambda hp: (0, hp)),
        out_shape=jax.ShapeDtypeStruct((TM, NA_W), BF16),
        compiler_params=_cparams(("arbitrary",)),
        name="na_ctx_attention",
    )(qa, ka, va)


def _mla_ctx_kernel(q_ref, kt_ref, v_ref, o_ref):
    s = jnp.dot(q_ref[...], kt_ref[...], preferred_element_type=F32)
    m = jnp.max(s, axis=-1, keepdims=True)
    p = jnp.exp2(s - m)
    l = jnp.sum(p, axis=-1, keepdims=True)
    o = jnp.dot(p.astype(BF16), v_ref[...], preferred_element_type=F32)
    o_ref[...] = (o / l).astype(BF16)


def _mla_ctx_attention(qm, kmt, vm, s):
    blk = s // TM
    return pl.pallas_call(
        _mla_ctx_kernel,
        grid=(N_HEADS,),
        in_specs=[
            pl.BlockSpec((TM, HEAD_PAD), lambda h: (blk, h)),
            pl.BlockSpec((None, HEAD_PAD, TM), lambda h: (h, 0, blk)),
            pl.BlockSpec((TM, HEAD_PAD), lambda h: (blk, h)),
        ],
        out_specs=pl.BlockSpec((TM, HEAD_PAD), lambda h: (0, h)),
        out_shape=jax.ShapeDtypeStruct((TM, HP), BF16),
        compiler_params=_cparams(("arbitrary",)),
        name="mla_ctx_attention",
    )(qm, kmt, vm)


def _mla_kernel(q_ref, qn_ref, kt_ref, v_ref, *rest, n_chunks, n_cast):
    cast_in, o_ref, cast_out = rest[:n_cast], rest[n_cast], rest[n_cast + 1:2 * n_cast + 1]
    s0_ref, s1_ref = rest[2 * n_cast + 1:]
    for w_ref, wb_ref in zip(cast_in, cast_out):
        wb_ref[...] = w_ref[...].astype(BF16)
    q = q_ref[...]
    tq = q.shape[0]

    def scores(c, s_ref, q_tile=q):
        off = pl.multiple_of(c * MLA_TK, MLA_TK)
        s_ref[...] = jnp.dot(q_tile, kt_ref[:, pl.ds(off, MLA_TK)], preferred_element_type=F32)

    def update(c, s_ref, carry):
        m, acc = carry
        off = pl.multiple_of(c * MLA_TK, MLA_TK)
        s = s_ref[...]
        m_new = jnp.maximum(m, jnp.max(s, axis=-1, keepdims=True))
        alpha = jnp.exp2(m - m_new)
        p = jnp.exp2(s - m_new).astype(BF16)
        acc = alpha * acc + jnp.dot(p, v_ref[pl.ds(off, MLA_TK), :], preferred_element_type=F32)
        return m_new, acc

    bufs = (s0_ref, s1_ref)

    def group(g, carry):
        c0 = g * MLA_UNROLL
        for u in range(MLA_UNROLL):
            scores(c0 + u + 1, bufs[(u + 1) % 2])
            carry = update(c0 + u, bufs[u % 2], carry)
        return carry

    carry = (jnp.full((tq, 1), NEG_BIG, F32), jnp.zeros((tq, HEAD_PAD), F32))

    @pl.when(pl.program_id(1) == 0)
    def _():
        scores(0, s0_ref)

    n_groups = (n_chunks - 2) // MLA_UNROLL
    carry = lax.fori_loop(0, n_groups, group, carry)
    for c in range(n_groups * MLA_UNROLL, n_chunks):
        if c + 1 < n_chunks:
            scores(c + 1, bufs[(c + 1) % 2])
        else:
            scores(0, s0_ref, qn_ref[...])
        carry = update(c, bufs[c % 2], carry)
    _, acc = carry
    o_ref[...] = (acc / acc[:, MLA_V:MLA_V + 1]).astype(BF16)


def _mla_attention(qm, kmt, vm, s, cast_weights=(), cast_layer=0):
    m = qm.shape[0]
    n_chunks = m // MLA_TK
    assert n_chunks % 2 == 0 and n_chunks >= 2
    n_q = s // MLA_TQ
    assert N_HEADS == N_EXPERTS or not cast_weights
    cast_in_specs, cast_out_specs, cast_shapes = [], [], []
    for w in cast_weights:
        rows, cols = w.shape[2], w.shape[3]
        slab = rows // n_q
        assert rows % n_q == 0 and slab % 16 == 0
        cast_in_specs.append(pl.BlockSpec((None, None, slab, cols), lambda h, i: (cast_layer, h, i, 0)))
        cast_out_specs.append(pl.BlockSpec((None, slab, cols), lambda h, i: (h, i, 0)))
        cast_shapes.append(jax.ShapeDtypeStruct((N_EXPERTS, rows, cols), BF16))
    outs = pl.pallas_call(
        functools.partial(_mla_kernel, n_chunks=n_chunks, n_cast=len(cast_weights)),
        grid=(N_HEADS, n_q),
        in_specs=[
            pl.BlockSpec((MLA_TQ, HEAD_PAD), lambda h, i: (i, h)),
            pl.BlockSpec((MLA_TQ, HEAD_PAD), lambda h, i: (jnp.minimum(i + 1, n_q - 1), h)),
            pl.BlockSpec((None, HEAD_PAD, m), lambda h, i: (h, 0, 0)),
            pl.BlockSpec((m, HEAD_PAD), lambda h, i: (0, h)),
        ] + cast_in_specs,
        out_specs=[pl.BlockSpec((MLA_TQ, HEAD_PAD), lambda h, i: (i, h))] + cast_out_specs,
        out_shape=[jax.ShapeDtypeStruct((s, HP), BF16)] + cast_shapes,
        scratch_shapes=[pltpu.VMEM((MLA_TQ, MLA_TK), F32), pltpu.VMEM((MLA_TQ, MLA_TK), F32)],
        compiler_params=_cparams(("arbitrary", "arbitrary")),
        name="mla_attention",
    )(qm, qm, kmt, vm, *cast_weights)
    return outs[0], tuple(outs[1:])


def _out_ffn_kernel(*refs, n_x, n_lat_tiles, with_ctx):
    x_refs = refs[:n_x]
    mod_ref, oal_ref, obl_ref = refs[n_x:n_x + 3]
    rest = refs[n_x + 3:]
    if with_ctx:
        oac_ref, obc_ref = rest[:2]
        rest = rest[2:]
        is_ctx = pl.program_id(0) >= n_lat_tiles
        oa = jnp.where(is_ctx, oac_ref[...], oal_ref[...])
        ob = jnp.where(is_ctx, obc_ref[...], obl_ref[...])
    else:
        oa = oal_ref[...]
        ob = obl_ref[...]
    wa_ref, wb_ref, g1_ref, b1_ref, wg_ref, wu_ref, wd_ref, g2_ref, b2_ref, o_ref = rest
    y = (jnp.dot(oa, wa_ref[...], preferred_element_type=F32)
         + jnp.dot(ob, wb_ref[...], preferred_element_type=F32))
    v = DEEPNORM_ALPHA * _load_tokens(x_refs, n_lat_tiles) + mod_ref[2:3, :] * y
    x1 = _layer_norm(v, g1_ref[...], b1_ref[...])
    h = (x1 * (1.0 + mod_ref[4:5, :]) + mod_ref[3:4, :]).astype(BF16)
    gate = jnp.dot(h, wg_ref[...], preferred_element_type=F32)
    up = jnp.dot(h, wu_ref[...], preferred_element_type=F32)
    a = (_silu(gate) * up).astype(BF16)
    y2 = jnp.dot(a, wd_ref[...], preferred_element_type=F32)
    v2 = DEEPNORM_ALPHA * x1 + mod_ref[5:6, :] * y2
    o_ref[...] = _layer_norm(v2, g2_ref[...], b2_ref[...])


def _out_ffn(xs, mods, layer, n_lat_tiles, oa_lat, ob_lat, oa_ctx, ob_ctx, wa, wb, ln1_g, ln1_b,
             wg, wu, wd, ln2_g, ln2_b):
    with_ctx = oa_ctx is not None
    m = _token_rows(xs) if with_ctx else n_lat_tiles * TM
    tok = lambda i: (i, 0)
    lat = lambda i: (jnp.minimum(i, n_lat_tiles - 1), 0)
    full = lambda i: (0, 0)
    ctx_args = [oa_ctx, ob_ctx] if with_ctx else []
    x_args, x_specs = _token_specs(xs, n_lat_tiles)
    resident = lambda w: pl.BlockSpec(w.shape, full, pipeline_mode=pl.Buffered(1))
    vec = pl.BlockSpec((1, D_MODEL), full)
    return pl.pallas_call(
        functools.partial(_out_ffn_kernel, n_x=len(x_args), n_lat_tiles=n_lat_tiles, with_ctx=with_ctx),
        grid=(m // TM,),
        in_specs=x_specs + [
            pl.BlockSpec((None, None, 6, D_MODEL), lambda i: (layer, i // n_lat_tiles, 0, 0)),
            pl.BlockSpec((TM, NA_W), lat),
            pl.BlockSpec((TM, HP), lat),
        ] + ([pl.BlockSpec((TM, NA_W), full), pl.BlockSpec((TM, HP), full)] if with_ctx else []) + [
            resident(wa), resident(wb), vec, vec, resident(wg), resident(wu), resident(wd), vec, vec,
        ],
        out_specs=pl.BlockSpec((TM, D_MODEL), tok),
        out_shape=jax.ShapeDtypeStruct((m, D_MODEL), F32),
        compiler_params=_cparams(("arbitrary",)),
        name="out_proj_ffn_ln",
    )(*x_args, mods, oa_lat, ob_lat, *ctx_args, wa, wb, ln1_g, ln1_b, wg, wu, wd, ln2_g, ln2_b)


def _pool_kernel(x_ref, xp_ref, xn_ref, mod_ref, w_ref, sc_ref, g_ref, b_ref, o_ref, ext_ref,
                 *, n_lat_tiles, s):
    i = pl.program_id(0)
    is_ctx = i >= n_lat_tiles
    j = jnp.where(is_ctx, i - n_lat_tiles, i)
    last = jnp.where(is_ctx, 0, n_lat_tiles - 1)
    n_seq = jnp.where(is_ctx, TM, s)
    scale1 = 1.0 + mod_ref[1:2, :]
    shift1 = mod_ref[0:1, :]
    x = x_ref[...]
    h = x * scale1 + shift1
    hp = jnp.where(j != 0, xp_ref[...] * scale1 + shift1, 0.0)
    hn = jnp.where(j != last, xn_ref[...] * scale1 + shift1, 0.0)
    ext_ref[0:POOL_HALO, :] = hp
    ext_ref[POOL_HALO:POOL_HALO + TM, :] = h
    ext_ref[POOL_HALO + TM:POOL_HALO + TM + POOL_HALO, :] = hn
    t = j * TM + lax.broadcasted_iota(jnp.int32, (TM, 1), 0)
    ys = []
    for g, win in enumerate(POOL_WINDOWS):
        half = win // 2
        lo = g * POOL_GROUP
        acc = ext_ref[POOL_HALO - half:POOL_HALO - half + TM, lo:lo + POOL_GROUP]
        for dlt in range(-half + 1, half):
            acc = acc + ext_ref[POOL_HALO + dlt:POOL_HALO + dlt + TM, lo:lo + POOL_GROUP]
        cnt = (jnp.minimum(t + half, n_seq) - jnp.maximum(t - half, 0)).astype(F32)
        mixed = acc / cnt - h[:, lo:lo + POOL_GROUP]
        yg = jnp.dot(mixed.astype(BF16), w_ref[g], preferred_element_type=F32)
        ys.append(yg)
    y = jnp.concatenate(ys, axis=-1) * sc_ref[...]
    v = DEEPNORM_ALPHA * x + mod_ref[2:3, :] * y
    o_ref[...] = _layer_norm(v, g_ref[...], b_ref[...])


def _pool(xs, mods, layer, n_lat_tiles, s, pool_w, pool_scale, ln_g, ln_b):
    m = xs.shape[0]
    per = TM // POOL_HALO
    n_halo_blocks = m // POOL_HALO
    tok = lambda i: (i, 0)
    full = lambda i: (0, 0)
    return pl.pallas_call(
        functools.partial(_pool_kernel, n_lat_tiles=n_lat_tiles, s=s),
        grid=(m // TM,),
        in_specs=[
            pl.BlockSpec((TM, D_MODEL), tok),
            pl.BlockSpec((POOL_HALO, D_MODEL), lambda i: (jnp.maximum(i * per - 1, 0), 0)),
            pl.BlockSpec((POOL_HALO, D_MODEL), lambda i: (jnp.minimum((i + 1) * per, n_halo_blocks - 1), 0)),
            pl.BlockSpec((None, None, 6, D_MODEL), lambda i: (layer, i // n_lat_tiles, 0, 0)),
            pl.BlockSpec(pool_w.shape, lambda i: (0, 0, 0)),
            pl.BlockSpec((1, D_MODEL), full),
            pl.BlockSpec((1, D_MODEL), full),
            pl.BlockSpec((1, D_MODEL), full),
        ],
        out_specs=pl.BlockSpec((TM, D_MODEL), tok),
        out_shape=jax.ShapeDtypeStruct((m, D_MODEL), F32),
        scratch_shapes=[pltpu.VMEM((TM + 2 * POOL_HALO, D_MODEL), F32)],
        compiler_params=_cparams(("arbitrary",)),
        name="pool_ln",
    )(xs, xs, xs, mods, pool_w, pool_scale, ln_g, ln_b)


def _router_kernel(x_ref, mod_ref, rw_ref, tri_ref, idx_ref, wts_ref, cnt_ref, base_ref):
    @pl.when(pl.program_id(0) == 0)
    def _():
        base_ref[...] = jnp.zeros_like(base_ref)

    h = x_ref[...] * (1.0 + mod_ref[4:5, :]) + mod_ref[3:4, :]
    h_hi = h.astype(BF16)
    h_lo = (h - h_hi.astype(F32)).astype(BF16)
    logits = (jnp.dot(h_hi, rw_ref[0], preferred_element_type=F32)
              + jnp.dot(h_hi, rw_ref[1], preferred_element_type=F32)
              + jnp.dot(h_lo, rw_ref[0], preferred_element_type=F32))
    lane = lax.broadcasted_iota(jnp.int32, logits.shape, 1)
    logits = jnp.where(lane < N_EXPERTS, logits, -jnp.inf)
    m1 = jnp.max(logits, axis=-1, keepdims=True)
    i1 = jnp.min(jnp.where(logits == m1, lane, HEAD_PAD), axis=-1, keepdims=True)
    rest = jnp.where(lane == i1, -jnp.inf, logits)
    m2 = jnp.max(rest, axis=-1, keepdims=True)
    i2 = jnp.min(jnp.where(rest == m2, lane, HEAD_PAD), axis=-1, keepdims=True)
    e2 = jnp.exp(m2 - m1)
    w1 = 1.0 / (1.0 + e2)
    w2 = e2 / (1.0 + e2)
    wts_ref[...] = jnp.where(lane == 0, w1, jnp.where(lane == 1, w2, 0.0))

    chosen = jnp.where(lane == i1, 1.0, jnp.where(lane == i2, 1.0, 0.0))
    before = jnp.dot(tri_ref[...], chosen.astype(BF16), preferred_element_type=F32)
    rank = before + base_ref[0:1, :]
    r1 = jnp.sum(jnp.where(lane == i1, rank, 0.0), axis=-1, keepdims=True)
    r2 = jnp.sum(jnp.where(lane == i2, rank, 0.0), axis=-1, keepdims=True)
    base_ref[0:1, :] = base_ref[0:1, :] + jnp.sum(chosen, axis=0, keepdims=True)
    cnt_ref[...] = base_ref[...]
    packed = jnp.where(lane == 0, i1.astype(F32),
                       jnp.where(lane == 1, i2.astype(F32),
                                 jnp.where(lane == 2, r1, jnp.where(lane == 3, r2, 0.0))))
    idx_ref[...] = packed.T[0:8, :].astype(jnp.int32)


def _router(xs, mods, layer, n_lat_tiles, rw):
    m = xs.shape[0]
    nt = m // TM
    tok = lambda i: (i, 0)
    tri = jnp.asarray(np.tril(np.ones((TM, TM), np.float32), -1), dtype=BF16)
    return pl.pallas_call(
        _router_kernel,
        grid=(nt,),
        in_specs=[
            pl.BlockSpec((TM, D_MODEL), tok),
            pl.BlockSpec((None, None, 6, D_MODEL), lambda i: (layer, i // n_lat_tiles, 0, 0)),
            pl.BlockSpec(rw.shape, lambda i: (0, 0, 0)),
            pl.BlockSpec((TM, TM), lambda i: (0, 0)),
        ],
        out_specs=[
            pl.BlockSpec((None, 8, TM), lambda i: (i, 0, 0)),
            pl.BlockSpec((TM, HEAD_PAD), tok),
            pl.BlockSpec((8, HEAD_PAD), lambda i: (0, 0)),
        ],
        out_shape=[
            jax.ShapeDtypeStruct((nt, 8, TM), jnp.int32),
            jax.ShapeDtypeStruct((m, HEAD_PAD), F32),
            jax.ShapeDtypeStruct((8, HEAD_PAD), F32),
        ],
        scratch_shapes=[pltpu.VMEM((8, HEAD_PAD), F32)],
        compiler_params=_cparams(("arbitrary",)),
        name="moe_router",
    )(xs, mods, rw, tri)


def _dispatch_kernel(fill_ref, nu_ref, rows_ref, x_ref, mod_ref, xs_ref, hbuf, zbuf, sems, zsem):
    i = pl.program_id(0)
    n = pl.num_programs(0)
    slot = i % 2
    n_tiles = xs_ref.shape[0] // MOE_TM

    @pl.when(i == 0)
    def _():
        zbuf[...] = jnp.zeros_like(zbuf)
        fills = [pltpu.make_async_copy(zbuf, xs_ref.at[pl.ds(pl.multiple_of(fill_ref[e], 8), MOE_TM)],
                                       zsem.at[0]) for e in range(N_EXPERTS)]
        for cp in fills:
            cp.start()
        for cp in fills:
            cp.wait()

        def zero_tail(t, carry):
            row0 = pl.multiple_of(t * MOE_TM, MOE_TM)
            cp = pltpu.make_async_copy(zbuf, xs_ref.at[pl.ds(row0, MOE_TM)], zsem.at[0])
            cp.start()
            cp.wait()
            return carry

        lax.fori_loop(nu_ref[0], n_tiles, zero_tail, 0)

    def wait_slot(sl):
        for _ in range(2):
            pltpu.make_async_copy(hbuf.at[sl], xs_ref.at[pl.ds(0, TM)], sems.at[sl]).wait()

    @pl.when(i >= 2)
    def _():
        wait_slot(slot)

    hbuf[slot] = x_ref[...] * (1.0 + mod_ref[4:5, :]) + mod_ref[3:4, :]

    def issue(r, carry):
        for k in range(2):
            dst = rows_ref[0, k * TM + r]
            pltpu.make_async_copy(hbuf.at[slot, pl.ds(r, 1)], xs_ref.at[pl.ds(dst, 1)], sems.at[slot]).start()
        return carry

    lax.fori_loop(0, TM, issue, 0, unroll=8)

    @pl.when(i == n - 1)
    def _():
        wait_slot(slot)

        @pl.when(n >= 2)
        def _():
            wait_slot(1 - slot)


def _dispatch(fill, n_used, rows, xs, mods, layer, n_lat_tiles, n_tiles):
    m = xs.shape[0]
    grid_spec = pltpu.PrefetchScalarGridSpec(
        num_scalar_prefetch=2,
        grid=(m // TM,),
        in_specs=[
            pl.BlockSpec((None, 1, 2 * TM), lambda i, fill, nu: (i, 0, 0), memory_space=pltpu.SMEM),
            pl.BlockSpec((TM, D_MODEL), lambda i, fill, nu: (i, 0)),
            pl.BlockSpec((None, None, 6, D_MODEL), lambda i, fill, nu: (layer, i // n_lat_tiles, 0, 0)),
        ],
        out_specs=pl.BlockSpec(memory_space=pl.ANY),
        scratch_shapes=[pltpu.VMEM((2, TM, D_MODEL), F32), pltpu.VMEM((MOE_TM, D_MODEL), F32),
                        pltpu.SemaphoreType.DMA((2,)), pltpu.SemaphoreType.DMA((1,))],
    )
    return pl.pallas_call(
        _dispatch_kernel,
        grid_spec=grid_spec,
        out_shape=jax.ShapeDtypeStruct(((n_tiles + 1) * MOE_TM, D_MODEL), F32),
        compiler_params=_cparams(("arbitrary",)),
        name="moe_dispatch",
    )(fill, n_used, rows, xs, mods)


def _experts_kernel(te_ref, nu_ref, x_ref, wg_ref, wu_ref, wd_ref, o_ref, xb_ref, a_ref):
    del te_ref
    f = pl.program_id(1)
    live = pl.program_id(0) < nu_ref[0]

    @pl.when(jnp.logical_not(live) & (f == 0))
    def _():
        o_ref[...] = jnp.zeros_like(o_ref)

    def hidden(xb):
        gate = jnp.dot(xb, wg_ref[...], preferred_element_type=F32)
        up = jnp.dot(xb, wu_ref[...], preferred_element_type=F32)
        return (_silu(gate) * up).astype(BF16)

    @pl.when(live & (f == 0))
    def _():
        xb = x_ref[...].astype(BF16)
        xb_ref[...] = xb
        a_ref[:, 0:MOE_TF] = hidden(xb)

    @pl.when(live & (f == 1))
    def _():
        a_ref[:, MOE_TF:2 * MOE_TF] = hidden(xb_ref[...])
        o_ref[...] = jnp.dot(a_ref[...], wd_ref[...], preferred_element_type=F32)


def _experts(tile_expert, n_used, xs_sorted, wg, wu, wd, n_tiles):
    nf = EXPERT_DIM // MOE_TF
    row_blk = lambda i, f, te, nu: (jnp.minimum(i, nu[0] - 1), 0)
    f_blk = lambda i, f, nu: jnp.where(i < nu[0], f, nf - 1)
    grid_spec = pltpu.PrefetchScalarGridSpec(
        num_scalar_prefetch=2,
        grid=(n_tiles, nf),
        in_specs=[
            pl.BlockSpec((MOE_TM, D_MODEL), row_blk),
            pl.BlockSpec((None, D_MODEL, MOE_TF), lambda i, f, te, nu: (te[i], 0, f_blk(i, f, nu))),
            pl.BlockSpec((None, D_MODEL, MOE_TF), lambda i, f, te, nu: (te[i], 0, f_blk(i, f, nu))),
            pl.BlockSpec((None, EXPERT_DIM, D_MODEL), lambda i, f, te, nu: (te[i], 0, 0)),
        ],
        out_specs=pl.BlockSpec((MOE_TM, D_MODEL), lambda i, f, te, nu: (i, 0)),
        scratch_shapes=[pltpu.VMEM((MOE_TM, D_MODEL), BF16), pltpu.VMEM((MOE_TM, EXPERT_DIM), BF16)],
    )
    assert nf == 2
    return pl.pallas_call(
        _experts_kernel,
        grid_spec=grid_spec,
        out_shape=jax.ShapeDtypeStruct((n_tiles * MOE_TM, D_MODEL), F32),
        compiler_params=_cparams(("arbitrary", "arbitrary")),
        name="moe_experts",
    )(tile_expert, n_used, xs_sorted, wg, wu, wd)


def _combine_kernel(rows_ref, rowsn_ref, wts_ref, x_ref, mod_ref, g_ref, b_ref, ys_ref, o_ref, ybuf, sems):
    i = pl.program_id(0)
    n = pl.num_programs(0)
    slot = i % 2

    def fetch(ids_ref, sl):
        def issue(r, carry):
            for k in range(2):
                src = ids_ref[0, k * TM + r]
                pltpu.make_async_copy(ys_ref.at[pl.ds(src, 1)], ybuf.at[sl, k, pl.ds(r, 1)], sems.at[sl]).start()
            return carry

        lax.fori_loop(0, TM, issue, 0, unroll=8)

    @pl.when(i == 0)
    def _():
        fetch(rows_ref, 0)

    @pl.when(i + 1 < n)
    def _():
        fetch(rowsn_ref, 1 - slot)

    for k in range(2):
        pltpu.make_async_copy(ys_ref.at[pl.ds(0, TM)], ybuf.at[slot, k], sems.at[slot]).wait()

    wts = wts_ref[...]
    y = wts[:, 0:1] * ybuf[slot, 0] + wts[:, 1:2] * ybuf[slot, 1]
    v = DEEPNORM_ALPHA * x_ref[...] + mod_ref[5:6, :] * y
    o_ref[...] = _layer_norm(v, g_ref[...], b_ref[...])


def _combine(rows, wts, xs, mods, layer, n_lat_tiles, ys_sorted, ln_g, ln_b):
    m = xs.shape[0]
    nt = m // TM
    tok = lambda i: (i, 0)
    full = lambda i: (0, 0)
    return pl.pallas_call(
        _combine_kernel,
        grid=(nt,),
        in_specs=[
            pl.BlockSpec((None, 1, 2 * TM), lambda i: (i, 0, 0), memory_space=pltpu.SMEM),
            pl.BlockSpec((None, 1, 2 * TM), lambda i: (jnp.minimum(i + 1, nt - 1), 0, 0),
                         memory_space=pltpu.SMEM),
            pl.BlockSpec((TM, HEAD_PAD), tok),
            pl.BlockSpec((TM, D_MODEL), tok),
            pl.BlockSpec((None, None, 6, D_MODEL), lambda i: (layer, i // n_lat_tiles, 0, 0)),
            pl.BlockSpec((1, D_MODEL), full),
            pl.BlockSpec((1, D_MODEL), full),
            pl.BlockSpec(memory_space=pl.ANY),
        ],
        out_specs=pl.BlockSpec((TM, D_MODEL), tok),
        out_shape=jax.ShapeDtypeStruct((m, D_MODEL), F32),
        scratch_shapes=[pltpu.VMEM((2, 2, TM, D_MODEL), F32), pltpu.SemaphoreType.DMA((2,))],
        compiler_params=_cparams(("arbitrary",)),
        name="moe_combine_ln",
    )(rows, rows, wts, xs, mods, ln_g, ln_b, ys_sorted)


def _moe_layer(xs, mods, layer, n_lat_tiles, router_w, wg, wu, wd, ln_g, ln_b):
    m = xs.shape[0]
    n_tiles = (2 * m + N_EXPERTS * (MOE_TM - 1) + MOE_TM - 1) // MOE_TM
    rw = jnp.pad(router_w, [(0, 0), (0, HEAD_PAD - N_EXPERTS)])
    rw_hi = rw.astype(BF16)
    rw = jnp.stack([rw_hi, (rw - rw_hi.astype(F32)).astype(BF16)])
    idx, wts, cnt = _router(xs, mods, layer, n_lat_tiles, rw)
    counts = cnt[0, :N_EXPERTS].astype(jnp.int32)
    padded = (counts + MOE_TM - 1) // MOE_TM * MOE_TM
    ends = jnp.cumsum(padded)
    offs = ends - padded
    n_used = (ends[-1:] // MOE_TM).astype(jnp.int32)
    tile_row = jnp.arange(n_tiles, dtype=jnp.int32) * MOE_TM
    tile_row = jnp.minimum(tile_row, ends[-1] - MOE_TM)
    tile_expert = jnp.sum((tile_row[:, None] >= ends[None, :]).astype(jnp.int32), axis=1)
    expert_ids = idx[:, 0:2, :]
    group_off = jnp.sum(jnp.where(expert_ids[..., None] == jnp.arange(N_EXPERTS), offs, 0), axis=-1)
    rows = (group_off + idx[:, 2:4, :]).reshape(m // TM, 1, 2 * TM)
    fill = (offs + counts) // 8 * 8
    xs_sorted = _dispatch(fill, n_used, rows, xs, mods, layer, n_lat_tiles, n_tiles)
    ys_sorted = _experts(tile_expert, n_used, xs_sorted, wg, wu, wd, n_tiles)
    return _combine(rows, wts, xs, mods, layer, n_lat_tiles, ys_sorted, ln_g, ln_b)


_ROPE_SWAP = np.array(list(range(8, 16)) + list(range(0, 8)) + list(range(24, 32)) + list(range(16, 24)))


def _prep_attn_weights(w_in, w_q_up, w_kv_up, w_out):
    o = 3 * NA_W
    w_qc = w_in[:, o:o + MLA_Q_LORA]
    o += MLA_Q_LORA
    w_kvc = w_in[:, o:o + MLA_KV_LORA]
    o += MLA_KV_LORA
    w_kr = w_in[:, o:o + MLA_ROPE]
    rope_pad = [(0, 0), (MLA_NOPE, HEAD_PAD - MLA_NOPE - MLA_ROPE)]
    w_in_aug = jnp.concatenate(
        [w_in[:, :3 * NA_W], w_qc, w_kvc, jnp.pad(w_kr, rope_pad), jnp.pad(w_kr[:, _ROPE_SWAP], rope_pad)],
        axis=1).astype(BF16)

    wq = w_q_up.reshape(MLA_Q_LORA, N_HEADS, MLA_NOPE + MLA_ROPE)
    wq_full = jnp.pad(wq, [(0, 0), (0, 0), (0, HEAD_PAD - MLA_NOPE - MLA_ROPE)])
    wq_swap = jnp.pad(wq[:, :, MLA_NOPE:][:, :, _ROPE_SWAP], [(0, 0), (0, 0), rope_pad[1]])
    w_q = jnp.concatenate([wq_full.reshape(MLA_Q_LORA, HP), wq_swap.reshape(MLA_Q_LORA, HP)], axis=1).astype(BF16)

    wkv = w_kv_up.reshape(MLA_KV_LORA, N_HEADS, MLA_NOPE + MLA_V)
    wkn = jnp.pad(wkv[:, :, :MLA_NOPE], [(0, 0), (0, 0), (0, HEAD_PAD - MLA_NOPE)])
    wv = jnp.pad(wkv[:, :, MLA_NOPE:], [(0, 0), (0, 0), (0, HEAD_PAD - MLA_V)])
    w_kv = jnp.concatenate([wkn.reshape(MLA_KV_LORA, HP), wv.reshape(MLA_KV_LORA, HP)], axis=1).astype(BF16)

    wa = w_out[:NA_W].astype(BF16)
    wb = w_out[NA_W:].reshape(N_HEADS, MLA_V, D_MODEL)
    wb = jnp.pad(wb, [(0, 0), (0, HEAD_PAD - MLA_V), (0, 0)]).reshape(HP, D_MODEL).astype(BF16)
    return w_in_aug, w_q, w_kv, wa, wb


def _rope_tables(s, n_ctx):
    t = jnp.arange(s, dtype=jnp.int32)
    row = (t // GRID_W).astype(F32)
    col = (t % GRID_W).astype(F32)
    n_freq = MLA_ROPE // 4
    inv = 1.0 / (ROPE_THETA ** (jnp.arange(n_freq, dtype=F32) / n_freq))
    ar = row[:, None] * inv
    ac = col[:, None] * inv
    cos = jnp.concatenate([jnp.cos(ar), jnp.cos(ar), jnp.cos(ac), jnp.cos(ac)], axis=1)
    sin = jnp.concatenate([-jnp.sin(ar), jnp.sin(ar), -jnp.sin(ac), jnp.sin(ac)], axis=1)
    right = HEAD_PAD - MLA_NOPE - MLA_ROPE
    cos = jnp.pad(cos, [(0, n_ctx), (MLA_NOPE, right)], constant_values=1.0)
    sin = jnp.pad(sin, [(0, n_ctx), (MLA_NOPE, right)])
    return cos, sin


def _na_bias_mask(rel_bias, n_rows):
    nb = n_rows // NA_QROWS
    c = np.arange(GRID_W)
    cs = np.clip(c - NA_KW // 2, 0, GRID_W - NA_KW)
    kc = np.arange(GRID_W)
    ok_c = (kc[None, :] >= cs[:, None]) & (kc[None, :] < cs[:, None] + NA_KW)
    n_dcol = 2 * NA_KW - 1
    left = GRID_W - NA_KW
    period = 2 * GRID_W
    u = jnp.pad(rel_bias, [(0, 0), (0, 0), (left, period - n_dcol - left)])
    flat = jnp.tile(u, (1, 1, GRID_W + 1))[:, :, :GRID_W * (period - 1)]
    toep = flat.reshape(N_HEADS, 2 * NA_KH - 1, GRID_W, period - 1)[:, :, :, GRID_W - 1:]
    toep = jnp.where(ok_c, toep, NEG_BIG)
    masked = jnp.full((N_HEADS, GRID_W, GRID_W), NEG_BIG, F32)
    out = []
    for b in (0, 1, nb - 1):
        kr0 = int(np.clip(b * NA_QROWS - NA_KH // 2, 0, n_rows - NA_KROWS))
        q_blocks = []
        for qr in range(NA_QROWS):
            r = b * NA_QROWS + qr
            rs = int(np.clip(r - NA_KH // 2, 0, n_rows - NA_KH))
            k_blocks = []
            for klr in range(NA_KROWS):
                kr = kr0 + klr
                k_blocks.append(toep[:, kr - r + NA_KH - 1] if rs <= kr < rs + NA_KH else masked)
            q_blocks.append(jnp.concatenate(k_blocks, axis=-1))
        out.append(jnp.concatenate(q_blocks, axis=1))
    return jnp.stack(out, axis=0)


def kernel(x, c, ctx, c_ctx, mod_w, mod_b, ln1_g, ln1_b, ln2_g, ln2_b, attn_w_in, na_rel_bias, mla_q_norm,
           mla_w_q_up, mla_kv_norm, mla_w_kv_up, attn_w_out, ffn_w_gate, ffn_w_up, ffn_w_down, pool_w,
           pool_scale, moe_router, moe_w_gate, moe_w_up, moe_w_down):
    assert x.shape[0] == 1 and c.shape[0] == 1 and ctx.shape[0] == 1
    s = x.shape[1]
    n_ctx = ctx.shape[1]
    assert n_ctx == TM and s % (NA_QROWS * GRID_W) == 0 and s % MLA_TQ == 0
    assert (s + n_ctx) % MLA_TK == 0
    n_lat_tiles = s // TM
    depth = mod_w.shape[0]

    xs = (x[0], ctx[0])
    ct = jnp.stack([c[0], c_ctx], axis=1)
    mods = _modulation(ct, mod_w, mod_b).reshape(depth, 2, 6, D_MODEL)
    cos_t, sin_t = _rope_tables(s, n_ctx)
    row = lambda v: v.reshape(1, -1)
    moe_bf16 = None

    for i in range(depth):
        j = i // 2
        ctx_live = any(l % 2 == 0 for l in range(i + 1, depth))
        if i % 2 == 0:
            w_in, w_q, w_kv, wa, wb = _prep_attn_weights(attn_w_in[j], mla_w_q_up[j], mla_w_kv_up[j],
                                                         attn_w_out[j])
            qa, ka, va, qm, kmt, vm = _attn_proj(xs, mods, i, n_lat_tiles, w_in, row(mla_q_norm[j]), w_q,
                                                 row(mla_kv_norm[j]), w_kv, cos_t, sin_t)
            bias_mask = _na_bias_mask(na_rel_bias[j], s // GRID_W)
            oa_lat = _na_attention(qa, ka, va, bias_mask, s)
            cast = (moe_w_gate, moe_w_up, moe_w_down) if i + 1 < depth else ()
            ob_lat, moe_bf16 = _mla_attention(qm, kmt, vm, s, cast, j)
            oa_ctx = ob_ctx = None
            if ctx_live:
                oa_ctx = _na_ctx_attention(qa, ka, va, s)
                ob_ctx = _mla_ctx_attention(qm, kmt, vm, s)
            xs = _out_ffn(xs, mods, i, n_lat_tiles, oa_lat, ob_lat, oa_ctx, ob_ctx, wa, wb,
                          row(ln1_g[i]), row(ln1_b[i]), ffn_w_gate[j].astype(BF16), ffn_w_up[j].astype(BF16),
                          ffn_w_down[j].astype(BF16), row(ln2_g[i]), row(ln2_b[i]))
        else:
            xs = _pool(xs, mods, i, n_lat_tiles, s, pool_w[j].astype(BF16), row(pool_scale[j]),
                       row(ln1_g[i]), row(ln1_b[i]))
            xs = _moe_layer(xs, mods, i, n_lat_tiles, moe_router[j], *moe_bf16, row(ln2_g[i]), row(ln2_b[i]))
    return xs[:s][None] if xs.shape[0] != s else xs[None]
```

```python
import functools

import numpy as np
import jax
import jax.numpy as jnp
from jax import lax
from jax.experimental import pallas as pl
from jax.experimental.pallas import tpu as pltpu

F32 = jnp.float32
BF16 = jnp.bfloat16

D_MODEL = 1024
GRID_W = 64
N_HEADS = 8
HEAD_PAD = 128
HP = N_HEADS * HEAD_PAD
NA_HEAD_DIM = 64
NA_W = N_HEADS * NA_HEAD_DIM
NA_PAIRS = N_HEADS // 2
NA_KH = 8
NA_KW = 16
MLA_NOPE = 64
MLA_ROPE = 32
MLA_V = 64
MLA_Q_LORA = 256
MLA_KV_LORA = 128
ROPE_THETA = 10000.0
POOL_WINDOWS = (2, 4, 8, 16)
POOL_GROUP = D_MODEL // len(POOL_WINDOWS)
POOL_HALO = 8
POOL_PAD = 16
FFN_DIM = 2816
N_EXPERTS = 8
EXPERT_DIM = 3584
DEPTH = 4
DEEPNORM_ALPHA = (2 * DEPTH) ** 0.25
LN_EPS = 1e-5
RMS_EPS = 1e-6
LOG2E = 1.4426950408889634
NEG_BIG = -1e30

TM = 256
NA_QROWS = 4
NA_KROWS = NA_QROWS + NA_KH - 1
NA_QB = 4
MLA_TQ = 1024
MLA_TK = 640
MLA_UNROLL = 12
MOE_TM = 512
MOE_TF = 1792
VMEM_LIMIT = 56 * 1024 * 1024


def _cparams(sem):
    return pltpu.CompilerParams(dimension_semantics=sem, vmem_limit_bytes=VMEM_LIMIT)


def _layer_norm(v, g, b):
    mu = jnp.mean(v, axis=-1, keepdims=True)
    d = v - mu
    var = jnp.mean(d * d, axis=-1, keepdims=True)
    return d * lax.rsqrt(var + LN_EPS) * g + b


def _silu(v):
    return v * jax.nn.sigmoid(v)


def _token_specs(xs, n_lat_tiles):
    if isinstance(xs, tuple):
        return list(xs), [pl.BlockSpec((TM, D_MODEL), lambda i: (jnp.minimum(i, n_lat_tiles - 1), 0)),
                          pl.BlockSpec((TM, D_MODEL), lambda i: (0, 0))]
    return [xs], [pl.BlockSpec((TM, D_MODEL), lambda i: (i, 0))]


def _load_tokens(x_refs, n_lat_tiles):
    if len(x_refs) == 2:
        return jnp.where(pl.program_id(0) >= n_lat_tiles, x_refs[1][...], x_refs[0][...])
    return x_refs[0][...]


def _token_rows(xs):
    return sum(a.shape[0] for a in xs) if isinstance(xs, tuple) else xs.shape[0]


def _mod_kernel(ct_ref, w_ref, b_ref, o_ref):
    s = _silu(ct_ref[...])
    w = w_ref[...]
    b = b_ref[...]
    r0 = jnp.sum(w * s[:, 0:1], axis=0, keepdims=True) + b
    r1 = jnp.sum(w * s[:, 1:2], axis=0, keepdims=True) + b
    o_ref[...] = jnp.concatenate([r0, r1], axis=0)


def _modulation(ct, mod_w, mod_b):
    depth, d, n6 = mod_w.shape
    tn = 1536
    return pl.pallas_call(
        _mod_kernel,
        grid=(depth, n6 // tn),
        in_specs=[
            pl.BlockSpec((d, 2), lambda l, j: (0, 0)),
            pl.BlockSpec((None, d, tn), lambda l, j: (l, 0, j)),
            pl.BlockSpec((None, 1, tn), lambda l, j: (l, 0, j)),
        ],
        out_specs=pl.BlockSpec((None, 2, tn), lambda l, j: (l, 0, j)),
        out_shape=jax.ShapeDtypeStruct((depth, 2, n6), F32),
        compiler_params=_cparams(("arbitrary", "arbitrary")),
        name="modulation",
    )(ct, mod_w, mod_b.reshape(depth, 1, n6))


def _attn_proj_kernel(*refs, n_x, n_lat_tiles):
    x_refs = refs[:n_x]
    (mod_ref, win_ref, qn_ref, wq_ref, kvn_ref, wkv_ref, cos_ref, sin_ref,
     qa_ref, ka_ref, va_ref, qm_ref, kmt_ref, vm_ref) = refs[n_x:]
    h = _load_tokens(x_refs, n_lat_tiles) * (1.0 + mod_ref[1:2, :]) + mod_ref[0:1, :]
    p = jnp.dot(h.astype(BF16), win_ref[...], preferred_element_type=F32)
    qa_ref[...] = (p[:, 0:NA_W] * (NA_HEAD_DIM ** -0.5)).astype(BF16)
    ka_ref[...] = p[:, NA_W:2 * NA_W].astype(BF16)
    va_ref[...] = p[:, 2 * NA_W:3 * NA_W].astype(BF16)
    o = 3 * NA_W
    q_c = p[:, o:o + MLA_Q_LORA]
    o += MLA_Q_LORA
    kv_c = p[:, o:o + MLA_KV_LORA]
    o += MLA_KV_LORA
    krp = p[:, o:o + HEAD_PAD]
    krs = p[:, o + HEAD_PAD:o + 2 * HEAD_PAD]
    qn = q_c * lax.rsqrt(jnp.mean(q_c * q_c, axis=-1, keepdims=True) + RMS_EPS) * qn_ref[...]
    kvn = kv_c * lax.rsqrt(jnp.mean(kv_c * kv_c, axis=-1, keepdims=True) + RMS_EPS) * kvn_ref[...]
    q2 = jnp.dot(qn.astype(BF16), wq_ref[...], preferred_element_type=F32)
    kv2 = jnp.dot(kvn.astype(BF16), wkv_ref[...], preferred_element_type=F32)
    cos = cos_ref[...]
    sin = sin_ref[...]
    kr = krp * cos + krs * sin
    q_scale = (MLA_NOPE + MLA_ROPE) ** -0.5 * LOG2E
    for hd in range(N_HEADS):
        lo = hd * HEAD_PAD
        qh = (q2[:, lo:lo + HEAD_PAD] * cos + q2[:, HP + lo:HP + lo + HEAD_PAD] * sin) * q_scale
        qm_ref[:, lo:lo + HEAD_PAD] = qh.astype(BF16)
        kh = kv2[:, lo:lo + HEAD_PAD] + kr
        kmt_ref[hd] = kh.T.astype(BF16)
    lane = lax.broadcasted_iota(jnp.int32, (1, HP), 1)
    ones_col = jnp.where(lane % HEAD_PAD == MLA_V, 1.0, 0.0)
    vm_ref[...] = (kv2[:, HP:2 * HP] + ones_col).astype(BF16)


def _attn_proj(xs, mods, layer, n_lat_tiles, w_in, q_norm, w_q, kv_norm, w_kv, cos_t, sin_t):
    m = _token_rows(xs)
    nt = m // TM
    tok = lambda i: (i, 0)
    full = lambda i: (0, 0)
    act = jax.ShapeDtypeStruct((m, HP), BF16)
    na_act = jax.ShapeDtypeStruct((m, NA_W), BF16)
    x_args, x_specs = _token_specs(xs, n_lat_tiles)
    return pl.pallas_call(
        functools.partial(_attn_proj_kernel, n_x=len(x_args), n_lat_tiles=n_lat_tiles),
        grid=(nt,),
        in_specs=x_specs + [
            pl.BlockSpec((None, None, 6, D_MODEL), lambda i: (layer, i // n_lat_tiles, 0, 0)),
            pl.BlockSpec(w_in.shape, full),
            pl.BlockSpec(q_norm.shape, full),
            pl.BlockSpec(w_q.shape, full),
            pl.BlockSpec(kv_norm.shape, full),
            pl.BlockSpec(w_kv.shape, full),
            pl.BlockSpec((TM, HEAD_PAD), tok),
            pl.BlockSpec((TM, HEAD_PAD), tok),
        ],
        out_specs=[
            pl.BlockSpec((TM, NA_W), tok),
            pl.BlockSpec((TM, NA_W), tok),
            pl.BlockSpec((TM, NA_W), tok),
            pl.BlockSpec((TM, HP), tok),
            pl.BlockSpec((N_HEADS, HEAD_PAD, TM), lambda i: (0, 0, i)),
            pl.BlockSpec((TM, HP), tok),
        ],
        out_shape=[na_act, na_act, na_act, act, jax.ShapeDtypeStruct((N_HEADS, HEAD_PAD, m), BF16), act],
        compiler_params=_cparams(("arbitrary",)),
        name="attn_proj",
    )(*x_args, mods, w_in, q_norm, w_q, kv_norm, w_kv, cos_t, sin_t)


_NT_DIMS = (((1,), (1,)), ((), ()))


def _pair_masks():
    lane = lax.broadcasted_iota(jnp.int32, (1, HEAD_PAD), 1)
    return lane < NA_HEAD_DIM, lane >= NA_HEAD_DIM


def _na_kernel(q_ref, k_ref, v_ref, kc_ref, vc_ref, *rest, n_rows):
    bm_refs, o_ref = rest[:NA_QB], rest[NA_QB]
    tq = NA_QROWS * GRID_W
    nk = NA_KROWS * GRID_W
    kc = kc_ref[...]
    vc = vc_ref[...]
    head_lanes = _pair_masks()
    for u in range(NA_QB):
        b = pl.program_id(1) * NA_QB + u
        kr0 = jnp.clip(b * NA_QROWS - NA_KH // 2, 0, n_rows - NA_KROWS)
        start = pl.multiple_of(kr0 * GRID_W, GRID_W)
        q = q_ref[u * tq:(u + 1) * tq, :]
        kw = k_ref[pl.ds(start, nk), :]
        vw = v_ref[pl.ds(start, nk), :]
        outs = []
        for a in range(2):
            qh = jnp.where(head_lanes[a], q, jnp.zeros_like(q))
            s_loc = lax.dot_general(qh, kw, _NT_DIMS, preferred_element_type=F32) + bm_refs[u][a]
            s_ctx = lax.dot_general(qh, kc, _NT_DIMS, preferred_element_type=F32)
            m = jnp.maximum(jnp.max(s_loc, axis=-1, keepdims=True), jnp.max(s_ctx, axis=-1, keepdims=True))
            p_loc = jnp.exp(s_loc - m)
            p_ctx = jnp.exp(s_ctx - m)
            l = jnp.sum(p_loc, axis=-1, keepdims=True) + jnp.sum(p_ctx, axis=-1, keepdims=True)
            o = (jnp.dot(p_loc.astype(BF16), vw, preferred_element_type=F32)
                 + jnp.dot(p_ctx.astype(BF16), vc, preferred_element_type=F32))
            outs.append(o / l)
        o_ref[u * tq:(u + 1) * tq, :] = jnp.where(head_lanes[0], outs[0], outs[1]).astype(BF16)


def _na_attention(qa, ka, va, bias_mask, s):
    n_rows = s // GRID_W
    nb = n_rows // NA_QROWS
    tq = NA_QROWS * GRID_W
    assert nb % NA_QB == 0
    ctx_blk = s // tq
    pat = lambda b: jnp.where(b == 0, 0, jnp.where(b == nb - 1, 2, 1))
    bm_spec = lambda u: pl.BlockSpec((None, 2, tq, NA_KROWS * GRID_W),
                                     lambda hp, j: (pat(j * NA_QB + u), hp, 0, 0))
    return pl.pallas_call(
        functools.partial(_na_kernel, n_rows=n_rows),
        grid=(NA_PAIRS, nb // NA_QB),
        in_specs=[
            pl.BlockSpec((NA_QB * tq, HEAD_PAD), lambda hp, j: (j, hp)),
            pl.BlockSpec((s, HEAD_PAD), lambda hp, j: (0, hp)),
            pl.BlockSpec((s, HEAD_PAD), lambda hp, j: (0, hp)),
            pl.BlockSpec((tq, HEAD_PAD), lambda hp, j: (ctx_blk, hp)),
            pl.BlockSpec((tq, HEAD_PAD), lambda hp, j: (ctx_blk, hp)),
        ] + [bm_spec(u) for u in range(NA_QB)],
        out_specs=pl.BlockSpec((NA_QB * tq, HEAD_PAD), lambda hp, j: (j, hp)),
        out_shape=jax.ShapeDtypeStruct((s, NA_W), BF16),
        compiler_params=_cparams(("arbitrary", "arbitrary")),
        name="na_attention",
    )(qa, ka, va, ka, va, *([bias_mask] * NA_QB))


def _na_ctx_kernel(q_ref, k_ref, v_ref, o_ref):
    q = q_ref[...]
    k = k_ref[...]
    v = v_ref[...]
    head_lanes = _pair_masks()
    outs = []
    for a in range(2):
        qh = jnp.where(head_lanes[a], q, jnp.zeros_like(q))
        s = lax.dot_general(qh, k, _NT_DIMS, preferred_element_type=F32)
        m = jnp.max(s, axis=-1, keepdims=True)
        p = jnp.exp(s - m)
        l = jnp.sum(p, axis=-1, keepdims=True)
        outs.append(jnp.dot(p.astype(BF16), v, preferred_element_type=F32) / l)
    o_ref[...] = jnp.where(head_lanes[0], outs[0], outs[1]).astype(BF16)


def _na_ctx_attention(qa, ka, va, s):
    blk = s // TM
    spec = pl.BlockSpec((TM, HEAD_PAD), lambda hp: (blk, hp))
    return pl.pallas_call(
        _na_ctx_kernel,
        grid=(NA_PAIRS,),
        in_specs=[spec, spec, spec],
        out_specs=pl.BlockSpec((TM, HEAD_PAD), lambda hp: (0, hp)),
        out_shape=jax.ShapeDtypeStruct((TM, NA_W), BF16),
        compiler_params=_cparams(("arbitrary",)),
        name="na_ctx_attention",
    )(qa, ka, va)


def _mla_ctx_kernel(q_ref, kt_ref, v_ref, o_ref):
    s = jnp.dot(q_ref[...], kt_ref[...], preferred_element_type=F32)
    m = jnp.max(s, axis=-1, keepdims=True)
    p = jnp.exp2(s - m)
    l = jnp.sum(p, axis=-1, keepdims=True)
    o = jnp.dot(p.astype(BF16), v_ref[...], preferred_element_type=F32)
    o_ref[...] = (o / l).astype(BF16)


def _mla_ctx_attention(qm, kmt, vm, s):
    blk = s // TM
    return pl.pallas_call(
        _mla_ctx_kernel,
        grid=(N_HEADS,),
        in_specs=[
            pl.BlockSpec((TM, HEAD_PAD), lambda h: (blk, h)),
            pl.BlockSpec((None, HEAD_PAD, TM), lambda h: (h, 0, blk)),
            pl.BlockSpec((TM, HEAD_PAD), lambda h: (blk, h)),
        ],
        out_specs=pl.BlockSpec((TM, HEAD_PAD), lambda h: (0, h)),
        out_shape=jax.ShapeDtypeStruct((TM, HP), BF16),
        compiler_params=_cparams(("arbitrary",)),
        name="mla_ctx_attention",
    )(qm, kmt, vm)


def _mla_kernel(q_ref, qn_ref, kt_ref, v_ref, *rest, n_chunks, n_cast):
    cast_in, o_ref, cast_out = rest[:n_cast], rest[n_cast], rest[n_cast + 1:2 * n_cast + 1]
    s0_ref, s1_ref = rest[2 * n_cast + 1:]
    for w_ref, wb_ref in zip(cast_in, cast_out):
        wb_ref[...] = w_ref[...].astype(BF16)
    q = q_ref[...]
    tq = q.shape[0]

    def scores(c, s_ref, q_tile=q):
        off = pl.multiple_of(c * MLA_TK, MLA_TK)
        s_ref[...] = jnp.dot(q_tile, kt_ref[:, pl.ds(off, MLA_TK)], preferred_element_type=F32)

    def update(c, s_ref, carry):
        m, acc = carry
        off = pl.multiple_of(c * MLA_TK, MLA_TK)
        s = s_ref[...]
        m_new = jnp.maximum(m, jnp.max(s, axis=-1, keepdims=True))
        alpha = jnp.exp2(m - m_new)
        p = jnp.exp2(s - m_new).astype(BF16)
        acc = alpha * acc + jnp.dot(p, v_ref[pl.ds(off, MLA_TK), :], preferred_element_type=F32)
        return m_new, acc

    bufs = (s0_ref, s1_ref)

    def group(g, carry):
        c0 = g * MLA_UNROLL
        for u in range(MLA_UNROLL):
            scores(c0 + u + 1, bufs[(u + 1) % 2])
            carry = update(c0 + u, bufs[u % 2], carry)
        return carry

    carry = (jnp.full((tq, 1), NEG_BIG, F32), jnp.zeros((tq, HEAD_PAD), F32))

    @pl.when(pl.program_id(1) == 0)
    def _():
        scores(0, s0_ref)

    n_groups = (n_chunks - 2) // MLA_UNROLL
    carry = lax.fori_loop(0, n_groups, group, carry)
    for c in range(n_groups * MLA_UNROLL, n_chunks):
        if c + 1 < n_chunks:
            scores(c + 1, bufs[(c + 1) % 2])
        else:
            scores(0, s0_ref, qn_ref[...])
        carry = update(c, bufs[c % 2], carry)
    _, acc = carry
    o_ref[...] = (acc / acc[:, MLA_V:MLA_V + 1]).astype(BF16)


def _mla_attention(qm, kmt, vm, s, cast_weights=(), cast_layer=0):
    m = qm.shape[0]
    n_chunks = m // MLA_TK
    assert n_chunks % 2 == 0 and n_chunks >= 2
    n_q = s // MLA_TQ
    assert N_HEADS == N_EXPERTS or not cast_weights
    cast_in_specs, cast_out_specs, cast_shapes = [], [], []
    for w in cast_weights:
        rows, cols = w.shape[2], w.shape[3]
        slab = rows // n_q
        assert rows % n_q == 0 and slab % 16 == 0
        cast_in_specs.append(pl.BlockSpec((None, None, slab, cols), lambda h, i: (cast_layer, h, i, 0)))
        cast_out_specs.append(pl.BlockSpec((None, slab, cols), lambda h, i: (h, i, 0)))
        cast_shapes.append(jax.ShapeDtypeStruct((N_EXPERTS, rows, cols), BF16))
    outs = pl.pallas_call(
        functools.partial(_mla_kernel, n_chunks=n_chunks, n_cast=len(cast_weights)),
        grid=(N_HEADS, n_q),
        in_specs=[
            pl.BlockSpec((MLA_TQ, HEAD_PAD), lambda h, i: (i, h)),
            pl.BlockSpec((MLA_TQ, HEAD_PAD), lambda h, i: (jnp.minimum(i + 1, n_q - 1), h)),
            pl.BlockSpec((None, HEAD_PAD, m), lambda h, i: (h, 0, 0)),
            pl.BlockSpec((m, HEAD_PAD), lambda h, i: (0, h)),
        ] + cast_in_specs,
        out_specs=[pl.BlockSpec((MLA_TQ, HEAD_PAD), lambda h, i: (i, h))] + cast_out_specs,
        out_shape=[jax.ShapeDtypeStruct((s, HP), BF16)] + cast_shapes,
        scratch_shapes=[pltpu.VMEM((MLA_TQ, MLA_TK), F32), pltpu.VMEM((MLA_TQ, MLA_TK), F32)],
        compiler_params=_cparams(("arbitrary", "arbitrary")),
        name="mla_attention",
    )(qm, qm, kmt, vm, *cast_weights)
    return outs[0], tuple(outs[1:])


def _out_ffn_kernel(*refs, n_x, n_lat_tiles, with_ctx):
    x_refs = refs[:n_x]
    mod_ref, oal_ref, obl_ref = refs[n_x:n_x + 3]
    rest = refs[n_x + 3:]
    if with_ctx:
        oac_ref, obc_ref = rest[:2]
        rest = rest[2:]
        is_ctx = pl.program_id(0) >= n_lat_tiles
        oa = jnp.where(is_ctx, oac_ref[...], oal_ref[...])
        ob = jnp.where(is_ctx, obc_ref[...], obl_ref[...])
    else:
        oa = oal_ref[...]
        ob = obl_ref[...]
    wa_ref, wb_ref, g1_ref, b1_ref, wg_ref, wu_ref, wd_ref, g2_ref, b2_ref, o_ref = rest
    y = (jnp.dot(oa, wa_ref[...], preferred_element_type=F32)
         + jnp.dot(ob, wb_ref[...], preferred_element_type=F32))
    v = DEEPNORM_ALPHA * _load_tokens(x_refs, n_lat_tiles) + mod_ref[2:3, :] * y
    x1 = _layer_norm(v, g1_ref[...], b1_ref[...])
    h = (x1 * (1.0 + mod_ref[4:5, :]) + mod_ref[3:4, :]).astype(BF16)
    gate = jnp.dot(h, wg_ref[...], preferred_element_type=F32)
    up = jnp.dot(h, wu_ref[...], preferred_element_type=F32)
    a = (_silu(gate) * up).astype(BF16)
    y2 = jnp.dot(a, wd_ref[...], preferred_element_type=F32)
    v2 = DEEPNORM_ALPHA * x1 + mod_ref[5:6, :] * y2
    o_ref[...] = _layer_norm(v2, g2_ref[...], b2_ref[...])


def _out_ffn(xs, mods, layer, n_lat_tiles, oa_lat, ob_lat, oa_ctx, ob_ctx, wa, wb, ln1_g, ln1_b,
             wg, wu, wd, ln2_g, ln2_b):
    with_ctx = oa_ctx is not None
    m = _token_rows(xs) if with_ctx else n_lat_tiles * TM
    tok = lambda i: (i, 0)
    lat = lambda i: (jnp.minimum(i, n_lat_tiles - 1), 0)
    full = lambda i: (0, 0)
    ctx_args = [oa_ctx, ob_ctx] if with_ctx else []
    x_args, x_specs = _token_specs(xs, n_lat_tiles)
    resident = lambda w: pl.BlockSpec(w.shape, full, pipeline_mode=pl.Buffered(1))
    vec = pl.BlockSpec((1, D_MODEL), full)
    return pl.pallas_call(
        functools.partial(_out_ffn_kernel, n_x=len(x_args), n_lat_tiles=n_lat_tiles, with_ctx=with_ctx),
        grid=(m // TM,),
        in_specs=x_specs + [
            pl.BlockSpec((None, None, 6, D_MODEL), lambda i: (layer, i // n_lat_tiles, 0, 0)),
            pl.BlockSpec((TM, NA_W), lat),
            pl.BlockSpec((TM, HP), lat),
        ] + ([pl.BlockSpec((TM, NA_W), full), pl.BlockSpec((TM, HP), full)] if with_ctx else []) + [
            resident(wa), resident(wb), vec, vec, resident(wg), resident(wu), resident(wd), vec, vec,
        ],
        out_specs=pl.BlockSpec((TM, D_MODEL), tok),
        out_shape=jax.ShapeDtypeStruct((m, D_MODEL), F32),
        compiler_params=_cparams(("arbitrary",)),
        name="out_proj_ffn_ln",
    )(*x_args, mods, oa_lat, ob_lat, *ctx_args, wa, wb, ln1_g, ln1_b, wg, wu, wd, ln2_g, ln2_b)


def _pool_router_kernel(x_ref, xp_ref, xn_ref, mod_ref, w_ref, sc_ref, g_ref, b_ref, rw_ref, tri_ref,
                        o_ref, idx_ref, wts_ref, cnt_ref, ext_ref, lvl_ref, base_ref, *, n_lat_tiles, s):
    i = pl.program_id(0)

    @pl.when(i == 0)
    def _():
        base_ref[...] = jnp.zeros_like(base_ref)

    is_ctx = i >= n_lat_tiles
    j = jnp.where(is_ctx, i - n_lat_tiles, i)
    last = jnp.where(is_ctx, 0, n_lat_tiles - 1)
    n_seq = jnp.where(is_ctx, TM, s)
    scale1 = 1.0 + mod_ref[1:2, :]
    shift1 = mod_ref[0:1, :]
    x = x_ref[...]
    h = x * scale1 + shift1
    hp = jnp.where(j != 0, xp_ref[...] * scale1 + shift1, 0.0)
    hn = jnp.where(j != last, xn_ref[...] * scale1 + shift1, 0.0)
    n_ext = TM + 2 * POOL_HALO
    ext_ref[0:POOL_PAD, :] = jnp.zeros((POOL_PAD, D_MODEL), F32)
    ext_ref[POOL_PAD:POOL_PAD + POOL_HALO, :] = hp
    ext_ref[POOL_PAD + POOL_HALO:POOL_PAD + POOL_HALO + TM, :] = h
    ext_ref[POOL_PAD + POOL_HALO + TM:POOL_PAD + n_ext, :] = hn
    t = j * TM + lax.broadcasted_iota(jnp.int32, (TM, 1), 0)
    ys = []
    for g, win in enumerate(POOL_WINDOWS):
        half = win // 2
        lo = g * POOL_GROUP
        run = lambda r0, r1: ext_ref[r0:r1, lo:lo + POOL_GROUP]
        n, k = 1, 0
        while n < win:
            nxt = lvl_ref.at[g, k % 2]
            nxt[0:POOL_PAD, :] = jnp.zeros((POOL_PAD, POOL_GROUP), F32)
            nxt[POOL_PAD:POOL_PAD + n_ext, :] = (run(POOL_PAD, POOL_PAD + n_ext)
                                                 + run(POOL_PAD - n, POOL_PAD - n + n_ext))
            run = lambda r0, r1, ref=nxt: ref[r0:r1, :]
            n, k = 2 * n, k + 1
        first = POOL_PAD + POOL_HALO + half - 1
        acc = run(first, first + TM)
        cnt = (jnp.minimum(t + half, n_seq) - jnp.maximum(t - half, 0)).astype(F32)
        mixed = acc / cnt - h[:, lo:lo + POOL_GROUP]
        yg = jnp.dot(mixed.astype(BF16), w_ref[g], preferred_element_type=F32)
        ys.append(yg)
    y = jnp.concatenate(ys, axis=-1) * sc_ref[...]
    v = DEEPNORM_ALPHA * x + mod_ref[2:3, :] * y
    x1 = _layer_norm(v, g_ref[...], b_ref[...])
    o_ref[...] = x1
    _route(x1, mod_ref, rw_ref, tri_ref, idx_ref, wts_ref, cnt_ref, base_ref)


def _pool_router(xs, mods, layer, n_lat_tiles, s, pool_w, pool_scale, ln_g, ln_b, router_w):
    m = xs.shape[0]
    nt = m // TM
    rw = jnp.pad(router_w, [(0, 0), (0, HEAD_PAD - N_EXPERTS)])
    rw_hi = rw.astype(BF16)
    rw = jnp.stack([rw_hi, (rw - rw_hi.astype(F32)).astype(BF16)])
    tri = jnp.asarray(np.tril(np.ones((TM, TM), np.float32), -1), dtype=BF16)
    per = TM // POOL_HALO
    n_halo_blocks = m // POOL_HALO
    tok = lambda i: (i, 0)
    full = lambda i: (0, 0)
    return pl.pallas_call(
        functools.partial(_pool_router_kernel, n_lat_tiles=n_lat_tiles, s=s),
        grid=(nt,),
        in_specs=[
            pl.BlockSpec((TM, D_MODEL), tok),
            pl.BlockSpec((POOL_HALO, D_MODEL), lambda i: (jnp.maximum(i * per - 1, 0), 0)),
            pl.BlockSpec((POOL_HALO, D_MODEL), lambda i: (jnp.minimum((i + 1) * per, n_halo_blocks - 1), 0)),
            pl.BlockSpec((None, None, 6, D_MODEL), lambda i: (layer, i // n_lat_tiles, 0, 0)),
            pl.BlockSpec(pool_w.shape, lambda i: (0, 0, 0)),
            pl.BlockSpec((1, D_MODEL), full),
            pl.BlockSpec((1, D_MODEL), full),
            pl.BlockSpec((1, D_MODEL), full),
            pl.BlockSpec(rw.shape, lambda i: (0, 0, 0)),
            pl.BlockSpec((TM, TM), full),
        ],
        out_specs=[
            pl.BlockSpec((TM, D_MODEL), tok),
            pl.BlockSpec((None, 8, TM), lambda i: (i, 0, 0)),
            pl.BlockSpec((TM, HEAD_PAD), tok),
            pl.BlockSpec((8, HEAD_PAD), full),
        ],
        out_shape=[
            jax.ShapeDtypeStruct((m, D_MODEL), F32),
            jax.ShapeDtypeStruct((nt, 8, TM), jnp.int32),
            jax.ShapeDtypeStruct((m, HEAD_PAD), F32),
            jax.ShapeDtypeStruct((8, HEAD_PAD), F32),
        ],
        scratch_shapes=[pltpu.VMEM((POOL_PAD + TM + 2 * POOL_HALO, D_MODEL), F32),
                        pltpu.VMEM((len(POOL_WINDOWS), 2, POOL_PAD + TM + 2 * POOL_HALO, POOL_GROUP), F32),
                        pltpu.VMEM((8, HEAD_PAD), F32)],
        compiler_params=_cparams(("arbitrary",)),
        name="pool_ln_router",
    )(xs, xs, xs, mods, pool_w, pool_scale, ln_g, ln_b, rw, tri)


def _route(x1, mod_ref, rw_ref, tri_ref, idx_ref, wts_ref, cnt_ref, base_ref):
    h = x1 * (1.0 + mod_ref[4:5, :]) + mod_ref[3:4, :]
    h_hi = h.astype(BF16)
    h_lo = (h - h_hi.astype(F32)).astype(BF16)
    logits = (jnp.dot(h_hi, rw_ref[0], preferred_element_type=F32)
              + jnp.dot(h_hi, rw_ref[1], preferred_element_type=F32)
              + jnp.dot(h_lo, rw_ref[0], preferred_element_type=F32))
    lane = lax.broadcasted_iota(jnp.int32, logits.shape, 1)
    logits = jnp.where(lane < N_EXPERTS, logits, -jnp.inf)
    m1 = jnp.max(logits, axis=-1, keepdims=True)
    i1 = jnp.min(jnp.where(logits == m1, lane, HEAD_PAD), axis=-1, keepdims=True)
    rest = jnp.where(lane == i1, -jnp.inf, logits)
    m2 = jnp.max(rest, axis=-1, keepdims=True)
    i2 = jnp.min(jnp.where(rest == m2, lane, HEAD_PAD), axis=-1, keepdims=True)
    e2 = jnp.exp(m2 - m1)
    w1 = 1.0 / (1.0 + e2)
    w2 = e2 / (1.0 + e2)
    wts_ref[...] = jnp.where(lane == 0, w1, jnp.where(lane == 1, w2, 0.0))

    chosen = jnp.where(lane == i1, 1.0, jnp.where(lane == i2, 1.0, 0.0))
    before = jnp.dot(tri_ref[...], chosen.astype(BF16), preferred_element_type=F32)
    rank = before + base_ref[0:1, :]
    r1 = jnp.sum(jnp.where(lane == i1, rank, 0.0), axis=-1, keepdims=True)
    r2 = jnp.sum(jnp.where(lane == i2, rank, 0.0), axis=-1, keepdims=True)
    base_ref[0:1, :] = base_ref[0:1, :] + jnp.sum(chosen, axis=0, keepdims=True)
    cnt_ref[...] = base_ref[...]
    packed = jnp.where(lane == 0, i1.astype(F32),
                       jnp.where(lane == 1, i2.astype(F32),
                                 jnp.where(lane == 2, r1, jnp.where(lane == 3, r2, 0.0))))
    idx_ref[...] = packed.T[0:8, :].astype(jnp.int32)


def _dispatch_kernel(fill_ref, nu_ref, rows_ref, x_ref, mod_ref, xs_ref, hbuf, zbuf, sems, zsem):
    i = pl.program_id(0)
    n = pl.num_programs(0)
    slot = i % 2
    n_tiles = xs_ref.shape[0] // MOE_TM

    @pl.when(i == 0)
    def _():
        zbuf[...] = jnp.zeros_like(zbuf)
        fills = [pltpu.make_async_copy(zbuf, xs_ref.at[pl.ds(pl.multiple_of(fill_ref[e], 8), MOE_TM)],
                                       zsem.at[0]) for e in range(N_EXPERTS)]
        for cp in fills:
            cp.start()
        for cp in fills:
            cp.wait()

        def zero_tail(t, carry):
            row0 = pl.multiple_of(t * MOE_TM, MOE_TM)
            cp = pltpu.make_async_copy(zbuf, xs_ref.at[pl.ds(row0, MOE_TM)], zsem.at[0])
            cp.start()
            cp.wait()
            return carry

        lax.fori_loop(nu_ref[0], n_tiles, zero_tail, 0)

    def wait_slot(sl):
        for _ in range(2):
            pltpu.make_async_copy(hbuf.at[sl], xs_ref.at[pl.ds(0, TM)], sems.at[sl]).wait()

    @pl.when(i >= 2)
    def _():
        wait_slot(slot)

    hbuf[slot] = x_ref[...] * (1.0 + mod_ref[4:5, :]) + mod_ref[3:4, :]

    def issue(r, carry):
        for k in range(2):
            dst = rows_ref[0, k * TM + r]
            pltpu.make_async_copy(hbuf.at[slot, pl.ds(r, 1)], xs_ref.at[pl.ds(dst, 1)], sems.at[slot]).start()
        return carry

    lax.fori_loop(0, TM, issue, 0, unroll=8)

    @pl.when(i == n - 1)
    def _():
        wait_slot(slot)

        @pl.when(n >= 2)
        def _():
            wait_slot(1 - slot)


def _dispatch(fill, n_used, rows, xs, mods, layer, n_lat_tiles, n_tiles):
    m = xs.shape[0]
    grid_spec = pltpu.PrefetchScalarGridSpec(
        num_scalar_prefetch=2,
        grid=(m // TM,),
        in_specs=[
            pl.BlockSpec((None, 1, 2 * TM), lambda i, fill, nu: (i, 0, 0), memory_space=pltpu.SMEM),
            pl.BlockSpec((TM, D_MODEL), lambda i, fill, nu: (i, 0)),
            pl.BlockSpec((None, None, 6, D_MODEL), lambda i, fill, nu: (layer, i // n_lat_tiles, 0, 0)),
        ],
        out_specs=pl.BlockSpec(memory_space=pl.ANY),
        scratch_shapes=[pltpu.VMEM((2, TM, D_MODEL), F32), pltpu.VMEM((MOE_TM, D_MODEL), F32),
                        pltpu.SemaphoreType.DMA((2,)), pltpu.SemaphoreType.DMA((1,))],
    )
    return pl.pallas_call(
        _dispatch_kernel,
        grid_spec=grid_spec,
        out_shape=jax.ShapeDtypeStruct(((n_tiles + 1) * MOE_TM, D_MODEL), F32),
        compiler_params=_cparams(("arbitrary",)),
        name="moe_dispatch",
    )(fill, n_used, rows, xs, mods)


def _experts_kernel(te_ref, nu_ref, x_ref, wg_ref, wu_ref, wd_ref, o_ref, xb_ref, a_ref):
    del te_ref
    f = pl.program_id(1)
    live = pl.program_id(0) < nu_ref[0]

    @pl.when(jnp.logical_not(live) & (f == 0))
    def _():
        o_ref[...] = jnp.zeros_like(o_ref)

    def hidden(xb):
        gate = jnp.dot(xb, wg_ref[...], preferred_element_type=F32)
        up = jnp.dot(xb, wu_ref[...], preferred_element_type=F32)
        return (_silu(gate) * up).astype(BF16)

    @pl.when(live & (f == 0))
    def _():
        xb = x_ref[...].astype(BF16)
        xb_ref[...] = xb
        a_ref[:, 0:MOE_TF] = hidden(xb)

    @pl.when(live & (f == 1))
    def _():
        a_ref[:, MOE_TF:2 * MOE_TF] = hidden(xb_ref[...])
        o_ref[...] = jnp.dot(a_ref[...], wd_ref[...], preferred_element_type=F32)


def _experts(tile_expert, n_used, xs_sorted, wg, wu, wd, n_tiles):
    nf = EXPERT_DIM // MOE_TF
    row_blk = lambda i, f, te, nu: (jnp.minimum(i, nu[0] - 1), 0)
    f_blk = lambda i, f, nu: jnp.where(i < nu[0], f, nf - 1)
    grid_spec = pltpu.PrefetchScalarGridSpec(
        num_scalar_prefetch=2,
        grid=(n_tiles, nf),
        in_specs=[
            pl.BlockSpec((MOE_TM, D_MODEL), row_blk),
            pl.BlockSpec((None, D_MODEL, MOE_TF), lambda i, f, te, nu: (te[i], 0, f_blk(i, f, nu))),
            pl.BlockSpec((None, D_MODEL, MOE_TF), lambda i, f, te, nu: (te[i], 0, f_blk(i, f, nu))),
            pl.BlockSpec((None, EXPERT_DIM, D_MODEL), lambda i, f, te, nu: (te[i], 0, 0)),
        ],
        out_specs=pl.BlockSpec((MOE_TM, D_MODEL), lambda i, f, te, nu: (i, 0)),
        scratch_shapes=[pltpu.VMEM((MOE_TM, D_MODEL), BF16), pltpu.VMEM((MOE_TM, EXPERT_DIM), BF16)],
    )
    assert nf == 2
    return pl.pallas_call(
        _experts_kernel,
        grid_spec=grid_spec,
        out_shape=jax.ShapeDtypeStruct((n_tiles * MOE_TM, D_MODEL), F32),
        compiler_params=_cparams(("arbitrary", "arbitrary")),
        name="moe_experts",
    )(tile_expert, n_used, xs_sorted, wg, wu, wd)


def _combine_kernel(rows_ref, rowsn_ref, wts_ref, x_ref, mod_ref, g_ref, b_ref, ys_ref, o_ref, ybuf, sems):
    i = pl.program_id(0)
    n = pl.num_programs(0)
    slot = i % 2

    def fetch(ids_ref, sl):
        def issue(r, carry):
            for k in range(2):
                src = ids_ref[0, k * TM + r]
                pltpu.make_async_copy(ys_ref.at[pl.ds(src, 1)], ybuf.at[sl, k, pl.ds(r, 1)], sems.at[sl]).start()
            return carry

        lax.fori_loop(0, TM, issue, 0, unroll=8)

    @pl.when(i == 0)
    def _():
        fetch(rows_ref, 0)

    @pl.when(i + 1 < n)
    def _():
        fetch(rowsn_ref, 1 - slot)

    for k in range(2):
        pltpu.make_async_copy(ys_ref.at[pl.ds(0, TM)], ybuf.at[slot, k], sems.at[slot]).wait()

    wts = wts_ref[...]
    y = wts[:, 0:1] * ybuf[slot, 0] + wts[:, 1:2] * ybuf[slot, 1]
    v = DEEPNORM_ALPHA * x_ref[...] + mod_ref[5:6, :] * y
    o_ref[...] = _layer_norm(v, g_ref[...], b_ref[...])


def _combine(rows, wts, xs, mods, layer, n_lat_tiles, ys_sorted, ln_g, ln_b):
    m = xs.shape[0]
    nt = m // TM
    tok = lambda i: (i, 0)
    full = lambda i: (0, 0)
    return pl.pallas_call(
        _combine_kernel,
        grid=(nt,),
        in_specs=[
            pl.BlockSpec((None, 1, 2 * TM), lambda i: (i, 0, 0), memory_space=pltpu.SMEM),
            pl.BlockSpec((None, 1, 2 * TM), lambda i: (jnp.minimum(i + 1, nt - 1), 0, 0),
                         memory_space=pltpu.SMEM),
            pl.BlockSpec((TM, HEAD_PAD), tok),
            pl.BlockSpec((TM, D_MODEL), tok),
            pl.BlockSpec((None, None, 6, D_MODEL), lambda i: (layer, i // n_lat_tiles, 0, 0)),
            pl.BlockSpec((1, D_MODEL), full),
            pl.BlockSpec((1, D_MODEL), full),
            pl.BlockSpec(memory_space=pl.ANY),
        ],
        out_specs=pl.BlockSpec((TM, D_MODEL), tok),
        out_shape=jax.ShapeDtypeStruct((m, D_MODEL), F32),
        scratch_shapes=[pltpu.VMEM((2, 2, TM, D_MODEL), F32), pltpu.SemaphoreType.DMA((2,))],
        compiler_params=_cparams(("arbitrary",)),
        name="moe_combine_ln",
    )(rows, rows, wts, xs, mods, ln_g, ln_b, ys_sorted)


def _moe_layer(xs, routing, mods, layer, n_lat_tiles, wg, wu, wd, ln_g, ln_b):
    m = xs.shape[0]
    n_tiles = (2 * m + N_EXPERTS * (MOE_TM - 1) + MOE_TM - 1) // MOE_TM
    idx, wts, cnt = routing
    counts = cnt[0, :N_EXPERTS].astype(jnp.int32)
    padded = (counts + MOE_TM - 1) // MOE_TM * MOE_TM
    ends = jnp.cumsum(padded)
    offs = ends - padded
    n_used = (ends[-1:] // MOE_TM).astype(jnp.int32)
    tile_row = jnp.arange(n_tiles, dtype=jnp.int32) * MOE_TM
    tile_row = jnp.minimum(tile_row, ends[-1] - MOE_TM)
    tile_expert = jnp.sum((tile_row[:, None] >= ends[None, :]).astype(jnp.int32), axis=1)
    expert_ids = idx[:, 0:2, :]
    group_off = jnp.sum(jnp.where(expert_ids[..., None] == jnp.arange(N_EXPERTS), offs, 0), axis=-1)
    rows = (group_off + idx[:, 2:4, :]).reshape(m // TM, 1, 2 * TM)
    fill = (offs + counts) // 8 * 8
    xs_sorted = _dispatch(fill, n_used, rows, xs, mods, layer, n_lat_tiles, n_tiles)
    ys_sorted = _experts(tile_expert, n_used, xs_sorted, wg, wu, wd, n_tiles)
    return _combine(rows, wts, xs, mods, layer, n_lat_tiles, ys_sorted, ln_g, ln_b)


_ROPE_SWAP = np.array(list(range(8, 16)) + list(range(0, 8)) + list(range(24, 32)) + list(range(16, 24)))


def _prep_attn_weights(w_in, w_q_up, w_kv_up, w_out):
    o = 3 * NA_W
    w_qc = w_in[:, o:o + MLA_Q_LORA]
    o += MLA_Q_LORA
    w_kvc = w_in[:, o:o + MLA_KV_LORA]
    o += MLA_KV_LORA
    w_kr = w_in[:, o:o + MLA_ROPE]
    rope_pad = [(0, 0), (MLA_NOPE, HEAD_PAD - MLA_NOPE - MLA_ROPE)]
    w_in_aug = jnp.concatenate(
        [w_in[:, :3 * NA_W], w_qc, w_kvc, jnp.pad(w_kr, rope_pad), jnp.pad(w_kr[:, _ROPE_SWAP], rope_pad)],
        axis=1).astype(BF16)

    wq = w_q_up.reshape(MLA_Q_LORA, N_HEADS, MLA_NOPE + MLA_ROPE)
    wq_full = jnp.pad(wq, [(0, 0), (0, 0), (0, HEAD_PAD - MLA_NOPE - MLA_ROPE)])
    wq_swap = jnp.pad(wq[:, :, MLA_NOPE:][:, :, _ROPE_SWAP], [(0, 0), (0, 0), rope_pad[1]])
    w_q = jnp.concatenate([wq_full.reshape(MLA_Q_LORA, HP), wq_swap.reshape(MLA_Q_LORA, HP)], axis=1).astype(BF16)

    wkv = w_kv_up.reshape(MLA_KV_LORA, N_HEADS, MLA_NOPE + MLA_V)
    wkn = jnp.pad(wkv[:, :, :MLA_NOPE], [(0, 0), (0, 0), (0, HEAD_PAD - MLA_NOPE)])
    wv = jnp.pad(wkv[:, :, MLA_NOPE:], [(0, 0), (0, 0), (0, HEAD_PAD - MLA_V)])
    w_kv = jnp.concatenate([wkn.reshape(MLA_KV_LORA, HP), wv.reshape(MLA_KV_LORA, HP)], axis=1).astype(BF16)

    wa = w_out[:NA_W].astype(BF16)
    wb = w_out[NA_W:].reshape(N_HEADS, MLA_V, D_MODEL)
    wb = jnp.pad(wb, [(0, 0), (0, HEAD_PAD - MLA_V), (0, 0)]).reshape(HP, D_MODEL).astype(BF16)
    return w_in_aug, w_q, w_kv, wa, wb


def _rope_tables(s, n_ctx):
    t = jnp.arange(s, dtype=jnp.int32)
    row = (t // GRID_W).astype(F32)
    col = (t % GRID_W).astype(F32)
    n_freq = MLA_ROPE // 4
    inv = 1.0 / (ROPE_THETA ** (jnp.arange(n_freq, dtype=F32) / n_freq))
    ar = row[:, None] * inv
    ac = col[:, None] * inv
    cos = jnp.concatenate([jnp.cos(ar), jnp.cos(ar), jnp.cos(ac), jnp.cos(ac)], axis=1)
    sin = jnp.concatenate([-jnp.sin(ar), jnp.sin(ar), -jnp.sin(ac), jnp.sin(ac)], axis=1)
    right = HEAD_PAD - MLA_NOPE - MLA_ROPE
    cos = jnp.pad(cos, [(0, n_ctx), (MLA_NOPE, right)], constant_values=1.0)
    sin = jnp.pad(sin, [(0, n_ctx), (MLA_NOPE, right)])
    return cos, sin


def _na_bias_mask(rel_bias, n_rows):
    nb = n_rows // NA_QROWS
    c = np.arange(GRID_W)
    cs = np.clip(c - NA_KW // 2, 0, GRID_W - NA_KW)
    kc = np.arange(GRID_W)
    ok_c = (kc[None, :] >= cs[:, None]) & (kc[None, :] < cs[:, None] + NA_KW)
    n_dcol = 2 * NA_KW - 1
    left = GRID_W - NA_KW
    period = 2 * GRID_W
    u = jnp.pad(rel_bias, [(0, 0), (0, 0), (left, period - n_dcol - left)])
    flat = jnp.tile(u, (1, 1, GRID_W + 1))[:, :, :GRID_W * (period - 1)]
    toep = flat.reshape(N_HEADS, 2 * NA_KH - 1, GRID_W, period - 1)[:, :, :, GRID_W - 1:]
    toep = jnp.where(ok_c, toep, NEG_BIG)
    masked = jnp.full((N_HEADS, GRID_W, GRID_W), NEG_BIG, F32)
    out = []
    for b in (0, 1, nb - 1):
        kr0 = int(np.clip(b * NA_QROWS - NA_KH // 2, 0, n_rows - NA_KROWS))
        q_blocks = []
        for qr in range(NA_QROWS):
            r = b * NA_QROWS + qr
            rs = int(np.clip(r - NA_KH // 2, 0, n_rows - NA_KH))
            k_blocks = []
            for klr in range(NA_KROWS):
                kr = kr0 + klr
                k_blocks.append(toep[:, kr - r + NA_KH - 1] if rs <= kr < rs + NA_KH else masked)
            q_blocks.append(jnp.concatenate(k_blocks, axis=-1))
        out.append(jnp.concatenate(q_blocks, axis=1))
    return jnp.stack(out, axis=0)


def kernel(x, c, ctx, c_ctx, mod_w, mod_b, ln1_g, ln1_b, ln2_g, ln2_b, attn_w_in, na_rel_bias, mla_q_norm,
           mla_w_q_up, mla_kv_norm, mla_w_kv_up, attn_w_out, ffn_w_gate, ffn_w_up, ffn_w_down, pool_w,
           pool_scale, moe_router, moe_w_gate, moe_w_up, moe_w_down):
    assert x.shape[0] == 1 and c.shape[0] == 1 and ctx.shape[0] == 1
    s = x.shape[1]
    n_ctx = ctx.shape[1]
    assert n_ctx == TM and s % (NA_QROWS * GRID_W) == 0 and s % MLA_TQ == 0
    assert (s + n_ctx) % MLA_TK == 0
    n_lat_tiles = s // TM
    depth = mod_w.shape[0]

    xs = (x[0], ctx[0])
    ct = jnp.stack([c[0], c_ctx], axis=1)
    mods = _modulation(ct, mod_w, mod_b).reshape(depth, 2, 6, D_MODEL)
    cos_t, sin_t = _rope_tables(s, n_ctx)
    row = lambda v: v.reshape(1, -1)
    moe_bf16 = None

    for i in range(depth):
        j = i // 2
        ctx_live = any(l % 2 == 0 for l in range(i + 1, depth))
        if i % 2 == 0:
            w_in, w_q, w_kv, wa, wb = _prep_attn_weights(attn_w_in[j], mla_w_q_up[j], mla_w_kv_up[j],
                                                         attn_w_out[j])
            qa, ka, va, qm, kmt, vm = _attn_proj(xs, mods, i, n_lat_tiles, w_in, row(mla_q_norm[j]), w_q,
                                                 row(mla_kv_norm[j]), w_kv, cos_t, sin_t)
            bias_mask = _na_bias_mask(na_rel_bias[j], s // GRID_W)
            oa_lat = _na_attention(qa, ka, va, bias_mask, s)
            cast = (moe_w_gate, moe_w_up, moe_w_down) if i + 1 < depth else ()
            ob_lat, moe_bf16 = _mla_attention(qm, kmt, vm, s, cast, j)
            oa_ctx = ob_ctx = None
            if ctx_live:
                oa_ctx = _na_ctx_attention(qa, ka, va, s)
                ob_ctx = _mla_ctx_attention(qm, kmt, vm, s)
            xs = _out_ffn(xs, mods, i, n_lat_tiles, oa_lat, ob_lat, oa_ctx, ob_ctx, wa, wb,
                          row(ln1_g[i]), row(ln1_b[i]), ffn_w_gate[j].astype(BF16), ffn_w_up[j].astype(BF16),
                          ffn_w_down[j].astype(BF16), row(ln2_g[i]), row(ln2_b[i]))
        else:
            xs, *routing = _pool_router(xs, mods, i, n_lat_tiles, s, pool_w[j].astype(BF16), row(pool_scale[j]),
                                        row(ln1_g[i]), row(ln1_b[i]), moe_router[j])
            xs = _moe_layer(xs, routing, mods, i, n_lat_tiles, *moe_bf16, row(ln2_g[i]), row(ln2_b[i]))
    return xs[:s][None] if xs.shape[0] != s else xs[None]
```

```python
import functools

import numpy as np
import jax
import jax.numpy as jnp
from jax import lax
from jax.experimental import pallas as pl
from jax.experimental.pallas import tpu as pltpu

F32 = jnp.float32
BF16 = jnp.bfloat16

D_MODEL = 1024
GRID_W = 64
N_HEADS = 8
HEAD_PAD = 128
HP = N_HEADS * HEAD_PAD
NA_HEAD_DIM = 64
NA_W = N_HEADS * NA_HEAD_DIM
NA_PAIRS = N_HEADS // 2
NA_KH = 8
NA_KW = 16
MLA_NOPE = 64
MLA_ROPE = 32
MLA_V = 64
MLA_Q_LORA = 256
MLA_KV_LORA = 128
ROPE_THETA = 10000.0
POOL_WINDOWS = (2, 4, 8, 16)
POOL_GROUP = D_MODEL // len(POOL_WINDOWS)
POOL_HALO = 8
POOL_PAD = 16
FFN_DIM = 2816
N_EXPERTS = 8
EXPERT_DIM = 3584
DEPTH = 4
DEEPNORM_ALPHA = (2 * DEPTH) ** 0.25
LN_EPS = 1e-5
RMS_EPS = 1e-6
LOG2E = 1.4426950408889634
NEG_BIG = -1e30

TM = 256
NA_QROWS = 4
NA_KROWS = NA_QROWS + NA_KH - 1
NA_QB = 4
MLA_TQ = 1024
MLA_TK = 640
MLA_UNROLL = 12
MOE_TM = 512
MOE_TF = 1792
VMEM_LIMIT = 56 * 1024 * 1024


def _cparams(sem):
    return pltpu.CompilerParams(dimension_semantics=sem, vmem_limit_bytes=VMEM_LIMIT)


def _layer_norm(v, g, b):
    mu = jnp.mean(v, axis=-1, keepdims=True)
    d = v - mu
    var = jnp.mean(d * d, axis=-1, keepdims=True)
    return d * lax.rsqrt(var + LN_EPS) * g + b


def _silu(v):
    return v * jax.nn.sigmoid(v)


def _layer_block(w, j, **kw):
    return pl.BlockSpec((None,) + w.shape[1:], lambda i: (j, 0, 0), **kw)


def _token_specs(xs, n_lat_tiles):
    if isinstance(xs, tuple):
        return list(xs), [pl.BlockSpec((TM, D_MODEL), lambda i: (jnp.minimum(i, n_lat_tiles - 1), 0)),
                          pl.BlockSpec((TM, D_MODEL), lambda i: (0, 0))]
    return [xs], [pl.BlockSpec((TM, D_MODEL), lambda i: (i, 0))]


def _load_tokens(x_refs, n_lat_tiles):
    if len(x_refs) == 2:
        return jnp.where(pl.program_id(0) >= n_lat_tiles, x_refs[1][...], x_refs[0][...])
    return x_refs[0][...]


def _token_rows(xs):
    return sum(a.shape[0] for a in xs) if isinstance(xs, tuple) else xs.shape[0]


def _mod_kernel(ct_ref, w_ref, b_ref, o_ref):
    s = _silu(ct_ref[...])
    w = w_ref[...]
    b = b_ref[...]
    r0 = jnp.sum(w * s[:, 0:1], axis=0, keepdims=True) + b
    r1 = jnp.sum(w * s[:, 1:2], axis=0, keepdims=True) + b
    o_ref[...] = jnp.concatenate([r0, r1], axis=0)


def _modulation(ct, mod_w, mod_b):
    depth, d, n6 = mod_w.shape
    tn = 1536
    return pl.pallas_call(
        _mod_kernel,
        grid=(depth, n6 // tn),
        in_specs=[
            pl.BlockSpec((d, 2), lambda l, j: (0, 0)),
            pl.BlockSpec((None, d, tn), lambda l, j: (l, 0, j)),
            pl.BlockSpec((None, 1, tn), lambda l, j: (l, 0, j)),
        ],
        out_specs=pl.BlockSpec((None, 2, tn), lambda l, j: (l, 0, j)),
        out_shape=jax.ShapeDtypeStruct((depth, 2, n6), F32),
        compiler_params=_cparams(("arbitrary", "arbitrary")),
        name="modulation",
    )(ct, mod_w, mod_b.reshape(depth, 1, n6))


def _attn_proj_kernel(*refs, n_x, n_lat_tiles):
    x_refs = refs[:n_x]
    (mod_ref, win_ref, qn_ref, wq_ref, kvn_ref, wkv_ref, cos_ref, sin_ref,
     qa_ref, ka_ref, va_ref, qm_ref, kmt_ref, vm_ref) = refs[n_x:]
    h = _load_tokens(x_refs, n_lat_tiles) * (1.0 + mod_ref[1:2, :]) + mod_ref[0:1, :]
    p = jnp.dot(h.astype(BF16), win_ref[...], preferred_element_type=F32)
    qa_ref[...] = (p[:, 0:NA_W] * (NA_HEAD_DIM ** -0.5)).astype(BF16)
    ka_ref[...] = p[:, NA_W:2 * NA_W].astype(BF16)
    va_ref[...] = p[:, 2 * NA_W:3 * NA_W].astype(BF16)
    o = 3 * NA_W
    q_c = p[:, o:o + MLA_Q_LORA]
    o += MLA_Q_LORA
    kv_c = p[:, o:o + MLA_KV_LORA]
    o += MLA_KV_LORA
    krp = p[:, o:o + HEAD_PAD]
    krs = p[:, o + HEAD_PAD:o + 2 * HEAD_PAD]
    qn = q_c * lax.rsqrt(jnp.mean(q_c * q_c, axis=-1, keepdims=True) + RMS_EPS) * qn_ref[...]
    kvn = kv_c * lax.rsqrt(jnp.mean(kv_c * kv_c, axis=-1, keepdims=True) + RMS_EPS) * kvn_ref[...]
    q2 = jnp.dot(qn.astype(BF16), wq_ref[...], preferred_element_type=F32)
    kv2 = jnp.dot(kvn.astype(BF16), wkv_ref[...], preferred_element_type=F32)
    cos = cos_ref[...]
    sin = sin_ref[...]
    kr = krp * cos + krs * sin
    q_scale = (MLA_NOPE + MLA_ROPE) ** -0.5 * LOG2E
    for hd in range(N_HEADS):
        lo = hd * HEAD_PAD
        qh = (q2[:, lo:lo + HEAD_PAD] * cos + q2[:, HP + lo:HP + lo + HEAD_PAD] * sin) * q_scale
        qm_ref[:, lo:lo + HEAD_PAD] = qh.astype(BF16)
        kh = kv2[:, lo:lo + HEAD_PAD] + kr
        kmt_ref[hd] = kh.T.astype(BF16)
    lane = lax.broadcasted_iota(jnp.int32, (1, HP), 1)
    ones_col = jnp.where(lane % HEAD_PAD == MLA_V, 1.0, 0.0)
    vm_ref[...] = (kv2[:, HP:2 * HP] + ones_col).astype(BF16)


def _attn_proj(xs, mods, layer, n_lat_tiles, j, w_in, q_norm, w_q, kv_norm, w_kv, cos_t, sin_t):
    m = _token_rows(xs)
    nt = m // TM
    tok = lambda i: (i, 0)
    full = lambda i: (0, 0)
    act = jax.ShapeDtypeStruct((m, HP), BF16)
    na_act = jax.ShapeDtypeStruct((m, NA_W), BF16)
    x_args, x_specs = _token_specs(xs, n_lat_tiles)
    return pl.pallas_call(
        functools.partial(_attn_proj_kernel, n_x=len(x_args), n_lat_tiles=n_lat_tiles),
        grid=(nt,),
        in_specs=x_specs + [
            pl.BlockSpec((None, None, 6, D_MODEL), lambda i: (layer, i // n_lat_tiles, 0, 0)),
            _layer_block(w_in, j),
            pl.BlockSpec(q_norm.shape, full),
            _layer_block(w_q, j),
            pl.BlockSpec(kv_norm.shape, full),
            _layer_block(w_kv, j),
            pl.BlockSpec((TM, HEAD_PAD), tok),
            pl.BlockSpec((TM, HEAD_PAD), tok),
        ],
        out_specs=[
            pl.BlockSpec((TM, NA_W), tok),
            pl.BlockSpec((TM, NA_W), tok),
            pl.BlockSpec((TM, NA_W), tok),
            pl.BlockSpec((TM, HP), tok),
            pl.BlockSpec((N_HEADS, HEAD_PAD, TM), lambda i: (0, 0, i)),
            pl.BlockSpec((TM, HP), tok),
        ],
        out_shape=[na_act, na_act, na_act, act, jax.ShapeDtypeStruct((N_HEADS, HEAD_PAD, m), BF16), act],
        compiler_params=_cparams(("arbitrary",)),
        name="attn_proj",
    )(*x_args, mods, w_in, q_norm, w_q, kv_norm, w_kv, cos_t, sin_t)


_NT_DIMS = (((1,), (1,)), ((), ()))


def _pair_masks():
    lane = lax.broadcasted_iota(jnp.int32, (1, HEAD_PAD), 1)
    return lane < NA_HEAD_DIM, lane >= NA_HEAD_DIM


def _na_kernel(q_ref, k_ref, v_ref, kc_ref, vc_ref, *rest, n_rows):
    bm_refs, o_ref = rest[:NA_QB], rest[NA_QB]
    tq = NA_QROWS * GRID_W
    nk = NA_KROWS * GRID_W
    kc = kc_ref[...]
    vc = vc_ref[...]
    head_lanes = _pair_masks()
    for u in range(NA_QB):
        b = pl.program_id(1) * NA_QB + u
        kr0 = jnp.clip(b * NA_QROWS - NA_KH // 2, 0, n_rows - NA_KROWS)
        start = pl.multiple_of(kr0 * GRID_W, GRID_W)
        q = q_ref[u * tq:(u + 1) * tq, :]
        kw = k_ref[pl.ds(start, nk), :]
        vw = v_ref[pl.ds(start, nk), :]
        outs = []
        for a in range(2):
            qh = jnp.where(head_lanes[a], q, jnp.zeros_like(q))
            s_loc = lax.dot_general(qh, kw, _NT_DIMS, preferred_element_type=F32) + bm_refs[u][a]
            s_ctx = lax.dot_general(qh, kc, _NT_DIMS, preferred_element_type=F32)
            m = jnp.maximum(jnp.max(s_loc, axis=-1, keepdims=True), jnp.max(s_ctx, axis=-1, keepdims=True))
            p_loc = jnp.exp(s_loc - m)
            p_ctx = jnp.exp(s_ctx - m)
            l = jnp.sum(p_loc, axis=-1, keepdims=True) + jnp.sum(p_ctx, axis=-1, keepdims=True)
            o = (jnp.dot(p_loc.astype(BF16), vw, preferred_element_type=F32)
                 + jnp.dot(p_ctx.astype(BF16), vc, preferred_element_type=F32))
            outs.append(o / l)
        o_ref[u * tq:(u + 1) * tq, :] = jnp.where(head_lanes[0], outs[0], outs[1]).astype(BF16)


def _na_attention(qa, ka, va, bias_mask, j, s):
    n_rows = s // GRID_W
    nb = n_rows // NA_QROWS
    tq = NA_QROWS * GRID_W
    assert nb % NA_QB == 0
    ctx_blk = s // tq
    pat = lambda b: jnp.where(b == 0, 0, jnp.where(b == nb - 1, 2, 1))
    bm_spec = lambda u: pl.BlockSpec((None, None, 2, tq, NA_KROWS * GRID_W),
                                     lambda hp, g: (j, pat(g * NA_QB + u), hp, 0, 0))
    return pl.pallas_call(
        functools.partial(_na_kernel, n_rows=n_rows),
        grid=(NA_PAIRS, nb // NA_QB),
        in_specs=[
            pl.BlockSpec((NA_QB * tq, HEAD_PAD), lambda hp, j: (j, hp)),
            pl.BlockSpec((s, HEAD_PAD), lambda hp, j: (0, hp)),
            pl.BlockSpec((s, HEAD_PAD), lambda hp, j: (0, hp)),
            pl.BlockSpec((tq, HEAD_PAD), lambda hp, j: (ctx_blk, hp)),
            pl.BlockSpec((tq, HEAD_PAD), lambda hp, j: (ctx_blk, hp)),
        ] + [bm_spec(u) for u in range(NA_QB)],
        out_specs=pl.BlockSpec((NA_QB * tq, HEAD_PAD), lambda hp, j: (j, hp)),
        out_shape=jax.ShapeDtypeStruct((s, NA_W), BF16),
        compiler_params=_cparams(("arbitrary", "arbitrary")),
        name="na_attention",
    )(qa, ka, va, ka, va, *([bias_mask] * NA_QB))


def _na_ctx_kernel(q_ref, k_ref, v_ref, o_ref):
    q = q_ref[...]
    k = k_ref[...]
    v = v_ref[...]
    head_lanes = _pair_masks()
    outs = []
    for a in range(2):
        qh = jnp.where(head_lanes[a], q, jnp.zeros_like(q))
        s = lax.dot_general(qh, k, _NT_DIMS, preferred_element_type=F32)
        m = jnp.max(s, axis=-1, keepdims=True)
        p = jnp.exp(s - m)
        l = jnp.sum(p, axis=-1, keepdims=True)
        outs.append(jnp.dot(p.astype(BF16), v, preferred_element_type=F32) / l)
    o_ref[...] = jnp.where(head_lanes[0], outs[0], outs[1]).astype(BF16)


def _na_ctx_attention(qa, ka, va, s):
    blk = s // TM
    spec = pl.BlockSpec((TM, HEAD_PAD), lambda hp: (blk, hp))
    return pl.pallas_call(
        _na_ctx_kernel,
        grid=(NA_PAIRS,),
        in_specs=[spec, spec, spec],
        out_specs=pl.BlockSpec((TM, HEAD_PAD), lambda hp: (0, hp)),
        out_shape=jax.ShapeDtypeStruct((TM, NA_W), BF16),
        compiler_params=_cparams(("arbitrary",)),
        name="na_ctx_attention",
    )(qa, ka, va)


def _mla_ctx_kernel(q_ref, kt_ref, v_ref, o_ref):
    s = jnp.dot(q_ref[...], kt_ref[...], preferred_element_type=F32)
    m = jnp.max(s, axis=-1, keepdims=True)
    p = jnp.exp2(s - m)
    l = jnp.sum(p, axis=-1, keepdims=True)
    o = jnp.dot(p.astype(BF16), v_ref[...], preferred_element_type=F32)
    o_ref[...] = (o / l).astype(BF16)


def _mla_ctx_attention(qm, kmt, vm, s):
    blk = s // TM
    return pl.pallas_call(
        _mla_ctx_kernel,
        grid=(N_HEADS,),
        in_specs=[
            pl.BlockSpec((TM, HEAD_PAD), lambda h: (blk, h)),
            pl.BlockSpec((None, HEAD_PAD, TM), lambda h: (h, 0, blk)),
            pl.BlockSpec((TM, HEAD_PAD), lambda h: (blk, h)),
        ],
        out_specs=pl.BlockSpec((TM, HEAD_PAD), lambda h: (0, h)),
        out_shape=jax.ShapeDtypeStruct((TM, HP), BF16),
        compiler_params=_cparams(("arbitrary",)),
        name="mla_ctx_attention",
    )(qm, kmt, vm)


def _mla_kernel(q_ref, qn_ref, kt_ref, v_ref, *rest, n_chunks, n_cast):
    cast_in, o_ref, cast_out = rest[:n_cast], rest[n_cast], rest[n_cast + 1:2 * n_cast + 1]
    s0_ref, s1_ref = rest[2 * n_cast + 1:]
    for w_ref, wb_ref in zip(cast_in, cast_out):
        wb_ref[...] = w_ref[...].astype(BF16)
    q = q_ref[...]
    tq = q.shape[0]

    def scores(c, s_ref, q_tile=q):
        off = pl.multiple_of(c * MLA_TK, MLA_TK)
        s_ref[...] = jnp.dot(q_tile, kt_ref[:, pl.ds(off, MLA_TK)], preferred_element_type=F32)

    def update(c, s_ref, carry):
        m, acc = carry
        off = pl.multiple_of(c * MLA_TK, MLA_TK)
        s = s_ref[...]
        m_new = jnp.maximum(m, jnp.max(s, axis=-1, keepdims=True))
        alpha = jnp.exp2(m - m_new)
        p = jnp.exp2(s - m_new).astype(BF16)
        acc = alpha * acc + jnp.dot(p, v_ref[pl.ds(off, MLA_TK), :], preferred_element_type=F32)
        return m_new, acc

    bufs = (s0_ref, s1_ref)

    def group(g, carry):
        c0 = g * MLA_UNROLL
        for u in range(MLA_UNROLL):
            scores(c0 + u + 1, bufs[(u + 1) % 2])
            carry = update(c0 + u, bufs[u % 2], carry)
        return carry

    carry = (jnp.full((tq, 1), NEG_BIG, F32), jnp.zeros((tq, HEAD_PAD), F32))

    @pl.when(pl.program_id(1) == 0)
    def _():
        scores(0, s0_ref)

    n_groups = (n_chunks - 2) // MLA_UNROLL
    carry = lax.fori_loop(0, n_groups, group, carry)
    for c in range(n_groups * MLA_UNROLL, n_chunks):
        if c + 1 < n_chunks:
            scores(c + 1, bufs[(c + 1) % 2])
        else:
            scores(0, s0_ref, qn_ref[...])
        carry = update(c, bufs[c % 2], carry)
    _, acc = carry
    o_ref[...] = (acc / acc[:, MLA_V:MLA_V + 1]).astype(BF16)


def _mla_attention(qm, kmt, vm, s, cast_weights=(), cast_layer=0):
    m = qm.shape[0]
    n_chunks = m // MLA_TK
    assert n_chunks % 2 == 0 and n_chunks >= 2
    n_q = s // MLA_TQ
    assert N_HEADS == N_EXPERTS or not cast_weights
    cast_in_specs, cast_out_specs, cast_shapes = [], [], []
    for w in cast_weights:
        rows, cols = w.shape[2], w.shape[3]
        slab = rows // n_q
        assert rows % n_q == 0 and slab % 16 == 0
        cast_in_specs.append(pl.BlockSpec((None, None, slab, cols), lambda h, i: (cast_layer, h, i, 0)))
        cast_out_specs.append(pl.BlockSpec((None, slab, cols), lambda h, i: (h, i, 0)))
        cast_shapes.append(jax.ShapeDtypeStruct((N_EXPERTS, rows, cols), BF16))
    outs = pl.pallas_call(
        functools.partial(_mla_kernel, n_chunks=n_chunks, n_cast=len(cast_weights)),
        grid=(N_HEADS, n_q),
        in_specs=[
            pl.BlockSpec((MLA_TQ, HEAD_PAD), lambda h, i: (i, h)),
            pl.BlockSpec((MLA_TQ, HEAD_PAD), lambda h, i: (jnp.minimum(i + 1, n_q - 1), h)),
            pl.BlockSpec((None, HEAD_PAD, m), lambda h, i: (h, 0, 0)),
            pl.BlockSpec((m, HEAD_PAD), lambda h, i: (0, h)),
        ] + cast_in_specs,
        out_specs=[pl.BlockSpec((MLA_TQ, HEAD_PAD), lambda h, i: (i, h))] + cast_out_specs,
        out_shape=[jax.ShapeDtypeStruct((s, HP), BF16)] + cast_shapes,
        scratch_shapes=[pltpu.VMEM((MLA_TQ, MLA_TK), F32), pltpu.VMEM((MLA_TQ, MLA_TK), F32)],
        compiler_params=_cparams(("arbitrary", "arbitrary")),
        name="mla_attention",
    )(qm, qm, kmt, vm, *cast_weights)
    return outs[0], tuple(outs[1:])


def _out_ffn_kernel(*refs, n_x, n_lat_tiles, with_ctx):
    x_refs = refs[:n_x]
    mod_ref, oal_ref, obl_ref = refs[n_x:n_x + 3]
    rest = refs[n_x + 3:]
    if with_ctx:
        oac_ref, obc_ref = rest[:2]
        rest = rest[2:]
        is_ctx = pl.program_id(0) >= n_lat_tiles
        oa = jnp.where(is_ctx, oac_ref[...], oal_ref[...])
        ob = jnp.where(is_ctx, obc_ref[...], obl_ref[...])
    else:
        oa = oal_ref[...]
        ob = obl_ref[...]
    wa_ref, wb_ref, g1_ref, b1_ref, wg_ref, wu_ref, wd_ref, g2_ref, b2_ref, o_ref = rest
    y = (jnp.dot(oa, wa_ref[...], preferred_element_type=F32)
         + jnp.dot(ob, wb_ref[...], preferred_element_type=F32))
    v = DEEPNORM_ALPHA * _load_tokens(x_refs, n_lat_tiles) + mod_ref[2:3, :] * y
    x1 = _layer_norm(v, g1_ref[...], b1_ref[...])
    h = (x1 * (1.0 + mod_ref[4:5, :]) + mod_ref[3:4, :]).astype(BF16)
    gate = jnp.dot(h, wg_ref[...], preferred_element_type=F32)
    up = jnp.dot(h, wu_ref[...], preferred_element_type=F32)
    a = (_silu(gate) * up).astype(BF16)
    y2 = jnp.dot(a, wd_ref[...], preferred_element_type=F32)
    v2 = DEEPNORM_ALPHA * x1 + mod_ref[5:6, :] * y2
    o_ref[...] = _layer_norm(v2, g2_ref[...], b2_ref[...])


def _out_ffn(xs, mods, layer, n_lat_tiles, j, oa_lat, ob_lat, oa_ctx, ob_ctx, wa, wb, ln1_g, ln1_b,
             wg, wu, wd, ln2_g, ln2_b):
    with_ctx = oa_ctx is not None
    m = _token_rows(xs) if with_ctx else n_lat_tiles * TM
    tok = lambda i: (i, 0)
    lat = lambda i: (jnp.minimum(i, n_lat_tiles - 1), 0)
    full = lambda i: (0, 0)
    ctx_args = [oa_ctx, ob_ctx] if with_ctx else []
    x_args, x_specs = _token_specs(xs, n_lat_tiles)
    resident = lambda w: _layer_block(w, j, pipeline_mode=pl.Buffered(1))
    vec = pl.BlockSpec((1, D_MODEL), full)
    return pl.pallas_call(
        functools.partial(_out_ffn_kernel, n_x=len(x_args), n_lat_tiles=n_lat_tiles, with_ctx=with_ctx),
        grid=(m // TM,),
        in_specs=x_specs + [
            pl.BlockSpec((None, None, 6, D_MODEL), lambda i: (layer, i // n_lat_tiles, 0, 0)),
            pl.BlockSpec((TM, NA_W), lat),
            pl.BlockSpec((TM, HP), lat),
        ] + ([pl.BlockSpec((TM, NA_W), full), pl.BlockSpec((TM, HP), full)] if with_ctx else []) + [
            resident(wa), resident(wb), vec, vec, resident(wg), resident(wu), resident(wd), vec, vec,
        ],
        out_specs=pl.BlockSpec((TM, D_MODEL), tok),
        out_shape=jax.ShapeDtypeStruct((m, D_MODEL), F32),
        compiler_params=_cparams(("arbitrary",)),
        name="out_proj_ffn_ln",
    )(*x_args, mods, oa_lat, ob_lat, *ctx_args, wa, wb, ln1_g, ln1_b, wg, wu, wd, ln2_g, ln2_b)


def _pool_router_kernel(x_ref, xp_ref, xn_ref, mod_ref, w_ref, sc_ref, g_ref, b_ref, rw_ref, tri_ref,
                        o_ref, idx_ref, wts_ref, cnt_ref, ext_ref, lvl_ref, base_ref, *, n_lat_tiles, s):
    i = pl.program_id(0)

    @pl.when(i == 0)
    def _():
        base_ref[...] = jnp.zeros_like(base_ref)

    is_ctx = i >= n_lat_tiles
    j = jnp.where(is_ctx, i - n_lat_tiles, i)
    last = jnp.where(is_ctx, 0, n_lat_tiles - 1)
    n_seq = jnp.where(is_ctx, TM, s)
    scale1 = 1.0 + mod_ref[1:2, :]
    shift1 = mod_ref[0:1, :]
    x = x_ref[...]
    h = x * scale1 + shift1
    hp = jnp.where(j != 0, xp_ref[...] * scale1 + shift1, 0.0)
    hn = jnp.where(j != last, xn_ref[...] * scale1 + shift1, 0.0)
    n_ext = TM + 2 * POOL_HALO
    ext_ref[0:POOL_PAD, :] = jnp.zeros((POOL_PAD, D_MODEL), F32)
    ext_ref[POOL_PAD:POOL_PAD + POOL_HALO, :] = hp
    ext_ref[POOL_PAD + POOL_HALO:POOL_PAD + POOL_HALO + TM, :] = h
    ext_ref[POOL_PAD + POOL_HALO + TM:POOL_PAD + n_ext, :] = hn
    t = j * TM + lax.broadcasted_iota(jnp.int32, (TM, 1), 0)
    ys = []
    for g, win in enumerate(POOL_WINDOWS):
        half = win // 2
        lo = g * POOL_GROUP
        run = lambda r0, r1: ext_ref[r0:r1, lo:lo + POOL_GROUP]
        n, k = 1, 0
        while n < win:
            nxt = lvl_ref.at[g, k % 2]
            nxt[0:POOL_PAD, :] = jnp.zeros((POOL_PAD, POOL_GROUP), F32)
            nxt[POOL_PAD:POOL_PAD + n_ext, :] = (run(POOL_PAD, POOL_PAD + n_ext)
                                                 + run(POOL_PAD - n, POOL_PAD - n + n_ext))
            run = lambda r0, r1, ref=nxt: ref[r0:r1, :]
            n, k = 2 * n, k + 1
        first = POOL_PAD + POOL_HALO + half - 1
        acc = run(first, first + TM)
        cnt = (jnp.minimum(t + half, n_seq) - jnp.maximum(t - half, 0)).astype(F32)
        mixed = acc / cnt - h[:, lo:lo + POOL_GROUP]
        yg = jnp.dot(mixed.astype(BF16), w_ref[g], preferred_element_type=F32)
        ys.append(yg)
    y = jnp.concatenate(ys, axis=-1) * sc_ref[...]
    v = DEEPNORM_ALPHA * x + mod_ref[2:3, :] * y
    x1 = _layer_norm(v, g_ref[...], b_ref[...])
    o_ref[...] = x1
    _route(x1, mod_ref, rw_ref, tri_ref, idx_ref, wts_ref, cnt_ref, base_ref)


def _pool_router(xs, mods, layer, n_lat_tiles, s, pool_w, pool_scale, ln_g, ln_b, router_w):
    m = xs.shape[0]
    nt = m // TM
    rw = jnp.pad(router_w, [(0, 0), (0, HEAD_PAD - N_EXPERTS)])
    rw_hi = rw.astype(BF16)
    rw = jnp.stack([rw_hi, (rw - rw_hi.astype(F32)).astype(BF16)])
    tri = jnp.asarray(np.tril(np.ones((TM, TM), np.float32), -1), dtype=BF16)
    per = TM // POOL_HALO
    n_halo_blocks = m // POOL_HALO
    tok = lambda i: (i, 0)
    full = lambda i: (0, 0)
    return pl.pallas_call(
        functools.partial(_pool_router_kernel, n_lat_tiles=n_lat_tiles, s=s),
        grid=(nt,),
        in_specs=[
            pl.BlockSpec((TM, D_MODEL), tok),
            pl.BlockSpec((POOL_HALO, D_MODEL), lambda i: (jnp.maximum(i * per - 1, 0), 0)),
            pl.BlockSpec((POOL_HALO, D_MODEL), lambda i: (jnp.minimum((i + 1) * per, n_halo_blocks - 1), 0)),
            pl.BlockSpec((None, None, 6, D_MODEL), lambda i: (layer, i // n_lat_tiles, 0, 0)),
            pl.BlockSpec(pool_w.shape, lambda i: (0, 0, 0)),
            pl.BlockSpec((1, D_MODEL), full),
            pl.BlockSpec((1, D_MODEL), full),
            pl.BlockSpec((1, D_MODEL), full),
            pl.BlockSpec(rw.shape, lambda i: (0, 0, 0)),
            pl.BlockSpec((TM, TM), full),
        ],
        out_specs=[
            pl.BlockSpec((TM, D_MODEL), tok),
            pl.BlockSpec((None, 8, TM), lambda i: (i, 0, 0)),
            pl.BlockSpec((TM, HEAD_PAD), tok),
            pl.BlockSpec((8, HEAD_PAD), full),
        ],
        out_shape=[
            jax.ShapeDtypeStruct((m, D_MODEL), F32),
            jax.ShapeDtypeStruct((nt, 8, TM), jnp.int32),
            jax.ShapeDtypeStruct((m, HEAD_PAD), F32),
            jax.ShapeDtypeStruct((8, HEAD_PAD), F32),
        ],
        scratch_shapes=[pltpu.VMEM((POOL_PAD + TM + 2 * POOL_HALO, D_MODEL), F32),
                        pltpu.VMEM((len(POOL_WINDOWS), 2, POOL_PAD + TM + 2 * POOL_HALO, POOL_GROUP), F32),
                        pltpu.VMEM((8, HEAD_PAD), F32)],
        compiler_params=_cparams(("arbitrary",)),
        name="pool_ln_router",
    )(xs, xs, xs, mods, pool_w, pool_scale, ln_g, ln_b, rw, tri)


def _route(x1, mod_ref, rw_ref, tri_ref, idx_ref, wts_ref, cnt_ref, base_ref):
    h = x1 * (1.0 + mod_ref[4:5, :]) + mod_ref[3:4, :]
    h_hi = h.astype(BF16)
    h_lo = (h - h_hi.astype(F32)).astype(BF16)
    logits = (jnp.dot(h_hi, rw_ref[0], preferred_element_type=F32)
              + jnp.dot(h_hi, rw_ref[1], preferred_element_type=F32)
              + jnp.dot(h_lo, rw_ref[0], preferred_element_type=F32))
    lane = lax.broadcasted_iota(jnp.int32, logits.shape, 1)
    logits = jnp.where(lane < N_EXPERTS, logits, -jnp.inf)
    m1 = jnp.max(logits, axis=-1, keepdims=True)
    i1 = jnp.min(jnp.where(logits == m1, lane, HEAD_PAD), axis=-1, keepdims=True)
    rest = jnp.where(lane == i1, -jnp.inf, logits)
    m2 = jnp.max(rest, axis=-1, keepdims=True)
    i2 = jnp.min(jnp.where(rest == m2, lane, HEAD_PAD), axis=-1, keepdims=True)
    e2 = jnp.exp(m2 - m1)
    w1 = 1.0 / (1.0 + e2)
    w2 = e2 / (1.0 + e2)
    wts_ref[...] = jnp.where(lane == 0, w1, jnp.where(lane == 1, w2, 0.0))

    chosen = jnp.where(lane == i1, 1.0, jnp.where(lane == i2, 1.0, 0.0))
    before = jnp.dot(tri_ref[...], chosen.astype(BF16), preferred_element_type=F32)
    rank = before + base_ref[0:1, :]
    r1 = jnp.sum(jnp.where(lane == i1, rank, 0.0), axis=-1, keepdims=True)
    r2 = jnp.sum(jnp.where(lane == i2, rank, 0.0), axis=-1, keepdims=True)
    base_ref[0:1, :] = base_ref[0:1, :] + jnp.sum(chosen, axis=0, keepdims=True)
    cnt_ref[...] = base_ref[...]
    packed = jnp.where(lane == 0, i1.astype(F32),
                       jnp.where(lane == 1, i2.astype(F32),
                                 jnp.where(lane == 2, r1, jnp.where(lane == 3, r2, 0.0))))
    idx_ref[...] = packed.T[0:8, :].astype(jnp.int32)


def _dispatch_kernel(fill_ref, nu_ref, rows_ref, x_ref, mod_ref, xs_ref, hbuf, zbuf, sems, zsem):
    i = pl.program_id(0)
    n = pl.num_programs(0)
    slot = i % 2
    n_tiles = xs_ref.shape[0] // MOE_TM

    @pl.when(i == 0)
    def _():
        zbuf[...] = jnp.zeros_like(zbuf)
        fills = [pltpu.make_async_copy(zbuf, xs_ref.at[pl.ds(pl.multiple_of(fill_ref[e], 8), MOE_TM)],
                                       zsem.at[0]) for e in range(N_EXPERTS)]
        for cp in fills:
            cp.start()
        for cp in fills:
            cp.wait()

        def zero_tail(t, carry):
            row0 = pl.multiple_of(t * MOE_TM, MOE_TM)
            cp = pltpu.make_async_copy(zbuf, xs_ref.at[pl.ds(row0, MOE_TM)], zsem.at[0])
            cp.start()
            cp.wait()
            return carry

        lax.fori_loop(nu_ref[0], n_tiles, zero_tail, 0)

    def wait_slot(sl):
        for _ in range(2):
            pltpu.make_async_copy(hbuf.at[sl], xs_ref.at[pl.ds(0, TM)], sems.at[sl]).wait()

    @pl.when(i >= 2)
    def _():
        wait_slot(slot)

    hbuf[slot] = x_ref[...] * (1.0 + mod_ref[4:5, :]) + mod_ref[3:4, :]

    def issue(r, carry):
        for k in range(2):
            dst = rows_ref[0, k * TM + r]
            pltpu.make_async_copy(hbuf.at[slot, pl.ds(r, 1)], xs_ref.at[pl.ds(dst, 1)], sems.at[slot]).start()
        return carry

    lax.fori_loop(0, TM, issue, 0, unroll=8)

    @pl.when(i == n - 1)
    def _():
        wait_slot(slot)

        @pl.when(n >= 2)
        def _():
            wait_slot(1 - slot)


def _dispatch(fill, n_used, rows, xs, mods, layer, n_lat_tiles, n_tiles):
    m = xs.shape[0]
    grid_spec = pltpu.PrefetchScalarGridSpec(
        num_scalar_prefetch=2,
        grid=(m // TM,),
        in_specs=[
            pl.BlockSpec((None, 1, 2 * TM), lambda i, fill, nu: (i, 0, 0), memory_space=pltpu.SMEM),
            pl.BlockSpec((TM, D_MODEL), lambda i, fill, nu: (i, 0)),
            pl.BlockSpec((None, None, 6, D_MODEL), lambda i, fill, nu: (layer, i // n_lat_tiles, 0, 0)),
        ],
        out_specs=pl.BlockSpec(memory_space=pl.ANY),
        scratch_shapes=[pltpu.VMEM((2, TM, D_MODEL), F32), pltpu.VMEM((MOE_TM, D_MODEL), F32),
                        pltpu.SemaphoreType.DMA((2,)), pltpu.SemaphoreType.DMA((1,))],
    )
    return pl.pallas_call(
        _dispatch_kernel,
        grid_spec=grid_spec,
        out_shape=jax.ShapeDtypeStruct(((n_tiles + 1) * MOE_TM, D_MODEL), F32),
        compiler_params=_cparams(("arbitrary",)),
        name="moe_dispatch",
    )(fill, n_used, rows, xs, mods)


def _experts_kernel(te_ref, nu_ref, x_ref, wg_ref, wu_ref, wd_ref, o_ref, xb_ref, a_ref):
    del te_ref
    f = pl.program_id(1)
    live = pl.program_id(0) < nu_ref[0]

    @pl.when(jnp.logical_not(live) & (f == 0))
    def _():
        o_ref[...] = jnp.zeros_like(o_ref)

    def hidden(xb):
        gate = jnp.dot(xb, wg_ref[...], preferred_element_type=F32)
        up = jnp.dot(xb, wu_ref[...], preferred_element_type=F32)
        return (_silu(gate) * up).astype(BF16)

    @pl.when(live & (f == 0))
    def _():
        xb = x_ref[...].astype(BF16)
        xb_ref[...] = xb
        a_ref[:, 0:MOE_TF] = hidden(xb)

    @pl.when(live & (f == 1))
    def _():
        a_ref[:, MOE_TF:2 * MOE_TF] = hidden(xb_ref[...])
        o_ref[...] = jnp.dot(a_ref[...], wd_ref[...], preferred_element_type=F32)


def _experts(tile_expert, n_used, xs_sorted, wg, wu, wd, n_tiles):
    nf = EXPERT_DIM // MOE_TF
    row_blk = lambda i, f, te, nu: (jnp.minimum(i, nu[0] - 1), 0)
    f_blk = lambda i, f, nu: jnp.where(i < nu[0], f, nf - 1)
    grid_spec = pltpu.PrefetchScalarGridSpec(
        num_scalar_prefetch=2,
        grid=(n_tiles, nf),
        in_specs=[
            pl.BlockSpec((MOE_TM, D_MODEL), row_blk),
            pl.BlockSpec((None, D_MODEL, MOE_TF), lambda i, f, te, nu: (te[i], 0, f_blk(i, f, nu))),
            pl.BlockSpec((None, D_MODEL, MOE_TF), lambda i, f, te, nu: (te[i], 0, f_blk(i, f, nu))),
            pl.BlockSpec((None, EXPERT_DIM, D_MODEL), lambda i, f, te, nu: (te[i], 0, 0)),
        ],
        out_specs=pl.BlockSpec((MOE_TM, D_MODEL), lambda i, f, te, nu: (i, 0)),
        scratch_shapes=[pltpu.VMEM((MOE_TM, D_MODEL), BF16), pltpu.VMEM((MOE_TM, EXPERT_DIM), BF16)],
    )
    assert nf == 2
    return pl.pallas_call(
        _experts_kernel,
        grid_spec=grid_spec,
        out_shape=jax.ShapeDtypeStruct((n_tiles * MOE_TM, D_MODEL), F32),
        compiler_params=_cparams(("arbitrary", "arbitrary")),
        name="moe_experts",
    )(tile_expert, n_used, xs_sorted, wg, wu, wd)


def _combine_kernel(rows_ref, rowsn_ref, wts_ref, x_ref, mod_ref, g_ref, b_ref, ys_ref, o_ref, ybuf, sems):
    i = pl.program_id(0)
    n = pl.num_programs(0)
    slot = i % 2

    def fetch(ids_ref, sl):
        def issue(r, carry):
            for k in range(2):
                src = ids_ref[0, k * TM + r]
                pltpu.make_async_copy(ys_ref.at[pl.ds(src, 1)], ybuf.at[sl, k, pl.ds(r, 1)], sems.at[sl]).start()
            return carry

        lax.fori_loop(0, TM, issue, 0, unroll=8)

    @pl.when(i == 0)
    def _():
        fetch(rows_ref, 0)

    @pl.when(i + 1 < n)
    def _():
        fetch(rowsn_ref, 1 - slot)

    for k in range(2):
        pltpu.make_async_copy(ys_ref.at[pl.ds(0, TM)], ybuf.at[slot, k], sems.at[slot]).wait()

    wts = wts_ref[...]
    y = wts[:, 0:1] * ybuf[slot, 0] + wts[:, 1:2] * ybuf[slot, 1]
    v = DEEPNORM_ALPHA * x_ref[...] + mod_ref[5:6, :] * y
    o_ref[...] = _layer_norm(v, g_ref[...], b_ref[...])


def _combine(rows, wts, xs, mods, layer, n_lat_tiles, ys_sorted, ln_g, ln_b):
    m = xs.shape[0]
    nt = m // TM
    tok = lambda i: (i, 0)
    full = lambda i: (0, 0)
    return pl.pallas_call(
        _combine_kernel,
        grid=(nt,),
        in_specs=[
            pl.BlockSpec((None, 1, 2 * TM), lambda i: (i, 0, 0), memory_space=pltpu.SMEM),
            pl.BlockSpec((None, 1, 2 * TM), lambda i: (jnp.minimum(i + 1, nt - 1), 0, 0),
                         memory_space=pltpu.SMEM),
            pl.BlockSpec((TM, HEAD_PAD), tok),
            pl.BlockSpec((TM, D_MODEL), tok),
            pl.BlockSpec((None, None, 6, D_MODEL), lambda i: (layer, i // n_lat_tiles, 0, 0)),
            pl.BlockSpec((1, D_MODEL), full),
            pl.BlockSpec((1, D_MODEL), full),
            pl.BlockSpec(memory_space=pl.ANY),
        ],
        out_specs=pl.BlockSpec((TM, D_MODEL), tok),
        out_shape=jax.ShapeDtypeStruct((m, D_MODEL), F32),
        scratch_shapes=[pltpu.VMEM((2, 2, TM, D_MODEL), F32), pltpu.SemaphoreType.DMA((2,))],
        compiler_params=_cparams(("arbitrary",)),
        name="moe_combine_ln",
    )(rows, rows, wts, xs, mods, ln_g, ln_b, ys_sorted)


def _moe_layer(xs, routing, mods, layer, n_lat_tiles, wg, wu, wd, ln_g, ln_b):
    m = xs.shape[0]
    n_tiles = (2 * m + N_EXPERTS * (MOE_TM - 1) + MOE_TM - 1) // MOE_TM
    idx, wts, cnt = routing
    counts = cnt[0, :N_EXPERTS].astype(jnp.int32)
    padded = (counts + MOE_TM - 1) // MOE_TM * MOE_TM
    ends = jnp.cumsum(padded)
    offs = ends - padded
    n_used = (ends[-1:] // MOE_TM).astype(jnp.int32)
    tile_row = jnp.arange(n_tiles, dtype=jnp.int32) * MOE_TM
    tile_row = jnp.minimum(tile_row, ends[-1] - MOE_TM)
    tile_expert = jnp.sum((tile_row[:, None] >= ends[None, :]).astype(jnp.int32), axis=1)
    expert_ids = idx[:, 0:2, :]
    group_off = jnp.sum(jnp.where(expert_ids[..., None] == jnp.arange(N_EXPERTS), offs, 0), axis=-1)
    rows = (group_off + idx[:, 2:4, :]).reshape(m // TM, 1, 2 * TM)
    fill = (offs + counts) // 8 * 8
    xs_sorted = _dispatch(fill, n_used, rows, xs, mods, layer, n_lat_tiles, n_tiles)
    ys_sorted = _experts(tile_expert, n_used, xs_sorted, wg, wu, wd, n_tiles)
    return _combine(rows, wts, xs, mods, layer, n_lat_tiles, ys_sorted, ln_g, ln_b)


_ROPE_SWAP = np.array(list(range(8, 16)) + list(range(0, 8)) + list(range(24, 32)) + list(range(16, 24)))


def _prep_attn_weights(w_in, w_q_up, w_kv_up, w_out):
    n_l = w_in.shape[0]
    lead = [(0, 0)]
    o = 3 * NA_W
    w_qc = w_in[..., o:o + MLA_Q_LORA]
    o += MLA_Q_LORA
    w_kvc = w_in[..., o:o + MLA_KV_LORA]
    o += MLA_KV_LORA
    w_kr = w_in[..., o:o + MLA_ROPE]
    lane_pad = (MLA_NOPE, HEAD_PAD - MLA_NOPE - MLA_ROPE)
    rope_pad = lead + [(0, 0), lane_pad]
    w_in_aug = jnp.concatenate(
        [w_in[..., :3 * NA_W], w_qc, w_kvc, jnp.pad(w_kr, rope_pad), jnp.pad(w_kr[..., _ROPE_SWAP], rope_pad)],
        axis=-1).astype(BF16)

    wq = w_q_up.reshape(n_l, MLA_Q_LORA, N_HEADS, MLA_NOPE + MLA_ROPE)
    wq_full = jnp.pad(wq, lead + [(0, 0), (0, 0), (0, HEAD_PAD - MLA_NOPE - MLA_ROPE)])
    wq_swap = jnp.pad(wq[..., MLA_NOPE:][..., _ROPE_SWAP], lead + [(0, 0), (0, 0), lane_pad])
    w_q = jnp.concatenate([wq_full.reshape(n_l, MLA_Q_LORA, HP), wq_swap.reshape(n_l, MLA_Q_LORA, HP)],
                          axis=-1).astype(BF16)

    wkv = w_kv_up.reshape(n_l, MLA_KV_LORA, N_HEADS, MLA_NOPE + MLA_V)
    wkn = jnp.pad(wkv[..., :MLA_NOPE], lead + [(0, 0), (0, 0), (0, HEAD_PAD - MLA_NOPE)])
    wv = jnp.pad(wkv[..., MLA_NOPE:], lead + [(0, 0), (0, 0), (0, HEAD_PAD - MLA_V)])
    w_kv = jnp.concatenate([wkn.reshape(n_l, MLA_KV_LORA, HP), wv.reshape(n_l, MLA_KV_LORA, HP)],
                           axis=-1).astype(BF16)

    wa = w_out[:, :NA_W].astype(BF16)
    wb = w_out[:, NA_W:].reshape(n_l, N_HEADS, MLA_V, D_MODEL)
    wb = jnp.pad(wb, lead + [(0, 0), (0, HEAD_PAD - MLA_V), (0, 0)]).reshape(n_l, HP, D_MODEL).astype(BF16)
    return w_in_aug, w_q, w_kv, wa, wb


def _rope_tables(s, n_ctx):
    t = jnp.arange(s, dtype=jnp.int32)
    row = (t // GRID_W).astype(F32)
    col = (t % GRID_W).astype(F32)
    n_freq = MLA_ROPE // 4
    inv = 1.0 / (ROPE_THETA ** (jnp.arange(n_freq, dtype=F32) / n_freq))
    ar = row[:, None] * inv
    ac = col[:, None] * inv
    cos = jnp.concatenate([jnp.cos(ar), jnp.cos(ar), jnp.cos(ac), jnp.cos(ac)], axis=1)
    sin = jnp.concatenate([-jnp.sin(ar), jnp.sin(ar), -jnp.sin(ac), jnp.sin(ac)], axis=1)
    right = HEAD_PAD - MLA_NOPE - MLA_ROPE
    cos = jnp.pad(cos, [(0, n_ctx), (MLA_NOPE, right)], constant_values=1.0)
    sin = jnp.pad(sin, [(0, n_ctx), (MLA_NOPE, right)])
    return cos, sin


def _na_bias_mask(rel_bias, n_rows):
    n_l = rel_bias.shape[0]
    nb = n_rows // NA_QROWS
    c = np.arange(GRID_W)
    cs = np.clip(c - NA_KW // 2, 0, GRID_W - NA_KW)
    kc = np.arange(GRID_W)
    ok_c = (kc[None, :] >= cs[:, None]) & (kc[None, :] < cs[:, None] + NA_KW)
    n_dcol = 2 * NA_KW - 1
    left = GRID_W - NA_KW
    period = 2 * GRID_W
    u = jnp.pad(rel_bias, [(0, 0), (0, 0), (0, 0), (left, period - n_dcol - left)])
    flat = jnp.tile(u, (1, 1, 1, GRID_W + 1))[..., :GRID_W * (period - 1)]
    toep = flat.reshape(n_l, N_HEADS, 2 * NA_KH - 1, GRID_W, period - 1)[..., GRID_W - 1:]
    toep = jnp.where(ok_c, toep, NEG_BIG)
    masked = jnp.full((n_l, N_HEADS, GRID_W, GRID_W), NEG_BIG, F32)
    out = []
    for b in (0, 1, nb - 1):
        kr0 = int(np.clip(b * NA_QROWS - NA_KH // 2, 0, n_rows - NA_KROWS))
        q_blocks = []
        for qr in range(NA_QROWS):
            r = b * NA_QROWS + qr
            rs = int(np.clip(r - NA_KH // 2, 0, n_rows - NA_KH))
            k_blocks = []
            for klr in range(NA_KROWS):
                kr = kr0 + klr
                k_blocks.append(toep[:, :, kr - r + NA_KH - 1] if rs <= kr < rs + NA_KH else masked)
            q_blocks.append(jnp.concatenate(k_blocks, axis=-1))
        out.append(jnp.concatenate(q_blocks, axis=2))
    return jnp.stack(out, axis=1)


def kernel(x, c, ctx, c_ctx, mod_w, mod_b, ln1_g, ln1_b, ln2_g, ln2_b, attn_w_in, na_rel_bias, mla_q_norm,
           mla_w_q_up, mla_kv_norm, mla_w_kv_up, attn_w_out, ffn_w_gate, ffn_w_up, ffn_w_down, pool_w,
           pool_scale, moe_router, moe_w_gate, moe_w_up, moe_w_down):
    assert x.shape[0] == 1 and c.shape[0] == 1 and ctx.shape[0] == 1
    s = x.shape[1]
    n_ctx = ctx.shape[1]
    assert n_ctx == TM and s % (NA_QROWS * GRID_W) == 0 and s % MLA_TQ == 0
    assert (s + n_ctx) % MLA_TK == 0
    n_lat_tiles = s // TM
    depth = mod_w.shape[0]

    xs = (x[0], ctx[0])
    ct = jnp.stack([c[0], c_ctx], axis=1)
    mods = _modulation(ct, mod_w, mod_b).reshape(depth, 2, 6, D_MODEL)
    cos_t, sin_t = _rope_tables(s, n_ctx)
    row = lambda v: v.reshape(1, -1)
    moe_bf16 = None
    w_in, w_q, w_kv, wa, wb = _prep_attn_weights(attn_w_in, mla_w_q_up, mla_w_kv_up, attn_w_out)
    ffn_wg, ffn_wu, ffn_wd = ffn_w_gate.astype(BF16), ffn_w_up.astype(BF16), ffn_w_down.astype(BF16)
    bias_mask = _na_bias_mask(na_rel_bias, s // GRID_W)

    for i in range(depth):
        j = i // 2
        ctx_live = any(l % 2 == 0 for l in range(i + 1, depth))
        if i % 2 == 0:
            qa, ka, va, qm, kmt, vm = _attn_proj(xs, mods, i, n_lat_tiles, j, w_in, row(mla_q_norm[j]), w_q,
                                                 row(mla_kv_norm[j]), w_kv, cos_t, sin_t)
            oa_lat = _na_attention(qa, ka, va, bias_mask, j, s)
            cast = (moe_w_gate, moe_w_up, moe_w_down) if i + 1 < depth else ()
            ob_lat, moe_bf16 = _mla_attention(qm, kmt, vm, s, cast, j)
            oa_ctx = ob_ctx = None
            if ctx_live:
                oa_ctx = _na_ctx_attention(qa, ka, va, s)
                ob_ctx = _mla_ctx_attention(qm, kmt, vm, s)
            xs = _out_ffn(xs, mods, i, n_lat_tiles, j, oa_lat, ob_lat, oa_ctx, ob_ctx, wa, wb,
                          row(ln1_g[i]), row(ln1_b[i]), ffn_wg, ffn_wu, ffn_wd, row(ln2_g[i]), row(ln2_b[i]))
        else:
            xs, *routing = _pool_router(xs, mods, i, n_lat_tiles, s, pool_w[j].astype(BF16), row(pool_scale[j]),
                                        row(ln1_g[i]), row(ln1_b[i]), moe_router[j])
            xs = _moe_layer(xs, routing, mods, i, n_lat_tiles, *moe_bf16, row(ln2_g[i]), row(ln2_b[i]))
    return xs[:s][None] if xs.shape[0] != s else xs[None]
```

```python
import functools

import numpy as np
import jax
import jax.numpy as jnp
from jax import lax
from jax.experimental import pallas as pl
from jax.experimental.pallas import tpu as pltpu

F32 = jnp.float32
BF16 = jnp.bfloat16

D_MODEL = 1024
GRID_W = 64
N_HEADS = 8
HEAD_PAD = 128
HP = N_HEADS * HEAD_PAD
NA_HEAD_DIM = 64
NA_W = N_HEADS * NA_HEAD_DIM
NA_PAIRS = N_HEADS // 2
NA_KH = 8
NA_KW = 16
MLA_NOPE = 64
MLA_ROPE = 32
MLA_V = 64
MLA_Q_LORA = 256
MLA_KV_LORA = 128
ROPE_THETA = 10000.0
POOL_WINDOWS = (2, 4, 8, 16)
POOL_GROUP = D_MODEL // len(POOL_WINDOWS)
POOL_HALO = 8
POOL_PAD = 16
FFN_DIM = 2816
N_EXPERTS = 8
EXPERT_DIM = 3584
DEPTH = 4
DEEPNORM_ALPHA = (2 * DEPTH) ** 0.25
LN_EPS = 1e-5
RMS_EPS = 1e-6
LOG2E = 1.4426950408889634
NEG_BIG = -1e30

TM = 256
NA_QROWS = 4
NA_KROWS = NA_QROWS + NA_KH - 1
NA_QB = 4
MLA_TQ = 1024
MLA_TK = 640
MLA_UNROLL = 12
MOE_TM = 512
MOE_TF = 1792
VMEM_LIMIT = 56 * 1024 * 1024


def _cparams(sem):
    return pltpu.CompilerParams(dimension_semantics=sem, vmem_limit_bytes=VMEM_LIMIT)


def _layer_norm(v, g, b):
    mu = jnp.mean(v, axis=-1, keepdims=True)
    d = v - mu
    var = jnp.mean(d * d, axis=-1, keepdims=True)
    return d * lax.rsqrt(var + LN_EPS) * g + b


def _silu(v):
    return v * jax.nn.sigmoid(v)


def _token_specs(xs, n_lat_tiles):
    if isinstance(xs, tuple):
        return list(xs), [pl.BlockSpec((TM, D_MODEL), lambda i: (jnp.minimum(i, n_lat_tiles - 1), 0)),
                          pl.BlockSpec((TM, D_MODEL), lambda i: (0, 0))]
    return [xs], [pl.BlockSpec((TM, D_MODEL), lambda i: (i, 0))]


def _load_tokens(x_refs, n_lat_tiles):
    if len(x_refs) == 2:
        return jnp.where(pl.program_id(0) >= n_lat_tiles, x_refs[1][...], x_refs[0][...])
    return x_refs[0][...]


def _token_rows(xs):
    return sum(a.shape[0] for a in xs) if isinstance(xs, tuple) else xs.shape[0]


def _mod_kernel(ct_ref, w_ref, b_ref, o_ref):
    s = _silu(ct_ref[...])
    w = w_ref[...]
    b = b_ref[...]
    r0 = jnp.sum(w * s[:, 0:1], axis=0, keepdims=True) + b
    r1 = jnp.sum(w * s[:, 1:2], axis=0, keepdims=True) + b
    o_ref[...] = jnp.concatenate([r0, r1], axis=0)


def _modulation(ct, mod_w, mod_b):
    depth, d, n6 = mod_w.shape
    tn = 1536
    return pl.pallas_call(
        _mod_kernel,
        grid=(depth, n6 // tn),
        in_specs=[
            pl.BlockSpec((d, 2), lambda l, j: (0, 0)),
            pl.BlockSpec((None, d, tn), lambda l, j: (l, 0, j)),
            pl.BlockSpec((None, 1, tn), lambda l, j: (l, 0, j)),
        ],
        out_specs=pl.BlockSpec((None, 2, tn), lambda l, j: (l, 0, j)),
        out_shape=jax.ShapeDtypeStruct((depth, 2, n6), F32),
        compiler_params=_cparams(("arbitrary", "arbitrary")),
        name="modulation",
    )(ct, mod_w, mod_b.reshape(depth, 1, n6))


def _attn_proj_kernel(*refs, n_x, n_lat_tiles):
    x_refs = refs[:n_x]
    (mod_ref, win_ref, qn_ref, wq_ref, kvn_ref, wkv_ref, cos_ref, sin_ref,
     qa_ref, ka_ref, va_ref, qm_ref, kmt_ref, vm_ref) = refs[n_x:]
    h = _load_tokens(x_refs, n_lat_tiles) * (1.0 + mod_ref[1:2, :]) + mod_ref[0:1, :]
    p = jnp.dot(h.astype(BF16), win_ref[...], preferred_element_type=F32)
    qa_ref[...] = (p[:, 0:NA_W] * (NA_HEAD_DIM ** -0.5)).astype(BF16)
    ka_ref[...] = p[:, NA_W:2 * NA_W].astype(BF16)
    va_ref[...] = p[:, 2 * NA_W:3 * NA_W].astype(BF16)
    o = 3 * NA_W
    q_c = p[:, o:o + MLA_Q_LORA]
    o += MLA_Q_LORA
    kv_c = p[:, o:o + MLA_KV_LORA]
    o += MLA_KV_LORA
    krp = p[:, o:o + HEAD_PAD]
    krs = p[:, o + HEAD_PAD:o + 2 * HEAD_PAD]
    qn = q_c * lax.rsqrt(jnp.mean(q_c * q_c, axis=-1, keepdims=True) + RMS_EPS) * qn_ref[...]
    kvn = kv_c * lax.rsqrt(jnp.mean(kv_c * kv_c, axis=-1, keepdims=True) + RMS_EPS) * kvn_ref[...]
    q2 = jnp.dot(qn.astype(BF16), wq_ref[...], preferred_element_type=F32)
    kv2 = jnp.dot(kvn.astype(BF16), wkv_ref[...], preferred_element_type=F32)
    cos = cos_ref[...]
    sin = sin_ref[...]
    kr = krp * cos + krs * sin
    q_scale = (MLA_NOPE + MLA_ROPE) ** -0.5 * LOG2E
    for hd in range(N_HEADS):
        lo = hd * HEAD_PAD
        qh = (q2[:, lo:lo + HEAD_PAD] * cos + q2[:, HP + lo:HP + lo + HEAD_PAD] * sin) * q_scale
        qm_ref[:, lo:lo + HEAD_PAD] = qh.astype(BF16)
        kh = kv2[:, lo:lo + HEAD_PAD] + kr
        kmt_ref[hd] = kh.T.astype(BF16)
    lane = lax.broadcasted_iota(jnp.int32, (1, HP), 1)
    ones_col = jnp.where(lane % HEAD_PAD == MLA_V, 1.0, 0.0)
    vm_ref[...] = (kv2[:, HP:2 * HP] + ones_col).astype(BF16)


def _attn_proj(xs, mods, layer, n_lat_tiles, w_in, q_norm, w_q, kv_norm, w_kv, cos_t, sin_t):
    m = _token_rows(xs)
    nt = m // TM
    tok = lambda i: (i, 0)
    full = lambda i: (0, 0)
    act = jax.ShapeDtypeStruct((m, HP), BF16)
    na_act = jax.ShapeDtypeStruct((m, NA_W), BF16)
    x_args, x_specs = _token_specs(xs, n_lat_tiles)
    return pl.pallas_call(
        functools.partial(_attn_proj_kernel, n_x=len(x_args), n_lat_tiles=n_lat_tiles),
        grid=(nt,),
        in_specs=x_specs + [
            pl.BlockSpec((None, None, 6, D_MODEL), lambda i: (layer, i // n_lat_tiles, 0, 0)),
            pl.BlockSpec(w_in.shape, full),
            pl.BlockSpec(q_norm.shape, full),
            pl.BlockSpec(w_q.shape, full),
            pl.BlockSpec(kv_norm.shape, full),
            pl.BlockSpec(w_kv.shape, full),
            pl.BlockSpec((TM, HEAD_PAD), tok),
            pl.BlockSpec((TM, HEAD_PAD), tok),
        ],
        out_specs=[
            pl.BlockSpec((TM, NA_W), tok),
            pl.BlockSpec((TM, NA_W), tok),
            pl.BlockSpec((TM, NA_W), tok),
            pl.BlockSpec((TM, HP), tok),
            pl.BlockSpec((N_HEADS, HEAD_PAD, TM), lambda i: (0, 0, i)),
            pl.BlockSpec((TM, HP), tok),
        ],
        out_shape=[na_act, na_act, na_act, act, jax.ShapeDtypeStruct((N_HEADS, HEAD_PAD, m), BF16), act],
        compiler_params=_cparams(("arbitrary",)),
        name="attn_proj",
    )(*x_args, mods, w_in, q_norm, w_q, kv_norm, w_kv, cos_t, sin_t)


_NT_DIMS = (((1,), (1,)), ((), ()))


def _pair_masks():
    lane = lax.broadcasted_iota(jnp.int32, (1, HEAD_PAD), 1)
    return lane < NA_HEAD_DIM, lane >= NA_HEAD_DIM


def _na_kernel(q_ref, k_ref, v_ref, kc_ref, vc_ref, *rest, n_rows):
    bm_refs, o_ref = rest[:NA_QB], rest[NA_QB]
    tq = NA_QROWS * GRID_W
    nk = NA_KROWS * GRID_W
    kc = kc_ref[...]
    vc = vc_ref[...]
    head_lanes = _pair_masks()
    for u in range(NA_QB):
        b = pl.program_id(1) * NA_QB + u
        kr0 = jnp.clip(b * NA_QROWS - NA_KH // 2, 0, n_rows - NA_KROWS)
        start = pl.multiple_of(kr0 * GRID_W, GRID_W)
        q = q_ref[u * tq:(u + 1) * tq, :]
        kw = k_ref[pl.ds(start, nk), :]
        vw = v_ref[pl.ds(start, nk), :]
        outs = []
        for a in range(2):
            qh = jnp.where(head_lanes[a], q, jnp.zeros_like(q))
            s_loc = lax.dot_general(qh, kw, _NT_DIMS, preferred_element_type=F32) + bm_refs[u][a]
            s_ctx = lax.dot_general(qh, kc, _NT_DIMS, preferred_element_type=F32)
            m = jnp.maximum(jnp.max(s_loc, axis=-1, keepdims=True), jnp.max(s_ctx, axis=-1, keepdims=True))
            p_loc = jnp.exp(s_loc - m)
            p_ctx = jnp.exp(s_ctx - m)
            l = jnp.sum(p_loc, axis=-1, keepdims=True) + jnp.sum(p_ctx, axis=-1, keepdims=True)
            o = (jnp.dot(p_loc.astype(BF16), vw, preferred_element_type=F32)
                 + jnp.dot(p_ctx.astype(BF16), vc, preferred_element_type=F32))
            outs.append(o / l)
        o_ref[u * tq:(u + 1) * tq, :] = jnp.where(head_lanes[0], outs[0], outs[1]).astype(BF16)


def _na_attention(qa, ka, va, bias_mask, s):
    n_rows = s // GRID_W
    nb = n_rows // NA_QROWS
    tq = NA_QROWS * GRID_W
    assert nb % NA_QB == 0
    ctx_blk = s // tq
    pat = lambda b: jnp.where(b == 0, 0, jnp.where(b == nb - 1, 2, 1))
    bm_spec = lambda u: pl.BlockSpec((None, 2, tq, NA_KROWS * GRID_W),
                                     lambda hp, j: (pat(j * NA_QB + u), hp, 0, 0))
    return pl.pallas_call(
        functools.partial(_na_kernel, n_rows=n_rows),
        grid=(NA_PAIRS, nb // NA_QB),
        in_specs=[
            pl.BlockSpec((NA_QB * tq, HEAD_PAD), lambda hp, j: (j, hp)),
            pl.BlockSpec((s, HEAD_PAD), lambda hp, j: (0, hp)),
            pl.BlockSpec((s, HEAD_PAD), lambda hp, j: (0, hp)),
            pl.BlockSpec((tq, HEAD_PAD), lambda hp, j: (ctx_blk, hp)),
            pl.BlockSpec((tq, HEAD_PAD), lambda hp, j: (ctx_blk, hp)),
        ] + [bm_spec(u) for u in range(NA_QB)],
        out_specs=pl.BlockSpec((NA_QB * tq, HEAD_PAD), lambda hp, j: (j, hp)),
        out_shape=jax.ShapeDtypeStruct((s, NA_W), BF16),
        compiler_params=_cparams(("arbitrary", "arbitrary")),
        name="na_attention",
    )(qa, ka, va, ka, va, *([bias_mask] * NA_QB))


def _na_ctx_kernel(q_ref, k_ref, v_ref, o_ref):
    q = q_ref[...]
    k = k_ref[...]
    v = v_ref[...]
    head_lanes = _pair_masks()
    outs = []
    for a in range(2):
        qh = jnp.where(head_lanes[a], q, jnp.zeros_like(q))
        s = lax.dot_general(qh, k, _NT_DIMS, preferred_element_type=F32)
        m = jnp.max(s, axis=-1, keepdims=True)
        p = jnp.exp(s - m)
        l = jnp.sum(p, axis=-1, keepdims=True)
        outs.append(jnp.dot(p.astype(BF16), v, preferred_element_type=F32) / l)
    o_ref[...] = jnp.where(head_lanes[0], outs[0], outs[1]).astype(BF16)


def _na_ctx_attention(qa, ka, va, s):
    blk = s // TM
    spec = pl.BlockSpec((TM, HEAD_PAD), lambda hp: (blk, hp))
    return pl.pallas_call(
        _na_ctx_kernel,
        grid=(NA_PAIRS,),
        in_specs=[spec, spec, spec],
        out_specs=pl.BlockSpec((TM, HEAD_PAD), lambda hp: (0, hp)),
        out_shape=jax.ShapeDtypeStruct((TM, NA_W), BF16),
        compiler_params=_cparams(("arbitrary",)),
        name="na_ctx_attention",
    )(qa, ka, va)


def _mla_ctx_kernel(q_ref, kt_ref, v_ref, o_ref):
    s = jnp.dot(q_ref[...], kt_ref[...], preferred_element_type=F32)
    m = jnp.max(s, axis=-1, keepdims=True)
    p = jnp.exp2(s - m)
    l = jnp.sum(p, axis=-1, keepdims=True)
    o = jnp.dot(p.astype(BF16), v_ref[...], preferred_element_type=F32)
    o_ref[...] = (o / l).astype(BF16)


def _mla_ctx_attention(qm, kmt, vm, s):
    blk = s // TM
    return pl.pallas_call(
        _mla_ctx_kernel,
        grid=(N_HEADS,),
        in_specs=[
            pl.BlockSpec((TM, HEAD_PAD), lambda h: (blk, h)),
            pl.BlockSpec((None, HEAD_PAD, TM), lambda h: (h, 0, blk)),
            pl.BlockSpec((TM, HEAD_PAD), lambda h: (blk, h)),
        ],
        out_specs=pl.BlockSpec((TM, HEAD_PAD), lambda h: (0, h)),
        out_shape=jax.ShapeDtypeStruct((TM, HP), BF16),
        compiler_params=_cparams(("arbitrary",)),
        name="mla_ctx_attention",
    )(qm, kmt, vm)


def _mla_kernel(q_ref, qn_ref, kt_ref, v_ref, *rest, n_chunks, n_cast):
    cast_in, o_ref, cast_out = rest[:n_cast], rest[n_cast], rest[n_cast + 1:2 * n_cast + 1]
    s0_ref, s1_ref = rest[2 * n_cast + 1:]
    for w_ref, wb_ref in zip(cast_in, cast_out):
        wb_ref[...] = w_ref[...].astype(BF16)
    q = q_ref[...]
    tq = q.shape[0]

    def scores(c, s_ref, q_tile=q):
        off = pl.multiple_of(c * MLA_TK, MLA_TK)
        s_ref[...] = jnp.dot(q_tile, kt_ref[:, pl.ds(off, MLA_TK)], preferred_element_type=F32)

    def update(c, s_ref, carry):
        m, acc = carry
        off = pl.multiple_of(c * MLA_TK, MLA_TK)
        s = s_ref[...]
        m_new = jnp.maximum(m, jnp.max(s, axis=-1, keepdims=True))
        alpha = jnp.exp2(m - m_new)
        p = jnp.exp2(s - m_new).astype(BF16)
        acc = alpha * acc + jnp.dot(p, v_ref[pl.ds(off, MLA_TK), :], preferred_element_type=F32)
        return m_new, acc

    bufs = (s0_ref, s1_ref)

    def group(g, carry):
        c0 = g * MLA_UNROLL
        for u in range(MLA_UNROLL):
            scores(c0 + u + 1, bufs[(u + 1) % 2])
            carry = update(c0 + u, bufs[u % 2], carry)
        return carry

    carry = (jnp.full((tq, 1), NEG_BIG, F32), jnp.zeros((tq, HEAD_PAD), F32))

    @pl.when(pl.program_id(1) == 0)
    def _():
        scores(0, s0_ref)

    n_groups = (n_chunks - 2) // MLA_UNROLL
    carry = lax.fori_loop(0, n_groups, group, carry)
    for c in range(n_groups * MLA_UNROLL, n_chunks):
        if c + 1 < n_chunks:
            scores(c + 1, bufs[(c + 1) % 2])
        else:
            scores(0, s0_ref, qn_ref[...])
        carry = update(c, bufs[c % 2], carry)
    _, acc = carry
    o_ref[...] = (acc / acc[:, MLA_V:MLA_V + 1]).astype(BF16)


def _mla_attention(qm, kmt, vm, s, cast_weights=(), cast_layer=0):
    m = qm.shape[0]
    n_chunks = m // MLA_TK
    assert n_chunks % 2 == 0 and n_chunks >= 2
    n_q = s // MLA_TQ
    assert N_HEADS == N_EXPERTS or not cast_weights
    cast_in_specs, cast_out_specs, cast_shapes = [], [], []
    for w in cast_weights:
        rows, cols = w.shape[2], w.shape[3]
        slab = next(sl for sl in range(16, rows + 1, 16) if rows % sl == 0 and rows // sl <= n_q)
        blk = lambda i, n_blk=rows // slab: jnp.minimum(i, n_blk - 1)
        cast_in_specs.append(pl.BlockSpec((None, None, slab, cols), lambda h, i, blk=blk: (cast_layer, h, blk(i), 0)))
        cast_out_specs.append(pl.BlockSpec((None, slab, cols), lambda h, i, blk=blk: (h, blk(i), 0)))
        cast_shapes.append(jax.ShapeDtypeStruct((N_EXPERTS, rows, cols), BF16))
    outs = pl.pallas_call(
        functools.partial(_mla_kernel, n_chunks=n_chunks, n_cast=len(cast_weights)),
        grid=(N_HEADS, n_q),
        in_specs=[
            pl.BlockSpec((MLA_TQ, HEAD_PAD), lambda h, i: (i, h)),
            pl.BlockSpec((MLA_TQ, HEAD_PAD), lambda h, i: (jnp.minimum(i + 1, n_q - 1), h)),
            pl.BlockSpec((None, HEAD_PAD, m), lambda h, i: (h, 0, 0)),
            pl.BlockSpec((m, HEAD_PAD), lambda h, i: (0, h)),
        ] + cast_in_specs,
        out_specs=[pl.BlockSpec((MLA_TQ, HEAD_PAD), lambda h, i: (i, h))] + cast_out_specs,
        out_shape=[jax.ShapeDtypeStruct((s, HP), BF16)] + cast_shapes,
        scratch_shapes=[pltpu.VMEM((MLA_TQ, MLA_TK), F32), pltpu.VMEM((MLA_TQ, MLA_TK), F32)],
        compiler_params=_cparams(("arbitrary", "arbitrary")),
        name="mla_attention",
    )(qm, qm, kmt, vm, *cast_weights)
    return outs[0], tuple(outs[1:])


def _out_ffn_kernel(*refs, n_x, n_lat_tiles, with_ctx):
    x_refs = refs[:n_x]
    mod_ref, oal_ref, obl_ref = refs[n_x:n_x + 3]
    rest = refs[n_x + 3:]
    if with_ctx:
        oac_ref, obc_ref = rest[:2]
        rest = rest[2:]
        is_ctx = pl.program_id(0) >= n_lat_tiles
        oa = jnp.where(is_ctx, oac_ref[...], oal_ref[...])
        ob = jnp.where(is_ctx, obc_ref[...], obl_ref[...])
    else:
        oa = oal_ref[...]
        ob = obl_ref[...]
    wa_ref, wb_ref, g1_ref, b1_ref, wg_ref, wu_ref, wd_ref, g2_ref, b2_ref, o_ref = rest
    y = (jnp.dot(oa, wa_ref[...], preferred_element_type=F32)
         + jnp.dot(ob, wb_ref[...], preferred_element_type=F32))
    v = DEEPNORM_ALPHA * _load_tokens(x_refs, n_lat_tiles) + mod_ref[2:3, :] * y
    x1 = _layer_norm(v, g1_ref[...], b1_ref[...])
    h = (x1 * (1.0 + mod_ref[4:5, :]) + mod_ref[3:4, :]).astype(BF16)
    gate = jnp.dot(h, wg_ref[...], preferred_element_type=F32)
    up = jnp.dot(h, wu_ref[...], preferred_element_type=F32)
    a = (_silu(gate) * up).astype(BF16)
    y2 = jnp.dot(a, wd_ref[...], preferred_element_type=F32)
    v2 = DEEPNORM_ALPHA * x1 + mod_ref[5:6, :] * y2
    o_ref[...] = _layer_norm(v2, g2_ref[...], b2_ref[...])


def _out_ffn(xs, mods, layer, n_lat_tiles, oa_lat, ob_lat, oa_ctx, ob_ctx, wa, wb, ln1_g, ln1_b,
             wg, wu, wd, ln2_g, ln2_b):
    with_ctx = oa_ctx is not None
    m = _token_rows(xs) if with_ctx else n_lat_tiles * TM
    tok = lambda i: (i, 0)
    lat = lambda i: (jnp.minimum(i, n_lat_tiles - 1), 0)
    full = lambda i: (0, 0)
    ctx_args = [oa_ctx, ob_ctx] if with_ctx else []
    x_args, x_specs = _token_specs(xs, n_lat_tiles)
    resident = lambda w: pl.BlockSpec(w.shape, full, pipeline_mode=pl.Buffered(1))
    vec = pl.BlockSpec((1, D_MODEL), full)
    return pl.pallas_call(
        functools.partial(_out_ffn_kernel, n_x=len(x_args), n_lat_tiles=n_lat_tiles, with_ctx=with_ctx),
        grid=(m // TM,),
        in_specs=x_specs + [
            pl.BlockSpec((None, None, 6, D_MODEL), lambda i: (layer, i // n_lat_tiles, 0, 0)),
            pl.BlockSpec((TM, NA_W), lat),
            pl.BlockSpec((TM, HP), lat),
        ] + ([pl.BlockSpec((TM, NA_W), full), pl.BlockSpec((TM, HP), full)] if with_ctx else []) + [
            resident(wa), resident(wb), vec, vec, resident(wg), resident(wu), resident(wd), vec, vec,
        ],
        out_specs=pl.BlockSpec((TM, D_MODEL), tok),
        out_shape=jax.ShapeDtypeStruct((m, D_MODEL), F32),
        compiler_params=_cparams(("arbitrary",)),
        name="out_proj_ffn_ln",
    )(*x_args, mods, oa_lat, ob_lat, *ctx_args, wa, wb, ln1_g, ln1_b, wg, wu, wd, ln2_g, ln2_b)


def _pool_router_kernel(x_ref, xp_ref, xn_ref, mod_ref, w_ref, sc_ref, g_ref, b_ref, rw_ref, tri_ref,
                        o_ref, idx_ref, wts_ref, cnt_ref, ext_ref, lvl_ref, base_ref, *, n_lat_tiles, s):
    i = pl.program_id(0)

    @pl.when(i == 0)
    def _():
        base_ref[...] = jnp.zeros_like(base_ref)

    is_ctx = i >= n_lat_tiles
    j = jnp.where(is_ctx, i - n_lat_tiles, i)
    last = jnp.where(is_ctx, 0, n_lat_tiles - 1)
    n_seq = jnp.where(is_ctx, TM, s)
    scale1 = 1.0 + mod_ref[1:2, :]
    shift1 = mod_ref[0:1, :]
    x = x_ref[...]
    h = x * scale1 + shift1
    hp = jnp.where(j != 0, xp_ref[...] * scale1 + shift1, 0.0)
    hn = jnp.where(j != last, xn_ref[...] * scale1 + shift1, 0.0)
    n_ext = TM + 2 * POOL_HALO
    ext_ref[0:POOL_PAD, :] = jnp.zeros((POOL_PAD, D_MODEL), F32)
    ext_ref[POOL_PAD:POOL_PAD + POOL_HALO, :] = hp
    ext_ref[POOL_PAD + POOL_HALO:POOL_PAD + POOL_HALO + TM, :] = h
    ext_ref[POOL_PAD + POOL_HALO + TM:POOL_PAD + n_ext, :] = hn
    t = j * TM + lax.broadcasted_iota(jnp.int32, (TM, 1), 0)
    ys = []
    for g, win in enumerate(POOL_WINDOWS):
        half = win // 2
        lo = g * POOL_GROUP
        run = lambda r0, r1: ext_ref[r0:r1, lo:lo + POOL_GROUP]
        n, k = 1, 0
        while n < win:
            nxt = lvl_ref.at[g, k % 2]
            nxt[0:POOL_PAD, :] = jnp.zeros((POOL_PAD, POOL_GROUP), F32)
            nxt[POOL_PAD:POOL_PAD + n_ext, :] = (run(POOL_PAD, POOL_PAD + n_ext)
                                                 + run(POOL_PAD - n, POOL_PAD - n + n_ext))
            run = lambda r0, r1, ref=nxt: ref[r0:r1, :]
            n, k = 2 * n, k + 1
        first = POOL_PAD + POOL_HALO + half - 1
        acc = run(first, first + TM)
        cnt = (jnp.minimum(t + half, n_seq) - jnp.maximum(t - half, 0)).astype(F32)
        mixed = acc / cnt - h[:, lo:lo + POOL_GROUP]
        yg = jnp.dot(mixed.astype(BF16), w_ref[g], preferred_element_type=F32)
        ys.append(yg)
    y = jnp.concatenate(ys, axis=-1) * sc_ref[...]
    v = DEEPNORM_ALPHA * x + mod_ref[2:3, :] * y
    x1 = _layer_norm(v, g_ref[...], b_ref[...])
    o_ref[...] = x1
    _route(x1, mod_ref, rw_ref, tri_ref, idx_ref, wts_ref, cnt_ref, base_ref)


def _pool_router(xs, mods, layer, n_lat_tiles, s, pool_w, pool_scale, ln_g, ln_b, router_w):
    m = xs.shape[0]
    nt = m // TM
    rw = jnp.pad(router_w, [(0, 0), (0, HEAD_PAD - N_EXPERTS)])
    rw_hi = rw.astype(BF16)
    rw = jnp.stack([rw_hi, (rw - rw_hi.astype(F32)).astype(BF16)])
    tri = jnp.asarray(np.tril(np.ones((TM, TM), np.float32), -1), dtype=BF16)
    per = TM // POOL_HALO
    n_halo_blocks = m // POOL_HALO
    tok = lambda i: (i, 0)
    full = lambda i: (0, 0)
    return pl.pallas_call(
        functools.partial(_pool_router_kernel, n_lat_tiles=n_lat_tiles, s=s),
        grid=(nt,),
        in_specs=[
            pl.BlockSpec((TM, D_MODEL), tok),
            pl.BlockSpec((POOL_HALO, D_MODEL), lambda i: (jnp.maximum(i * per - 1, 0), 0)),
            pl.BlockSpec((POOL_HALO, D_MODEL), lambda i: (jnp.minimum((i + 1) * per, n_halo_blocks - 1), 0)),
            pl.BlockSpec((None, None, 6, D_MODEL), lambda i: (layer, i // n_lat_tiles, 0, 0)),
            pl.BlockSpec(pool_w.shape, lambda i: (0, 0, 0)),
            pl.BlockSpec((1, D_MODEL), full),
            pl.BlockSpec((1, D_MODEL), full),
            pl.BlockSpec((1, D_MODEL), full),
            pl.BlockSpec(rw.shape, lambda i: (0, 0, 0)),
            pl.BlockSpec((TM, TM), full),
        ],
        out_specs=[
            pl.BlockSpec((TM, D_MODEL), tok),
            pl.BlockSpec((None, 8, TM), lambda i: (i, 0, 0)),
            pl.BlockSpec((TM, HEAD_PAD), tok),
            pl.BlockSpec((8, HEAD_PAD), full),
        ],
        out_shape=[
            jax.ShapeDtypeStruct((m, D_MODEL), F32),
            jax.ShapeDtypeStruct((nt, 8, TM), jnp.int32),
            jax.ShapeDtypeStruct((m, HEAD_PAD), F32),
            jax.ShapeDtypeStruct((8, HEAD_PAD), F32),
        ],
        scratch_shapes=[pltpu.VMEM((POOL_PAD + TM + 2 * POOL_HALO, D_MODEL), F32),
                        pltpu.VMEM((len(POOL_WINDOWS), 2, POOL_PAD + TM + 2 * POOL_HALO, POOL_GROUP), F32),
                        pltpu.VMEM((8, HEAD_PAD), F32)],
        compiler_params=_cparams(("arbitrary",)),
        name="pool_ln_router",
    )(xs, xs, xs, mods, pool_w, pool_scale, ln_g, ln_b, rw, tri)


def _route(x1, mod_ref, rw_ref, tri_ref, idx_ref, wts_ref, cnt_ref, base_ref):
    h = x1 * (1.0 + mod_ref[4:5, :]) + mod_ref[3:4, :]
    h_hi = h.astype(BF16)
    h_lo = (h - h_hi.astype(F32)).astype(BF16)
    logits = (jnp.dot(h_hi, rw_ref[0], preferred_element_type=F32)
              + jnp.dot(h_hi, rw_ref[1], preferred_element_type=F32)
              + jnp.dot(h_lo, rw_ref[0], preferred_element_type=F32))
    lane = lax.broadcasted_iota(jnp.int32, logits.shape, 1)
    logits = jnp.where(lane < N_EXPERTS, logits, -jnp.inf)
    m1 = jnp.max(logits, axis=-1, keepdims=True)
    i1 = jnp.min(jnp.where(logits == m1, lane, HEAD_PAD), axis=-1, keepdims=True)
    rest = jnp.where(lane == i1, -jnp.inf, logits)
    m2 = jnp.max(rest, axis=-1, keepdims=True)
    i2 = jnp.min(jnp.where(rest == m2, lane, HEAD_PAD), axis=-1, keepdims=True)
    e2 = jnp.exp(m2 - m1)
    w1 = 1.0 / (1.0 + e2)
    w2 = e2 / (1.0 + e2)
    wts_ref[...] = jnp.where(lane == 0, w1, jnp.where(lane == 1, w2, 0.0))

    chosen = jnp.where(lane == i1, 1.0, jnp.where(lane == i2, 1.0, 0.0))
    before = jnp.dot(tri_ref[...], chosen.astype(BF16), preferred_element_type=F32)
    rank = before + base_ref[0:1, :]
    r1 = jnp.sum(jnp.where(lane == i1, rank, 0.0), axis=-1, keepdims=True)
    r2 = jnp.sum(jnp.where(lane == i2, rank, 0.0), axis=-1, keepdims=True)
    base_ref[0:1, :] = base_ref[0:1, :] + jnp.sum(chosen, axis=0, keepdims=True)
    cnt_ref[...] = base_ref[...]
    packed = jnp.where(lane == 0, i1.astype(F32),
                       jnp.where(lane == 1, i2.astype(F32),
                                 jnp.where(lane == 2, r1, jnp.where(lane == 3, r2, 0.0))))
    idx_ref[...] = packed.T[0:8, :].astype(jnp.int32)


def _dispatch_kernel(fill_ref, nu_ref, rows_ref, x_ref, mod_ref, xs_ref, hbuf, zbuf, sems, zsem):
    i = pl.program_id(0)
    n = pl.num_programs(0)
    slot = i % 2
    n_tiles = xs_ref.shape[0] // MOE_TM

    @pl.when(i == 0)
    def _():
        zbuf[...] = jnp.zeros_like(zbuf)
        fills = [pltpu.make_async_copy(zbuf, xs_ref.at[pl.ds(pl.multiple_of(fill_ref[e], 8), MOE_TM)],
                                       zsem.at[0]) for e in range(N_EXPERTS)]
        for cp in fills:
            cp.start()
        for cp in fills:
            cp.wait()

        def zero_tail(t, carry):
            row0 = pl.multiple_of(t * MOE_TM, MOE_TM)
            cp = pltpu.make_async_copy(zbuf, xs_ref.at[pl.ds(row0, MOE_TM)], zsem.at[0])
            cp.start()
            cp.wait()
            return carry

        lax.fori_loop(nu_ref[0], n_tiles, zero_tail, 0)

    def wait_slot(sl):
        for _ in range(2):
            pltpu.make_async_copy(hbuf.at[sl], xs_ref.at[pl.ds(0, TM)], sems.at[sl]).wait()

    @pl.when(i >= 2)
    def _():
        wait_slot(slot)

    hbuf[slot] = x_ref[...] * (1.0 + mod_ref[4:5, :]) + mod_ref[3:4, :]

    def issue(r, carry):
        for k in range(2):
            dst = rows_ref[0, k * TM + r]
            pltpu.make_async_copy(hbuf.at[slot, pl.ds(r, 1)], xs_ref.at[pl.ds(dst, 1)], sems.at[slot]).start()
        return carry

    lax.fori_loop(0, TM, issue, 0, unroll=8)

    @pl.when(i == n - 1)
    def _():
        wait_slot(slot)

        @pl.when(n >= 2)
        def _():
            wait_slot(1 - slot)


def _dispatch(fill, n_used, rows, xs, mods, layer, n_lat_tiles, n_tiles):
    m = xs.shape[0]
    grid_spec = pltpu.PrefetchScalarGridSpec(
        num_scalar_prefetch=2,
        grid=(m // TM,),
        in_specs=[
            pl.BlockSpec((None, 1, 2 * TM), lambda i, fill, nu: (i, 0, 0), memory_space=pltpu.SMEM),
            pl.BlockSpec((TM, D_MODEL), lambda i, fill, nu: (i, 0)),
            pl.BlockSpec((None, None, 6, D_MODEL), lambda i, fill, nu: (layer, i // n_lat_tiles, 0, 0)),
        ],
        out_specs=pl.BlockSpec(memory_space=pl.ANY),
        scratch_shapes=[pltpu.VMEM((2, TM, D_MODEL), F32), pltpu.VMEM((MOE_TM, D_MODEL), F32),
                        pltpu.SemaphoreType.DMA((2,)), pltpu.SemaphoreType.DMA((1,))],
    )
    return pl.pallas_call(
        _dispatch_kernel,
        grid_spec=grid_spec,
        out_shape=jax.ShapeDtypeStruct(((n_tiles + 1) * MOE_TM, D_MODEL), F32),
        compiler_params=_cparams(("arbitrary",)),
        name="moe_dispatch",
    )(fill, n_used, rows, xs, mods)


def _experts_kernel(te_ref, nu_ref, x_ref, wg_ref, wu_ref, wd_ref, o_ref, xb_ref, a_ref):
    del te_ref
    f = pl.program_id(1)
    live = pl.program_id(0) < nu_ref[0]

    @pl.when(jnp.logical_not(live) & (f == 0))
    def _():
        o_ref[...] = jnp.zeros_like(o_ref)

    def hidden(xb):
        gate = jnp.dot(xb, wg_ref[...], preferred_element_type=F32)
        up = jnp.dot(xb, wu_ref[...], preferred_element_type=F32)
        return (_silu(gate) * up).astype(BF16)

    @pl.when(live & (f == 0))
    def _():
        xb = x_ref[...].astype(BF16)
        xb_ref[...] = xb
        a_ref[:, 0:MOE_TF] = hidden(xb)

    @pl.when(live & (f == 1))
    def _():
        a_ref[:, MOE_TF:2 * MOE_TF] = hidden(xb_ref[...])
        o_ref[...] = jnp.dot(a_ref[...], wd_ref[...], preferred_element_type=F32)


def _experts(tile_expert, n_used, xs_sorted, wg, wu, wd, n_tiles):
    nf = EXPERT_DIM // MOE_TF
    row_blk = lambda i, f, te, nu: (jnp.minimum(i, nu[0] - 1), 0)
    f_blk = lambda i, f, nu: jnp.where(i < nu[0], f, nf - 1)
    grid_spec = pltpu.PrefetchScalarGridSpec(
        num_scalar_prefetch=2,
        grid=(n_tiles, nf),
        in_specs=[
            pl.BlockSpec((MOE_TM, D_MODEL), row_blk),
            pl.BlockSpec((None, D_MODEL, MOE_TF), lambda i, f, te, nu: (te[i], 0, f_blk(i, f, nu))),
            pl.BlockSpec((None, D_MODEL, MOE_TF), lambda i, f, te, nu: (te[i], 0, f_blk(i, f, nu))),
            pl.BlockSpec((None, EXPERT_DIM, D_MODEL), lambda i, f, te, nu: (te[i], 0, 0)),
        ],
        out_specs=pl.BlockSpec((MOE_TM, D_MODEL), lambda i, f, te, nu: (i, 0)),
        scratch_shapes=[pltpu.VMEM((MOE_TM, D_MODEL), BF16), pltpu.VMEM((MOE_TM, EXPERT_DIM), BF16)],
    )
    assert nf == 2
    return pl.pallas_call(
        _experts_kernel,
        grid_spec=grid_spec,
        out_shape=jax.ShapeDtypeStruct((n_tiles * MOE_TM, D_MODEL), F32),
        compiler_params=_cparams(("arbitrary", "arbitrary")),
        name="moe_experts",
    )(tile_expert, n_used, xs_sorted, wg, wu, wd)


def _combine_kernel(rows_ref, rowsn_ref, wts_ref, x_ref, mod_ref, g_ref, b_ref, ys_ref, o_ref, ybuf, sems):
    i = pl.program_id(0)
    n = pl.num_programs(0)
    slot = i % 2

    def fetch(ids_ref, sl):
        def issue(r, carry):
            for k in range(2):
                src = ids_ref[0, k * TM + r]
                pltpu.make_async_copy(ys_ref.at[pl.ds(src, 1)], ybuf.at[sl, k, pl.ds(r, 1)], sems.at[sl]).start()
            return carry

        lax.fori_loop(0, TM, issue, 0, unroll=8)

    @pl.when(i == 0)
    def _():
        fetch(rows_ref, 0)

    @pl.when(i + 1 < n)
    def _():
        fetch(rowsn_ref, 1 - slot)

    for k in range(2):
        pltpu.make_async_copy(ys_ref.at[pl.ds(0, TM)], ybuf.at[slot, k], sems.at[slot]).wait()

    wts = wts_ref[...]
    y = wts[:, 0:1] * ybuf[slot, 0] + wts[:, 1:2] * ybuf[slot, 1]
    v = DEEPNORM_ALPHA * x_ref[...] + mod_ref[5:6, :] * y
    o_ref[...] = _layer_norm(v, g_ref[...], b_ref[...])


def _combine(rows, wts, xs, mods, layer, n_lat_tiles, ys_sorted, ln_g, ln_b):
    m = xs.shape[0]
    nt = m // TM
    tok = lambda i: (i, 0)
    full = lambda i: (0, 0)
    return pl.pallas_call(
        _combine_kernel,
        grid=(nt,),
        in_specs=[
            pl.BlockSpec((None, 1, 2 * TM), lambda i: (i, 0, 0), memory_space=pltpu.SMEM),
            pl.BlockSpec((None, 1, 2 * TM), lambda i: (jnp.minimum(i + 1, nt - 1), 0, 0),
                         memory_space=pltpu.SMEM),
            pl.BlockSpec((TM, HEAD_PAD), tok),
            pl.BlockSpec((TM, D_MODEL), tok),
            pl.BlockSpec((None, None, 6, D_MODEL), lambda i: (layer, i // n_lat_tiles, 0, 0)),
            pl.BlockSpec((1, D_MODEL), full),
            pl.BlockSpec((1, D_MODEL), full),
            pl.BlockSpec(memory_space=pl.ANY),
        ],
        out_specs=pl.BlockSpec((TM, D_MODEL), tok),
        out_shape=jax.ShapeDtypeStruct((m, D_MODEL), F32),
        scratch_shapes=[pltpu.VMEM((2, 2, TM, D_MODEL), F32), pltpu.SemaphoreType.DMA((2,))],
        compiler_params=_cparams(("arbitrary",)),
        name="moe_combine_ln",
    )(rows, rows, wts, xs, mods, ln_g, ln_b, ys_sorted)


def _moe_layer(xs, routing, mods, layer, n_lat_tiles, wg, wu, wd, ln_g, ln_b):
    m = xs.shape[0]
    n_tiles = (2 * m + N_EXPERTS * (MOE_TM - 1) + MOE_TM - 1) // MOE_TM
    idx, wts, cnt = routing
    counts = cnt[0, :N_EXPERTS].astype(jnp.int32)
    padded = (counts + MOE_TM - 1) // MOE_TM * MOE_TM
    ends = jnp.cumsum(padded)
    offs = ends - padded
    n_used = (ends[-1:] // MOE_TM).astype(jnp.int32)
    tile_row = jnp.arange(n_tiles, dtype=jnp.int32) * MOE_TM
    tile_row = jnp.minimum(tile_row, ends[-1] - MOE_TM)
    tile_expert = jnp.sum((tile_row[:, None] >= ends[None, :]).astype(jnp.int32), axis=1)
    expert_ids = idx[:, 0:2, :]
    group_off = jnp.sum(jnp.where(expert_ids[..., None] == jnp.arange(N_EXPERTS), offs, 0), axis=-1)
    rows = (group_off + idx[:, 2:4, :]).reshape(m // TM, 1, 2 * TM)
    fill = (offs + counts) // 8 * 8
    xs_sorted = _dispatch(fill, n_used, rows, xs, mods, layer, n_lat_tiles, n_tiles)
    ys_sorted = _experts(tile_expert, n_used, xs_sorted, wg, wu, wd, n_tiles)
    return _combine(rows, wts, xs, mods, layer, n_lat_tiles, ys_sorted, ln_g, ln_b)


_ROPE_SWAP = np.array(list(range(8, 16)) + list(range(0, 8)) + list(range(24, 32)) + list(range(16, 24)))


def _prep_attn_weights(w_in, w_q_up, w_kv_up, w_out):
    o = 3 * NA_W
    w_qc = w_in[:, o:o + MLA_Q_LORA]
    o += MLA_Q_LORA
    w_kvc = w_in[:, o:o + MLA_KV_LORA]
    o += MLA_KV_LORA
    w_kr = w_in[:, o:o + MLA_ROPE]
    rope_pad = [(0, 0), (MLA_NOPE, HEAD_PAD - MLA_NOPE - MLA_ROPE)]
    w_in_aug = jnp.concatenate(
        [w_in[:, :3 * NA_W], w_qc, w_kvc, jnp.pad(w_kr, rope_pad), jnp.pad(w_kr[:, _ROPE_SWAP], rope_pad)],
        axis=1).astype(BF16)

    wq = w_q_up.reshape(MLA_Q_LORA, N_HEADS, MLA_NOPE + MLA_ROPE)
    wq_full = jnp.pad(wq, [(0, 0), (0, 0), (0, HEAD_PAD - MLA_NOPE - MLA_ROPE)])
    wq_swap = jnp.pad(wq[:, :, MLA_NOPE:][:, :, _ROPE_SWAP], [(0, 0), (0, 0), rope_pad[1]])
    w_q = jnp.concatenate([wq_full.reshape(MLA_Q_LORA, HP), wq_swap.reshape(MLA_Q_LORA, HP)], axis=1).astype(BF16)

    wkv = w_kv_up.reshape(MLA_KV_LORA, N_HEADS, MLA_NOPE + MLA_V)
    wkn = jnp.pad(wkv[:, :, :MLA_NOPE], [(0, 0), (0, 0), (0, HEAD_PAD - MLA_NOPE)])
    wv = jnp.pad(wkv[:, :, MLA_NOPE:], [(0, 0), (0, 0), (0, HEAD_PAD - MLA_V)])
    w_kv = jnp.concatenate([wkn.reshape(MLA_KV_LORA, HP), wv.reshape(MLA_KV_LORA, HP)], axis=1).astype(BF16)

    wa = w_out[:NA_W].astype(BF16)
    wb = w_out[NA_W:].reshape(N_HEADS, MLA_V, D_MODEL)
    wb = jnp.pad(wb, [(0, 0), (0, HEAD_PAD - MLA_V), (0, 0)]).reshape(HP, D_MODEL).astype(BF16)
    return w_in_aug, w_q, w_kv, wa, wb


def _rope_tables(s, n_ctx):
    t = jnp.arange(s, dtype=jnp.int32)
    row = (t // GRID_W).astype(F32)
    col = (t % GRID_W).astype(F32)
    n_freq = MLA_ROPE // 4
    inv = 1.0 / (ROPE_THETA ** (jnp.arange(n_freq, dtype=F32) / n_freq))
    ar = row[:, None] * inv
    ac = col[:, None] * inv
    cos = jnp.concatenate([jnp.cos(ar), jnp.cos(ar), jnp.cos(ac), jnp.cos(ac)], axis=1)
    sin = jnp.concatenate([-jnp.sin(ar), jnp.sin(ar), -jnp.sin(ac), jnp.sin(ac)], axis=1)
    right = HEAD_PAD - MLA_NOPE - MLA_ROPE
    cos = jnp.pad(cos, [(0, n_ctx), (MLA_NOPE, right)], constant_values=1.0)
    sin = jnp.pad(sin, [(0, n_ctx), (MLA_NOPE, right)])
    return cos, sin


def _na_bias_mask(rel_bias, n_rows):
    nb = n_rows // NA_QROWS
    c = np.arange(GRID_W)
    cs = np.clip(c - NA_KW // 2, 0, GRID_W - NA_KW)
    kc = np.arange(GRID_W)
    ok_c = (kc[None, :] >= cs[:, None]) & (kc[None, :] < cs[:, None] + NA_KW)
    n_dcol = 2 * NA_KW - 1
    left = GRID_W - NA_KW
    period = 2 * GRID_W
    u = jnp.pad(rel_bias, [(0, 0), (0, 0), (left, period - n_dcol - left)])
    flat = jnp.tile(u, (1, 1, GRID_W + 1))[:, :, :GRID_W * (period - 1)]
    toep = flat.reshape(N_HEADS, 2 * NA_KH - 1, GRID_W, period - 1)[:, :, :, GRID_W - 1:]
    toep = jnp.where(ok_c, toep, NEG_BIG)
    masked = jnp.full((N_HEADS, GRID_W, GRID_W), NEG_BIG, F32)
    out = []
    for b in (0, 1, nb - 1):
        kr0 = int(np.clip(b * NA_QROWS - NA_KH // 2, 0, n_rows - NA_KROWS))
        q_blocks = []
        for qr in range(NA_QROWS):
            r = b * NA_QROWS + qr
            rs = int(np.clip(r - NA_KH // 2, 0, n_rows - NA_KH))
            k_blocks = []
            for klr in range(NA_KROWS):
                kr = kr0 + klr
                k_blocks.append(toep[:, kr - r + NA_KH - 1] if rs <= kr < rs + NA_KH else masked)
            q_blocks.append(jnp.concatenate(k_blocks, axis=-1))
        out.append(jnp.concatenate(q_blocks, axis=1))
    return jnp.stack(out, axis=0)


def kernel(x, c, ctx, c_ctx, mod_w, mod_b, ln1_g, ln1_b, ln2_g, ln2_b, attn_w_in, na_rel_bias, mla_q_norm,
           mla_w_q_up, mla_kv_norm, mla_w_kv_up, attn_w_out, ffn_w_gate, ffn_w_up, ffn_w_down, pool_w,
           pool_scale, moe_router, moe_w_gate, moe_w_up, moe_w_down):
    assert x.shape[0] == 1 and c.shape[0] == 1 and ctx.shape[0] == 1
    s = x.shape[1]
    n_ctx = ctx.shape[1]
    assert n_ctx == TM and s % (NA_QROWS * GRID_W) == 0 and s % MLA_TQ == 0
    assert (s + n_ctx) % MLA_TK == 0
    n_lat_tiles = s // TM
    depth = mod_w.shape[0]

    xs = (x[0], ctx[0])
    ct = jnp.stack([c[0], c_ctx], axis=1)
    mods = _modulation(ct, mod_w, mod_b).reshape(depth, 2, 6, D_MODEL)
    cos_t, sin_t = _rope_tables(s, n_ctx)
    row = lambda v: v.reshape(1, -1)
    moe_bf16 = None

    for i in range(depth):
        j = i // 2
        ctx_live = any(l % 2 == 0 for l in range(i + 1, depth))
        if i % 2 == 0:
            w_in, w_q, w_kv, wa, wb = _prep_attn_weights(attn_w_in[j], mla_w_q_up[j], mla_w_kv_up[j],
                                                         attn_w_out[j])
            qa, ka, va, qm, kmt, vm = _attn_proj(xs, mods, i, n_lat_tiles, w_in, row(mla_q_norm[j]), w_q,
                                                 row(mla_kv_norm[j]), w_kv, cos_t, sin_t)
            bias_mask = _na_bias_mask(na_rel_bias[j], s // GRID_W)
            oa_lat = _na_attention(qa, ka, va, bias_mask, s)
            grouped = lambda w: w.reshape(w.shape[0], N_EXPERTS, w.shape[1] // N_EXPERTS, w.shape[2])
            cast = (grouped(ffn_w_gate), grouped(ffn_w_up), grouped(ffn_w_down))
            if i + 1 < depth:
                cast += (moe_w_gate, moe_w_up, moe_w_down)
            ob_lat, cast_bf16 = _mla_attention(qm, kmt, vm, s, cast, j)
            ffn_wg, ffn_wu, ffn_wd = (w.reshape(-1, w.shape[2]) for w in cast_bf16[:3])
            moe_bf16 = cast_bf16[3:]
            oa_ctx = ob_ctx = None
            if ctx_live:
                oa_ctx = _na_ctx_attention(qa, ka, va, s)
                ob_ctx = _mla_ctx_attention(qm, kmt, vm, s)
            xs = _out_ffn(xs, mods, i, n_lat_tiles, oa_lat, ob_lat, oa_ctx, ob_ctx, wa, wb,
                          row(ln1_g[i]), row(ln1_b[i]), ffn_wg, ffn_wu, ffn_wd, row(ln2_g[i]), row(ln2_b[i]))
        else:
            xs, *routing = _pool_router(xs, mods, i, n_lat_tiles, s, pool_w[j].astype(BF16), row(pool_scale[j]),
                                        row(ln1_g[i]), row(ln1_b[i]), moe_router[j])
            xs = _moe_layer(xs, routing, mods, i, n_lat_tiles, *moe_bf16, row(ln2_g[i]), row(ln2_b[i]))
    return xs[:s][None] if xs.shape[0] != s else xs[None]
```

```python
import functools

import numpy as np
import jax
import jax.numpy as jnp
from jax import lax
from jax.experimental import pallas as pl
from jax.experimental.pallas import tpu as pltpu

F32 = jnp.float32
BF16 = jnp.bfloat16

D_MODEL = 1024
GRID_W = 64
N_HEADS = 8
HEAD_PAD = 128
HP = N_HEADS * HEAD_PAD
NA_HEAD_DIM = 64
NA_W = N_HEADS * NA_HEAD_DIM
NA_PAIRS = N_HEADS // 2
NA_KH = 8
NA_KW = 16
MLA_NOPE = 64
MLA_ROPE = 32
MLA_V = 64
MLA_Q_LORA = 256
MLA_KV_LORA = 128
ROPE_THETA = 10000.0
POOL_WINDOWS = (2, 4, 8, 16)
POOL_GROUP = D_MODEL // len(POOL_WINDOWS)
POOL_HALO = 8
POOL_PAD = 16
FFN_DIM = 2816
N_EXPERTS = 8
EXPERT_DIM = 3584
DEPTH = 4
DEEPNORM_ALPHA = (2 * DEPTH) ** 0.25
LN_EPS = 1e-5
RMS_EPS = 1e-6
LOG2E = 1.4426950408889634
NEG_BIG = -1e30

TM = 256
NA_QROWS = 4
NA_KROWS = NA_QROWS + NA_KH - 1
NA_QB = 8
MLA_TQ = 1024
MLA_TK = 640
MLA_UNROLL = 12
MOE_TM = 512
MOE_TF = 1792
VMEM_LIMIT = 56 * 1024 * 1024


def _cparams(sem):
    return pltpu.CompilerParams(dimension_semantics=sem, vmem_limit_bytes=VMEM_LIMIT)


def _layer_norm(v, g, b):
    mu = jnp.mean(v, axis=-1, keepdims=True)
    d = v - mu
    var = jnp.mean(d * d, axis=-1, keepdims=True)
    return d * lax.rsqrt(var + LN_EPS) * g + b


def _silu(v):
    return v * jax.nn.sigmoid(v)


def _token_specs(xs, n_lat_tiles):
    if isinstance(xs, tuple):
        return list(xs), [pl.BlockSpec((TM, D_MODEL), lambda i: (jnp.minimum(i, n_lat_tiles - 1), 0)),
                          pl.BlockSpec((TM, D_MODEL), lambda i: (0, 0))]
    return [xs], [pl.BlockSpec((TM, D_MODEL), lambda i: (i, 0))]


def _load_tokens(x_refs, n_lat_tiles):
    if len(x_refs) == 2:
        return jnp.where(pl.program_id(0) >= n_lat_tiles, x_refs[1][...], x_refs[0][...])
    return x_refs[0][...]


def _token_rows(xs):
    return sum(a.shape[0] for a in xs) if isinstance(xs, tuple) else xs.shape[0]


def _mod_kernel(ct_ref, w_ref, b_ref, o_ref):
    s = _silu(ct_ref[...])
    w = w_ref[...]
    b = b_ref[...]
    r0 = jnp.sum(w * s[:, 0:1], axis=0, keepdims=True) + b
    r1 = jnp.sum(w * s[:, 1:2], axis=0, keepdims=True) + b
    o_ref[...] = jnp.concatenate([r0, r1], axis=0)


def _modulation(ct, mod_w, mod_b):
    depth, d, n6 = mod_w.shape
    tn = 1536
    return pl.pallas_call(
        _mod_kernel,
        grid=(depth, n6 // tn),
        in_specs=[
            pl.BlockSpec((d, 2), lambda l, j: (0, 0)),
            pl.BlockSpec((None, d, tn), lambda l, j: (l, 0, j)),
            pl.BlockSpec((None, 1, tn), lambda l, j: (l, 0, j)),
        ],
        out_specs=pl.BlockSpec((None, 2, tn), lambda l, j: (l, 0, j)),
        out_shape=jax.ShapeDtypeStruct((depth, 2, n6), F32),
        compiler_params=_cparams(("arbitrary", "arbitrary")),
        name="modulation",
    )(ct, mod_w, mod_b.reshape(depth, 1, n6))


def _attn_proj_kernel(*refs, n_x, n_lat_tiles):
    x_refs = refs[:n_x]
    (mod_ref, win_ref, qn_ref, wq_ref, kvn_ref, wkv_ref, cos_ref, sin_ref,
     qa_ref, ka_ref, va_ref, qm_ref, kmt_ref, vm_ref) = refs[n_x:]
    h = _load_tokens(x_refs, n_lat_tiles) * (1.0 + mod_ref[1:2, :]) + mod_ref[0:1, :]
    p = jnp.dot(h.astype(BF16), win_ref[...], preferred_element_type=F32)
    qa_ref[...] = (p[:, 0:NA_W] * (NA_HEAD_DIM ** -0.5)).astype(BF16)
    ka_ref[...] = p[:, NA_W:2 * NA_W].astype(BF16)
    va_ref[...] = p[:, 2 * NA_W:3 * NA_W].astype(BF16)
    o = 3 * NA_W
    q_c = p[:, o:o + MLA_Q_LORA]
    o += MLA_Q_LORA
    kv_c = p[:, o:o + MLA_KV_LORA]
    o += MLA_KV_LORA
    krp = p[:, o:o + HEAD_PAD]
    krs = p[:, o + HEAD_PAD:o + 2 * HEAD_PAD]
    qn = q_c * lax.rsqrt(jnp.mean(q_c * q_c, axis=-1, keepdims=True) + RMS_EPS) * qn_ref[...]
    kvn = kv_c * lax.rsqrt(jnp.mean(kv_c * kv_c, axis=-1, keepdims=True) + RMS_EPS) * kvn_ref[...]
    q2 = jnp.dot(qn.astype(BF16), wq_ref[...], preferred_element_type=F32)
    kv2 = jnp.dot(kvn.astype(BF16), wkv_ref[...], preferred_element_type=F32)
    cos = cos_ref[...]
    sin = sin_ref[...]
    kr = krp * cos + krs * sin
    q_scale = (MLA_NOPE + MLA_ROPE) ** -0.5 * LOG2E
    for hd in range(N_HEADS):
        lo = hd * HEAD_PAD
        qh = (q2[:, lo:lo + HEAD_PAD] * cos + q2[:, HP + lo:HP + lo + HEAD_PAD] * sin) * q_scale
        qm_ref[:, lo:lo + HEAD_PAD] = qh.astype(BF16)
        kh = kv2[:, lo:lo + HEAD_PAD] + kr
        kmt_ref[hd] = kh.T.astype(BF16)
    lane = lax.broadcasted_iota(jnp.int32, (1, HP), 1)
    ones_col = jnp.where(lane % HEAD_PAD == MLA_V, 1.0, 0.0)
    vm_ref[...] = (kv2[:, HP:2 * HP] + ones_col).astype(BF16)


def _attn_proj(xs, mods, layer, n_lat_tiles, w_in, q_norm, w_q, kv_norm, w_kv, cos_t, sin_t):
    m = _token_rows(xs)
    nt = m // TM
    tok = lambda i: (i, 0)
    full = lambda i: (0, 0)
    act = jax.ShapeDtypeStruct((m, HP), BF16)
    na_act = jax.ShapeDtypeStruct((m, NA_W), BF16)
    x_args, x_specs = _token_specs(xs, n_lat_tiles)
    return pl.pallas_call(
        functools.partial(_attn_proj_kernel, n_x=len(x_args), n_lat_tiles=n_lat_tiles),
        grid=(nt,),
        in_specs=x_specs + [
            pl.BlockSpec((None, None, 6, D_MODEL), lambda i: (layer, i // n_lat_tiles, 0, 0)),
            pl.BlockSpec(w_in.shape, full),
            pl.BlockSpec(q_norm.shape, full),
            pl.BlockSpec(w_q.shape, full),
            pl.BlockSpec(kv_norm.shape, full),
            pl.BlockSpec(w_kv.shape, full),
            pl.BlockSpec((TM, HEAD_PAD), tok),
            pl.BlockSpec((TM, HEAD_PAD), tok),
        ],
        out_specs=[
            pl.BlockSpec((TM, NA_W), tok),
            pl.BlockSpec((TM, NA_W), tok),
            pl.BlockSpec((TM, NA_W), tok),
            pl.BlockSpec((TM, HP), tok),
            pl.BlockSpec((N_HEADS, HEAD_PAD, TM), lambda i: (0, 0, i)),
            pl.BlockSpec((TM, HP), tok),
        ],
        out_shape=[na_act, na_act, na_act, act, jax.ShapeDtypeStruct((N_HEADS, HEAD_PAD, m), BF16), act],
        compiler_params=_cparams(("arbitrary",)),
        name="attn_proj",
    )(*x_args, mods, w_in, q_norm, w_q, kv_norm, w_kv, cos_t, sin_t)


_NT_DIMS = (((1,), (1,)), ((), ()))


def _pair_masks():
    lane = lax.broadcasted_iota(jnp.int32, (1, HEAD_PAD), 1)
    return lane < NA_HEAD_DIM, lane >= NA_HEAD_DIM


def _na_kernel(q_ref, k_ref, v_ref, kc_ref, vc_ref, *rest, n_rows):
    bm_refs, o_ref = rest[:NA_QB], rest[NA_QB]
    tq = NA_QROWS * GRID_W
    nk = NA_KROWS * GRID_W
    kc = kc_ref[...]
    vc = vc_ref[...]
    head_lanes = _pair_masks()
    for u in range(NA_QB):
        b = pl.program_id(1) * NA_QB + u
        kr0 = jnp.clip(b * NA_QROWS - NA_KH // 2, 0, n_rows - NA_KROWS)
        start = pl.multiple_of(kr0 * GRID_W, GRID_W)
        q = q_ref[u * tq:(u + 1) * tq, :]
        kw = k_ref[pl.ds(start, nk), :]
        vw = v_ref[pl.ds(start, nk), :]
        outs = []
        for a in range(2):
            qh = jnp.where(head_lanes[a], q, jnp.zeros_like(q))
            s_loc = lax.dot_general(qh, kw, _NT_DIMS, preferred_element_type=F32) + bm_refs[u][a]
            s_ctx = lax.dot_general(qh, kc, _NT_DIMS, preferred_element_type=F32)
            m = jnp.maximum(jnp.max(s_loc, axis=-1, keepdims=True), jnp.max(s_ctx, axis=-1, keepdims=True))
            p_loc = jnp.exp(s_loc - m)
            p_ctx = jnp.exp(s_ctx - m)
            l = jnp.sum(p_loc, axis=-1, keepdims=True) + jnp.sum(p_ctx, axis=-1, keepdims=True)
            o = (jnp.dot(p_loc.astype(BF16), vw, preferred_element_type=F32)
                 + jnp.dot(p_ctx.astype(BF16), vc, preferred_element_type=F32))
            outs.append(o / l)
        o_ref[u * tq:(u + 1) * tq, :] = jnp.where(head_lanes[0], outs[0], outs[1]).astype(BF16)


def _na_attention(qa, ka, va, bias_mask, s):
    n_rows = s // GRID_W
    nb = n_rows // NA_QROWS
    tq = NA_QROWS * GRID_W
    assert nb % NA_QB == 0
    ctx_blk = s // tq
    pat = lambda b: jnp.where(b == 0, 0, jnp.where(b == nb - 1, 2, 1))
    bm_spec = lambda u: pl.BlockSpec((None, 2, tq, NA_KROWS * GRID_W),
                                     lambda hp, j: (pat(j * NA_QB + u), hp, 0, 0), pipeline_mode=pl.Buffered(1))
    return pl.pallas_call(
        functools.partial(_na_kernel, n_rows=n_rows),
        grid=(NA_PAIRS, nb // NA_QB),
        in_specs=[
            pl.BlockSpec((NA_QB * tq, HEAD_PAD), lambda hp, j: (j, hp)),
            pl.BlockSpec((s, HEAD_PAD), lambda hp, j: (0, hp)),
            pl.BlockSpec((s, HEAD_PAD), lambda hp, j: (0, hp)),
            pl.BlockSpec((tq, HEAD_PAD), lambda hp, j: (ctx_blk, hp)),
            pl.BlockSpec((tq, HEAD_PAD), lambda hp, j: (ctx_blk, hp)),
        ] + [bm_spec(u) for u in range(NA_QB)],
        out_specs=pl.BlockSpec((NA_QB * tq, HEAD_PAD), lambda hp, j: (j, hp)),
        out_shape=jax.ShapeDtypeStruct((s, NA_W), BF16),
        compiler_params=_cparams(("arbitrary", "arbitrary")),
        name="na_attention",
    )(qa, ka, va, ka, va, *([bias_mask] * NA_QB))


def _na_ctx_kernel(q_ref, k_ref, v_ref, o_ref):
    q = q_ref[...]
    k = k_ref[...]
    v = v_ref[...]
    head_lanes = _pair_masks()
    outs = []
    for a in range(2):
        qh = jnp.where(head_lanes[a], q, jnp.zeros_like(q))
        s = lax.dot_general(qh, k, _NT_DIMS, preferred_element_type=F32)
        m = jnp.max(s, axis=-1, keepdims=True)
        p = jnp.exp(s - m)
        l = jnp.sum(p, axis=-1, keepdims=True)
        outs.append(jnp.dot(p.astype(BF16), v, preferred_element_type=F32) / l)
    o_ref[...] = jnp.where(head_lanes[0], outs[0], outs[1]).astype(BF16)


def _na_ctx_attention(qa, ka, va, s):
    blk = s // TM
    spec = pl.BlockSpec((TM, HEAD_PAD), lambda hp: (blk, hp))
    return pl.pallas_call(
        _na_ctx_kernel,
        grid=(NA_PAIRS,),
        in_specs=[spec, spec, spec],
        out_specs=pl.BlockSpec((TM, HEAD_PAD), lambda hp: (0, hp)),
        out_shape=jax.ShapeDtypeStruct((TM, NA_W), BF16),
        compiler_params=_cparams(("arbitrary",)),
        name="na_ctx_attention",
    )(qa, ka, va)


def _mla_ctx_kernel(q_ref, kt_ref, v_ref, o_ref):
    s = jnp.dot(q_ref[...], kt_ref[...], preferred_element_type=F32)
    m = jnp.max(s, axis=-1, keepdims=True)
    p = jnp.exp2(s - m)
    l = jnp.sum(p, axis=-1, keepdims=True)
    o = jnp.dot(p.astype(BF16), v_ref[...], preferred_element_type=F32)
    o_ref[...] = (o / l).astype(BF16)


def _mla_ctx_attention(qm, kmt, vm, s):
    blk = s // TM
    return pl.pallas_call(
        _mla_ctx_kernel,
        grid=(N_HEADS,),
        in_specs=[
            pl.BlockSpec((TM, HEAD_PAD), lambda h: (blk, h)),
            pl.BlockSpec((None, HEAD_PAD, TM), lambda h: (h, 0, blk)),
            pl.BlockSpec((TM, HEAD_PAD), lambda h: (blk, h)),
        ],
        out_specs=pl.BlockSpec((TM, HEAD_PAD), lambda h: (0, h)),
        out_shape=jax.ShapeDtypeStruct((TM, HP), BF16),
        compiler_params=_cparams(("arbitrary",)),
        name="mla_ctx_attention",
    )(qm, kmt, vm)


def _mla_kernel(q_ref, qn_ref, kt_ref, v_ref, *rest, n_chunks, n_cast):
    cast_in, o_ref, cast_out = rest[:n_cast], rest[n_cast], rest[n_cast + 1:2 * n_cast + 1]
    s0_ref, s1_ref = rest[2 * n_cast + 1:]
    for w_ref, wb_ref in zip(cast_in, cast_out):
        wb_ref[...] = w_ref[...].astype(BF16)
    q = q_ref[...]
    tq = q.shape[0]

    def scores(c, s_ref, q_tile=q):
        off = pl.multiple_of(c * MLA_TK, MLA_TK)
        s_ref[...] = jnp.dot(q_tile, kt_ref[:, pl.ds(off, MLA_TK)], preferred_element_type=F32)

    def update(c, s_ref, carry):
        m, acc = carry
        off = pl.multiple_of(c * MLA_TK, MLA_TK)
        s = s_ref[...]
        m_new = jnp.maximum(m, jnp.max(s, axis=-1, keepdims=True))
        alpha = jnp.exp2(m - m_new)
        p = jnp.exp2(s - m_new).astype(BF16)
        acc = alpha * acc + jnp.dot(p, v_ref[pl.ds(off, MLA_TK), :], preferred_element_type=F32)
        return m_new, acc

    bufs = (s0_ref, s1_ref)

    def group(g, carry):
        c0 = g * MLA_UNROLL
        for u in range(MLA_UNROLL):
            scores(c0 + u + 1, bufs[(u + 1) % 2])
            carry = update(c0 + u, bufs[u % 2], carry)
        return carry

    carry = (jnp.full((tq, 1), NEG_BIG, F32), jnp.zeros((tq, HEAD_PAD), F32))

    @pl.when(pl.program_id(1) == 0)
    def _():
        scores(0, s0_ref)

    n_groups = (n_chunks - 2) // MLA_UNROLL
    carry = lax.fori_loop(0, n_groups, group, carry)
    for c in range(n_groups * MLA_UNROLL, n_chunks):
        if c + 1 < n_chunks:
            scores(c + 1, bufs[(c + 1) % 2])
        else:
            scores(0, s0_ref, qn_ref[...])
        carry = update(c, bufs[c % 2], carry)
    _, acc = carry
    o_ref[...] = (acc / acc[:, MLA_V:MLA_V + 1]).astype(BF16)


def _mla_attention(qm, kmt, vm, s, cast_weights=(), cast_layer=0):
    m = qm.shape[0]
    n_chunks = m // MLA_TK
    assert n_chunks % 2 == 0 and n_chunks >= 2
    n_q = s // MLA_TQ
    assert N_HEADS == N_EXPERTS or not cast_weights
    cast_in_specs, cast_out_specs, cast_shapes = [], [], []
    for w in cast_weights:
        rows, cols = w.shape[2], w.shape[3]
        slab = next(sl for sl in range(16, rows + 1, 16) if rows % sl == 0 and rows // sl <= n_q)
        blk = lambda i, n_blk=rows // slab: jnp.minimum(i, n_blk - 1)
        cast_in_specs.append(pl.BlockSpec((None, None, slab, cols), lambda h, i, blk=blk: (cast_layer, h, blk(i), 0)))
        cast_out_specs.append(pl.BlockSpec((None, slab, cols), lambda h, i, blk=blk: (h, blk(i), 0)))
        cast_shapes.append(jax.ShapeDtypeStruct((N_EXPERTS, rows, cols), BF16))
    outs = pl.pallas_call(
        functools.partial(_mla_kernel, n_chunks=n_chunks, n_cast=len(cast_weights)),
        grid=(N_HEADS, n_q),
        in_specs=[
            pl.BlockSpec((MLA_TQ, HEAD_PAD), lambda h, i: (i, h)),
            pl.BlockSpec((MLA_TQ, HEAD_PAD), lambda h, i: (jnp.minimum(i + 1, n_q - 1), h)),
            pl.BlockSpec((None, HEAD_PAD, m), lambda h, i: (h, 0, 0)),
            pl.BlockSpec((m, HEAD_PAD), lambda h, i: (0, h)),
        ] + cast_in_specs,
        out_specs=[pl.BlockSpec((MLA_TQ, HEAD_PAD), lambda h, i: (i, h))] + cast_out_specs,
        out_shape=[jax.ShapeDtypeStruct((s, HP), BF16)] + cast_shapes,
        scratch_shapes=[pltpu.VMEM((MLA_TQ, MLA_TK), F32), pltpu.VMEM((MLA_TQ, MLA_TK), F32)],
        compiler_params=_cparams(("arbitrary", "arbitrary")),
        name="mla_attention",
    )(qm, qm, kmt, vm, *cast_weights)
    return outs[0], tuple(outs[1:])


def _out_ffn_kernel(*refs, n_x, n_lat_tiles, with_ctx):
    x_refs = refs[:n_x]
    mod_ref, oal_ref, obl_ref = refs[n_x:n_x + 3]
    rest = refs[n_x + 3:]
    if with_ctx:
        oac_ref, obc_ref = rest[:2]
        rest = rest[2:]
        is_ctx = pl.program_id(0) >= n_lat_tiles
        oa = jnp.where(is_ctx, oac_ref[...], oal_ref[...])
        ob = jnp.where(is_ctx, obc_ref[...], obl_ref[...])
    else:
        oa = oal_ref[...]
        ob = obl_ref[...]
    wa_ref, wb_ref, g1_ref, b1_ref, wg_ref, wu_ref, wd_ref, g2_ref, b2_ref, o_ref = rest
    y = (jnp.dot(oa, wa_ref[...], preferred_element_type=F32)
         + jnp.dot(ob, wb_ref[...], preferred_element_type=F32))
    v = DEEPNORM_ALPHA * _load_tokens(x_refs, n_lat_tiles) + mod_ref[2:3, :] * y
    x1 = _layer_norm(v, g1_ref[...], b1_ref[...])
    h = (x1 * (1.0 + mod_ref[4:5, :]) + mod_ref[3:4, :]).astype(BF16)
    gate = jnp.dot(h, wg_ref[...], preferred_element_type=F32)
    up = jnp.dot(h, wu_ref[...], preferred_element_type=F32)
    a = (_silu(gate) * up).astype(BF16)
    y2 = jnp.dot(a, wd_ref[...], preferred_element_type=F32)
    v2 = DEEPNORM_ALPHA * x1 + mod_ref[5:6, :] * y2
    o_ref[...] = _layer_norm(v2, g2_ref[...], b2_ref[...])


def _out_ffn(xs, mods, layer, n_lat_tiles, oa_lat, ob_lat, oa_ctx, ob_ctx, wa, wb, ln1_g, ln1_b,
             wg, wu, wd, ln2_g, ln2_b):
    with_ctx = oa_ctx is not None
    m = _token_rows(xs) if with_ctx else n_lat_tiles * TM
    tok = lambda i: (i, 0)
    lat = lambda i: (jnp.minimum(i, n_lat_tiles - 1), 0)
    full = lambda i: (0, 0)
    ctx_args = [oa_ctx, ob_ctx] if with_ctx else []
    x_args, x_specs = _token_specs(xs, n_lat_tiles)
    resident = lambda w: pl.BlockSpec(w.shape, full, pipeline_mode=pl.Buffered(1))
    vec = pl.BlockSpec((1, D_MODEL), full)
    return pl.pallas_call(
        functools.partial(_out_ffn_kernel, n_x=len(x_args), n_lat_tiles=n_lat_tiles, with_ctx=with_ctx),
        grid=(m // TM,),
        in_specs=x_specs + [
            pl.BlockSpec((None, None, 6, D_MODEL), lambda i: (layer, i // n_lat_tiles, 0, 0)),
            pl.BlockSpec((TM, NA_W), lat),
            pl.BlockSpec((TM, HP), lat),
        ] + ([pl.BlockSpec((TM, NA_W), full), pl.BlockSpec((TM, HP), full)] if with_ctx else []) + [
            resident(wa), resident(wb), vec, vec, resident(wg), resident(wu), resident(wd), vec, vec,
        ],
        out_specs=pl.BlockSpec((TM, D_MODEL), tok),
        out_shape=jax.ShapeDtypeStruct((m, D_MODEL), F32),
        compiler_params=_cparams(("arbitrary",)),
        name="out_proj_ffn_ln",
    )(*x_args, mods, oa_lat, ob_lat, *ctx_args, wa, wb, ln1_g, ln1_b, wg, wu, wd, ln2_g, ln2_b)


def _pool_router_kernel(x_ref, xp_ref, xn_ref, mod_ref, w_ref, sc_ref, g_ref, b_ref, rw_ref, tri_ref,
                        o_ref, idx_ref, wts_ref, cnt_ref, ext_ref, lvl_ref, base_ref, *, n_lat_tiles, s):
    i = pl.program_id(0)

    @pl.when(i == 0)
    def _():
        base_ref[...] = jnp.zeros_like(base_ref)

    is_ctx = i >= n_lat_tiles
    j = jnp.where(is_ctx, i - n_lat_tiles, i)
    last = jnp.where(is_ctx, 0, n_lat_tiles - 1)
    n_seq = jnp.where(is_ctx, TM, s)
    scale1 = 1.0 + mod_ref[1:2, :]
    shift1 = mod_ref[0:1, :]
    x = x_ref[...]
    h = x * scale1 + shift1
    hp = jnp.where(j != 0, xp_ref[...] * scale1 + shift1, 0.0)
    hn = jnp.where(j != last, xn_ref[...] * scale1 + shift1, 0.0)
    n_ext = TM + 2 * POOL_HALO
    ext_ref[0:POOL_PAD, :] = jnp.zeros((POOL_PAD, D_MODEL), F32)
    ext_ref[POOL_PAD:POOL_PAD + POOL_HALO, :] = hp
    ext_ref[POOL_PAD + POOL_HALO:POOL_PAD + POOL_HALO + TM, :] = h
    ext_ref[POOL_PAD + POOL_HALO + TM:POOL_PAD + n_ext, :] = hn
    t = j * TM + lax.broadcasted_iota(jnp.int32, (TM, 1), 0)
    ys = []
    for g, win in enumerate(POOL_WINDOWS):
        half = win // 2
        lo = g * POOL_GROUP
        run = lambda r0, r1: ext_ref[r0:r1, lo:lo + POOL_GROUP]
        n, k = 1, 0
        while n < win:
            nxt = lvl_ref.at[g, k % 2]
            nxt[0:POOL_PAD, :] = jnp.zeros((POOL_PAD, POOL_GROUP), F32)
            nxt[POOL_PAD:POOL_PAD + n_ext, :] = (run(POOL_PAD, POOL_PAD + n_ext)
                                                 + run(POOL_PAD - n, POOL_PAD - n + n_ext))
            run = lambda r0, r1, ref=nxt: ref[r0:r1, :]
            n, k = 2 * n, k + 1
        first = POOL_PAD + POOL_HALO + half - 1
        acc = run(first, first + TM)
        cnt = (jnp.minimum(t + half, n_seq) - jnp.maximum(t - half, 0)).astype(F32)
        mixed = acc / cnt - h[:, lo:lo + POOL_GROUP]
        yg = jnp.dot(mixed.astype(BF16), w_ref[g], preferred_element_type=F32)
        ys.append(yg)
    y = jnp.concatenate(ys, axis=-1) * sc_ref[...]
    v = DEEPNORM_ALPHA * x + mod_ref[2:3, :] * y
    x1 = _layer_norm(v, g_ref[...], b_ref[...])
    o_ref[...] = x1
    _route(x1, mod_ref, rw_ref, tri_ref, idx_ref, wts_ref, cnt_ref, base_ref)


def _pool_router(xs, mods, layer, n_lat_tiles, s, pool_w, pool_scale, ln_g, ln_b, router_w):
    m = xs.shape[0]
    nt = m // TM
    rw = jnp.pad(router_w, [(0, 0), (0, HEAD_PAD - N_EXPERTS)])
    rw_hi = rw.astype(BF16)
    rw = jnp.stack([rw_hi, (rw - rw_hi.astype(F32)).astype(BF16)])
    tri = jnp.asarray(np.tril(np.ones((TM, TM), np.float32), -1), dtype=BF16)
    per = TM // POOL_HALO
    n_halo_blocks = m // POOL_HALO
    tok = lambda i: (i, 0)
    full = lambda i: (0, 0)
    return pl.pallas_call(
        functools.partial(_pool_router_kernel, n_lat_tiles=n_lat_tiles, s=s),
        grid=(nt,),
        in_specs=[
            pl.BlockSpec((TM, D_MODEL), tok),
            pl.BlockSpec((POOL_HALO, D_MODEL), lambda i: (jnp.maximum(i * per - 1, 0), 0)),
            pl.BlockSpec((POOL_HALO, D_MODEL), lambda i: (jnp.minimum((i + 1) * per, n_halo_blocks - 1), 0)),
            pl.BlockSpec((None, None, 6, D_MODEL), lambda i: (layer, i // n_lat_tiles, 0, 0)),
            pl.BlockSpec(pool_w.shape, lambda i: (0, 0, 0)),
            pl.BlockSpec((1, D_MODEL), full),
            pl.BlockSpec((1, D_MODEL), full),
            pl.BlockSpec((1, D_MODEL), full),
            pl.BlockSpec(rw.shape, lambda i: (0, 0, 0)),
            pl.BlockSpec((TM, TM), full),
        ],
        out_specs=[
            pl.BlockSpec((TM, D_MODEL), tok),
            pl.BlockSpec((None, 8, TM), lambda i: (i, 0, 0)),
            pl.BlockSpec((TM, HEAD_PAD), tok),
            pl.BlockSpec((8, HEAD_PAD), full),
        ],
        out_shape=[
            jax.ShapeDtypeStruct((m, D_MODEL), F32),
            jax.ShapeDtypeStruct((nt, 8, TM), jnp.int32),
            jax.ShapeDtypeStruct((m, HEAD_PAD), F32),
            jax.ShapeDtypeStruct((8, HEAD_PAD), F32),
        ],
        scratch_shapes=[pltpu.VMEM((POOL_PAD + TM + 2 * POOL_HALO, D_MODEL), F32),
                        pltpu.VMEM((len(POOL_WINDOWS), 2, POOL_PAD + TM + 2 * POOL_HALO, POOL_GROUP), F32),
                        pltpu.VMEM((8, HEAD_PAD), F32)],
        compiler_params=_cparams(("arbitrary",)),
        name="pool_ln_router",
    )(xs, xs, xs, mods, pool_w, pool_scale, ln_g, ln_b, rw, tri)


def _route(x1, mod_ref, rw_ref, tri_ref, idx_ref, wts_ref, cnt_ref, base_ref):
    h = x1 * (1.0 + mod_ref[4:5, :]) + mod_ref[3:4, :]
    h_hi = h.astype(BF16)
    h_lo = (h - h_hi.astype(F32)).astype(BF16)
    logits = (jnp.dot(h_hi, rw_ref[0], preferred_element_type=F32)
              + jnp.dot(h_hi, rw_ref[1], preferred_element_type=F32)
              + jnp.dot(h_lo, rw_ref[0], preferred_element_type=F32))
    lane = lax.broadcasted_iota(jnp.int32, logits.shape, 1)
    logits = jnp.where(lane < N_EXPERTS, logits, -jnp.inf)
    m1 = jnp.max(logits, axis=-1, keepdims=True)
    i1 = jnp.min(jnp.where(logits == m1, lane, HEAD_PAD), axis=-1, keepdims=True)
    rest = jnp.where(lane == i1, -jnp.inf, logits)
    m2 = jnp.max(rest, axis=-1, keepdims=True)
    i2 = jnp.min(jnp.where(rest == m2, lane, HEAD_PAD), axis=-1, keepdims=True)
    e2 = jnp.exp(m2 - m1)
    w1 = 1.0 / (1.0 + e2)
    w2 = e2 / (1.0 + e2)
    wts_ref[...] = jnp.where(lane == 0, w1, jnp.where(lane == 1, w2, 0.0))

    chosen = jnp.where(lane == i1, 1.0, jnp.where(lane == i2, 1.0, 0.0))
    before = jnp.dot(tri_ref[...], chosen.astype(BF16), preferred_element_type=F32)
    rank = before + base_ref[0:1, :]
    r1 = jnp.sum(jnp.where(lane == i1, rank, 0.0), axis=-1, keepdims=True)
    r2 = jnp.sum(jnp.where(lane == i2, rank, 0.0), axis=-1, keepdims=True)
    base_ref[0:1, :] = base_ref[0:1, :] + jnp.sum(chosen, axis=0, keepdims=True)
    cnt_ref[...] = base_ref[...]
    packed = jnp.where(lane == 0, i1.astype(F32),
                       jnp.where(lane == 1, i2.astype(F32),
                                 jnp.where(lane == 2, r1, jnp.where(lane == 3, r2, 0.0))))
    idx_ref[...] = packed.T[0:8, :].astype(jnp.int32)


def _dispatch_kernel(fill_ref, nu_ref, rows_ref, x_ref, mod_ref, xs_ref, hbuf, zbuf, sems, zsem):
    i = pl.program_id(0)
    n = pl.num_programs(0)
    slot = i % 2
    n_tiles = xs_ref.shape[0] // MOE_TM

    @pl.when(i == 0)
    def _():
        zbuf[...] = jnp.zeros_like(zbuf)
        fills = [pltpu.make_async_copy(zbuf, xs_ref.at[pl.ds(pl.multiple_of(fill_ref[e], 8), MOE_TM)],
                                       zsem.at[0]) for e in range(N_EXPERTS)]
        for cp in fills:
            cp.start()
        for cp in fills:
            cp.wait()

        def zero_tail(t, carry):
            row0 = pl.multiple_of(t * MOE_TM, MOE_TM)
            cp = pltpu.make_async_copy(zbuf, xs_ref.at[pl.ds(row0, MOE_TM)], zsem.at[0])
            cp.start()
            cp.wait()
            return carry

        lax.fori_loop(nu_ref[0], n_tiles, zero_tail, 0)

    def wait_slot(sl):
        for _ in range(2):
            pltpu.make_async_copy(hbuf.at[sl], xs_ref.at[pl.ds(0, TM)], sems.at[sl]).wait()

    @pl.when(i >= 2)
    def _():
        wait_slot(slot)

    hbuf[slot] = x_ref[...] * (1.0 + mod_ref[4:5, :]) + mod_ref[3:4, :]

    def issue(r, carry):
        for k in range(2):
            dst = rows_ref[0, k * TM + r]
            pltpu.make_async_copy(hbuf.at[slot, pl.ds(r, 1)], xs_ref.at[pl.ds(dst, 1)], sems.at[slot]).start()
        return carry

    lax.fori_loop(0, TM, issue, 0, unroll=8)

    @pl.when(i == n - 1)
    def _():
        wait_slot(slot)

        @pl.when(n >= 2)
        def _():
            wait_slot(1 - slot)


def _dispatch(fill, n_used, rows, xs, mods, layer, n_lat_tiles, n_tiles):
    m = xs.shape[0]
    grid_spec = pltpu.PrefetchScalarGridSpec(
        num_scalar_prefetch=2,
        grid=(m // TM,),
        in_specs=[
            pl.BlockSpec((None, 1, 2 * TM), lambda i, fill, nu: (i, 0, 0), memory_space=pltpu.SMEM),
            pl.BlockSpec((TM, D_MODEL), lambda i, fill, nu: (i, 0)),
            pl.BlockSpec((None, None, 6, D_MODEL), lambda i, fill, nu: (layer, i // n_lat_tiles, 0, 0)),
        ],
        out_specs=pl.BlockSpec(memory_space=pl.ANY),
        scratch_shapes=[pltpu.VMEM((2, TM, D_MODEL), F32), pltpu.VMEM((MOE_TM, D_MODEL), F32),
                        pltpu.SemaphoreType.DMA((2,)), pltpu.SemaphoreType.DMA((1,))],
    )
    return pl.pallas_call(
        _dispatch_kernel,
        grid_spec=grid_spec,
        out_shape=jax.ShapeDtypeStruct(((n_tiles + 1) * MOE_TM, D_MODEL), F32),
        compiler_params=_cparams(("arbitrary",)),
        name="moe_dispatch",
    )(fill, n_used, rows, xs, mods)


def _experts_kernel(te_ref, nu_ref, x_ref, wg_ref, wu_ref, wd_ref, o_ref, xb_ref, a_ref):
    del te_ref
    f = pl.program_id(1)
    live = pl.program_id(0) < nu_ref[0]

    @pl.when(jnp.logical_not(live) & (f == 0))
    def _():
        o_ref[...] = jnp.zeros_like(o_ref)

    def hidden(xb):
        gate = jnp.dot(xb, wg_ref[...], preferred_element_type=F32)
        up = jnp.dot(xb, wu_ref[...], preferred_element_type=F32)
        return (_silu(gate) * up).astype(BF16)

    @pl.when(live & (f == 0))
    def _():
        xb = x_ref[...].astype(BF16)
        xb_ref[...] = xb
        a_ref[:, 0:MOE_TF] = hidden(xb)

    @pl.when(live & (f == 1))
    def _():
        a_ref[:, MOE_TF:2 * MOE_TF] = hidden(xb_ref[...])
        o_ref[...] = jnp.dot(a_ref[...], wd_ref[...], preferred_element_type=F32)


def _experts(tile_expert, n_used, xs_sorted, wg, wu, wd, n_tiles):
    nf = EXPERT_DIM // MOE_TF
    row_blk = lambda i, f, te, nu: (jnp.minimum(i, nu[0] - 1), 0)
    f_blk = lambda i, f, nu: jnp.where(i < nu[0], f, nf - 1)
    grid_spec = pltpu.PrefetchScalarGridSpec(
        num_scalar_prefetch=2,
        grid=(n_tiles, nf),
        in_specs=[
            pl.BlockSpec((MOE_TM, D_MODEL), row_blk),
            pl.BlockSpec((None, D_MODEL, MOE_TF), lambda i, f, te, nu: (te[i], 0, f_blk(i, f, nu))),
            pl.BlockSpec((None, D_MODEL, MOE_TF), lambda i, f, te, nu: (te[i], 0, f_blk(i, f, nu))),
            pl.BlockSpec((None, EXPERT_DIM, D_MODEL), lambda i, f, te, nu: (te[i], 0, 0)),
        ],
        out_specs=pl.BlockSpec((MOE_TM, D_MODEL), lambda i, f, te, nu: (i, 0)),
        scratch_shapes=[pltpu.VMEM((MOE_TM, D_MODEL), BF16), pltpu.VMEM((MOE_TM, EXPERT_DIM), BF16)],
    )
    assert nf == 2
    return pl.pallas_call(
        _experts_kernel,
        grid_spec=grid_spec,
        out_shape=jax.ShapeDtypeStruct((n_tiles * MOE_TM, D_MODEL), F32),
        compiler_params=_cparams(("arbitrary", "arbitrary")),
        name="moe_experts",
    )(tile_expert, n_used, xs_sorted, wg, wu, wd)


def _combine_kernel(rows_ref, rowsn_ref, wts_ref, x_ref, mod_ref, g_ref, b_ref, ys_ref, o_ref, ybuf, sems):
    i = pl.program_id(0)
    n = pl.num_programs(0)
    slot = i % 2

    def fetch(ids_ref, sl):
        def issue(r, carry):
            for k in range(2):
                src = ids_ref[0, k * TM + r]
                pltpu.make_async_copy(ys_ref.at[pl.ds(src, 1)], ybuf.at[sl, k, pl.ds(r, 1)], sems.at[sl]).start()
            return carry

        lax.fori_loop(0, TM, issue, 0, unroll=8)

    @pl.when(i == 0)
    def _():
        fetch(rows_ref, 0)

    @pl.when(i + 1 < n)
    def _():
        fetch(rowsn_ref, 1 - slot)

    for k in range(2):
        pltpu.make_async_copy(ys_ref.at[pl.ds(0, TM)], ybuf.at[slot, k], sems.at[slot]).wait()

    wts = wts_ref[...]
    y = wts[:, 0:1] * ybuf[slot, 0] + wts[:, 1:2] * ybuf[slot, 1]
    v = DEEPNORM_ALPHA * x_ref[...] + mod_ref[5:6, :] * y
    o_ref[...] = _layer_norm(v, g_ref[...], b_ref[...])


def _combine(rows, wts, xs, mods, layer, n_lat_tiles, ys_sorted, ln_g, ln_b):
    m = xs.shape[0]
    nt = m // TM
    tok = lambda i: (i, 0)
    full = lambda i: (0, 0)
    return pl.pallas_call(
        _combine_kernel,
        grid=(nt,),
        in_specs=[
            pl.BlockSpec((None, 1, 2 * TM), lambda i: (i, 0, 0), memory_space=pltpu.SMEM),
            pl.BlockSpec((None, 1, 2 * TM), lambda i: (jnp.minimum(i + 1, nt - 1), 0, 0),
                         memory_space=pltpu.SMEM),
            pl.BlockSpec((TM, HEAD_PAD), tok),
            pl.BlockSpec((TM, D_MODEL), tok),
            pl.BlockSpec((None, None, 6, D_MODEL), lambda i: (layer, i // n_lat_tiles, 0, 0)),
            pl.BlockSpec((1, D_MODEL), full),
            pl.BlockSpec((1, D_MODEL), full),
            pl.BlockSpec(memory_space=pl.ANY),
        ],
        out_specs=pl.BlockSpec((TM, D_MODEL), tok),
        out_shape=jax.ShapeDtypeStruct((m, D_MODEL), F32),
        scratch_shapes=[pltpu.VMEM((2, 2, TM, D_MODEL), F32), pltpu.SemaphoreType.DMA((2,))],
        compiler_params=_cparams(("arbitrary",)),
        name="moe_combine_ln",
    )(rows, rows, wts, xs, mods, ln_g, ln_b, ys_sorted)


def _moe_layer(xs, routing, mods, layer, n_lat_tiles, wg, wu, wd, ln_g, ln_b):
    m = xs.shape[0]
    n_tiles = (2 * m + N_EXPERTS * (MOE_TM - 1) + MOE_TM - 1) // MOE_TM
    idx, wts, cnt = routing
    counts = cnt[0, :N_EXPERTS].astype(jnp.int32)
    padded = (counts + MOE_TM - 1) // MOE_TM * MOE_TM
    ends = jnp.cumsum(padded)
    offs = ends - padded
    n_used = (ends[-1:] // MOE_TM).astype(jnp.int32)
    tile_row = jnp.arange(n_tiles, dtype=jnp.int32) * MOE_TM
    tile_row = jnp.minimum(tile_row, ends[-1] - MOE_TM)
    tile_expert = jnp.sum((tile_row[:, None] >= ends[None, :]).astype(jnp.int32), axis=1)
    expert_ids = idx[:, 0:2, :]
    group_off = jnp.sum(jnp.where(expert_ids[..., None] == jnp.arange(N_EXPERTS), offs, 0), axis=-1)
    rows = (group_off + idx[:, 2:4, :]).reshape(m // TM, 1, 2 * TM)
    fill = (offs + counts) // 8 * 8
    xs_sorted = _dispatch(fill, n_used, rows, xs, mods, layer, n_lat_tiles, n_tiles)
    ys_sorted = _experts(tile_expert, n_used, xs_sorted, wg, wu, wd, n_tiles)
    return _combine(rows, wts, xs, mods, layer, n_lat_tiles, ys_sorted, ln_g, ln_b)


_ROPE_SWAP = np.array(list(range(8, 16)) + list(range(0, 8)) + list(range(24, 32)) + list(range(16, 24)))


def _prep_attn_weights(w_in, w_q_up, w_kv_up, w_out):
    o = 3 * NA_W
    w_qc = w_in[:, o:o + MLA_Q_LORA]
    o += MLA_Q_LORA
    w_kvc = w_in[:, o:o + MLA_KV_LORA]
    o += MLA_KV_LORA
    w_kr = w_in[:, o:o + MLA_ROPE]
    rope_pad = [(0, 0), (MLA_NOPE, HEAD_PAD - MLA_NOPE - MLA_ROPE)]
    w_in_aug = jnp.concatenate(
        [w_in[:, :3 * NA_W], w_qc, w_kvc, jnp.pad(w_kr, rope_pad), jnp.pad(w_kr[:, _ROPE_SWAP], rope_pad)],
        axis=1).astype(BF16)

    wq = w_q_up.reshape(MLA_Q_LORA, N_HEADS, MLA_NOPE + MLA_ROPE)
    wq_full = jnp.pad(wq, [(0, 0), (0, 0), (0, HEAD_PAD - MLA_NOPE - MLA_ROPE)])
    wq_swap = jnp.pad(wq[:, :, MLA_NOPE:][:, :, _ROPE_SWAP], [(0, 0), (0, 0), rope_pad[1]])
    w_q = jnp.concatenate([wq_full.reshape(MLA_Q_LORA, HP), wq_swap.reshape(MLA_Q_LORA, HP)], axis=1).astype(BF16)

    wkv = w_kv_up.reshape(MLA_KV_LORA, N_HEADS, MLA_NOPE + MLA_V)
    wkn = jnp.pad(wkv[:, :, :MLA_NOPE], [(0, 0), (0, 0), (0, HEAD_PAD - MLA_NOPE)])
    wv = jnp.pad(wkv[:, :, MLA_NOPE:], [(0, 0), (0, 0), (0, HEAD_PAD - MLA_V)])
    w_kv = jnp.concatenate([wkn.reshape(MLA_KV_LORA, HP), wv.reshape(MLA_KV_LORA, HP)], axis=1).astype(BF16)

    wa = w_out[:NA_W].astype(BF16)
    wb = w_out[NA_W:].reshape(N_HEADS, MLA_V, D_MODEL)
    wb = jnp.pad(wb, [(0, 0), (0, HEAD_PAD - MLA_V), (0, 0)]).reshape(HP, D_MODEL).astype(BF16)
    return w_in_aug, w_q, w_kv, wa, wb


def _rope_tables(s, n_ctx):
    t = jnp.arange(s, dtype=jnp.int32)
    row = (t // GRID_W).astype(F32)
    col = (t % GRID_W).astype(F32)
    n_freq = MLA_ROPE // 4
    inv = 1.0 / (ROPE_THETA ** (jnp.arange(n_freq, dtype=F32) / n_freq))
    ar = row[:, None] * inv
    ac = col[:, None] * inv
    cos = jnp.concatenate([jnp.cos(ar), jnp.cos(ar), jnp.cos(ac), jnp.cos(ac)], axis=1)
    sin = jnp.concatenate([-jnp.sin(ar), jnp.sin(ar), -jnp.sin(ac), jnp.sin(ac)], axis=1)
    right = HEAD_PAD - MLA_NOPE - MLA_ROPE
    cos = jnp.pad(cos, [(0, n_ctx), (MLA_NOPE, right)], constant_values=1.0)
    sin = jnp.pad(sin, [(0, n_ctx), (MLA_NOPE, right)])
    return cos, sin


def _na_bias_mask(rel_bias, n_rows):
    nb = n_rows // NA_QROWS
    c = np.arange(GRID_W)
    cs = np.clip(c - NA_KW // 2, 0, GRID_W - NA_KW)
    kc = np.arange(GRID_W)
    ok_c = (kc[None, :] >= cs[:, None]) & (kc[None, :] < cs[:, None] + NA_KW)
    n_dcol = 2 * NA_KW - 1
    left = GRID_W - NA_KW
    period = 2 * GRID_W
    u = jnp.pad(rel_bias, [(0, 0), (0, 0), (left, period - n_dcol - left)])
    flat = jnp.tile(u, (1, 1, GRID_W + 1))[:, :, :GRID_W * (period - 1)]
    toep = flat.reshape(N_HEADS, 2 * NA_KH - 1, GRID_W, period - 1)[:, :, :, GRID_W - 1:]
    toep = jnp.where(ok_c, toep, NEG_BIG)
    masked = jnp.full((N_HEADS, GRID_W, GRID_W), NEG_BIG, F32)
    out = []
    for b in (0, 1, nb - 1):
        kr0 = int(np.clip(b * NA_QROWS - NA_KH // 2, 0, n_rows - NA_KROWS))
        q_blocks = []
        for qr in range(NA_QROWS):
            r = b * NA_QROWS + qr
            rs = int(np.clip(r - NA_KH // 2, 0, n_rows - NA_KH))
            k_blocks = []
            for klr in range(NA_KROWS):
                kr = kr0 + klr
                k_blocks.append(toep[:, kr - r + NA_KH - 1] if rs <= kr < rs + NA_KH else masked)
            q_blocks.append(jnp.concatenate(k_blocks, axis=-1))
        out.append(jnp.concatenate(q_blocks, axis=1))
    return jnp.stack(out, axis=0)


def kernel(x, c, ctx, c_ctx, mod_w, mod_b, ln1_g, ln1_b, ln2_g, ln2_b, attn_w_in, na_rel_bias, mla_q_norm,
           mla_w_q_up, mla_kv_norm, mla_w_kv_up, attn_w_out, ffn_w_gate, ffn_w_up, ffn_w_down, pool_w,
           pool_scale, moe_router, moe_w_gate, moe_w_up, moe_w_down):
    assert x.shape[0] == 1 and c.shape[0] == 1 and ctx.shape[0] == 1
    s = x.shape[1]
    n_ctx = ctx.shape[1]
    assert n_ctx == TM and s % (NA_QROWS * GRID_W) == 0 and s % MLA_TQ == 0
    assert (s + n_ctx) % MLA_TK == 0
    n_lat_tiles = s // TM
    depth = mod_w.shape[0]

    xs = (x[0], ctx[0])
    ct = jnp.stack([c[0], c_ctx], axis=1)
    mods = _modulation(ct, mod_w, mod_b).reshape(depth, 2, 6, D_MODEL)
    cos_t, sin_t = _rope_tables(s, n_ctx)
    row = lambda v: v.reshape(1, -1)
    moe_bf16 = None

    for i in range(depth):
        j = i // 2
        ctx_live = any(l % 2 == 0 for l in range(i + 1, depth))
        if i % 2 == 0:
            w_in, w_q, w_kv, wa, wb = _prep_attn_weights(attn_w_in[j], mla_w_q_up[j], mla_w_kv_up[j],
                                                         attn_w_out[j])
            qa, ka, va, qm, kmt, vm = _attn_proj(xs, mods, i, n_lat_tiles, w_in, row(mla_q_norm[j]), w_q,
                                                 row(mla_kv_norm[j]), w_kv, cos_t, sin_t)
            bias_mask = _na_bias_mask(na_rel_bias[j], s // GRID_W)
            oa_lat = _na_attention(qa, ka, va, bias_mask, s)
            grouped = lambda w: w.reshape(w.shape[0], N_EXPERTS, w.shape[1] // N_EXPERTS, w.shape[2])
            cast = (grouped(ffn_w_gate), grouped(ffn_w_up), grouped(ffn_w_down))
            if i + 1 < depth:
                cast += (moe_w_gate, moe_w_up, moe_w_down)
            ob_lat, cast_bf16 = _mla_attention(qm, kmt, vm, s, cast, j)
            ffn_wg, ffn_wu, ffn_wd = (w.reshape(-1, w.shape[2]) for w in cast_bf16[:3])
            moe_bf16 = cast_bf16[3:]
            oa_ctx = ob_ctx = None
            if ctx_live:
                oa_ctx = _na_ctx_attention(qa, ka, va, s)
                ob_ctx = _mla_ctx_attention(qm, kmt, vm, s)
            xs = _out_ffn(xs, mods, i, n_lat_tiles, oa_lat, ob_lat, oa_ctx, ob_ctx, wa, wb,
                          row(ln1_g[i]), row(ln1_b[i]), ffn_wg, ffn_wu, ffn_wd, row(ln2_g[i]), row(ln2_b[i]))
        else:
            xs, *routing = _pool_router(xs, mods, i, n_lat_tiles, s, pool_w[j].astype(BF16), row(pool_scale[j]),
                                        row(ln1_g[i]), row(ln1_b[i]), moe_router[j])
            xs = _moe_layer(xs, routing, mods, i, n_lat_tiles, *moe_bf16, row(ln2_g[i]), row(ln2_b[i]))
    return xs[:s][None] if xs.shape[0] != s else xs[None]
```

```python
import functools

import numpy as np
import jax
import jax.numpy as jnp
from jax import lax
from jax.experimental import pallas as pl
from jax.experimental.pallas import tpu as pltpu

F32 = jnp.float32
BF16 = jnp.bfloat16

D_MODEL = 1024
GRID_W = 64
N_HEADS = 8
HEAD_PAD = 128
HP = N_HEADS * HEAD_PAD
NA_HEAD_DIM = 64
NA_W = N_HEADS * NA_HEAD_DIM
NA_PAIRS = N_HEADS // 2
NA_KH = 8
NA_KW = 16
MLA_NOPE = 64
MLA_ROPE = 32
MLA_V = 64
MLA_Q_LORA = 256
MLA_KV_LORA = 128
ROPE_THETA = 10000.0
POOL_WINDOWS = (2, 4, 8, 16)
POOL_GROUP = D_MODEL // len(POOL_WINDOWS)
POOL_HALO = 8
POOL_PAD = 16
FFN_DIM = 2816
N_EXPERTS = 8
EXPERT_DIM = 3584
DEPTH = 4
DEEPNORM_ALPHA = (2 * DEPTH) ** 0.25
LN_EPS = 1e-5
RMS_EPS = 1e-6
LOG2E = 1.4426950408889634
NEG_BIG = -1e30

TM = 256
NA_QROWS = 4
NA_KROWS = NA_QROWS + NA_KH - 1
NA_QB = 4
MLA_TQ = 1024
MLA_TK = 640
MLA_UNROLL = 12
MOE_TM = 512
MOE_TF = 1792
VMEM_LIMIT = 56 * 1024 * 1024


def _cparams(sem):
    return pltpu.CompilerParams(dimension_semantics=sem, vmem_limit_bytes=VMEM_LIMIT)


def _layer_norm(v, g, b):
    mu = jnp.mean(v, axis=-1, keepdims=True)
    d = v - mu
    var = jnp.mean(d * d, axis=-1, keepdims=True)
    return d * lax.rsqrt(var + LN_EPS) * g + b


def _silu(v):
    return v * jax.nn.sigmoid(v)


def _token_specs(xs, n_lat_tiles):
    if isinstance(xs, tuple):
        return list(xs), [pl.BlockSpec((TM, D_MODEL), lambda i: (jnp.minimum(i, n_lat_tiles - 1), 0)),
                          pl.BlockSpec((TM, D_MODEL), lambda i: (0, 0))]
    return [xs], [pl.BlockSpec((TM, D_MODEL), lambda i: (i, 0))]


def _load_tokens(x_refs, n_lat_tiles):
    if len(x_refs) == 2:
        return jnp.where(pl.program_id(0) >= n_lat_tiles, x_refs[1][...], x_refs[0][...])
    return x_refs[0][...]


def _token_rows(xs):
    return sum(a.shape[0] for a in xs) if isinstance(xs, tuple) else xs.shape[0]


def _mod_kernel(ct_ref, w_ref, b_ref, o_ref):
    s = _silu(ct_ref[...])
    w = w_ref[...]
    b = b_ref[...]
    r0 = jnp.sum(w * s[:, 0:1], axis=0, keepdims=True) + b
    r1 = jnp.sum(w * s[:, 1:2], axis=0, keepdims=True) + b
    o_ref[...] = jnp.concatenate([r0, r1], axis=0)


def _modulation(ct, mod_w, mod_b):
    depth, d, n6 = mod_w.shape
    tn = 1536
    return pl.pallas_call(
        _mod_kernel,
        grid=(depth, n6 // tn),
        in_specs=[
            pl.BlockSpec((d, 2), lambda l, j: (0, 0)),
            pl.BlockSpec((None, d, tn), lambda l, j: (l, 0, j)),
            pl.BlockSpec((None, 1, tn), lambda l, j: (l, 0, j)),
        ],
        out_specs=pl.BlockSpec((None, 2, tn), lambda l, j: (l, 0, j)),
        out_shape=jax.ShapeDtypeStruct((depth, 2, n6), F32),
        compiler_params=_cparams(("arbitrary", "arbitrary")),
        name="modulation",
    )(ct, mod_w, mod_b.reshape(depth, 1, n6))


def _attn_proj_kernel(*refs, n_x, n_lat_tiles):
    x_refs = refs[:n_x]
    (mod_ref, win_ref, qn_ref, wq_ref, kvn_ref, wkv_ref, cos_ref, sin_ref,
     qa_ref, ka_ref, va_ref, qm_ref, kmt_ref, vm_ref) = refs[n_x:]
    h = _load_tokens(x_refs, n_lat_tiles) * (1.0 + mod_ref[1:2, :]) + mod_ref[0:1, :]
    p = jnp.dot(h.astype(BF16), win_ref[...], preferred_element_type=F32)
    qa_ref[...] = (p[:, 0:NA_W] * (NA_HEAD_DIM ** -0.5)).astype(BF16)
    ka_ref[...] = p[:, NA_W:2 * NA_W].astype(BF16)
    va_ref[...] = p[:, 2 * NA_W:3 * NA_W].astype(BF16)
    o = 3 * NA_W
    q_c = p[:, o:o + MLA_Q_LORA]
    o += MLA_Q_LORA
    kv_c = p[:, o:o + MLA_KV_LORA]
    o += MLA_KV_LORA
    krp = p[:, o:o + HEAD_PAD]
    krs = p[:, o + HEAD_PAD:o + 2 * HEAD_PAD]
    qn = q_c * lax.rsqrt(jnp.mean(q_c * q_c, axis=-1, keepdims=True) + RMS_EPS) * qn_ref[...]
    kvn = kv_c * lax.rsqrt(jnp.mean(kv_c * kv_c, axis=-1, keepdims=True) + RMS_EPS) * kvn_ref[...]
    q2 = jnp.dot(qn.astype(BF16), wq_ref[...], preferred_element_type=F32)
    kv2 = jnp.dot(kvn.astype(BF16), wkv_ref[...], preferred_element_type=F32)
    cos = cos_ref[...]
    sin = sin_ref[...]
    kr = krp * cos + krs * sin
    q_scale = (MLA_NOPE + MLA_ROPE) ** -0.5 * LOG2E
    for hd in range(N_HEADS):
        lo = hd * HEAD_PAD
        qh = (q2[:, lo:lo + HEAD_PAD] * cos + q2[:, HP + lo:HP + lo + HEAD_PAD] * sin) * q_scale
        qm_ref[:, lo:lo + HEAD_PAD] = qh.astype(BF16)
        kh = kv2[:, lo:lo + HEAD_PAD] + kr
        kmt_ref[hd] = kh.T.astype(BF16)
    lane = lax.broadcasted_iota(jnp.int32, (1, HP), 1)
    ones_col = jnp.where(lane % HEAD_PAD == MLA_V, 1.0, 0.0)
    vm_ref[...] = (kv2[:, HP:2 * HP] + ones_col).astype(BF16)


def _attn_proj(xs, mods, layer, n_lat_tiles, w_in, q_norm, w_q, kv_norm, w_kv, cos_t, sin_t):
    m = _token_rows(xs)
    nt = m // TM
    tok = lambda i: (i, 0)
    full = lambda i: (0, 0)
    act = jax.ShapeDtypeStruct((m, HP), BF16)
    na_act = jax.ShapeDtypeStruct((m, NA_W), BF16)
    x_args, x_specs = _token_specs(xs, n_lat_tiles)
    return pl.pallas_call(
        functools.partial(_attn_proj_kernel, n_x=len(x_args), n_lat_tiles=n_lat_tiles),
        grid=(nt,),
        in_specs=x_specs + [
            pl.BlockSpec((None, None, 6, D_MODEL), lambda i: (layer, i // n_lat_tiles, 0, 0)),
            pl.BlockSpec(w_in.shape, full),
            pl.BlockSpec(q_norm.shape, full),
            pl.BlockSpec(w_q.shape, full),
            pl.BlockSpec(kv_norm.shape, full),
            pl.BlockSpec(w_kv.shape, full),
            pl.BlockSpec((TM, HEAD_PAD), tok),
            pl.BlockSpec((TM, HEAD_PAD), tok),
        ],
        out_specs=[
            pl.BlockSpec((TM, NA_W), tok),
            pl.BlockSpec((TM, NA_W), tok),
            pl.BlockSpec((TM, NA_W), tok),
            pl.BlockSpec((TM, HP), tok),
            pl.BlockSpec((N_HEADS, HEAD_PAD, TM), lambda i: (0, 0, i)),
            pl.BlockSpec((TM, HP), tok),
        ],
        out_shape=[na_act, na_act, na_act, act, jax.ShapeDtypeStruct((N_HEADS, HEAD_PAD, m), BF16), act],
        compiler_params=_cparams(("arbitrary",)),
        name="attn_proj",
    )(*x_args, mods, w_in, q_norm, w_q, kv_norm, w_kv, cos_t, sin_t)


_NT_DIMS = (((1,), (1,)), ((), ()))


def _pair_masks():
    lane = lax.broadcasted_iota(jnp.int32, (1, HEAD_PAD), 1)
    return lane < NA_HEAD_DIM, lane >= NA_HEAD_DIM


def _na_kernel(q_ref, k_ref, v_ref, kc_ref, vc_ref, *rest, n_rows):
    bm_refs, o_ref = rest[:NA_QB], rest[NA_QB]
    tq = NA_QROWS * GRID_W
    nk = NA_KROWS * GRID_W
    kc = kc_ref[...]
    vc = vc_ref[...]
    head_lanes = _pair_masks()
    for u in range(NA_QB):
        b = pl.program_id(1) * NA_QB + u
        kr0 = jnp.clip(b * NA_QROWS - NA_KH // 2, 0, n_rows - NA_KROWS)
        start = pl.multiple_of(kr0 * GRID_W, GRID_W)
        q = q_ref[u * tq:(u + 1) * tq, :]
        kw = k_ref[pl.ds(start, nk), :]
        vw = v_ref[pl.ds(start, nk), :]
        outs = []
        for a in range(2):
            qh = jnp.where(head_lanes[a], q, jnp.zeros_like(q))
            s_loc = lax.dot_general(qh, kw, _NT_DIMS, preferred_element_type=F32) + bm_refs[u][a]
            s_ctx = lax.dot_general(qh, kc, _NT_DIMS, preferred_element_type=F32)
            m = jnp.maximum(jnp.max(s_loc, axis=-1, keepdims=True), jnp.max(s_ctx, axis=-1, keepdims=True))
            p_loc = jnp.exp(s_loc - m)
            p_ctx = jnp.exp(s_ctx - m)
            l = jnp.sum(p_loc, axis=-1, keepdims=True) + jnp.sum(p_ctx, axis=-1, keepdims=True)
            o = (jnp.dot(p_loc.astype(BF16), vw, preferred_element_type=F32)
                 + jnp.dot(p_ctx.astype(BF16), vc, preferred_element_type=F32))
            outs.append(o / l)
        o_ref[u * tq:(u + 1) * tq, :] = jnp.where(head_lanes[0], outs[0], outs[1]).astype(BF16)


def _na_attention(qa, ka, va, bias_mask, s):
    n_rows = s // GRID_W
    nb = n_rows // NA_QROWS
    tq = NA_QROWS * GRID_W
    assert nb % NA_QB == 0
    ctx_blk = s // tq
    pat = lambda b: jnp.where(b == 0, 0, jnp.where(b == nb - 1, 2, 1))
    bm_spec = lambda u: pl.BlockSpec((None, 2, tq, NA_KROWS * GRID_W),
                                     lambda hp, j: (pat(j * NA_QB + u), hp, 0, 0))
    return pl.pallas_call(
        functools.partial(_na_kernel, n_rows=n_rows),
        grid=(NA_PAIRS, nb // NA_QB),
        in_specs=[
            pl.BlockSpec((NA_QB * tq, HEAD_PAD), lambda hp, j: (j, hp)),
            pl.BlockSpec((s, HEAD_PAD), lambda hp, j: (0, hp)),
            pl.BlockSpec((s, HEAD_PAD), lambda hp, j: (0, hp)),
            pl.BlockSpec((tq, HEAD_PAD), lambda hp, j: (ctx_blk, hp)),
            pl.BlockSpec((tq, HEAD_PAD), lambda hp, j: (ctx_blk, hp)),
        ] + [bm_spec(u) for u in range(NA_QB)],
        out_specs=pl.BlockSpec((NA_QB * tq, HEAD_PAD), lambda hp, j: (j, hp)),
        out_shape=jax.ShapeDtypeStruct((s, NA_W), BF16),
        compiler_params=_cparams(("arbitrary", "arbitrary")),
        name="na_attention",
    )(qa, ka, va, ka, va, *([bias_mask] * NA_QB))


def _na_ctx_kernel(q_ref, k_ref, v_ref, o_ref):
    q = q_ref[...]
    k = k_ref[...]
    v = v_ref[...]
    head_lanes = _pair_masks()
    outs = []
    for a in range(2):
        qh = jnp.where(head_lanes[a], q, jnp.zeros_like(q))
        s = lax.dot_general(qh, k, _NT_DIMS, preferred_element_type=F32)
        m = jnp.max(s, axis=-1, keepdims=True)
        p = jnp.exp(s - m)
        l = jnp.sum(p, axis=-1, keepdims=True)
        outs.append(jnp.dot(p.astype(BF16), v, preferred_element_type=F32) / l)
    o_ref[...] = jnp.where(head_lanes[0], outs[0], outs[1]).astype(BF16)


def _na_ctx_attention(qa, ka, va, s):
    blk = s // TM
    spec = pl.BlockSpec((TM, HEAD_PAD), lambda hp: (blk, hp))
    return pl.pallas_call(
        _na_ctx_kernel,
        grid=(NA_PAIRS,),
        in_specs=[spec, spec, spec],
        out_specs=pl.BlockSpec((TM, HEAD_PAD), lambda hp: (0, hp)),
        out_shape=jax.ShapeDtypeStruct((TM, NA_W), BF16),
        compiler_params=_cparams(("arbitrary",)),
        name="na_ctx_attention",
    )(qa, ka, va)


def _mla_ctx_kernel(q_ref, kt_ref, v_ref, o_ref):
    s = jnp.dot(q_ref[...], kt_ref[...], preferred_element_type=F32)
    m = jnp.max(s, axis=-1, keepdims=True)
    p = jnp.exp2(s - m)
    l = jnp.sum(p, axis=-1, keepdims=True)
    o = jnp.dot(p.astype(BF16), v_ref[...], preferred_element_type=F32)
    o_ref[...] = (o / l).astype(BF16)


def _mla_ctx_attention(qm, kmt, vm, s):
    blk = s // TM
    return pl.pallas_call(
        _mla_ctx_kernel,
        grid=(N_HEADS,),
        in_specs=[
            pl.BlockSpec((TM, HEAD_PAD), lambda h: (blk, h)),
            pl.BlockSpec((None, HEAD_PAD, TM), lambda h: (h, 0, blk)),
            pl.BlockSpec((TM, HEAD_PAD), lambda h: (blk, h)),
        ],
        out_specs=pl.BlockSpec((TM, HEAD_PAD), lambda h: (0, h)),
        out_shape=jax.ShapeDtypeStruct((TM, HP), BF16),
        compiler_params=_cparams(("arbitrary",)),
        name="mla_ctx_attention",
    )(qm, kmt, vm)


def _mla_kernel(q_ref, qn_ref, kt_ref, v_ref, *rest, n_chunks, n_cast):
    cast_in, o_ref, cast_out = rest[:n_cast], rest[n_cast], rest[n_cast + 1:2 * n_cast + 1]
    s0_ref, s1_ref = rest[2 * n_cast + 1:]
    for w_ref, wb_ref in zip(cast_in, cast_out):
        wb_ref[...] = w_ref[...].astype(BF16)
    q = q_ref[...]
    tq = q.shape[0]

    def scores(c, s_ref, q_tile=q):
        off = pl.multiple_of(c * MLA_TK, MLA_TK)
        s_ref[...] = jnp.dot(q_tile, kt_ref[:, pl.ds(off, MLA_TK)], preferred_element_type=F32)

    def update(c, s_ref, carry):
        m, acc = carry
        off = pl.multiple_of(c * MLA_TK, MLA_TK)
        s = s_ref[...]
        m_new = jnp.maximum(m, jnp.max(s, axis=-1, keepdims=True))
        alpha = jnp.exp2(m - m_new)
        p = jnp.exp2(s - m_new).astype(BF16)
        acc = alpha * acc + jnp.dot(p, v_ref[pl.ds(off, MLA_TK), :], preferred_element_type=F32)
        return m_new, acc

    bufs = (s0_ref, s1_ref)

    def group(g, carry):
        c0 = g * MLA_UNROLL
        for u in range(MLA_UNROLL):
            scores(c0 + u + 1, bufs[(u + 1) % 2])
            carry = update(c0 + u, bufs[u % 2], carry)
        return carry

    carry = (jnp.full((tq, 1), NEG_BIG, F32), jnp.zeros((tq, HEAD_PAD), F32))

    @pl.when(pl.program_id(1) == 0)
    def _():
        scores(0, s0_ref)

    n_groups = (n_chunks - 2) // MLA_UNROLL
    carry = lax.fori_loop(0, n_groups, group, carry)
    for c in range(n_groups * MLA_UNROLL, n_chunks):
        if c + 1 < n_chunks:
            scores(c + 1, bufs[(c + 1) % 2])
        else:
            scores(0, s0_ref, qn_ref[...])
        carry = update(c, bufs[c % 2], carry)
    _, acc = carry
    o_ref[...] = (acc / acc[:, MLA_V:MLA_V + 1]).astype(BF16)


def _mla_attention(qm, kmt, vm, s, cast_weights=(), cast_layer=0):
    m = qm.shape[0]
    n_chunks = m // MLA_TK
    assert n_chunks % 2 == 0 and n_chunks >= 2
    n_q = s // MLA_TQ
    assert N_HEADS == N_EXPERTS or not cast_weights
    cast_in_specs, cast_out_specs, cast_shapes = [], [], []
    for w in cast_weights:
        rows, cols = w.shape[2], w.shape[3]
        slab = next(sl for sl in range(16, rows + 1, 16) if rows % sl == 0 and rows // sl <= n_q)
        blk = lambda i, n_blk=rows // slab: jnp.minimum(i, n_blk - 1)
        cast_in_specs.append(pl.BlockSpec((None, None, slab, cols), lambda h, i, blk=blk: (cast_layer, h, blk(i), 0)))
        cast_out_specs.append(pl.BlockSpec((None, slab, cols), lambda h, i, blk=blk: (h, blk(i), 0)))
        cast_shapes.append(jax.ShapeDtypeStruct((N_EXPERTS, rows, cols), BF16))
    outs = pl.pallas_call(
        functools.partial(_mla_kernel, n_chunks=n_chunks, n_cast=len(cast_weights)),
        grid=(N_HEADS, n_q),
        in_specs=[
            pl.BlockSpec((MLA_TQ, HEAD_PAD), lambda h, i: (i, h)),
            pl.BlockSpec((MLA_TQ, HEAD_PAD), lambda h, i: (jnp.minimum(i + 1, n_q - 1), h)),
            pl.BlockSpec((None, HEAD_PAD, m), lambda h, i: (h, 0, 0)),
            pl.BlockSpec((m, HEAD_PAD), lambda h, i: (0, h)),
        ] + cast_in_specs,
        out_specs=[pl.BlockSpec((MLA_TQ, HEAD_PAD), lambda h, i: (i, h))] + cast_out_specs,
        out_shape=[jax.ShapeDtypeStruct((s, HP), BF16)] + cast_shapes,
        scratch_shapes=[pltpu.VMEM((MLA_TQ, MLA_TK), F32), pltpu.VMEM((MLA_TQ, MLA_TK), F32)],
        compiler_params=_cparams(("arbitrary", "arbitrary")),
        name="mla_attention",
    )(qm, qm, kmt, vm, *cast_weights)
    return outs[0], tuple(outs[1:])


def _out_ffn_kernel(*refs, n_x, n_lat_tiles, with_ctx):
    x_refs = refs[:n_x]
    mod_ref, oal_ref, obl_ref = refs[n_x:n_x + 3]
    rest = refs[n_x + 3:]
    if with_ctx:
        oac_ref, obc_ref = rest[:2]
        rest = rest[2:]
        is_ctx = pl.program_id(0) >= n_lat_tiles
        oa = jnp.where(is_ctx, oac_ref[...], oal_ref[...])
        ob = jnp.where(is_ctx, obc_ref[...], obl_ref[...])
    else:
        oa = oal_ref[...]
        ob = obl_ref[...]
    wa_ref, wb_ref, g1_ref, b1_ref, wg_ref, wu_ref, wd_ref, g2_ref, b2_ref, o_ref = rest
    y = (jnp.dot(oa, wa_ref[...], preferred_element_type=F32)
         + jnp.dot(ob, wb_ref[...], preferred_element_type=F32))
    v = DEEPNORM_ALPHA * _load_tokens(x_refs, n_lat_tiles) + mod_ref[2:3, :] * y
    x1 = _layer_norm(v, g1_ref[...], b1_ref[...])
    h = (x1 * (1.0 + mod_ref[4:5, :]) + mod_ref[3:4, :]).astype(BF16)
    gate = jnp.dot(h, wg_ref[...], preferred_element_type=F32)
    up = jnp.dot(h, wu_ref[...], preferred_element_type=F32)
    a = (_silu(gate) * up).astype(BF16)
    y2 = jnp.dot(a, wd_ref[...], preferred_element_type=F32)
    v2 = DEEPNORM_ALPHA * x1 + mod_ref[5:6, :] * y2
    o_ref[...] = _layer_norm(v2, g2_ref[...], b2_ref[...])


def _out_ffn(xs, mods, layer, n_lat_tiles, oa_lat, ob_lat, oa_ctx, ob_ctx, wa, wb, ln1_g, ln1_b,
             wg, wu, wd, ln2_g, ln2_b):
    with_ctx = oa_ctx is not None
    m = _token_rows(xs) if with_ctx else n_lat_tiles * TM
    tok = lambda i: (i, 0)
    lat = lambda i: (jnp.minimum(i, n_lat_tiles - 1), 0)
    full = lambda i: (0, 0)
    ctx_args = [oa_ctx, ob_ctx] if with_ctx else []
    x_args, x_specs = _token_specs(xs, n_lat_tiles)
    resident = lambda w: pl.BlockSpec(w.shape, full, pipeline_mode=pl.Buffered(1))
    vec = pl.BlockSpec((1, D_MODEL), full)
    return pl.pallas_call(
        functools.partial(_out_ffn_kernel, n_x=len(x_args), n_lat_tiles=n_lat_tiles, with_ctx=with_ctx),
        grid=(m // TM,),
        in_specs=x_specs + [
            pl.BlockSpec((None, None, 6, D_MODEL), lambda i: (layer, i // n_lat_tiles, 0, 0)),
            pl.BlockSpec((TM, NA_W), lat),
            pl.BlockSpec((TM, HP), lat),
        ] + ([pl.BlockSpec((TM, NA_W), full), pl.BlockSpec((TM, HP), full)] if with_ctx else []) + [
            resident(wa), resident(wb), vec, vec, resident(wg), resident(wu), resident(wd), vec, vec,
        ],
        out_specs=pl.BlockSpec((TM, D_MODEL), tok),
        out_shape=jax.ShapeDtypeStruct((m, D_MODEL), F32),
        compiler_params=_cparams(("arbitrary",)),
        name="out_proj_ffn_ln",
    )(*x_args, mods, oa_lat, ob_lat, *ctx_args, wa, wb, ln1_g, ln1_b, wg, wu, wd, ln2_g, ln2_b)


def _pool_router_kernel(x_ref, xp_ref, xn_ref, mod_ref, w_ref, sc_ref, g_ref, b_ref, rw_ref, tri_ref,
                        o_ref, idx_ref, wts_ref, cnt_ref, ext_ref, lvl_ref, base_ref, *, n_lat_tiles, s):
    i = pl.program_id(0)

    @pl.when(i == 0)
    def _():
        base_ref[...] = jnp.zeros_like(base_ref)

    is_ctx = i >= n_lat_tiles
    j = jnp.where(is_ctx, i - n_lat_tiles, i)
    last = jnp.where(is_ctx, 0, n_lat_tiles - 1)
    n_seq = jnp.where(is_ctx, TM, s)
    scale1 = 1.0 + mod_ref[1:2, :]
    shift1 = mod_ref[0:1, :]
    x = x_ref[...]
    h = x * scale1 + shift1
    hp = jnp.where(j != 0, xp_ref[...] * scale1 + shift1, 0.0)
    hn = jnp.where(j != last, xn_ref[...] * scale1 + shift1, 0.0)
    n_ext = TM + 2 * POOL_HALO
    ext_ref[0:POOL_PAD, :] = jnp.zeros((POOL_PAD, D_MODEL), F32)
    ext_ref[POOL_PAD:POOL_PAD + POOL_HALO, :] = hp
    ext_ref[POOL_PAD + POOL_HALO:POOL_PAD + POOL_HALO + TM, :] = h
    ext_ref[POOL_PAD + POOL_HALO + TM:POOL_PAD + n_ext, :] = hn
    t = j * TM + lax.broadcasted_iota(jnp.int32, (TM, 1), 0)
    ys = []
    for g, win in enumerate(POOL_WINDOWS):
        half = win // 2
        lo = g * POOL_GROUP
        run = lambda r0, r1: ext_ref[r0:r1, lo:lo + POOL_GROUP]
        n, k = 1, 0
        while n < win:
            nxt = lvl_ref.at[g, k % 2]
            nxt[0:POOL_PAD, :] = jnp.zeros((POOL_PAD, POOL_GROUP), F32)
            nxt[POOL_PAD:POOL_PAD + n_ext, :] = (run(POOL_PAD, POOL_PAD + n_ext)
                                                 + run(POOL_PAD - n, POOL_PAD - n + n_ext))
            run = lambda r0, r1, ref=nxt: ref[r0:r1, :]
            n, k = 2 * n, k + 1
        first = POOL_PAD + POOL_HALO + half - 1
        acc = run(first, first + TM)
        cnt = (jnp.minimum(t + half, n_seq) - jnp.maximum(t - half, 0)).astype(F32)
        mixed = acc / cnt - h[:, lo:lo + POOL_GROUP]
        yg = jnp.dot(mixed.astype(BF16), w_ref[g], preferred_element_type=F32)
        ys.append(yg)
    y = jnp.concatenate(ys, axis=-1) * sc_ref[...]
    v = DEEPNORM_ALPHA * x + mod_ref[2:3, :] * y
    x1 = _layer_norm(v, g_ref[...], b_ref[...])
    o_ref[...] = x1
    _route(x1, mod_ref, rw_ref, tri_ref, idx_ref, wts_ref, cnt_ref, base_ref)


def _pool_router(xs, mods, layer, n_lat_tiles, s, pool_w, pool_scale, ln_g, ln_b, router_w):
    m = xs.shape[0]
    nt = m // TM
    rw = jnp.pad(router_w, [(0, 0), (0, HEAD_PAD - N_EXPERTS)])
    rw_hi = rw.astype(BF16)
    rw = jnp.stack([rw_hi, (rw - rw_hi.astype(F32)).astype(BF16)])
    tri = jnp.asarray(np.tril(np.ones((TM, TM), np.float32), -1), dtype=BF16)
    per = TM // POOL_HALO
    n_halo_blocks = m // POOL_HALO
    tok = lambda i: (i, 0)
    full = lambda i: (0, 0)
    return pl.pallas_call(
        functools.partial(_pool_router_kernel, n_lat_tiles=n_lat_tiles, s=s),
        grid=(nt,),
        in_specs=[
            pl.BlockSpec((TM, D_MODEL), tok),
            pl.BlockSpec((POOL_HALO, D_MODEL), lambda i: (jnp.maximum(i * per - 1, 0), 0)),
            pl.BlockSpec((POOL_HALO, D_MODEL), lambda i: (jnp.minimum((i + 1) * per, n_halo_blocks - 1), 0)),
            pl.BlockSpec((None, None, 6, D_MODEL), lambda i: (layer, i // n_lat_tiles, 0, 0)),
            pl.BlockSpec(pool_w.shape, lambda i: (0, 0, 0)),
            pl.BlockSpec((1, D_MODEL), full),
            pl.BlockSpec((1, D_MODEL), full),
            pl.BlockSpec((1, D_MODEL), full),
            pl.BlockSpec(rw.shape, lambda i: (0, 0, 0)),
            pl.BlockSpec((TM, TM), full),
        ],
        out_specs=[
            pl.BlockSpec((TM, D_MODEL), tok),
            pl.BlockSpec((None, 8, TM), lambda i: (i, 0, 0)),
            pl.BlockSpec((TM, HEAD_PAD), tok),
            pl.BlockSpec((8, HEAD_PAD), full),
        ],
        out_shape=[
            jax.ShapeDtypeStruct((m, D_MODEL), F32),
            jax.ShapeDtypeStruct((nt, 8, TM), jnp.int32),
            jax.ShapeDtypeStruct((m, HEAD_PAD), F32),
            jax.ShapeDtypeStruct((8, HEAD_PAD), F32),
        ],
        scratch_shapes=[pltpu.VMEM((POOL_PAD + TM + 2 * POOL_HALO, D_MODEL), F32),
                        pltpu.VMEM((len(POOL_WINDOWS), 2, POOL_PAD + TM + 2 * POOL_HALO, POOL_GROUP), F32),
                        pltpu.VMEM((8, HEAD_PAD), F32)],
        compiler_params=_cparams(("arbitrary",)),
        name="pool_ln_router",
    )(xs, xs, xs, mods, pool_w, pool_scale, ln_g, ln_b, rw, tri)


def _route(x1, mod_ref, rw_ref, tri_ref, idx_ref, wts_ref, cnt_ref, base_ref):
    h = x1 * (1.0 + mod_ref[4:5, :]) + mod_ref[3:4, :]
    h_hi = h.astype(BF16)
    h_lo = (h - h_hi.astype(F32)).astype(BF16)
    logits = (jnp.dot(h_hi, rw_ref[0], preferred_element_type=F32)
              + jnp.dot(h_hi, rw_ref[1], preferred_element_type=F32)
              + jnp.dot(h_lo, rw_ref[0], preferred_element_type=F32))
    lane = lax.broadcasted_iota(jnp.int32, logits.shape, 1)
    logits = jnp.where(lane < N_EXPERTS, logits, -jnp.inf)
    m1 = jnp.max(logits, axis=-1, keepdims=True)
    i1 = jnp.min(jnp.where(logits == m1, lane, HEAD_PAD), axis=-1, keepdims=True)
    rest = jnp.where(lane == i1, -jnp.inf, logits)
    m2 = jnp.max(rest, axis=-1, keepdims=True)
    i2 = jnp.min(jnp.where(rest == m2, lane, HEAD_PAD), axis=-1, keepdims=True)
    e2 = jnp.exp(m2 - m1)
    w1 = 1.0 / (1.0 + e2)
    w2 = e2 / (1.0 + e2)
    wts_ref[...] = jnp.where(lane == 0, w1, jnp.where(lane == 1, w2, 0.0))

    chosen = jnp.where(lane == i1, 1.0, jnp.where(lane == i2, 1.0, 0.0))
    before = jnp.dot(tri_ref[...], chosen.astype(BF16), preferred_element_type=F32)
    rank = before + base_ref[0:1, :]
    r1 = jnp.sum(jnp.where(lane == i1, rank, 0.0), axis=-1, keepdims=True)
    r2 = jnp.sum(jnp.where(lane == i2, rank, 0.0), axis=-1, keepdims=True)
    base_ref[0:1, :] = base_ref[0:1, :] + jnp.sum(chosen, axis=0, keepdims=True)
    cnt_ref[...] = base_ref[...]
    packed = jnp.where(lane == 0, i1.astype(F32),
                       jnp.where(lane == 1, i2.astype(F32),
                                 jnp.where(lane == 2, r1, jnp.where(lane == 3, r2, 0.0))))
    idx_ref[...] = packed.T[0:8, :].astype(jnp.int32)


def _dispatch_kernel(fill_ref, nu_ref, rows_ref, x_ref, mod_ref, xs_ref, hbuf, zbuf, sems, zsem):
    i = pl.program_id(0)
    n = pl.num_programs(0)
    slot = i % 2
    n_tiles = xs_ref.shape[0] // MOE_TM

    @pl.when(i == 0)
    def _():
        zbuf[...] = jnp.zeros_like(zbuf)
        fills = [pltpu.make_async_copy(zbuf, xs_ref.at[pl.ds(pl.multiple_of(fill_ref[e], 8), MOE_TM)],
                                       zsem.at[0]) for e in range(N_EXPERTS)]
        for cp in fills:
            cp.start()
        for cp in fills:
            cp.wait()

        def zero_tail(t, carry):
            row0 = pl.multiple_of(t * MOE_TM, MOE_TM)
            cp = pltpu.make_async_copy(zbuf, xs_ref.at[pl.ds(row0, MOE_TM)], zsem.at[0])
            cp.start()
            cp.wait()
            return carry

        lax.fori_loop(nu_ref[0], n_tiles, zero_tail, 0)

    def wait_slot(sl):
        for _ in range(2):
            pltpu.make_async_copy(hbuf.at[sl], xs_ref.at[pl.ds(0, TM)], sems.at[sl]).wait()

    @pl.when(i >= 2)
    def _():
        wait_slot(slot)

    hbuf[slot] = x_ref[...] * (1.0 + mod_ref[4:5, :]) + mod_ref[3:4, :]

    def issue(r, carry):
        for k in range(2):
            dst = rows_ref[0, k * TM + r]
            pltpu.make_async_copy(hbuf.at[slot, pl.ds(r, 1)], xs_ref.at[pl.ds(dst, 1)],
                                  sems.at[slot]).start(priority=k)
        return carry

    lax.fori_loop(0, TM, issue, 0, unroll=8)

    @pl.when(i == n - 1)
    def _():
        wait_slot(slot)

        @pl.when(n >= 2)
        def _():
            wait_slot(1 - slot)


def _dispatch(fill, n_used, rows, xs, mods, layer, n_lat_tiles, n_tiles):
    m = xs.shape[0]
    grid_spec = pltpu.PrefetchScalarGridSpec(
        num_scalar_prefetch=2,
        grid=(m // TM,),
        in_specs=[
            pl.BlockSpec((None, 1, 2 * TM), lambda i, fill, nu: (i, 0, 0), memory_space=pltpu.SMEM),
            pl.BlockSpec((TM, D_MODEL), lambda i, fill, nu: (i, 0)),
            pl.BlockSpec((None, None, 6, D_MODEL), lambda i, fill, nu: (layer, i // n_lat_tiles, 0, 0)),
        ],
        out_specs=pl.BlockSpec(memory_space=pl.ANY),
        scratch_shapes=[pltpu.VMEM((2, TM, D_MODEL), F32), pltpu.VMEM((MOE_TM, D_MODEL), F32),
                        pltpu.SemaphoreType.DMA((2,)), pltpu.SemaphoreType.DMA((1,))],
    )
    return pl.pallas_call(
        _dispatch_kernel,
        grid_spec=grid_spec,
        out_shape=jax.ShapeDtypeStruct(((n_tiles + 1) * MOE_TM, D_MODEL), F32),
        compiler_params=_cparams(("arbitrary",)),
        name="moe_dispatch",
    )(fill, n_used, rows, xs, mods)


def _experts_kernel(te_ref, nu_ref, x_ref, wg_ref, wu_ref, wd_ref, o_ref, xb_ref, a_ref):
    del te_ref
    f = pl.program_id(1)
    live = pl.program_id(0) < nu_ref[0]

    @pl.when(jnp.logical_not(live) & (f == 0))
    def _():
        o_ref[...] = jnp.zeros_like(o_ref)

    def hidden(xb):
        gate = jnp.dot(xb, wg_ref[...], preferred_element_type=F32)
        up = jnp.dot(xb, wu_ref[...], preferred_element_type=F32)
        return (_silu(gate) * up).astype(BF16)

    @pl.when(live & (f == 0))
    def _():
        xb = x_ref[...].astype(BF16)
        xb_ref[...] = xb
        a_ref[:, 0:MOE_TF] = hidden(xb)

    @pl.when(live & (f == 1))
    def _():
        a_ref[:, MOE_TF:2 * MOE_TF] = hidden(xb_ref[...])
        o_ref[...] = jnp.dot(a_ref[...], wd_ref[...], preferred_element_type=F32)


def _experts(tile_expert, n_used, xs_sorted, wg, wu, wd, n_tiles):
    nf = EXPERT_DIM // MOE_TF
    row_blk = lambda i, f, te, nu: (jnp.minimum(i, nu[0] - 1), 0)
    f_blk = lambda i, f, nu: jnp.where(i < nu[0], f, nf - 1)
    grid_spec = pltpu.PrefetchScalarGridSpec(
        num_scalar_prefetch=2,
        grid=(n_tiles, nf),
        in_specs=[
            pl.BlockSpec((MOE_TM, D_MODEL), row_blk),
            pl.BlockSpec((None, D_MODEL, MOE_TF), lambda i, f, te, nu: (te[i], 0, f_blk(i, f, nu))),
            pl.BlockSpec((None, D_MODEL, MOE_TF), lambda i, f, te, nu: (te[i], 0, f_blk(i, f, nu))),
            pl.BlockSpec((None, EXPERT_DIM, D_MODEL), lambda i, f, te, nu: (te[i], 0, 0)),
        ],
        out_specs=pl.BlockSpec((MOE_TM, D_MODEL), lambda i, f, te, nu: (i, 0)),
        scratch_shapes=[pltpu.VMEM((MOE_TM, D_MODEL), BF16), pltpu.VMEM((MOE_TM, EXPERT_DIM), BF16)],
    )
    assert nf == 2
    return pl.pallas_call(
        _experts_kernel,
        grid_spec=grid_spec,
        out_shape=jax.ShapeDtypeStruct((n_tiles * MOE_TM, D_MODEL), F32),
        compiler_params=_cparams(("arbitrary", "arbitrary")),
        name="moe_experts",
    )(tile_expert, n_used, xs_sorted, wg, wu, wd)


def _combine_kernel(rows_ref, rowsn_ref, wts_ref, x_ref, mod_ref, g_ref, b_ref, ys_ref, o_ref, ybuf, sems):
    i = pl.program_id(0)
    n = pl.num_programs(0)
    slot = i % 2

    def fetch(ids_ref, sl):
        def issue(r, carry):
            for k in range(2):
                src = ids_ref[0, k * TM + r]
                pltpu.make_async_copy(ys_ref.at[pl.ds(src, 1)], ybuf.at[sl, k, pl.ds(r, 1)],
                                      sems.at[sl]).start(priority=k)
            return carry

        lax.fori_loop(0, TM, issue, 0, unroll=8)

    @pl.when(i == 0)
    def _():
        fetch(rows_ref, 0)

    @pl.when(i + 1 < n)
    def _():
        fetch(rowsn_ref, 1 - slot)

    for k in range(2):
        pltpu.make_async_copy(ys_ref.at[pl.ds(0, TM)], ybuf.at[slot, k], sems.at[slot]).wait()

    wts = wts_ref[...]
    y = wts[:, 0:1] * ybuf[slot, 0] + wts[:, 1:2] * ybuf[slot, 1]
    v = DEEPNORM_ALPHA * x_ref[...] + mod_ref[5:6, :] * y
    o_ref[...] = _layer_norm(v, g_ref[...], b_ref[...])


def _combine(rows, wts, xs, mods, layer, n_lat_tiles, ys_sorted, ln_g, ln_b):
    m = xs.shape[0]
    nt = m // TM
    tok = lambda i: (i, 0)
    full = lambda i: (0, 0)
    return pl.pallas_call(
        _combine_kernel,
        grid=(nt,),
        in_specs=[
            pl.BlockSpec((None, 1, 2 * TM), lambda i: (i, 0, 0), memory_space=pltpu.SMEM),
            pl.BlockSpec((None, 1, 2 * TM), lambda i: (jnp.minimum(i + 1, nt - 1), 0, 0),
                         memory_space=pltpu.SMEM),
            pl.BlockSpec((TM, HEAD_PAD), tok),
            pl.BlockSpec((TM, D_MODEL), tok),
            pl.BlockSpec((None, None, 6, D_MODEL), lambda i: (layer, i // n_lat_tiles, 0, 0)),
            pl.BlockSpec((1, D_MODEL), full),
            pl.BlockSpec((1, D_MODEL), full),
            pl.BlockSpec(memory_space=pl.ANY),
        ],
        out_specs=pl.BlockSpec((TM, D_MODEL), tok),
        out_shape=jax.ShapeDtypeStruct((m, D_MODEL), F32),
        scratch_shapes=[pltpu.VMEM((2, 2, TM, D_MODEL), F32), pltpu.SemaphoreType.DMA((2,))],
        compiler_params=_cparams(("arbitrary",)),
        name="moe_combine_ln",
    )(rows, rows, wts, xs, mods, ln_g, ln_b, ys_sorted)


def _moe_layer(xs, routing, mods, layer, n_lat_tiles, wg, wu, wd, ln_g, ln_b):
    m = xs.shape[0]
    n_tiles = (2 * m + N_EXPERTS * (MOE_TM - 1) + MOE_TM - 1) // MOE_TM
    idx, wts, cnt = routing
    counts = cnt[0, :N_EXPERTS].astype(jnp.int32)
    padded = (counts + MOE_TM - 1) // MOE_TM * MOE_TM
    ends = jnp.cumsum(padded)
    offs = ends - padded
    n_used = (ends[-1:] // MOE_TM).astype(jnp.int32)
    tile_row = jnp.arange(n_tiles, dtype=jnp.int32) * MOE_TM
    tile_row = jnp.minimum(tile_row, ends[-1] - MOE_TM)
    tile_expert = jnp.sum((tile_row[:, None] >= ends[None, :]).astype(jnp.int32), axis=1)
    expert_ids = idx[:, 0:2, :]
    group_off = jnp.sum(jnp.where(expert_ids[..., None] == jnp.arange(N_EXPERTS), offs, 0), axis=-1)
    rows = (group_off + idx[:, 2:4, :]).reshape(m // TM, 1, 2 * TM)
    fill = (offs + counts) // 8 * 8
    xs_sorted = _dispatch(fill, n_used, rows, xs, mods, layer, n_lat_tiles, n_tiles)
    ys_sorted = _experts(tile_expert, n_used, xs_sorted, wg, wu, wd, n_tiles)
    return _combine(rows, wts, xs, mods, layer, n_lat_tiles, ys_sorted, ln_g, ln_b)


_ROPE_SWAP = np.array(list(range(8, 16)) + list(range(0, 8)) + list(range(24, 32)) + list(range(16, 24)))


def _prep_attn_weights(w_in, w_q_up, w_kv_up, w_out):
    o = 3 * NA_W
    w_qc = w_in[:, o:o + MLA_Q_LORA]
    o += MLA_Q_LORA
    w_kvc = w_in[:, o:o + MLA_KV_LORA]
    o += MLA_KV_LORA
    w_kr = w_in[:, o:o + MLA_ROPE]
    rope_pad = [(0, 0), (MLA_NOPE, HEAD_PAD - MLA_NOPE - MLA_ROPE)]
    w_in_aug = jnp.concatenate(
        [w_in[:, :3 * NA_W], w_qc, w_kvc, jnp.pad(w_kr, rope_pad), jnp.pad(w_kr[:, _ROPE_SWAP], rope_pad)],
        axis=1).astype(BF16)

    wq = w_q_up.reshape(MLA_Q_LORA, N_HEADS, MLA_NOPE + MLA_ROPE)
    wq_full = jnp.pad(wq, [(0, 0), (0, 0), (0, HEAD_PAD - MLA_NOPE - MLA_ROPE)])
    wq_swap = jnp.pad(wq[:, :, MLA_NOPE:][:, :, _ROPE_SWAP], [(0, 0), (0, 0), rope_pad[1]])
    w_q = jnp.concatenate([wq_full.reshape(MLA_Q_LORA, HP), wq_swap.reshape(MLA_Q_LORA, HP)], axis=1).astype(BF16)

    wkv = w_kv_up.reshape(MLA_KV_LORA, N_HEADS, MLA_NOPE + MLA_V)
    wkn = jnp.pad(wkv[:, :, :MLA_NOPE], [(0, 0), (0, 0), (0, HEAD_PAD - MLA_NOPE)])
    wv = jnp.pad(wkv[:, :, MLA_NOPE:], [(0, 0), (0, 0), (0, HEAD_PAD - MLA_V)])
    w_kv = jnp.concatenate([wkn.reshape(MLA_KV_LORA, HP), wv.reshape(MLA_KV_LORA, HP)], axis=1).astype(BF16)

    wa = w_out[:NA_W].astype(BF16)
    wb = w_out[NA_W:].reshape(N_HEADS, MLA_V, D_MODEL)
    wb = jnp.pad(wb, [(0, 0), (0, HEAD_PAD - MLA_V), (0, 0)]).reshape(HP, D_MODEL).astype(BF16)
    return w_in_aug, w_q, w_kv, wa, wb


def _rope_tables(s, n_ctx):
    t = jnp.arange(s, dtype=jnp.int32)
    row = (t // GRID_W).astype(F32)
    col = (t % GRID_W).astype(F32)
    n_freq = MLA_ROPE // 4
    inv = 1.0 / (ROPE_THETA ** (jnp.arange(n_freq, dtype=F32) / n_freq))
    ar = row[:, None] * inv
    ac = col[:, None] * inv
    cos = jnp.concatenate([jnp.cos(ar), jnp.cos(ar), jnp.cos(ac), jnp.cos(ac)], axis=1)
    sin = jnp.concatenate([-jnp.sin(ar), jnp.sin(ar), -jnp.sin(ac), jnp.sin(ac)], axis=1)
    right = HEAD_PAD - MLA_NOPE - MLA_ROPE
    cos = jnp.pad(cos, [(0, n_ctx), (MLA_NOPE, right)], constant_values=1.0)
    sin = jnp.pad(sin, [(0, n_ctx), (MLA_NOPE, right)])
    return cos, sin


def _na_bias_mask(rel_bias, n_rows):
    nb = n_rows // NA_QROWS
    c = np.arange(GRID_W)
    cs = np.clip(c - NA_KW // 2, 0, GRID_W - NA_KW)
    kc = np.arange(GRID_W)
    ok_c = (kc[None, :] >= cs[:, None]) & (kc[None, :] < cs[:, None] + NA_KW)
    n_dcol = 2 * NA_KW - 1
    left = GRID_W - NA_KW
    period = 2 * GRID_W
    u = jnp.pad(rel_bias, [(0, 0), (0, 0), (left, period - n_dcol - left)])
    flat = jnp.tile(u, (1, 1, GRID_W + 1))[:, :, :GRID_W * (period - 1)]
    toep = flat.reshape(N_HEADS, 2 * NA_KH - 1, GRID_W, period - 1)[:, :, :, GRID_W - 1:]
    toep = jnp.where(ok_c, toep, NEG_BIG)
    masked = jnp.full((N_HEADS, GRID_W, GRID_W), NEG_BIG, F32)
    out = []
    for b in (0, 1, nb - 1):
        kr0 = int(np.clip(b * NA_QROWS - NA_KH // 2, 0, n_rows - NA_KROWS))
        q_blocks = []
        for qr in range(NA_QROWS):
            r = b * NA_QROWS + qr
            rs = int(np.clip(r - NA_KH // 2, 0, n_rows - NA_KH))
            k_blocks = []
            for klr in range(NA_KROWS):
                kr = kr0 + klr
                k_blocks.append(toep[:, kr - r + NA_KH - 1] if rs <= kr < rs + NA_KH else masked)
            q_blocks.append(jnp.concatenate(k_blocks, axis=-1))
        out.append(jnp.concatenate(q_blocks, axis=1))
    return jnp.stack(out, axis=0)


def kernel(x, c, ctx, c_ctx, mod_w, mod_b, ln1_g, ln1_b, ln2_g, ln2_b, attn_w_in, na_rel_bias, mla_q_norm,
           mla_w_q_up, mla_kv_norm, mla_w_kv_up, attn_w_out, ffn_w_gate, ffn_w_up, ffn_w_down, pool_w,
           pool_scale, moe_router, moe_w_gate, moe_w_up, moe_w_down):
    assert x.shape[0] == 1 and c.shape[0] == 1 and ctx.shape[0] == 1
    s = x.shape[1]
    n_ctx = ctx.shape[1]
    assert n_ctx == TM and s % (NA_QROWS * GRID_W) == 0 and s % MLA_TQ == 0
    assert (s + n_ctx) % MLA_TK == 0
    n_lat_tiles = s // TM
    depth = mod_w.shape[0]

    xs = (x[0], ctx[0])
    ct = jnp.stack([c[0], c_ctx], axis=1)
    mods = _modulation(ct, mod_w, mod_b).reshape(depth, 2, 6, D_MODEL)
    cos_t, sin_t = _rope_tables(s, n_ctx)
    row = lambda v: v.reshape(1, -1)
    moe_bf16 = None

    for i in range(depth):
        j = i // 2
        ctx_live = any(l % 2 == 0 for l in range(i + 1, depth))
        if i % 2 == 0:
            w_in, w_q, w_kv, wa, wb = _prep_attn_weights(attn_w_in[j], mla_w_q_up[j], mla_w_kv_up[j],
                                                         attn_w_out[j])
            qa, ka, va, qm, kmt, vm = _attn_proj(xs, mods, i, n_lat_tiles, w_in, row(mla_q_norm[j]), w_q,
                                                 row(mla_kv_norm[j]), w_kv, cos_t, sin_t)
            bias_mask = _na_bias_mask(na_rel_bias[j], s // GRID_W)
            oa_lat = _na_attention(qa, ka, va, bias_mask, s)
            grouped = lambda w: w.reshape(w.shape[0], N_EXPERTS, w.shape[1] // N_EXPERTS, w.shape[2])
            cast = (grouped(ffn_w_gate), grouped(ffn_w_up), grouped(ffn_w_down))
            if i + 1 < depth:
                cast += (moe_w_gate, moe_w_up, moe_w_down)
            ob_lat, cast_bf16 = _mla_attention(qm, kmt, vm, s, cast, j)
            ffn_wg, ffn_wu, ffn_wd = (w.reshape(-1, w.shape[2]) for w in cast_bf16[:3])
            moe_bf16 = cast_bf16[3:]
            oa_ctx = ob_ctx = None
            if ctx_live:
                oa_ctx = _na_ctx_attention(qa, ka, va, s)
                ob_ctx = _mla_ctx_attention(qm, kmt, vm, s)
            xs = _out_ffn(xs, mods, i, n_lat_tiles, oa_lat, ob_lat, oa_ctx, ob_ctx, wa, wb,
                          row(ln1_g[i]), row(ln1_b[i]), ffn_wg, ffn_wu, ffn_wd, row(ln2_g[i]), row(ln2_b[i]))
        else:
            xs, *routing = _pool_router(xs, mods, i, n_lat_tiles, s, pool_w[j].astype(BF16), row(pool_scale[j]),
                                        row(ln1_g[i]), row(ln1_b[i]), moe_router[j])
            xs = _moe_layer(xs, routing, mods, i, n_lat_tiles, *moe_bf16, row(ln2_g[i]), row(ln2_b[i]))
    return xs[:s][None] if xs.shape[0] != s else xs[None]
```
